```python
import math
import jax, jax.numpy as jnp
from jax import lax
import numpy as np

D_MODEL = 1024
BATCH = 4
SEQ = 4096
DEPTH = 4

HEAD_DIM = 64
ROPE_DIM = HEAD_DIM // 4
ROPE_THETA = 500000.0
NORM_EPS = 1e-6
Q_BLOCK = 128
SCALE = HEAD_DIM ** -0.5
NEG = -1e30
FORCE = 1e9

NSA_HEADS = 5
CMP_BLOCK = 32
CMP_STRIDE = 16
CMP_HIDDEN = 128
SLC_BLOCK = 64
SLC_TOPN = 16
WIN = 512

DSA_HEADS = 5
IDX_HEADS = 8
IDX_DIM = 32
IDX_ROPE_DIM = IDX_DIM // 4
DSA_TOPK_MAX = 256

DIL_PAIRS = ((128, 1), (512, 4), (2048, 16))
DIL_HEADS_PER_GROUP = 2
DIL_HEADS = len(DIL_PAIRS) * DIL_HEADS_PER_GROUP
DIL_KEYS = DIL_PAIRS[0][0] // DIL_PAIRS[0][1] + 1

D_MIX = (NSA_HEADS + DSA_HEADS + DIL_HEADS) * HEAD_DIM

D_FF = 2816
CONV_WIDTH = 3

IN_SPLITS = (
    NSA_HEADS * HEAD_DIM,
    HEAD_DIM, HEAD_DIM,
    HEAD_DIM, HEAD_DIM,
    HEAD_DIM, HEAD_DIM,
    3 * NSA_HEADS,
    DSA_HEADS * HEAD_DIM, HEAD_DIM, HEAD_DIM,
    IDX_HEADS * IDX_DIM, IDX_DIM, IDX_HEADS,
    DIL_HEADS * HEAD_DIM, DIL_HEADS * HEAD_DIM, DIL_HEADS * HEAD_DIM,
)
D_IN = sum(IN_SPLITS)
IN_OFFSETS = tuple(int(o) for o in np.cumsum(IN_SPLITS)[:-1])

kernel_name = 'hybrid_nsa_dsa_dilated_convffn'


def rmsnorm(x, g):
    xf = x.astype(jnp.float32)
    y = xf * lax.rsqrt(jnp.mean(xf * xf, axis=-1, keepdims=True) + NORM_EPS)
    return (y * g.astype(jnp.float32)).astype(x.dtype)


def rope_tables(L, rot_dim, dtype):
    inv = 1.0 / (ROPE_THETA ** (np.arange(0, rot_dim, 2, dtype=np.float32) / np.float32(rot_dim)))
    ang = np.arange(L, dtype=np.float32)[:, None] * inv[None, :]
    return jnp.asarray(np.cos(ang), dtype=dtype), jnp.asarray(np.sin(ang), dtype=dtype)


def apply_rope(x, cos, sin):
    r = 2 * cos.shape[-1]
    x1, x2, xp = x[..., : r // 2], x[..., r // 2 : r], x[..., r:]
    c, s = cos[None, :, None, :], sin[None, :, None, :]
    return jnp.concatenate([x1 * c - x2 * s, x2 * c + x1 * s, xp], axis=-1)


def rope1(k, cos, sin):
    return apply_rope(k[:, :, None], cos, sin)[:, :, 0]


def masked_softmax(s, mask):
    s = jnp.where(mask, s, NEG)
    m = jnp.max(s, axis=-1, keepdims=True)
    e = jnp.where(mask, jnp.exp(s - m), 0.0)
    den = jnp.maximum(jnp.sum(e, axis=-1, keepdims=True), 1e-30)
    return e / den, m + jnp.log(den)


def to_chunks(a):
    b, l = a.shape[:2]
    return jnp.moveaxis(a.reshape(b, l // Q_BLOCK, Q_BLOCK, *a.shape[2:]), 1, 0)


def from_chunks(a):
    a = jnp.moveaxis(a, 0, 1)
    return a.reshape(a.shape[0], a.shape[1] * a.shape[2], *a.shape[3:])


def q_positions(L):
    return jnp.arange(L).reshape(L // Q_BLOCK, Q_BLOCK)


def gather_rows(table, idx):
    return jax.vmap(lambda t, i: t[i])(table, idx)


def nsa_mixer(q, kc, vc, ks, vs, kw, vw, g, cmp_pos, cmp_w1, cmp_w2, cos, sin):
    B, L, H, D = q.shape
    t = jnp.arange(L)
    n_cmp = (L - CMP_BLOCK) // CMP_STRIDE + 1
    c_start = np.arange(n_cmp) * CMP_STRIDE
    blk_idx = c_start[:, None] + np.arange(CMP_BLOCK)[None, :]
    raw = jnp.stack([kc, vc], axis=0)[:, :, blk_idx] + cmp_pos[:, None, None]
    hid = jax.nn.gelu(jnp.einsum('cbnf,cfh->cbnh', raw.reshape(2, B, n_cmp, CMP_BLOCK * D), cmp_w1))
    kv_cmp = jnp.einsum('cbnh,chd->cbnd', hid, cmp_w2)
    k_cmp, v_cmp = kv_cmp[0], kv_cmp[1]
    s_c = jnp.einsum('bthd,bnd->bhtn', q, k_cmp).astype(jnp.float32) * SCALE
    mask_c = jnp.asarray(c_start + CMP_BLOCK - 1)[None, :] <= t[:, None]
    p_c, _ = masked_softmax(s_c, mask_c)
    o_cmp = jnp.einsum('bhtn,bnd->bthd', p_c.astype(v_cmp.dtype), v_cmp)
    n_slc = L // SLC_BLOCK
    s_start = np.arange(n_slc) * SLC_BLOCK
    overlap = ((c_start[:, None] < s_start[None, :] + SLC_BLOCK)
               & (c_start[:, None] + CMP_BLOCK > s_start[None, :])).astype(np.float32)
    imp = jnp.einsum('bhtn,nj->btj', p_c, jnp.asarray(overlap))
    cur = t // SLC_BLOCK
    j = jnp.arange(n_slc)
    valid_blk = j[None, :] <= cur[:, None]
    forced = (j[None, :] == 0) | (j[None, :] == cur[:, None]) | (j[None, :] == cur[:, None] - 1)
    imp = jnp.where(forced, FORCE, jnp.where(valid_blk, imp, NEG))
    n_top = min(SLC_TOPN, n_slc)
    top_s, sel = lax.top_k(imp, n_top)
    sel_ok = top_s > 0.5 * NEG
    q_r = apply_rope(q, cos, sin)
    ks_blocks = rope1(ks, cos, sin).reshape(B, n_slc, SLC_BLOCK, D)
    vs_blocks = vs.reshape(B, n_slc, SLC_BLOCK, D)
    in_blk = jnp.arange(SLC_BLOCK)

    def sel_chunk(args):
        qc, selc, okc, tc = args
        kg = gather_rows(ks_blocks, selc).reshape(B, Q_BLOCK, n_top * SLC_BLOCK, D)
        vg = gather_rows(vs_blocks, selc).reshape(B, Q_BLOCK, n_top * SLC_BLOCK, D)
        kpos = (selc[..., None] * SLC_BLOCK + in_blk).reshape(B, Q_BLOCK, -1)
        ok = jnp.broadcast_to(okc[..., None], okc.shape + (SLC_BLOCK,)).reshape(B, Q_BLOCK, -1)
        ok = ok & (kpos <= tc[None, :, None])
        s = jnp.einsum('bqhd,bqkd->bqhk', qc, kg).astype(jnp.float32) * SCALE
        p, _ = masked_softmax(s, ok[:, :, None, :])
        return jnp.einsum('bqhk,bqkd->bqhd', p.astype(vg.dtype), vg)

    o_slc = from_chunks(lax.map(sel_chunk, (to_chunks(q_r), to_chunks(sel), to_chunks(sel_ok), q_positions(L))))
    nb = L // Q_BLOCK
    kw_p = jnp.pad(rope1(kw, cos, sin), ((0, 0), (WIN, 0), (0, 0)))
    vw_p = jnp.pad(vw, ((0, 0), (WIN, 0), (0, 0)))
    span_idx = np.arange(nb)[:, None] * Q_BLOCK + np.arange(Q_BLOCK + WIN)[None, :]
    kband, vband = kw_p[:, span_idx], vw_p[:, span_idx]
    s_w = jnp.einsum('bnqhd,bnkd->bnhqk', q_r.reshape(B, nb, Q_BLOCK, H, D), kband).astype(jnp.float32) * SCALE
    key_pos = span_idx - WIN
    qpos = np.arange(nb)[:, None] * Q_BLOCK + np.arange(Q_BLOCK)[None, :]
    diff = qpos[:, :, None] - key_pos[:, None, :]
    mask_w = jnp.asarray((diff >= 0) & (diff < WIN) & (key_pos[:, None, :] >= 0))
    p_w, _ = masked_softmax(s_w, mask_w[None, :, None])
    o_win = jnp.einsum('bnhqk,bnkd->bnqhd', p_w.astype(vband.dtype), vband).reshape(B, L, H, D)
    gs = jax.nn.sigmoid(g)
    return gs[..., 0:1] * o_cmp + gs[..., 1:2] * o_slc + gs[..., 2:3] * o_win


def dsa_mixer(q, k, v, qi, ki, wi, cos_h, sin_h, cos_i, sin_i):
    B, L = q.shape[:2]
    q = apply_rope(q, cos_h, sin_h)
    k = rope1(k, cos_h, sin_h)
    qi = apply_rope(qi, cos_i, sin_i)
    ki = rope1(ki, cos_i, sin_i)
    top = min(DSA_TOPK_MAX, L // 4)
    keys = jnp.arange(L)

    def chunk(args):
        qc, qic, wc, tc = args
        dots = jnp.einsum('bqhi,bsi->bqhs', qic, ki).astype(jnp.float32) * IDX_DIM ** -0.5
        score = jnp.einsum('bqh,bqhs->bqs', wc.astype(jnp.float32) * IDX_HEADS ** -0.5, jax.nn.relu(dots))
        score = jnp.where(keys[None, None, :] <= tc[None, :, None], score, NEG)
        top_s, sel = lax.top_k(score, top)
        ok = top_s > 0.5 * NEG
        kg, vg = gather_rows(k, sel), gather_rows(v, sel)
        s = jnp.einsum('bqhd,bqkd->bqhk', qc, kg).astype(jnp.float32) * SCALE
        p, _ = masked_softmax(s, ok[:, :, None, :])
        return jnp.einsum('bqhk,bqkd->bqhd', p.astype(vg.dtype), vg)

    return from_chunks(lax.map(chunk, (to_chunks(q), to_chunks(qi), to_chunks(wi), q_positions(L))))


def dilated_mixer(q, k, v, cos, sin):
    B, L, _, D = q.shape
    G, Hg = len(DIL_PAIRS), DIL_HEADS_PER_GROUP
    q = apply_rope(q, cos, sin).reshape(B, L, G, Hg, D)
    k = apply_rope(k, cos, sin).reshape(B, L, G, Hg, D)
    v = v.reshape(B, L, G, Hg, D)
    offs = jnp.asarray(np.stack([np.arange(DIL_KEYS) * r for _, r in DIL_PAIRS]))
    gsel = jnp.arange(G)[None, :, None]

    def chunk(args):
        qc, tc = args
        kpos = tc[:, None, None] - offs[None]
        ok = kpos >= 0
        kidx = jnp.maximum(kpos, 0)
        kg, vg = k[:, kidx, gsel], v[:, kidx, gsel]
        s = jnp.einsum('bqghd,bqgkhd->bqghk', qc, kg).astype(jnp.float32) * SCALE
        p, lse = masked_softmax(s, ok[None, :, :, None, :])
        o = jnp.einsum('bqghk,bqgkhd->bqghd', p.astype(vg.dtype), vg)
        return o, lse[..., 0]

    o, lse = lax.map(chunk, (to_chunks(q), q_positions(L)))
    o, lse = from_chunks(o), from_chunks(lse)
    alpha = jax.nn.softmax(lse, axis=2)
    return (alpha[..., None].astype(o.dtype) * o).reshape(B, L, G * Hg, D)


def hybrid_mixer(h, w_in, cmp_pos, cmp_w1, cmp_w2, cos_h, sin_h, cos_i, sin_i):
    B, L, _ = h.shape
    (a_q, a_kc, a_vc, a_ks, a_vs, a_kw, a_vw, a_g,
     b_q, b_k, b_v, b_iq, b_ik, b_iw,
     c_q, c_k, c_v) = jnp.split(h @ w_in, IN_OFFSETS, axis=-1)
    heads = lambda a, n: a.reshape(B, L, n, -1)
    o_a = nsa_mixer(heads(a_q, NSA_HEADS), a_kc, a_vc, a_ks, a_vs, a_kw, a_vw, heads(a_g, NSA_HEADS),
                    cmp_pos, cmp_w1, cmp_w2, cos_h, sin_h)
    o_b = dsa_mixer(heads(b_q, DSA_HEADS), b_k, b_v, heads(b_iq, IDX_HEADS), b_ik, b_iw,
                    cos_h, sin_h, cos_i, sin_i)
    o_c = dilated_mixer(heads(c_q, DIL_HEADS), heads(c_k, DIL_HEADS), heads(c_v, DIL_HEADS), cos_h, sin_h)
    return jnp.concatenate([o_a.reshape(B, L, -1), o_b.reshape(B, L, -1), o_c.reshape(B, L, -1)], axis=-1)


def conv_glu_ffn(h, w_up, conv_w, conv_b, w_down):
    a, u = jnp.split(h @ w_up, 2, axis=-1)
    a = lax.conv_general_dilated(a, conv_w[:, None, :], window_strides=(1,),
                                 padding=((CONV_WIDTH - 1, 0),),
                                 dimension_numbers=('NWC', 'WIO', 'NWC'),
                                 feature_group_count=D_FF) + conv_b
    return (jax.nn.silu(a) * u) @ w_down


def setup_inputs(seed: int = 0) -> dict:
    key = jax.random.key(seed)
    ks = jax.random.split(key, 13)
    nrm = lambda k, shape, scale: jax.random.normal(k, shape, jnp.float32) * scale
    return {
        'x': nrm(ks[0], (BATCH, SEQ, D_MODEL), 1.0),
        'norm1_g': 1.0 + nrm(ks[1], (DEPTH, D_MODEL), 0.01),
        'w_in': nrm(ks[2], (DEPTH, D_MODEL, D_IN), D_MODEL ** -0.5),
        'cmp_pos': nrm(ks[3], (DEPTH, 2, CMP_BLOCK, HEAD_DIM), 0.02),
        'cmp_w1': nrm(ks[4], (DEPTH, 2, CMP_BLOCK * HEAD_DIM, CMP_HIDDEN), (CMP_BLOCK * HEAD_DIM) ** -0.5),
        'cmp_w2': nrm(ks[5], (DEPTH, 2, CMP_HIDDEN, HEAD_DIM), CMP_HIDDEN ** -0.5),
        'w_out': nrm(ks[6], (DEPTH, D_MIX, D_MODEL), D_MIX ** -0.5),
        'norm2_g': 1.0 + nrm(ks[7], (DEPTH, D_MODEL), 0.01),
        'w_up': nrm(ks[8], (DEPTH, D_MODEL, 2 * D_FF), D_MODEL ** -0.5),
        'conv_w': nrm(ks[9], (DEPTH, CONV_WIDTH, D_FF), CONV_WIDTH ** -0.5),
        'conv_b': nrm(ks[10], (DEPTH, D_FF), 0.01),
        'w_down': nrm(ks[11], (DEPTH, D_FF, D_MODEL), D_FF ** -0.5),
        'final_g': 1.0 + nrm(ks[12], (D_MODEL,), 0.01),
    }


def reference(x, norm1_g, w_in, cmp_pos, cmp_w1, cmp_w2, w_out, norm2_g, w_up, conv_w, conv_b, w_down, final_g):
    L = x.shape[1]
    cos_h, sin_h = rope_tables(L, ROPE_DIM, x.dtype)
    cos_i, sin_i = rope_tables(L, IDX_ROPE_DIM, x.dtype)
    for i in range(DEPTH):
        h = rmsnorm(x, norm1_g[i])
        mix = hybrid_mixer(h, w_in[i], cmp_pos[i], cmp_w1[i], cmp_w2[i], cos_h, sin_h, cos_i, sin_i)
        x = x + mix @ w_out[i]
        h = rmsnorm(x, norm2_g[i])
        x = x + conv_glu_ffn(h, w_up[i], conv_w[i], conv_b[i], w_down[i])
    return rmsnorm(x, final_g)
```

```python
import functools

import numpy as np
import jax
import jax.numpy as jnp
from jax import lax
from jax.experimental import pallas as pl
from jax.experimental.pallas import tpu as pltpu

F32 = jnp.float32
BF16 = jnp.bfloat16
I32 = jnp.int32

D_MODEL = 1024
HEAD_DIM = 64
ROPE_DIM = HEAD_DIM // 4
ROPE_THETA = 500000.0
NORM_EPS = 1e-6
SCALE = HEAD_DIM ** -0.5
NEG = -1e30
FORCE = 1e9
NSA_HEADS = 5
CMP_BLOCK = 32
CMP_STRIDE = 16
CMP_HIDDEN = 128
SLC_BLOCK = 64
SLC_TOPN = 16
WIN = 512
DSA_HEADS = 5
IDX_HEADS = 8
IDX_DIM = 32
IDX_ROPE_DIM = IDX_DIM // 4
DSA_TOPK_MAX = 256
DIL_PAIRS = ((128, 1), (512, 4), (2048, 16))
DIL_HEADS_PER_GROUP = 2
DIL_HEADS = len(DIL_PAIRS) * DIL_HEADS_PER_GROUP
D_FF = 2816
CONV_WIDTH = 3

LANES = 128
QB = 128
KC = 512
ROW_TILE = 512
HALO = 16
VMEM_LIMIT = 48 * 1024 * 1024
INT_MIN = -2 ** 31

_A0 = 0
_B0 = 719
_C0 = 1463
_SEGS = (
    ("c_q", _C0, 384, 384), ("c_k", _C0 + 384, 384, 384), ("c_v", _C0 + 768, 384, 384),
    ("a_q", _A0, 320, 384), ("b_q", _B0, 320, 384),
    ("a_kcvc", _A0 + 320, 128, 128),
    ("b_iq", _B0 + 448, 256, 256),
    ("a_ksvs", _A0 + 448, 128, 128), ("a_kwvw", _A0 + 576, 128, 128),
    ("a_g", _A0 + 704, 15, 128),
    ("b_kv", _B0 + 320, 128, 128),
    ("b_ikiw", _B0 + 704, 40, 128),
)
P_COLS = sum(s[3] for s in _SEGS)


def _seg_offsets():
    offs, o = {}, 0
    for name, _, _, pw in _SEGS:
        offs[name] = o
        o += pw
    return offs


_OFF = _seg_offsets()


def _pad_w_in(w):
    parts = []
    for _, src, wdt, pw in _SEGS:
        parts.append(w[:, src:src + wdt])
        if pw > wdt:
            parts.append(jnp.zeros((w.shape[0], pw - wdt), w.dtype))
    return jnp.concatenate(parts, axis=1)


def _rope_lane_tables(L, head_dim, rot_dim):
    half = rot_dim // 2
    inv = 1.0 / (ROPE_THETA ** (np.arange(0, rot_dim, 2, dtype=np.float32) / np.float32(rot_dim)))
    ang = np.arange(L, dtype=np.float32)[:, None] * inv[None, :]
    cos, sin = np.cos(ang).astype(np.float32), np.sin(ang).astype(np.float32)
    d = np.arange(LANES) % head_dim
    lo, hi = d < half, (d >= half) & (d < rot_dim)
    c = np.ones((L, LANES), np.float32)
    s1 = np.zeros((L, LANES), np.float32)
    s2 = np.zeros((L, LANES), np.float32)
    c[:, lo] = cos[:, d[lo]]
    c[:, hi] = cos[:, d[hi] - half]
    s1[:, lo] = -sin[:, d[lo]]
    s2[:, hi] = sin[:, d[hi] - half]
    return jnp.asarray(np.stack([c, s1, s2]))


def _rope(x, c, s1, s2, half):
    xp = pltpu.roll(x, LANES - half, 1)
    xm = pltpu.roll(x, half, 1)
    return x * c + xp * s1 + xm * s2


def _rope_wide(x, c, s1, s2, half):
    n = x.shape[1] // LANES
    return jnp.concatenate([_rope(x[:, LANES * s:LANES * (s + 1)], c, s1, s2, half) for s in range(n)], axis=1)


def _softmax_cols(s, mask):
    s = jnp.where(mask, s, NEG)
    m = jnp.max(s, axis=0, keepdims=True)
    e = jnp.where(mask, jnp.exp(s - m), 0.0)
    den = jnp.maximum(jnp.sum(e, axis=0, keepdims=True), 1e-30)
    return e, m, den


def _params(n_grid):
    return pltpu.CompilerParams(dimension_semantics=("arbitrary",) * n_grid, vmem_limit_bytes=VMEM_LIMIT)


def _rmsnorm_rows(x, g):
    return x * lax.rsqrt(jnp.mean(x * x, axis=-1, keepdims=True) + NORM_EPS) * g


def _inproj_kernel(x_ref, g_ref, w_ref, o_ref):
    hn = _rmsnorm_rows(x_ref[...], g_ref[...]).astype(BF16)
    n = w_ref.shape[1]
    for c0 in range(0, n, 512):
        c1 = min(c0 + 512, n)
        o_ref[:, c0:c1] = jnp.dot(hn, w_ref[:, c0:c1], preferred_element_type=F32)


def _inproj(x2, g, w_bf):
    n, d = x2.shape
    pc = w_bf.shape[1]
    return pl.pallas_call(
        _inproj_kernel,
        grid=(n // ROW_TILE,),
        in_specs=[pl.BlockSpec((ROW_TILE, d), lambda i: (i, 0)),
                  pl.BlockSpec((1, d), lambda i: (0, 0)),
                  pl.BlockSpec((d, pc), lambda i: (0, 0))],
        out_specs=pl.BlockSpec((ROW_TILE, pc), lambda i: (i, 0)),
        out_shape=jax.ShapeDtypeStruct((n, pc), F32),
        compiler_params=_params(1),
        name="inproj",
    )(x2, g.reshape(1, d), w_bf)


def _outproj_kernel(oa_ref, ob_ref, oc_ref, w_ref, x_ref, o_ref):
    mix = jnp.concatenate([oa_ref[...], ob_ref[...], oc_ref[...]], axis=1)
    o_ref[...] = x_ref[...] + jnp.dot(mix, w_ref[...], preferred_element_type=F32)


def _outproj(oa, ob, oc, w_bf, x2):
    n, d = x2.shape
    k = w_bf.shape[0]
    mw = oa.shape[1]
    return pl.pallas_call(
        _outproj_kernel,
        grid=(n // ROW_TILE,),
        in_specs=[pl.BlockSpec((ROW_TILE, mw), lambda i: (i, 0)),
                  pl.BlockSpec((ROW_TILE, mw), lambda i: (i, 0)),
                  pl.BlockSpec((ROW_TILE, mw), lambda i: (i, 0)),
                  pl.BlockSpec((k, d), lambda i: (0, 0)),
                  pl.BlockSpec((ROW_TILE, d), lambda i: (i, 0))],
        out_specs=pl.BlockSpec((ROW_TILE, d), lambda i: (i, 0)),
        out_shape=jax.ShapeDtypeStruct((n, d), F32),
        compiler_params=_params(1),
        name="outproj",
    )(oa, ob, oc, w_bf, x2)


def _ffn_up_kernel(x_ref, xh_ref, g_ref, wa_ref, wu_ref, cw_ref, cb_ref, o_ref, hn_scr, hh_scr, a_scr, *, tiles_per_seq):
    i, j = pl.program_id(0), pl.program_id(1)
    tm = x_ref.shape[0]

    @pl.when(j == 0)
    def _norm():
        hn_scr[...] = _rmsnorm_rows(x_ref[...], g_ref[...]).astype(BF16)
        keep = jnp.where(i % tiles_per_seq == 0, 0.0, 1.0)
        hh_scr[...] = (_rmsnorm_rows(xh_ref[...], g_ref[...]) * keep).astype(BF16)

    hn = hn_scr[...]
    a = jnp.dot(hn, wa_ref[...], preferred_element_type=F32)
    u = jnp.dot(hn, wu_ref[...], preferred_element_type=F32)
    a_scr[0:HALO, :] = jnp.dot(hh_scr[...], wa_ref[...], preferred_element_type=F32)
    a_scr[HALO:HALO + tm, :] = a
    a1 = a_scr[pl.ds(HALO - 1, tm), :]
    a2 = a_scr[pl.ds(HALO - 2, tm), :]
    cw = cw_ref[...]
    conv = cw[0:1, :] * a2 + cw[1:2, :] * a1 + cw[2:3, :] * a + cb_ref[...]
    o_ref[...] = (conv * jax.nn.sigmoid(conv) * u).astype(BF16)


def _ffn_up(x2, g, w_up_bf, conv_w, conv_b, seq_len):
    n, d = x2.shape
    tn = D_FF // 2
    nj = D_FF // tn
    hb = ROW_TILE // HALO
    kern = functools.partial(_ffn_up_kernel, tiles_per_seq=seq_len // ROW_TILE)
    return pl.pallas_call(
        kern,
        grid=(n // ROW_TILE, nj),
        in_specs=[pl.BlockSpec((ROW_TILE, d), lambda i, j: (i, 0)),
                  pl.BlockSpec((HALO, d), lambda i, j: (jnp.maximum(i * hb - 1, 0), 0)),
                  pl.BlockSpec((1, d), lambda i, j: (0, 0)),
                  pl.BlockSpec((d, tn), lambda i, j: (0, j)),
                  pl.BlockSpec((d, tn), lambda i, j: (0, j + nj)),
                  pl.BlockSpec((CONV_WIDTH, tn), lambda i, j: (0, j)),
                  pl.BlockSpec((1, tn), lambda i, j: (0, j))],
        out_specs=pl.BlockSpec((ROW_TILE, tn), lambda i, j: (i, j)),
        out_shape=jax.ShapeDtypeStruct((n, D_FF), BF16),
        scratch_shapes=[pltpu.VMEM((ROW_TILE, d), BF16), pltpu.VMEM((HALO, d), BF16),
                        pltpu.VMEM((ROW_TILE + HALO, tn), F32)],
        compiler_params=_params(2),
        name="ffn_up",
    )(x2, x2, g.reshape(1, d), w_up_bf, w_up_bf, conv_w, conv_b.reshape(1, D_FF))


def _ffn_down_kernel(a_ref, w_ref, x_ref, g_ref, o_ref, *, final_norm):
    y = x_ref[...] + jnp.dot(a_ref[...], w_ref[...], preferred_element_type=F32)
    if final_norm:
        y = _rmsnorm_rows(y, g_ref[...])
    o_ref[...] = y


def _ffn_down(act, w_bf, x2, final_g, final_norm):
    n, d = x2.shape
    k = act.shape[1]
    return pl.pallas_call(
        functools.partial(_ffn_down_kernel, final_norm=final_norm),
        grid=(n // ROW_TILE,),
        in_specs=[pl.BlockSpec((ROW_TILE, k), lambda i: (i, 0)),
                  pl.BlockSpec((k, d), lambda i: (0, 0)),
                  pl.BlockSpec((ROW_TILE, d), lambda i: (i, 0)),
                  pl.BlockSpec((1, d), lambda i: (0, 0))],
        out_specs=pl.BlockSpec((ROW_TILE, d), lambda i: (i, 0)),
        out_shape=jax.ShapeDtypeStruct((n, d), F32),
        compiler_params=_params(1),
        name="ffn_down",
    )(act, w_bf, x2, final_g.reshape(1, d))


def _cmp_kernel(kc_ref, vc_ref, pos_ref, w1_ref, w2_ref, o_ref):
    out = None
    for c, src in enumerate((kc_ref, vc_ref)):
        r = src[0]
        lo = jnp.dot((r + pos_ref[c, 0:1, :]).astype(BF16), w1_ref[c, 0], preferred_element_type=F32)
        hi = jnp.dot((r + pos_ref[c, 1:2, :]).astype(BF16), w1_ref[c, 1], preferred_element_type=F32)
        nrows = hi.shape[0]
        hid = jax.nn.gelu(lo + pltpu.roll(hi, nrows - 1, 0))
        y = jnp.dot(hid.astype(BF16), w2_ref[c], preferred_element_type=F32)
        out = y if out is None else out + y
    o_ref[0] = out


def _nsa_compress(kc_r, vc_r, pos, w1_bf, w2_pad_bf):
    b, nr, wd = kc_r.shape
    return pl.pallas_call(
        _cmp_kernel,
        grid=(b,),
        in_specs=[pl.BlockSpec((1, nr, wd), lambda i: (i, 0, 0)),
                  pl.BlockSpec((1, nr, wd), lambda i: (i, 0, 0)),
                  pl.BlockSpec(pos.shape, lambda i: (0, 0, 0)),
                  pl.BlockSpec(w1_bf.shape, lambda i: (0, 0, 0, 0)),
                  pl.BlockSpec(w2_pad_bf.shape, lambda i: (0, 0, 0))],
        out_specs=pl.BlockSpec((1, nr, LANES), lambda i: (i, 0, 0)),
        out_shape=jax.ShapeDtypeStruct((b, nr, LANES), F32),
        compiler_params=_params(1),
        name="nsa_compress",
    )(kc_r, vc_r, pos, w1_bf, w2_pad_bf)


def _stack_heads_t(x_t, n_heads, hd):
    z = jnp.zeros((LANES - hd, LANES), F32)
    return jnp.concatenate(
        [jnp.concatenate([x_t[hd * h:hd * (h + 1), :], z], axis=0) for h in range(n_heads)], axis=1)


def _nsa_kernel(q_ref, g_ref, ksvs_ref, kwvw_ref, kvc_ref, c_ref, s1_ref, s2_ref, ovt_ref, o_ref,
                ks_scr, vst_scr, kw_scr, vwt_scr, kc_scr, vct_scr, blk_scr, *, n_top):
    i = pl.program_id(1)
    H = NSA_HEADS
    HW = H * LANES
    seq = ks_scr.shape[0]
    lane_q = lax.broadcasted_iota(I32, (QB, LANES), 1)

    @pl.when(i == 0)
    def _init():
        ks_scr[...] = jnp.zeros_like(ks_scr)
        kw_scr[...] = jnp.zeros_like(kw_scr)
        vst_scr[...] = jnp.zeros_like(vst_scr)
        vwt_scr[...] = jnp.zeros_like(vwt_scr)
        kvc = kvc_ref[0]
        lane_c = lax.broadcasted_iota(I32, kvc.shape, 1)
        kc_scr[...] = jnp.where(lane_c < HEAD_DIM, kvc, 0.0).astype(BF16)
        vct_scr[...] = kvc.T[HEAD_DIM:2 * HEAD_DIM, :].astype(BF16)

    c, s1, s2 = c_ref[...], s1_ref[...], s2_ref[...]
    r0 = pl.multiple_of(i * QB, QB)
    for src, kdst, vdst in ((ksvs_ref, ks_scr, vst_scr), (kwvw_ref, kw_scr, vwt_scr)):
        kv = src[...]
        kr = _rope(kv, c, s1, s2, ROPE_DIM // 2)
        kdst[pl.ds(r0, QB), :] = jnp.where(lane_q < HEAD_DIM, kr, 0.0).astype(BF16)
        vdst[:, pl.ds(r0, QB)] = kv.T[HEAD_DIM:2 * HEAD_DIM, :].astype(BF16)

    q = q_ref[...] * SCALE
    qr = _rope_wide(q, c, s1, s2, ROPE_DIM // 2)
    q_st = _stack_heads_t(q.T, H, HEAD_DIM).astype(BF16)
    qr_st = _stack_heads_t(qr.T, H, HEAD_DIM).astype(BF16)

    ncp = kc_scr.shape[0]
    s_c = jnp.dot(kc_scr[...], q_st, preferred_element_type=F32)
    n_idx = lax.broadcasted_iota(I32, (ncp, HW), 0)
    t_c = i * QB + (lax.broadcasted_iota(I32, (ncp, HW), 1) & (LANES - 1))
    e_c, _, den_c = _softmax_cols(s_c, n_idx * CMP_STRIDE + (CMP_BLOCK - 1) <= t_c)
    p_c = e_c / den_c
    o_cmp = jnp.dot(vct_scr[...], p_c.astype(BF16), preferred_element_type=F32)
    psum = p_c[:, 0:LANES]
    for h in range(1, H):
        psum = psum + p_c[:, LANES * h:LANES * (h + 1)]

    hi = psum.astype(BF16)
    r1 = psum - hi.astype(F32)
    mid = r1.astype(BF16)
    lo = (r1 - mid.astype(F32)).astype(BF16)
    ovt = ovt_ref[...]
    imp = (jnp.dot(ovt, hi, preferred_element_type=F32) + jnp.dot(ovt, mid, preferred_element_type=F32)
           + jnp.dot(ovt, lo, preferred_element_type=F32))
    n_slc = imp.shape[0]
    jb = lax.broadcasted_iota(I32, (n_slc, LANES), 0)
    tq = i * QB + lax.broadcasted_iota(I32, (n_slc, LANES), 1)
    cur = tq >> (SLC_BLOCK.bit_length() - 1)
    forced = (jb == 0) | (jb == cur) | (jb == cur - 1)
    val = jnp.where(forced, FORCE, jnp.where(jb <= cur, imp, NEG))
    rank = jnp.zeros((n_slc, LANES), I32)
    for jp in range(n_slc):
        row = val[jp:jp + 1, :]
        tie = jnp.where(jb > jp, 1, 0)
        rank = rank + jnp.where(row > val, 1, jnp.where(row == val, tie, 0))
    blk_scr[...] = jnp.where(rank < n_top, jnp.where(val > 0.5 * NEG, 1.0, 0.0), 0.0)

    t_k = i * QB + (lax.broadcasted_iota(I32, (KC, HW), 1) & (LANES - 1))
    k_io = lax.broadcasted_iota(I32, (KC, HW), 0)
    bpc = KC // SLC_BLOCK

    def slc_body(ci, carry):
        m, l, acc = carry
        k0 = pl.multiple_of(ci * KC, KC)
        s = jnp.dot(ks_scr[pl.ds(k0, KC), :], qr_st, preferred_element_type=F32)
        rows = [jnp.broadcast_to(blk_scr[pl.ds(ci * bpc + r, 1), :], (SLC_BLOCK, LANES)) for r in range(bpc)]
        bm = jnp.concatenate(rows, axis=0)
        bm = jnp.concatenate([bm] * H, axis=1)
        mask = jnp.where(k_io + k0 <= t_k, bm, 0.0) > 0.5
        s = jnp.where(mask, s, NEG)
        m_new = jnp.maximum(m, jnp.max(s, axis=0, keepdims=True))
        alpha = jnp.exp(m - m_new)
        e = jnp.where(mask, jnp.exp(s - m_new), 0.0)
        l = alpha * l + jnp.sum(e, axis=0, keepdims=True)
        acc = alpha * acc + jnp.dot(vst_scr[:, pl.ds(k0, KC)], e.astype(BF16), preferred_element_type=F32)
        return m_new, l, acc

    n_chunks = (i * QB + QB + KC - 1) // KC
    init = (jnp.full((1, HW), NEG, F32), jnp.zeros((1, HW), F32), jnp.zeros((HEAD_DIM, HW), F32))
    _, l_s, acc_s = lax.fori_loop(0, n_chunks, slc_body, init)
    o_slc = acc_s / jnp.maximum(l_s, 1e-30)

    span = min(WIN + QB, seq)
    w0 = pl.multiple_of(jnp.maximum(i * QB + QB - span, 0), QB)
    s_w = jnp.dot(kw_scr[pl.ds(w0, span), :], qr_st, preferred_element_type=F32)
    t_w = i * QB + (lax.broadcasted_iota(I32, (span, HW), 1) & (LANES - 1))
    diff = t_w - (lax.broadcasted_iota(I32, (span, HW), 0) + w0)
    e_w, _, den_w = _softmax_cols(s_w, (diff >= 0) & (diff < WIN))
    o_win = jnp.dot(vwt_scr[:, pl.ds(w0, span)], e_w.astype(BF16), preferred_element_type=F32) / den_w

    gt = jax.nn.sigmoid(g_ref[...].T)
    outs = []
    for h in range(H):
        sl = slice(LANES * h, LANES * (h + 1))
        outs.append(gt[3 * h:3 * h + 1, :] * o_cmp[:, sl] + gt[3 * h + 1:3 * h + 2, :] * o_slc[:, sl]
                    + gt[3 * h + 2:3 * h + 3, :] * o_win[:, sl])
    outs.append(jnp.zeros((o_ref.shape[1] - H * HEAD_DIM, LANES), F32))
    o_ref[...] = jnp.concatenate(outs, axis=0).T.astype(BF16)


def _nsa(proj, kvcmp, tabs, ovt, batch, seq):
    nb = seq // QB
    n_slc = seq // SLC_BLOCK
    ncp = kvcmp.shape[1]
    col = lambda name, w: _OFF[name] // w
    qmap = lambda cidx: (lambda b, i: (b * nb + i, cidx))
    tmap = lambda k: (lambda b, i: (k, i, 0))
    kern = functools.partial(_nsa_kernel, n_top=min(SLC_TOPN, n_slc))
    return pl.pallas_call(
        kern,
        grid=(batch, nb),
        in_specs=[pl.BlockSpec((QB, 384), qmap(col("a_q", 384))),
                  pl.BlockSpec((QB, LANES), qmap(col("a_g", LANES))),
                  pl.BlockSpec((QB, LANES), qmap(col("a_ksvs", LANES))),
                  pl.BlockSpec((QB, LANES), qmap(col("a_kwvw", LANES))),
                  pl.BlockSpec((1, ncp, LANES), lambda b, i: (b, 0, 0)),
                  pl.BlockSpec((None, QB, LANES), tmap(0)),
                  pl.BlockSpec((None, QB, LANES), tmap(1)),
                  pl.BlockSpec((None, QB, LANES), tmap(2)),
                  pl.BlockSpec(ovt.shape, lambda b, i: (0, 0))],
        out_specs=pl.BlockSpec((QB, 384), lambda b, i: (b * nb + i, 0)),
        out_shape=jax.ShapeDtypeStruct((batch * seq, 384), BF16),
        scratch_shapes=[pltpu.VMEM((seq, LANES), BF16), pltpu.VMEM((HEAD_DIM, seq), BF16),
                        pltpu.VMEM((seq, LANES), BF16), pltpu.VMEM((HEAD_DIM, seq), BF16),
                        pltpu.VMEM((ncp, LANES), BF16), pltpu.VMEM((HEAD_DIM, ncp), BF16),
                        pltpu.VMEM((n_slc, LANES), F32)],
        compiler_params=_params(2),
        name="nsa",
    )(proj, proj, proj, proj, kvcmp, tabs, tabs, tabs, ovt)


def _dsa_kernel(q_ref, iq_ref, ikw_ref, kv_ref, c_ref, s1_ref, s2_ref, ci_ref, si1_ref, si2_ref, o_ref,
                k_scr, vt_scr, ik_scr, key_scr, *, top, idx_bits):
    i = pl.program_id(1)
    H = DSA_HEADS
    HW = H * LANES
    lane_q = lax.broadcasted_iota(I32, (QB, LANES), 1)

    @pl.when(i == 0)
    def _init():
        k_scr[...] = jnp.zeros_like(k_scr)
        vt_scr[...] = jnp.zeros_like(vt_scr)
        ik_scr[...] = jnp.zeros_like(ik_scr)

    c, s1, s2 = c_ref[...], s1_ref[...], s2_ref[...]
    ci, si1, si2 = ci_ref[...], si1_ref[...], si2_ref[...]
    r0 = pl.multiple_of(i * QB, QB)
    kv = kv_ref[...]
    k_scr[pl.ds(r0, QB), :] = jnp.where(lane_q < HEAD_DIM, _rope(kv, c, s1, s2, ROPE_DIM // 2), 0.0).astype(BF16)
    vt_scr[:, pl.ds(r0, QB)] = kv.T[HEAD_DIM:2 * HEAD_DIM, :].astype(BF16)
    ikw = ikw_ref[...]
    ik_scr[pl.ds(r0, QB), :] = jnp.where(lane_q < IDX_DIM, _rope(ikw, ci, si1, si2, IDX_ROPE_DIM // 2), 0.0).astype(BF16)

    qr = _rope_wide(q_ref[...] * SCALE, c, s1, s2, ROPE_DIM // 2)
    qr_st = _stack_heads_t(qr.T, H, HEAD_DIM).astype(BF16)
    iqr = _rope_wide(iq_ref[...], ci, si1, si2, IDX_ROPE_DIM // 2)
    iq_st = _stack_heads_t(iqr.T, IDX_HEADS, IDX_DIM).astype(BF16)
    wt = ikw.T * ((IDX_DIM ** -0.5) * (IDX_HEADS ** -0.5))
    w_st = jnp.concatenate([wt[IDX_DIM + h:IDX_DIM + h + 1, :] for h in range(IDX_HEADS)], axis=1)

    n_chunks = (i * QB + QB + KC - 1) // KC
    k_io = lax.broadcasted_iota(I32, (KC, LANES), 0)
    tq = i * QB + lax.broadcasted_iota(I32, (KC, LANES), 1)

    def score_body(cidx, _):
        k0 = pl.multiple_of(cidx * KC, KC)
        d = jnp.dot(ik_scr[pl.ds(k0, KC), :], iq_st, preferred_element_type=F32)
        r = jnp.maximum(d, 0.0) * w_st
        sc = r[:, 0:LANES]
        for h in range(1, IDX_HEADS):
            sc = sc + r[:, LANES * h:LANES * (h + 1)]
        sc = jnp.where(sc == 0.0, 0.0, sc)
        bits = lax.bitcast_convert_type(sc, I32)
        key = bits ^ ((bits >> 31) & 0x7FFFFFFF)
        key_scr[pl.ds(k0, KC), :] = jnp.where(k_io + k0 <= tq, key, INT_MIN)
        return 0

    lax.fori_loop(0, n_chunks, score_body, 0)

    def count(pred):
        def body(cidx, acc):
            k0 = pl.multiple_of(cidx * KC, KC)
            return acc + jnp.sum(pred(key_scr[pl.ds(k0, KC), :], k_io + k0), axis=0, keepdims=True)
        return lax.fori_loop(0, n_chunks, body, jnp.zeros((1, LANES), F32))

    def thr_body(bi, cur):
        cand_u = cur | lax.shift_left(jnp.int32(1), 31 - bi)
        cand = cand_u ^ INT_MIN
        cnt = count(lambda key, kpos: jnp.where(key >= cand, 1.0, 0.0))
        return jnp.where(cnt >= top, cand_u, cur)

    thr = lax.fori_loop(0, 32, thr_body, jnp.zeros((1, LANES), I32)) ^ INT_MIN

    need1 = (top - 1) - count(lambda key, kpos: jnp.where(key > thr, 1.0, 0.0))

    def tie_body(bi, cur):
        cand = cur | lax.shift_left(jnp.int32(1), idx_bits - 1 - bi)
        cnt = count(lambda key, kpos: jnp.where(key == thr, jnp.where(kpos < cand, 1.0, 0.0), 0.0))
        return jnp.where(cnt <= need1, cand, cur)

    last = lax.fori_loop(0, idx_bits, tie_body, jnp.zeros((1, LANES), I32))

    half_neg = int(np.float32(0.5 * NEG).view(np.int32))
    key_floor = half_neg ^ 0x7FFFFFFF

    def att_body(cidx, carry):
        m, l, acc = carry
        k0 = pl.multiple_of(cidx * KC, KC)
        s = jnp.dot(k_scr[pl.ds(k0, KC), :], qr_st, preferred_element_type=F32)
        key = key_scr[pl.ds(k0, KC), :]
        kpos = k_io + k0
        sel = jnp.where(key > thr, 1.0, jnp.where(key == thr, jnp.where(kpos <= last, 1.0, 0.0), 0.0))
        sel = jnp.where(kpos <= tq, jnp.where(key > key_floor, sel, 0.0), 0.0)
        mask = jnp.concatenate([sel] * H, axis=1) > 0.5
        s = jnp.where(mask, s, NEG)
        m_new = jnp.maximum(m, jnp.max(s, axis=0, keepdims=True))
        alpha = jnp.exp(m - m_new)
        e = jnp.where(mask, jnp.exp(s - m_new), 0.0)
        l = alpha * l + jnp.sum(e, axis=0, keepdims=True)
        acc = alpha * acc + jnp.dot(vt_scr[:, pl.ds(k0, KC)], e.astype(BF16), preferred_element_type=F32)
        return m_new, l, acc

    init = (jnp.full((1, HW), NEG, F32), jnp.zeros((1, HW), F32), jnp.zeros((HEAD_DIM, HW), F32))
    _, l_a, acc_a = lax.fori_loop(0, n_chunks, att_body, init)
    o = acc_a / jnp.maximum(l_a, 1e-30)
    outs = [o[:, LANES * h:LANES * (h + 1)] for h in range(H)]
    outs.append(jnp.zeros((o_ref.shape[1] - H * HEAD_DIM, LANES), F32))
    o_ref[...] = jnp.concatenate(outs, axis=0).T.astype(BF16)


def _dsa(proj, tabs, tabs_i, batch, seq):
    nb = seq // QB
    col = lambda name, w: _OFF[name] // w
    qmap = lambda cidx: (lambda b, i: (b * nb + i, cidx))
    tmap = lambda k: (lambda b, i: (k, i, 0))
    top = min(DSA_TOPK_MAX, seq // 4)
    kern = functools.partial(_dsa_kernel, top=top, idx_bits=int(seq).bit_length())
    return pl.pallas_call(
        kern,
        grid=(batch, nb),
        in_specs=[pl.BlockSpec((QB, 384), qmap(col("b_q", 384))),
                  pl.BlockSpec((QB, 256), qmap(col("b_iq", 256))),
                  pl.BlockSpec((QB, LANES), qmap(col("b_ikiw", LANES))),
                  pl.BlockSpec((QB, LANES), qmap(col("b_kv", LANES))),
                  pl.BlockSpec((None, QB, LANES), tmap(0)),
                  pl.BlockSpec((None, QB, LANES), tmap(1)),
                  pl.BlockSpec((None, QB, LANES), tmap(2)),
                  pl.BlockSpec((None, QB, LANES), tmap(0)),
                  pl.BlockSpec((None, QB, LANES), tmap(1)),
                  pl.BlockSpec((None, QB, LANES), tmap(2))],
        out_specs=pl.BlockSpec((QB, 384), lambda b, i: (b * nb + i, 0)),
        out_shape=jax.ShapeDtypeStruct((batch * seq, 384), BF16),
        scratch_shapes=[pltpu.VMEM((seq, LANES), BF16), pltpu.VMEM((HEAD_DIM, seq), BF16),
                        pltpu.VMEM((seq, LANES), BF16), pltpu.VMEM((seq, LANES), I32)],
        compiler_params=_params(2),
        name="dsa",
    )(proj, proj, proj, proj, tabs, tabs, tabs, tabs_i, tabs_i, tabs_i)


def _dil_kernel(q_ref, k_ref, v_ref, c_ref, s1_ref, s2_ref, o_ref, k_scr, vt_scr):
    i = pl.program_id(1)
    seq = k_scr.shape[0]
    G, HG = len(DIL_PAIRS), DIL_HEADS_PER_GROUP

    @pl.when(i == 0)
    def _init():
        k_scr[...] = jnp.zeros_like(k_scr)
        vt_scr[...] = jnp.zeros_like(vt_scr)

    c, s1, s2 = c_ref[...], s1_ref[...], s2_ref[...]
    r0 = pl.multiple_of(i * QB, QB)
    k_scr[pl.ds(r0, QB), :] = _rope_wide(k_ref[...], c, s1, s2, ROPE_DIM // 2).astype(BF16)
    vt_scr[:, pl.ds(r0, QB)] = v_ref[...].T.astype(BF16)

    qt = _rope_wide(q_ref[...] * SCALE, c, s1, s2, ROPE_DIM // 2).T
    zero = jnp.zeros((HEAD_DIM, LANES), F32)
    o_heads, lses = [], []
    for g in range(G):
        _, dil = DIL_PAIRS[g]
        reach = (DIL_PAIRS[g][0] // dil) * dil
        span = min(reach + QB, seq)
        k0 = pl.multiple_of(jnp.maximum(i * QB + QB - span, 0), QB)
        qa = qt[2 * g * HEAD_DIM:(2 * g + 1) * HEAD_DIM, :]
        qb = qt[(2 * g + 1) * HEAD_DIM:(2 * g + 2) * HEAD_DIM, :]
        q2 = jnp.concatenate([jnp.concatenate([qa, zero], axis=0), jnp.concatenate([zero, qb], axis=0)], axis=1)
        s = jnp.dot(k_scr[pl.ds(k0, span), LANES * g:LANES * (g + 1)], q2.astype(BF16),
                    preferred_element_type=F32)
        tq = i * QB + (lax.broadcasted_iota(I32, (span, 2 * LANES), 1) & (LANES - 1))
        diff = tq - (lax.broadcasted_iota(I32, (span, 2 * LANES), 0) + k0)
        mask = (diff >= 0) & (diff <= reach) & ((diff & (dil - 1)) == 0)
        e, m, den = _softmax_cols(s, mask)
        lse = m + jnp.log(den)
        p = (e / den).astype(BF16)
        for hg in range(HG):
            h = g * HG + hg
            vt = vt_scr[HEAD_DIM * h:HEAD_DIM * (h + 1), pl.ds(k0, span)]
            o_heads.append(jnp.dot(vt, p[:, LANES * hg:LANES * (hg + 1)], preferred_element_type=F32))
            lses.append(lse[:, LANES * hg:LANES * (hg + 1)])
    outs = [None] * (G * HG)
    for hg in range(HG):
        ls = [lses[g * HG + hg] for g in range(G)]
        mx = functools.reduce(jnp.maximum, ls)
        ex = [jnp.exp(x - mx) for x in ls]
        tot = functools.reduce(lambda a, b: a + b, ex)
        for g in range(G):
            outs[g * HG + hg] = (ex[g] / tot) * o_heads[g * HG + hg]
    o_ref[...] = jnp.concatenate(outs, axis=0).T.astype(BF16)


def _dil(proj, tabs, batch, seq):
    nb = seq // QB
    w = DIL_HEADS * HEAD_DIM
    col = lambda name: _OFF[name] // w
    qmap = lambda cidx: (lambda b, i: (b * nb + i, cidx))
    tmap = lambda k: (lambda b, i: (k, i, 0))
    return pl.pallas_call(
        _dil_kernel,
        grid=(batch, nb),
        in_specs=[pl.BlockSpec((QB, w), qmap(col("c_q"))),
                  pl.BlockSpec((QB, w), qmap(col("c_k"))),
                  pl.BlockSpec((QB, w), qmap(col("c_v"))),
                  pl.BlockSpec((None, QB, LANES), tmap(0)),
                  pl.BlockSpec((None, QB, LANES), tmap(1)),
                  pl.BlockSpec((None, QB, LANES), tmap(2))],
        out_specs=pl.BlockSpec((QB, w), lambda b, i: (b * nb + i, 0)),
        out_shape=jax.ShapeDtypeStruct((batch * seq, w), BF16),
        scratch_shapes=[pltpu.VMEM((seq, w), BF16), pltpu.VMEM((w, seq), BF16)],
        compiler_params=_params(2),
        name="dilated",
    )(proj, proj, proj, tabs, tabs, tabs)


def _overlap_t(seq):
    n_cmp_pad = seq // CMP_STRIDE
    n_slc = seq // SLC_BLOCK
    c_start = np.arange(n_cmp_pad) * CMP_STRIDE
    s_start = np.arange(n_slc) * SLC_BLOCK
    ov = ((c_start[None, :] < s_start[:, None] + SLC_BLOCK) & (c_start[None, :] + CMP_BLOCK > s_start[:, None]))
    return jnp.asarray(ov.astype(np.float32), dtype=BF16)


def _layer(x2, batch, seq, norm1_g, w_in, cmp_pos, cmp_w1, cmp_w2, w_out, norm2_g, w_up, conv_w, conv_b, w_down,
           final_g, final_norm, tabs, tabs_i, ovt):
    n = batch * seq
    proj = _inproj(x2, norm1_g, _pad_w_in(w_in).astype(BF16))

    o_kc = _OFF["a_kcvc"]
    rows = seq // CMP_STRIDE
    kc_r = proj[:, o_kc:o_kc + HEAD_DIM].reshape(batch, rows, CMP_STRIDE * HEAD_DIM)
    vc_r = proj[:, o_kc + HEAD_DIM:o_kc + 2 * HEAD_DIM].reshape(batch, rows, CMP_STRIDE * HEAD_DIM)
    pos = cmp_pos.reshape(2, 2, CMP_STRIDE * HEAD_DIM)
    w1 = cmp_w1.reshape(2, 2, CMP_STRIDE * HEAD_DIM, CMP_HIDDEN).astype(BF16)
    zpad = jnp.zeros((CMP_HIDDEN, HEAD_DIM), cmp_w2.dtype)
    w2 = jnp.stack([jnp.concatenate([cmp_w2[0], zpad], axis=1),
                    jnp.concatenate([zpad, cmp_w2[1]], axis=1)]).astype(BF16)
    kvcmp = _nsa_compress(kc_r, vc_r, pos, w1, w2)

    o_a = _nsa(proj, kvcmp, tabs, ovt, batch, seq)
    o_b = _dsa(proj, tabs, tabs_i, batch, seq)
    o_c = _dil(proj, tabs, batch, seq)

    hp = NSA_HEADS * HEAD_DIM
    zrow = jnp.zeros((384 - hp, D_MODEL), w_out.dtype)
    w_out_pad = jnp.concatenate([w_out[0:hp], zrow, w_out[hp:2 * hp], zrow, w_out[2 * hp:]], axis=0).astype(BF16)
    x2 = _outproj(o_a, o_b, o_c, w_out_pad, x2)

    act = _ffn_up(x2, norm2_g, w_up.astype(BF16), conv_w, conv_b, seq)
    return _ffn_down(act, w_down.astype(BF16), x2, final_g, final_norm)


def kernel(x, norm1_g, w_in, cmp_pos, cmp_w1, cmp_w2, w_out, norm2_g, w_up, conv_w, conv_b, w_down, final_g):
    batch, seq, d = x.shape
    depth = w_in.shape[0]
    tabs = _rope_lane_tables(seq, HEAD_DIM, ROPE_DIM)
    tabs_i = _rope_lane_tables(seq, IDX_DIM, IDX_ROPE_DIM)
    ovt = _overlap_t(seq)
    x2 = x.reshape(batch * seq, d)
    for li in range(depth):
        x2 = _layer(x2, batch, seq, norm1_g[li], w_in[li], cmp_pos[li], cmp_w1[li], cmp_w2[li], w_out[li],
                    norm2_g[li], w_up[li], conv_w[li], conv_b[li], w_down[li], final_g, li == depth - 1,
                    tabs, tabs_i, ovt)
    return x2.reshape(batch, seq, d)
```

```python
import functools

import numpy as np
import jax
import jax.numpy as jnp
from jax import lax
from jax.experimental import pallas as pl
from jax.experimental.pallas import tpu as pltpu

F32 = jnp.float32
BF16 = jnp.bfloat16
I32 = jnp.int32

D_MODEL = 1024
HEAD_DIM = 64
ROPE_DIM = HEAD_DIM // 4
ROPE_THETA = 500000.0
NORM_EPS = 1e-6
SCALE = HEAD_DIM ** -0.5
NEG = -1e30
FORCE = 1e9
NSA_HEADS = 5
CMP_BLOCK = 32
CMP_STRIDE = 16
CMP_HIDDEN = 128
SLC_BLOCK = 64
SLC_TOPN = 16
WIN = 512
DSA_HEADS = 5
IDX_HEADS = 8
IDX_DIM = 32
IDX_ROPE_DIM = IDX_DIM // 4
DSA_TOPK_MAX = 256
DIL_PAIRS = ((128, 1), (512, 4), (2048, 16))
DIL_HEADS_PER_GROUP = 2
DIL_HEADS = len(DIL_PAIRS) * DIL_HEADS_PER_GROUP
D_FF = 2816
CONV_WIDTH = 3

LANES = 128
QB = 128
KC = 512
ROW_TILE = 512
HALO = 16
VMEM_LIMIT = 48 * 1024 * 1024
INT_MIN = -2 ** 31

_A0 = 0
_B0 = 719
_C0 = 1463
_SEGS = (
    ("c_q", _C0, 384, 384), ("c_k", _C0 + 384, 384, 384), ("c_v", _C0 + 768, 384, 384),
    ("a_q", _A0, 320, 384), ("b_q", _B0, 320, 384),
    ("a_kcvc", _A0 + 320, 128, 128),
    ("b_iq", _B0 + 448, 256, 256),
    ("a_ksvs", _A0 + 448, 128, 128), ("a_kwvw", _A0 + 576, 128, 128),
    ("a_g", _A0 + 704, 15, 128),
    ("b_kv", _B0 + 320, 128, 128),
    ("b_ikiw", _B0 + 704, 40, 128),
)
P_COLS = sum(s[3] for s in _SEGS)


def _seg_offsets():
    offs, o = {}, 0
    for name, _, _, pw in _SEGS:
        offs[name] = o
        o += pw
    return offs


_OFF = _seg_offsets()


def _pad_w_in(w):
    parts = []
    for _, src, wdt, pw in _SEGS:
        parts.append(w[:, src:src + wdt])
        if pw > wdt:
            parts.append(jnp.zeros((w.shape[0], pw - wdt), w.dtype))
    return jnp.concatenate(parts, axis=1)


def _rope_lane_tables(L, head_dim, rot_dim):
    half = rot_dim // 2
    inv = 1.0 / (ROPE_THETA ** (np.arange(0, rot_dim, 2, dtype=np.float32) / np.float32(rot_dim)))
    ang = np.arange(L, dtype=np.float32)[:, None] * inv[None, :]
    cos, sin = np.cos(ang).astype(np.float32), np.sin(ang).astype(np.float32)
    d = np.arange(LANES) % head_dim
    lo, hi = d < half, (d >= half) & (d < rot_dim)
    c = np.ones((L, LANES), np.float32)
    s1 = np.zeros((L, LANES), np.float32)
    s2 = np.zeros((L, LANES), np.float32)
    c[:, lo] = cos[:, d[lo]]
    c[:, hi] = cos[:, d[hi] - half]
    s1[:, lo] = -sin[:, d[lo]]
    s2[:, hi] = sin[:, d[hi] - half]
    return jnp.asarray(np.stack([c, s1, s2]))


def _rope(x, c, s1, s2, half):
    xp = pltpu.roll(x, LANES - half, 1)
    xm = pltpu.roll(x, half, 1)
    return x * c + xp * s1 + xm * s2


def _rope_wide(x, c, s1, s2, half):
    n = x.shape[1] // LANES
    return jnp.concatenate([_rope(x[:, LANES * s:LANES * (s + 1)], c, s1, s2, half) for s in range(n)], axis=1)


def _softmax_cols(s, mask):
    s = jnp.where(mask, s, NEG)
    m = jnp.max(s, axis=0, keepdims=True)
    e = jnp.where(mask, jnp.exp(s - m), 0.0)
    den = jnp.maximum(jnp.sum(e, axis=0, keepdims=True), 1e-30)
    return e, m, den


def _params(n_grid):
    return pltpu.CompilerParams(dimension_semantics=("arbitrary",) * n_grid, vmem_limit_bytes=VMEM_LIMIT)


def _rmsnorm_rows(x, g):
    return x * lax.rsqrt(jnp.mean(x * x, axis=-1, keepdims=True) + NORM_EPS) * g


def _inproj_kernel(x_ref, g_ref, w_ref, o_ref):
    hn = _rmsnorm_rows(x_ref[...], g_ref[...]).astype(BF16)
    n = w_ref.shape[1]
    for c0 in range(0, n, 512):
        c1 = min(c0 + 512, n)
        o_ref[:, c0:c1] = jnp.dot(hn, w_ref[:, c0:c1], preferred_element_type=F32)


def _inproj(x2, g, w_bf):
    n, d = x2.shape
    pc = w_bf.shape[1]
    return pl.pallas_call(
        _inproj_kernel,
        grid=(n // ROW_TILE,),
        in_specs=[pl.BlockSpec((ROW_TILE, d), lambda i: (i, 0)),
                  pl.BlockSpec((1, d), lambda i: (0, 0)),
                  pl.BlockSpec((d, pc), lambda i: (0, 0))],
        out_specs=pl.BlockSpec((ROW_TILE, pc), lambda i: (i, 0)),
        out_shape=jax.ShapeDtypeStruct((n, pc), F32),
        compiler_params=_params(1),
        name="inproj",
    )(x2, g.reshape(1, d), w_bf)


def _outproj_kernel(oa_ref, ob_ref, oc_ref, w_ref, x_ref, o_ref):
    mix = jnp.concatenate([oa_ref[...], ob_ref[...], oc_ref[...]], axis=1)
    o_ref[...] = x_ref[...] + jnp.dot(mix, w_ref[...], preferred_element_type=F32)


def _outproj(oa, ob, oc, w_bf, x2):
    n, d = x2.shape
    k = w_bf.shape[0]
    mw = oa.shape[1]
    return pl.pallas_call(
        _outproj_kernel,
        grid=(n // ROW_TILE,),
        in_specs=[pl.BlockSpec((ROW_TILE, mw), lambda i: (i, 0)),
                  pl.BlockSpec((ROW_TILE, mw), lambda i: (i, 0)),
                  pl.BlockSpec((ROW_TILE, mw), lambda i: (i, 0)),
                  pl.BlockSpec((k, d), lambda i: (0, 0)),
                  pl.BlockSpec((ROW_TILE, d), lambda i: (i, 0))],
        out_specs=pl.BlockSpec((ROW_TILE, d), lambda i: (i, 0)),
        out_shape=jax.ShapeDtypeStruct((n, d), F32),
        compiler_params=_params(1),
        name="outproj",
    )(oa, ob, oc, w_bf, x2)


def _ffn_up_kernel(x_ref, xh_ref, g_ref, wa_ref, wu_ref, cw_ref, cb_ref, o_ref, hn_scr, hh_scr, a_scr, *, tiles_per_seq):
    i, j = pl.program_id(0), pl.program_id(1)
    tm = x_ref.shape[0]

    @pl.when(j == 0)
    def _norm():
        hn_scr[...] = _rmsnorm_rows(x_ref[...], g_ref[...]).astype(BF16)
        keep = jnp.where(i % tiles_per_seq == 0, 0.0, 1.0)
        hh_scr[...] = (_rmsnorm_rows(xh_ref[...], g_ref[...]) * keep).astype(BF16)

    hn = hn_scr[...]
    a = jnp.dot(hn, wa_ref[...], preferred_element_type=F32)
    u = jnp.dot(hn, wu_ref[...], preferred_element_type=F32)
    a_scr[0:HALO, :] = jnp.dot(hh_scr[...], wa_ref[...], preferred_element_type=F32)
    a_scr[HALO:HALO + tm, :] = a
    a1 = a_scr[pl.ds(HALO - 1, tm), :]
    a2 = a_scr[pl.ds(HALO - 2, tm), :]
    cw = cw_ref[...]
    conv = cw[0:1, :] * a2 + cw[1:2, :] * a1 + cw[2:3, :] * a + cb_ref[...]
    o_ref[...] = (conv * jax.nn.sigmoid(conv) * u).astype(BF16)


def _ffn_up(x2, g, w_up_bf, conv_w, conv_b, seq_len):
    n, d = x2.shape
    tn = D_FF // 2
    nj = D_FF // tn
    hb = ROW_TILE // HALO
    kern = functools.partial(_ffn_up_kernel, tiles_per_seq=seq_len // ROW_TILE)
    return pl.pallas_call(
        kern,
        grid=(n // ROW_TILE, nj),
        in_specs=[pl.BlockSpec((ROW_TILE, d), lambda i, j: (i, 0)),
                  pl.BlockSpec((HALO, d), lambda i, j: (jnp.maximum(i * hb - 1, 0), 0)),
                  pl.BlockSpec((1, d), lambda i, j: (0, 0)),
                  pl.BlockSpec((d, tn), lambda i, j: (0, j)),
                  pl.BlockSpec((d, tn), lambda i, j: (0, j + nj)),
                  pl.BlockSpec((CONV_WIDTH, tn), lambda i, j: (0, j)),
                  pl.BlockSpec((1, tn), lambda i, j: (0, j))],
        out_specs=pl.BlockSpec((ROW_TILE, tn), lambda i, j: (i, j)),
        out_shape=jax.ShapeDtypeStruct((n, D_FF), BF16),
        scratch_shapes=[pltpu.VMEM((ROW_TILE, d), BF16), pltpu.VMEM((HALO, d), BF16),
                        pltpu.VMEM((ROW_TILE + HALO, tn), F32)],
        compiler_params=_params(2),
        name="ffn_up",
    )(x2, x2, g.reshape(1, d), w_up_bf, w_up_bf, conv_w, conv_b.reshape(1, D_FF))


def _ffn_down_kernel(a_ref, w_ref, x_ref, g_ref, o_ref, *, final_norm):
    y = x_ref[...] + jnp.dot(a_ref[...], w_ref[...], preferred_element_type=F32)
    if final_norm:
        y = _rmsnorm_rows(y, g_ref[...])
    o_ref[...] = y


def _ffn_down(act, w_bf, x2, final_g, final_norm):
    n, d = x2.shape
    k = act.shape[1]
    return pl.pallas_call(
        functools.partial(_ffn_down_kernel, final_norm=final_norm),
        grid=(n // ROW_TILE,),
        in_specs=[pl.BlockSpec((ROW_TILE, k), lambda i: (i, 0)),
                  pl.BlockSpec((k, d), lambda i: (0, 0)),
                  pl.BlockSpec((ROW_TILE, d), lambda i: (i, 0)),
                  pl.BlockSpec((1, d), lambda i: (0, 0))],
        out_specs=pl.BlockSpec((ROW_TILE, d), lambda i: (i, 0)),
        out_shape=jax.ShapeDtypeStruct((n, d), F32),
        compiler_params=_params(1),
        name="ffn_down",
    )(act, w_bf, x2, final_g.reshape(1, d))


def _cmp_kernel(kc_ref, vc_ref, pos_ref, w1_ref, w2_ref, o_ref):
    out = None
    for c, src in enumerate((kc_ref, vc_ref)):
        r = src[0]
        lo = jnp.dot((r + pos_ref[c, 0:1, :]).astype(BF16), w1_ref[c, 0], preferred_element_type=F32)
        hi = jnp.dot((r + pos_ref[c, 1:2, :]).astype(BF16), w1_ref[c, 1], preferred_element_type=F32)
        nrows = hi.shape[0]
        hid = jax.nn.gelu(lo + pltpu.roll(hi, nrows - 1, 0))
        y = jnp.dot(hid.astype(BF16), w2_ref[c], preferred_element_type=F32)
        out = y if out is None else out + y
    o_ref[0] = out


def _nsa_compress(kc_r, vc_r, pos, w1_bf, w2_pad_bf):
    b, nr, wd = kc_r.shape
    return pl.pallas_call(
        _cmp_kernel,
        grid=(b,),
        in_specs=[pl.BlockSpec((1, nr, wd), lambda i: (i, 0, 0)),
                  pl.BlockSpec((1, nr, wd), lambda i: (i, 0, 0)),
                  pl.BlockSpec(pos.shape, lambda i: (0, 0, 0)),
                  pl.BlockSpec(w1_bf.shape, lambda i: (0, 0, 0, 0)),
                  pl.BlockSpec(w2_pad_bf.shape, lambda i: (0, 0, 0))],
        out_specs=pl.BlockSpec((1, nr, LANES), lambda i: (i, 0, 0)),
        out_shape=jax.ShapeDtypeStruct((b, nr, LANES), F32),
        compiler_params=_params(1),
        name="nsa_compress",
    )(kc_r, vc_r, pos, w1_bf, w2_pad_bf)


def _stack_heads_t(x_t, n_heads, hd):
    z = jnp.zeros((LANES - hd, LANES), F32)
    return jnp.concatenate(
        [jnp.concatenate([x_t[hd * h:hd * (h + 1), :], z], axis=0) for h in range(n_heads)], axis=1)


def _nsa_kernel(q_ref, g_ref, ksvs_ref, kwvw_ref, kvc_ref, c_ref, s1_ref, s2_ref, ovt_ref, o_ref,
                ks_scr, vst_scr, kw_scr, vwt_scr, kc_scr, vct_scr, blk_scr, *, n_top):
    i = pl.program_id(1)
    H = NSA_HEADS
    HW = H * LANES
    seq = ks_scr.shape[0]
    lane_q = lax.broadcasted_iota(I32, (QB, LANES), 1)

    @pl.when(i == 0)
    def _init():
        ks_scr[...] = jnp.zeros_like(ks_scr)
        kw_scr[...] = jnp.zeros_like(kw_scr)
        vst_scr[...] = jnp.zeros_like(vst_scr)
        vwt_scr[...] = jnp.zeros_like(vwt_scr)
        kvc = kvc_ref[0]
        lane_c = lax.broadcasted_iota(I32, kvc.shape, 1)
        kc_scr[...] = jnp.where(lane_c < HEAD_DIM, kvc, 0.0).astype(BF16)
        vct_scr[...] = kvc.T[HEAD_DIM:2 * HEAD_DIM, :].astype(BF16)

    c, s1, s2 = c_ref[...], s1_ref[...], s2_ref[...]
    r0 = pl.multiple_of(i * QB, QB)
    for src, kdst, vdst in ((ksvs_ref, ks_scr, vst_scr), (kwvw_ref, kw_scr, vwt_scr)):
        kv = src[...]
        kr = _rope(kv, c, s1, s2, ROPE_DIM // 2)
        kdst[pl.ds(r0, QB), :] = jnp.where(lane_q < HEAD_DIM, kr, 0.0).astype(BF16)
        vdst[:, pl.ds(r0, QB)] = kv.T[HEAD_DIM:2 * HEAD_DIM, :].astype(BF16)

    q = q_ref[...] * SCALE
    qr = _rope_wide(q, c, s1, s2, ROPE_DIM // 2)
    q_st = _stack_heads_t(q.T, H, HEAD_DIM).astype(BF16)
    qr_st = _stack_heads_t(qr.T, H, HEAD_DIM).astype(BF16)

    ncp = kc_scr.shape[0]
    s_c = jnp.dot(kc_scr[...], q_st, preferred_element_type=F32)
    n_idx = lax.broadcasted_iota(I32, (ncp, HW), 0)
    t_c = i * QB + (lax.broadcasted_iota(I32, (ncp, HW), 1) & (LANES - 1))
    e_c, _, den_c = _softmax_cols(s_c, n_idx * CMP_STRIDE + (CMP_BLOCK - 1) <= t_c)
    p_c = e_c / den_c
    o_cmp = jnp.dot(vct_scr[...], p_c.astype(BF16), preferred_element_type=F32)
    psum = p_c[:, 0:LANES]
    for h in range(1, H):
        psum = psum + p_c[:, LANES * h:LANES * (h + 1)]

    hi = psum.astype(BF16)
    r1 = psum - hi.astype(F32)
    mid = r1.astype(BF16)
    lo = (r1 - mid.astype(F32)).astype(BF16)
    ovt = ovt_ref[...]
    imp = (jnp.dot(ovt, hi, preferred_element_type=F32) + jnp.dot(ovt, mid, preferred_element_type=F32)
           + jnp.dot(ovt, lo, preferred_element_type=F32))
    n_slc = imp.shape[0]
    jb = lax.broadcasted_iota(I32, (n_slc, LANES), 0)
    tq = i * QB + lax.broadcasted_iota(I32, (n_slc, LANES), 1)
    cur = tq >> (SLC_BLOCK.bit_length() - 1)
    forced = (jb == 0) | (jb == cur) | (jb == cur - 1)
    val = jnp.where(forced, FORCE, jnp.where(jb <= cur, imp, NEG))
    rank = jnp.zeros((n_slc, LANES), I32)
    for jp in range(n_slc):
        row = val[jp:jp + 1, :]
        tie = jnp.where(jb > jp, 1, 0)
        rank = rank + jnp.where(row > val, 1, jnp.where(row == val, tie, 0))
    blk_scr[...] = jnp.where(rank < n_top, jnp.where(val > 0.5 * NEG, 1.0, 0.0), 0.0)

    t_k = i * QB + (lax.broadcasted_iota(I32, (KC, HW), 1) & (LANES - 1))
    k_io = lax.broadcasted_iota(I32, (KC, HW), 0)
    bpc = KC // SLC_BLOCK

    def slc_body(ci, carry):
        m, l, acc = carry
        k0 = pl.multiple_of(ci * KC, KC)
        s = jnp.dot(ks_scr[pl.ds(k0, KC), :], qr_st, preferred_element_type=F32)
        rows = [jnp.broadcast_to(blk_scr[pl.ds(ci * bpc + r, 1), :], (SLC_BLOCK, LANES)) for r in range(bpc)]
        bm = jnp.concatenate(rows, axis=0)
        bm = jnp.concatenate([bm] * H, axis=1)
        mask = jnp.where(k_io + k0 <= t_k, bm, 0.0) > 0.5
        s = jnp.where(mask, s, NEG)
        m_new = jnp.maximum(m, jnp.max(s, axis=0, keepdims=True))
        alpha = jnp.exp(m - m_new)
        e = jnp.where(mask, jnp.exp(s - m_new), 0.0)
        l = alpha * l + jnp.sum(e, axis=0, keepdims=True)
        acc = alpha * acc + jnp.dot(vst_scr[:, pl.ds(k0, KC)], e.astype(BF16), preferred_element_type=F32)
        return m_new, l, acc

    n_chunks = (i * QB + QB + KC - 1) // KC
    init = (jnp.full((1, HW), NEG, F32), jnp.zeros((1, HW), F32), jnp.zeros((HEAD_DIM, HW), F32))
    _, l_s, acc_s = lax.fori_loop(0, n_chunks, slc_body, init)
    o_slc = acc_s / jnp.maximum(l_s, 1e-30)

    span = min(WIN + QB, seq)
    w0 = pl.multiple_of(jnp.maximum(i * QB + QB - span, 0), QB)
    s_w = jnp.dot(kw_scr[pl.ds(w0, span), :], qr_st, preferred_element_type=F32)
    t_w = i * QB + (lax.broadcasted_iota(I32, (span, HW), 1) & (LANES - 1))
    diff = t_w - (lax.broadcasted_iota(I32, (span, HW), 0) + w0)
    e_w, _, den_w = _softmax_cols(s_w, (diff >= 0) & (diff < WIN))
    o_win = jnp.dot(vwt_scr[:, pl.ds(w0, span)], e_w.astype(BF16), preferred_element_type=F32) / den_w

    gt = jax.nn.sigmoid(g_ref[...].T)
    outs = []
    for h in range(H):
        sl = slice(LANES * h, LANES * (h + 1))
        outs.append(gt[3 * h:3 * h + 1, :] * o_cmp[:, sl] + gt[3 * h + 1:3 * h + 2, :] * o_slc[:, sl]
                    + gt[3 * h + 2:3 * h + 3, :] * o_win[:, sl])
    outs.append(jnp.zeros((o_ref.shape[1] - H * HEAD_DIM, LANES), F32))
    o_ref[...] = jnp.concatenate(outs, axis=0).T.astype(BF16)


def _nsa(proj, kvcmp, tabs, ovt, batch, seq):
    nb = seq // QB
    n_slc = seq // SLC_BLOCK
    ncp = kvcmp.shape[1]
    col = lambda name, w: _OFF[name] // w
    qmap = lambda cidx: (lambda b, i: (b * nb + i, cidx))
    tmap = lambda k: (lambda b, i: (k, i, 0))
    kern = functools.partial(_nsa_kernel, n_top=min(SLC_TOPN, n_slc))
    return pl.pallas_call(
        kern,
        grid=(batch, nb),
        in_specs=[pl.BlockSpec((QB, 384), qmap(col("a_q", 384))),
                  pl.BlockSpec((QB, LANES), qmap(col("a_g", LANES))),
                  pl.BlockSpec((QB, LANES), qmap(col("a_ksvs", LANES))),
                  pl.BlockSpec((QB, LANES), qmap(col("a_kwvw", LANES))),
                  pl.BlockSpec((1, ncp, LANES), lambda b, i: (b, 0, 0)),
                  pl.BlockSpec((None, QB, LANES), tmap(0)),
                  pl.BlockSpec((None, QB, LANES), tmap(1)),
                  pl.BlockSpec((None, QB, LANES), tmap(2)),
                  pl.BlockSpec(ovt.shape, lambda b, i: (0, 0))],
        out_specs=pl.BlockSpec((QB, 384), lambda b, i: (b * nb + i, 0)),
        out_shape=jax.ShapeDtypeStruct((batch * seq, 384), BF16),
        scratch_shapes=[pltpu.VMEM((seq, LANES), BF16), pltpu.VMEM((HEAD_DIM, seq), BF16),
                        pltpu.VMEM((seq, LANES), BF16), pltpu.VMEM((HEAD_DIM, seq), BF16),
                        pltpu.VMEM((ncp, LANES), BF16), pltpu.VMEM((HEAD_DIM, ncp), BF16),
                        pltpu.VMEM((n_slc, LANES), F32)],
        compiler_params=_params(2),
        name="nsa",
    )(proj, proj, proj, proj, kvcmp, tabs, tabs, tabs, ovt)


SEARCH_BISECT_FROM = 24
SEARCH_MAX_PASSES = 64


def _tree_rows(x, op):
    parts = [x[r:r + 8, :] for r in range(0, x.shape[0], 8)]
    while len(parts) > 1:
        nxt = [op(parts[a], parts[a + 1]) for a in range(0, len(parts) - 1, 2)]
        if len(parts) % 2:
            nxt.append(parts[-1])
        parts = nxt
    return parts[0]


def _f2key(v):
    bits = lax.bitcast_convert_type(v, I32)
    return bits ^ ((bits >> 31) & 0x7FFFFFFF)


def _key2f(k):
    return lax.bitcast_convert_type(k ^ ((k >> 31) & 0x7FFFFFFF), F32)


def _dsa_kernel(q_ref, iq_ref, ikw_ref, kv_ref, c_ref, s1_ref, s2_ref, ci_ref, si1_ref, si2_ref, o_ref,
                k_scr, vt_scr, ik_scr, key_scr, *, top, idx_bits):
    i = pl.program_id(1)
    H = DSA_HEADS
    HW = H * LANES
    lane_q = lax.broadcasted_iota(I32, (QB, LANES), 1)

    @pl.when(i == 0)
    def _init():
        k_scr[...] = jnp.zeros_like(k_scr)
        vt_scr[...] = jnp.zeros_like(vt_scr)
        ik_scr[...] = jnp.zeros_like(ik_scr)

    c, s1, s2 = c_ref[...], s1_ref[...], s2_ref[...]
    ci, si1, si2 = ci_ref[...], si1_ref[...], si2_ref[...]
    r0 = pl.multiple_of(i * QB, QB)
    kv = kv_ref[...]
    k_scr[pl.ds(r0, QB), :] = jnp.where(lane_q < HEAD_DIM, _rope(kv, c, s1, s2, ROPE_DIM // 2), 0.0).astype(BF16)
    vt_scr[:, pl.ds(r0, QB)] = kv.T[HEAD_DIM:2 * HEAD_DIM, :].astype(BF16)
    ikw = ikw_ref[...]
    ik_scr[pl.ds(r0, QB), :] = jnp.where(lane_q < IDX_DIM, _rope(ikw, ci, si1, si2, IDX_ROPE_DIM // 2), 0.0).astype(BF16)

    qr = _rope_wide(q_ref[...] * SCALE, c, s1, s2, ROPE_DIM // 2)
    qr_st = _stack_heads_t(qr.T, H, HEAD_DIM).astype(BF16)
    iqr = _rope_wide(iq_ref[...], ci, si1, si2, IDX_ROPE_DIM // 2)
    iq_st = _stack_heads_t(iqr.T, IDX_HEADS, IDX_DIM).astype(BF16)
    wt = ikw.T * ((IDX_DIM ** -0.5) * (IDX_HEADS ** -0.5))
    w_st = jnp.concatenate([wt[IDX_DIM + h:IDX_DIM + h + 1, :] for h in range(IDX_HEADS)], axis=1)

    n_chunks = (i * QB + QB + KC - 1) // KC
    k_io = lax.broadcasted_iota(I32, (KC, LANES), 0)
    tq = i * QB + lax.broadcasted_iota(I32, (KC, LANES), 1)

    def score_body(cidx, carry):
        mx, mn = carry
        k0 = pl.multiple_of(cidx * KC, KC)
        d = jnp.dot(ik_scr[pl.ds(k0, KC), :], iq_st, preferred_element_type=F32)
        r = jnp.maximum(d, 0.0) * w_st
        sc = r[:, 0:LANES]
        for h in range(1, IDX_HEADS):
            sc = sc + r[:, LANES * h:LANES * (h + 1)]
        sc = jnp.where(sc == 0.0, 0.0, sc)
        causal = k_io + k0 <= tq
        key_scr[pl.ds(k0, KC), :] = jnp.where(causal, _f2key(sc), INT_MIN)
        mx = jnp.maximum(mx, _tree_rows(jnp.where(causal, sc, -jnp.inf), jnp.maximum))
        mn = jnp.minimum(mn, _tree_rows(jnp.where(causal, sc, jnp.inf), jnp.minimum))
        return mx, mn

    mx8, mn8 = lax.fori_loop(0, n_chunks, score_body,
                             (jnp.full((8, LANES), -jnp.inf, F32), jnp.full((8, LANES), jnp.inf, F32)))

    def count(pred):
        def body(cidx, acc):
            k0 = pl.multiple_of(cidx * KC, KC)
            return acc + _tree_rows(pred(key_scr[pl.ds(k0, KC), :], k_io + k0), jnp.add)
        acc8 = lax.fori_loop(0, n_chunks, body, jnp.zeros((8, LANES), F32))
        return jnp.sum(acc8, axis=0, keepdims=True)

    topf = float(top)
    lo0 = _f2key(jnp.min(mn8, axis=0, keepdims=True))
    hi0 = _f2key(jnp.max(mx8, axis=0, keepdims=True)) + 1
    n_causal = (i * QB + 1 + lax.broadcasted_iota(I32, (1, LANES), 1)).astype(F32)
    log_top = float(np.log(top))

    def pending(lo, hi, c_lo):
        return (jnp.max(jnp.where(c_lo > topf, jnp.where(hi > lo + 1, 1.0, 0.0), 0.0)) > 0.5).astype(I32)

    def search_cond(c):
        return jnp.logical_and(c[0] < SEARCH_MAX_PASSES, c[1] > 0)

    def search_body(c):
        it, _, lo, hi, c_lo, c_hi, kept = c
        v_lo, v_hi = _key2f(lo), _key2f(hi)
        l_lo = jnp.log(c_lo)
        frac = (l_lo - log_top) / (l_lo - jnp.log(jnp.maximum(c_hi, 0.5)))
        cand = _f2key(v_lo + frac * (v_hi - v_lo))
        runs = jnp.abs(kept)
        cand = jnp.where(runs >= 2, _f2key(0.5 * v_lo + 0.5 * v_hi), cand)
        key_mid = (lo >> 1) + (hi >> 1) + (lo & hi & 1)
        cand = jnp.where(runs >= 4, key_mid, cand)
        cand = jnp.where(it >= SEARCH_BISECT_FROM, key_mid, cand)
        cand = jnp.where(it < 2, it, cand)
        cand = jnp.minimum(jnp.maximum(cand, lo + 1), hi - 1)
        cnt = count(lambda key, kpos: jnp.where(key >= cand, 1.0, 0.0))
        active = jnp.where(c_lo > topf, jnp.where(hi > lo + 1, 1.0, 0.0), 0.0) > 0.5
        up = jnp.where(active, jnp.where(cnt >= topf, 1.0, 0.0), 0.0) > 0.5
        dn = jnp.where(active, jnp.where(cnt >= topf, 0.0, 1.0), 0.0) > 0.5
        lo, c_lo = jnp.where(up, cand, lo), jnp.where(up, cnt, c_lo)
        hi, c_hi = jnp.where(dn, cand, hi), jnp.where(dn, cnt, c_hi)
        kept = jnp.where(up, jnp.where(kept < 0, kept - 1, -1), jnp.where(dn, jnp.where(kept > 0, kept + 1, 1), kept))
        return it + 1, pending(lo, hi, c_lo), lo, hi, c_lo, c_hi, kept

    zero_f = jnp.zeros((1, LANES), F32)
    _, _, thr, _, c_lo, c_hi, _ = lax.while_loop(
        search_cond, search_body,
        (jnp.int32(0), pending(lo0, hi0, n_causal), lo0, hi0, n_causal, zero_f, jnp.zeros((1, LANES), I32)))

    tied = c_lo > topf
    need1 = (topf - 1.0) - c_hi
    all_pos = jnp.full((1, LANES), 2 ** idx_bits - 1, I32)

    def tie_search():
        def tie_body(bi, cur):
            cand = cur | lax.shift_left(jnp.int32(1), idx_bits - 1 - bi)
            cnt = count(lambda key, kpos: jnp.where(key == thr, jnp.where(kpos < cand, 1.0, 0.0), 0.0))
            return jnp.where(cnt <= need1, cand, cur)
        return lax.fori_loop(0, idx_bits, tie_body, jnp.zeros((1, LANES), I32))

    any_tied = jnp.max(jnp.where(tied, 1.0, 0.0)) > 0.5
    last = jnp.where(tied, lax.cond(any_tied, tie_search, lambda: all_pos), all_pos)

    half_neg = int(np.float32(0.5 * NEG).view(np.int32))
    key_floor = half_neg ^ 0x7FFFFFFF

    def att_body(cidx, carry):
        m, l, acc = carry
        k0 = pl.multiple_of(cidx * KC, KC)
        s = jnp.dot(k_scr[pl.ds(k0, KC), :], qr_st, preferred_element_type=F32)
        key = key_scr[pl.ds(k0, KC), :]
        kpos = k_io + k0
        sel = jnp.where(key > thr, 1.0, jnp.where(key == thr, jnp.where(kpos <= last, 1.0, 0.0), 0.0))
        sel = jnp.where(kpos <= tq, jnp.where(key > key_floor, sel, 0.0), 0.0)
        mask = jnp.concatenate([sel] * H, axis=1) > 0.5
        s = jnp.where(mask, s, NEG)
        m_new = jnp.maximum(m, jnp.max(s, axis=0, keepdims=True))
        alpha = jnp.exp(m - m_new)
        e = jnp.where(mask, jnp.exp(s - m_new), 0.0)
        l = alpha * l + jnp.sum(e, axis=0, keepdims=True)
        acc = alpha * acc + jnp.dot(vt_scr[:, pl.ds(k0, KC)], e.astype(BF16), preferred_element_type=F32)
        return m_new, l, acc

    init = (jnp.full((1, HW), NEG, F32), jnp.zeros((1, HW), F32), jnp.zeros((HEAD_DIM, HW), F32))
    _, l_a, acc_a = lax.fori_loop(0, n_chunks, att_body, init)
    o = acc_a / jnp.maximum(l_a, 1e-30)
    outs = [o[:, LANES * h:LANES * (h + 1)] for h in range(H)]
    outs.append(jnp.zeros((o_ref.shape[1] - H * HEAD_DIM, LANES), F32))
    o_ref[...] = jnp.concatenate(outs, axis=0).T.astype(BF16)


def _dsa(proj, tabs, tabs_i, batch, seq):
    nb = seq // QB
    col = lambda name, w: _OFF[name] // w
    qmap = lambda cidx: (lambda b, i: (b * nb + i, cidx))
    tmap = lambda k: (lambda b, i: (k, i, 0))
    top = min(DSA_TOPK_MAX, seq // 4)
    kern = functools.partial(_dsa_kernel, top=top, idx_bits=int(seq).bit_length())
    return pl.pallas_call(
        kern,
        grid=(batch, nb),
        in_specs=[pl.BlockSpec((QB, 384), qmap(col("b_q", 384))),
                  pl.BlockSpec((QB, 256), qmap(col("b_iq", 256))),
                  pl.BlockSpec((QB, LANES), qmap(col("b_ikiw", LANES))),
                  pl.BlockSpec((QB, LANES), qmap(col("b_kv", LANES))),
                  pl.BlockSpec((None, QB, LANES), tmap(0)),
                  pl.BlockSpec((None, QB, LANES), tmap(1)),
                  pl.BlockSpec((None, QB, LANES), tmap(2)),
                  pl.BlockSpec((None, QB, LANES), tmap(0)),
                  pl.BlockSpec((None, QB, LANES), tmap(1)),
                  pl.BlockSpec((None, QB, LANES), tmap(2))],
        out_specs=pl.BlockSpec((QB, 384), lambda b, i: (b * nb + i, 0)),
        out_shape=jax.ShapeDtypeStruct((batch * seq, 384), BF16),
        scratch_shapes=[pltpu.VMEM((seq, LANES), BF16), pltpu.VMEM((HEAD_DIM, seq), BF16),
                        pltpu.VMEM((seq, LANES), BF16), pltpu.VMEM((seq, LANES), I32)],
        compiler_params=_params(2),
        name="dsa",
    )(proj, proj, proj, proj, tabs, tabs, tabs, tabs_i, tabs_i, tabs_i)


def _dil_kernel(q_ref, k_ref, v_ref, c_ref, s1_ref, s2_ref, o_ref, k_scr, vt_scr):
    i = pl.program_id(1)
    seq = k_scr.shape[0]
    G, HG = len(DIL_PAIRS), DIL_HEADS_PER_GROUP

    @pl.when(i == 0)
    def _init():
        k_scr[...] = jnp.zeros_like(k_scr)
        vt_scr[...] = jnp.zeros_like(vt_scr)

    c, s1, s2 = c_ref[...], s1_ref[...], s2_ref[...]
    r0 = pl.multiple_of(i * QB, QB)
    k_scr[pl.ds(r0, QB), :] = _rope_wide(k_ref[...], c, s1, s2, ROPE_DIM // 2).astype(BF16)
    vt_scr[:, pl.ds(r0, QB)] = v_ref[...].T.astype(BF16)

    qt = _rope_wide(q_ref[...] * SCALE, c, s1, s2, ROPE_DIM // 2).T
    zero = jnp.zeros((HEAD_DIM, LANES), F32)
    o_heads, lses = [], []
    for g in range(G):
        _, dil = DIL_PAIRS[g]
        reach = (DIL_PAIRS[g][0] // dil) * dil
        span = min(reach + QB, seq)
        k0 = pl.multiple_of(jnp.maximum(i * QB + QB - span, 0), QB)
        qa = qt[2 * g * HEAD_DIM:(2 * g + 1) * HEAD_DIM, :]
        qb = qt[(2 * g + 1) * HEAD_DIM:(2 * g + 2) * HEAD_DIM, :]
        q2 = jnp.concatenate([jnp.concatenate([qa, zero], axis=0), jnp.concatenate([zero, qb], axis=0)], axis=1)
        s = jnp.dot(k_scr[pl.ds(k0, span), LANES * g:LANES * (g + 1)], q2.astype(BF16),
                    preferred_element_type=F32)
        tq = i * QB + (lax.broadcasted_iota(I32, (span, 2 * LANES), 1) & (LANES - 1))
        diff = tq - (lax.broadcasted_iota(I32, (span, 2 * LANES), 0) + k0)
        mask = (diff >= 0) & (diff <= reach) & ((diff & (dil - 1)) == 0)
        e, m, den = _softmax_cols(s, mask)
        lse = m + jnp.log(den)
        p = (e / den).astype(BF16)
        for hg in range(HG):
            h = g * HG + hg
            vt = vt_scr[HEAD_DIM * h:HEAD_DIM * (h + 1), pl.ds(k0, span)]
            o_heads.append(jnp.dot(vt, p[:, LANES * hg:LANES * (hg + 1)], preferred_element_type=F32))
            lses.append(lse[:, LANES * hg:LANES * (hg + 1)])
    outs = [None] * (G * HG)
    for hg in range(HG):
        ls = [lses[g * HG + hg] for g in range(G)]
        mx = functools.reduce(jnp.maximum, ls)
        ex = [jnp.exp(x - mx) for x in ls]
        tot = functools.reduce(lambda a, b: a + b, ex)
        for g in range(G):
            outs[g * HG + hg] = (ex[g] / tot) * o_heads[g * HG + hg]
    o_ref[...] = jnp.concatenate(outs, axis=0).T.astype(BF16)


def _dil(proj, tabs, batch, seq):
    nb = seq // QB
    w = DIL_HEADS * HEAD_DIM
    col = lambda name: _OFF[name] // w
    qmap = lambda cidx: (lambda b, i: (b * nb + i, cidx))
    tmap = lambda k: (lambda b, i: (k, i, 0))
    return pl.pallas_call(
        _dil_kernel,
        grid=(batch, nb),
        in_specs=[pl.BlockSpec((QB, w), qmap(col("c_q"))),
                  pl.BlockSpec((QB, w), qmap(col("c_k"))),
                  pl.BlockSpec((QB, w), qmap(col("c_v"))),
                  pl.BlockSpec((None, QB, LANES), tmap(0)),
                  pl.BlockSpec((None, QB, LANES), tmap(1)),
                  pl.BlockSpec((None, QB, LANES), tmap(2))],
        out_specs=pl.BlockSpec((QB, w), lambda b, i: (b * nb + i, 0)),
        out_shape=jax.ShapeDtypeStruct((batch * seq, w), BF16),
        scratch_shapes=[pltpu.VMEM((seq, w), BF16), pltpu.VMEM((w, seq), BF16)],
        compiler_params=_params(2),
        name="dilated",
    )(proj, proj, proj, tabs, tabs, tabs)


def _overlap_t(seq):
    n_cmp_pad = seq // CMP_STRIDE
    n_slc = seq // SLC_BLOCK
    c_start = np.arange(n_cmp_pad) * CMP_STRIDE
    s_start = np.arange(n_slc) * SLC_BLOCK
    ov = ((c_start[None, :] < s_start[:, None] + SLC_BLOCK) & (c_start[None, :] + CMP_BLOCK > s_start[:, None]))
    return jnp.asarray(ov.astype(np.float32), dtype=BF16)


def _layer(x2, batch, seq, norm1_g, w_in, cmp_pos, cmp_w1, cmp_w2, w_out, norm2_g, w_up, conv_w, conv_b, w_down,
           final_g, final_norm, tabs, tabs_i, ovt):
    n = batch * seq
    proj = _inproj(x2, norm1_g, _pad_w_in(w_in).astype(BF16))

    o_kc = _OFF["a_kcvc"]
    rows = seq // CMP_STRIDE
    kc_r = proj[:, o_kc:o_kc + HEAD_DIM].reshape(batch, rows, CMP_STRIDE * HEAD_DIM)
    vc_r = proj[:, o_kc + HEAD_DIM:o_kc + 2 * HEAD_DIM].reshape(batch, rows, CMP_STRIDE * HEAD_DIM)
    pos = cmp_pos.reshape(2, 2, CMP_STRIDE * HEAD_DIM)
    w1 = cmp_w1.reshape(2, 2, CMP_STRIDE * HEAD_DIM, CMP_HIDDEN).astype(BF16)
    zpad = jnp.zeros((CMP_HIDDEN, HEAD_DIM), cmp_w2.dtype)
    w2 = jnp.stack([jnp.concatenate([cmp_w2[0], zpad], axis=1),
                    jnp.concatenate([zpad, cmp_w2[1]], axis=1)]).astype(BF16)
    kvcmp = _nsa_compress(kc_r, vc_r, pos, w1, w2)

    o_a = _nsa(proj, kvcmp, tabs, ovt, batch, seq)
    o_b = _dsa(proj, tabs, tabs_i, batch, seq)
    o_c = _dil(proj, tabs, batch, seq)

    hp = NSA_HEADS * HEAD_DIM
    zrow = jnp.zeros((384 - hp, D_MODEL), w_out.dtype)
    w_out_pad = jnp.concatenate([w_out[0:hp], zrow, w_out[hp:2 * hp], zrow, w_out[2 * hp:]], axis=0).astype(BF16)
    x2 = _outproj(o_a, o_b, o_c, w_out_pad, x2)

    act = _ffn_up(x2, norm2_g, w_up.astype(BF16), conv_w, conv_b, seq)
    return _ffn_down(act, w_down.astype(BF16), x2, final_g, final_norm)


def kernel(x, norm1_g, w_in, cmp_pos, cmp_w1, cmp_w2, w_out, norm2_g, w_up, conv_w, conv_b, w_down, final_g):
    batch, seq, d = x.shape
    depth = w_in.shape[0]
    tabs = _rope_lane_tables(seq, HEAD_DIM, ROPE_DIM)
    tabs_i = _rope_lane_tables(seq, IDX_DIM, IDX_ROPE_DIM)
    ovt = _overlap_t(seq)
    x2 = x.reshape(batch * seq, d)
    for li in range(depth):
        x2 = _layer(x2, batch, seq, norm1_g[li], w_in[li], cmp_pos[li], cmp_w1[li], cmp_w2[li], w_out[li],
                    norm2_g[li], w_up[li], conv_w[li], conv_b[li], w_down[li], final_g, li == depth - 1,
                    tabs, tabs_i, ovt)
    return x2.reshape(batch, seq, d)
```

```python
import functools

import numpy as np
import jax
import jax.numpy as jnp
from jax import lax
from jax.experimental import pallas as pl
from jax.experimental.pallas import tpu as pltpu

F32 = jnp.float32
BF16 = jnp.bfloat16
I32 = jnp.int32

D_MODEL = 1024
HEAD_DIM = 64
ROPE_DIM = HEAD_DIM // 4
ROPE_THETA = 500000.0
NORM_EPS = 1e-6
SCALE = HEAD_DIM ** -0.5
LOG2E = 1.4426950408889634
NEG = -1e30
FORCE = 1e9
NSA_HEADS = 5
CMP_BLOCK = 32
CMP_STRIDE = 16
CMP_HIDDEN = 128
SLC_BLOCK = 64
SLC_TOPN = 16
WIN = 512
DSA_HEADS = 5
IDX_HEADS = 8
IDX_DIM = 32
IDX_ROPE_DIM = IDX_DIM // 4
DSA_TOPK_MAX = 256
DIL_PAIRS = ((128, 1), (512, 4), (2048, 16))
DIL_HEADS_PER_GROUP = 2
DIL_HEADS = len(DIL_PAIRS) * DIL_HEADS_PER_GROUP
D_FF = 2816
CONV_WIDTH = 3

LANES = 128
QB = 128
KC = 512
ROW_TILE = 512
HALO = 16
VMEM_LIMIT = 48 * 1024 * 1024
INT_MIN = -2 ** 31

_A0 = 0
_B0 = 719
_C0 = 1463
_SEGS = (
    ("c_q", _C0, 384, 384), ("c_k", _C0 + 384, 384, 384), ("c_v", _C0 + 768, 384, 384),
    ("a_q", _A0, 320, 384), ("b_q", _B0, 320, 384),
    ("a_kcvc", _A0 + 320, 128, 128),
    ("b_iq", _B0 + 448, 256, 256),
    ("a_ksvs", _A0 + 448, 128, 128), ("a_kwvw", _A0 + 576, 128, 128),
    ("a_g", _A0 + 704, 15, 128),
    ("b_kv", _B0 + 320, 128, 128),
    ("b_ikiw", _B0 + 704, 40, 128),
)
P_COLS = sum(s[3] for s in _SEGS)


def _seg_offsets():
    offs, o = {}, 0
    for name, _, _, pw in _SEGS:
        offs[name] = o
        o += pw
    return offs


_OFF = _seg_offsets()


def _pad_w_in(w):
    parts = []
    for _, src, wdt, pw in _SEGS:
        parts.append(w[:, src:src + wdt])
        if pw > wdt:
            parts.append(jnp.zeros((w.shape[0], pw - wdt), w.dtype))
    return jnp.concatenate(parts, axis=1)


def _rope_lane_tables(L, head_dim, rot_dim):
    half = rot_dim // 2
    inv = 1.0 / (ROPE_THETA ** (np.arange(0, rot_dim, 2, dtype=np.float32) / np.float32(rot_dim)))
    ang = np.arange(L, dtype=np.float32)[:, None] * inv[None, :]
    cos, sin = np.cos(ang).astype(np.float32), np.sin(ang).astype(np.float32)
    d = np.arange(LANES) % head_dim
    lo, hi = d < half, (d >= half) & (d < rot_dim)
    c = np.ones((L, LANES), np.float32)
    s1 = np.zeros((L, LANES), np.float32)
    s2 = np.zeros((L, LANES), np.float32)
    c[:, lo] = cos[:, d[lo]]
    c[:, hi] = cos[:, d[hi] - half]
    s1[:, lo] = -sin[:, d[lo]]
    s2[:, hi] = sin[:, d[hi] - half]
    return jnp.asarray(np.stack([c, s1, s2]))


def _rope(x, c, s1, s2, half):
    xp = pltpu.roll(x, LANES - half, 1)
    xm = pltpu.roll(x, half, 1)
    return x * c + xp * s1 + xm * s2


def _rope_wide(x, c, s1, s2, half):
    n = x.shape[1] // LANES
    return jnp.concatenate([_rope(x[:, LANES * s:LANES * (s + 1)], c, s1, s2, half) for s in range(n)], axis=1)


def _softmax2_cols(s):
    m = jnp.max(s, axis=0, keepdims=True)
    e = jnp.exp2(s - m)
    return e, m, jnp.sum(e, axis=0, keepdims=True)


def _bias_table(ok):
    return jnp.asarray(np.where(ok, 0.0, NEG).astype(np.float32))


def _cmp_bias_table(seq):
    shift = (QB // CMP_STRIDE) * (seq // QB - 1)
    u = np.arange(seq // CMP_STRIDE + shift)[:, None]
    lane = np.arange(LANES)[None, :]
    return _bias_table(CMP_STRIDE * (u - shift) + CMP_BLOCK - 1 <= lane)


def _win_bias_table(seq):
    span = min(WIN + QB, seq)
    cmax = span - QB
    u = np.arange(span + cmax)[:, None]
    diff = cmax + np.arange(LANES)[None, :] - u
    return _bias_table((diff >= 0) & (diff < WIN))


def _causal_bias_table():
    u = np.arange(2 * KC)[:, None]
    return _bias_table(u <= KC + np.arange(LANES)[None, :])


def _params(n_grid, vmem=VMEM_LIMIT):
    return pltpu.CompilerParams(dimension_semantics=("arbitrary",) * n_grid, vmem_limit_bytes=vmem)


def _rmsnorm_rows(x, g):
    return x * lax.rsqrt(jnp.mean(x * x, axis=-1, keepdims=True) + NORM_EPS) * g


def _inproj_kernel(x_ref, g_ref, w_ref, o_ref):
    hn = _rmsnorm_rows(x_ref[...], g_ref[...]).astype(BF16)
    n = w_ref.shape[1]
    for c0 in range(0, n, 512):
        c1 = min(c0 + 512, n)
        o_ref[:, c0:c1] = jnp.dot(hn, w_ref[:, c0:c1], preferred_element_type=F32)


def _inproj(x2, g, w_bf):
    n, d = x2.shape
    pc = w_bf.shape[1]
    return pl.pallas_call(
        _inproj_kernel,
        grid=(n // ROW_TILE,),
        in_specs=[pl.BlockSpec((ROW_TILE, d), lambda i: (i, 0)),
                  pl.BlockSpec((1, d), lambda i: (0, 0)),
                  pl.BlockSpec((d, pc), lambda i: (0, 0))],
        out_specs=pl.BlockSpec((ROW_TILE, pc), lambda i: (i, 0)),
        out_shape=jax.ShapeDtypeStruct((n, pc), F32),
        compiler_params=_params(1),
        name="inproj",
    )(x2, g.reshape(1, d), w_bf)


def _outproj_kernel(oa_ref, ob_ref, oc_ref, w_ref, x_ref, o_ref):
    mix = jnp.concatenate([oa_ref[...], ob_ref[...], oc_ref[...]], axis=1)
    o_ref[...] = x_ref[...] + jnp.dot(mix, w_ref[...], preferred_element_type=F32)


def _outproj(oa, ob, oc, w_bf, x2):
    n, d = x2.shape
    k = w_bf.shape[0]
    mw = oa.shape[1]
    return pl.pallas_call(
        _outproj_kernel,
        grid=(n // ROW_TILE,),
        in_specs=[pl.BlockSpec((ROW_TILE, mw), lambda i: (i, 0)),
                  pl.BlockSpec((ROW_TILE, mw), lambda i: (i, 0)),
                  pl.BlockSpec((ROW_TILE, mw), lambda i: (i, 0)),
                  pl.BlockSpec((k, d), lambda i: (0, 0)),
                  pl.BlockSpec((ROW_TILE, d), lambda i: (i, 0))],
        out_specs=pl.BlockSpec((ROW_TILE, d), lambda i: (i, 0)),
        out_shape=jax.ShapeDtypeStruct((n, d), F32),
        compiler_params=_params(1),
        name="outproj",
    )(oa, ob, oc, w_bf, x2)


def _ffn_up_kernel(x_ref, xh_ref, g_ref, wa_ref, wu_ref, cw_ref, cb_ref, o_ref, hn_scr, hh_scr, a_scr, *, tiles_per_seq):
    i, j = pl.program_id(0), pl.program_id(1)
    tm = x_ref.shape[0]

    @pl.when(j == 0)
    def _norm():
        hn_scr[...] = _rmsnorm_rows(x_ref[...], g_ref[...]).astype(BF16)
        keep = jnp.where(i % tiles_per_seq == 0, 0.0, 1.0)
        hh_scr[...] = (_rmsnorm_rows(xh_ref[...], g_ref[...]) * keep).astype(BF16)

    hn = hn_scr[...]
    a = jnp.dot(hn, wa_ref[...], preferred_element_type=F32)
    u = jnp.dot(hn, wu_ref[...], preferred_element_type=F32)
    a_scr[0:HALO, :] = jnp.dot(hh_scr[...], wa_ref[...], preferred_element_type=F32)
    a_scr[HALO:HALO + tm, :] = a
    a1 = a_scr[pl.ds(HALO - 1, tm), :]
    a2 = a_scr[pl.ds(HALO - 2, tm), :]
    cw = cw_ref[...]
    conv = cw[0:1, :] * a2 + cw[1:2, :] * a1 + cw[2:3, :] * a + cb_ref[...]
    o_ref[...] = (conv * jax.nn.sigmoid(conv) * u).astype(BF16)


def _ffn_up(x2, g, w_up_bf, conv_w, conv_b, seq_len):
    n, d = x2.shape
    tn = D_FF // 2
    nj = D_FF // tn
    hb = ROW_TILE // HALO
    kern = functools.partial(_ffn_up_kernel, tiles_per_seq=seq_len // ROW_TILE)
    return pl.pallas_call(
        kern,
        grid=(n // ROW_TILE, nj),
        in_specs=[pl.BlockSpec((ROW_TILE, d), lambda i, j: (i, 0)),
                  pl.BlockSpec((HALO, d), lambda i, j: (jnp.maximum(i * hb - 1, 0), 0)),
                  pl.BlockSpec((1, d), lambda i, j: (0, 0)),
                  pl.BlockSpec((d, tn), lambda i, j: (0, j)),
                  pl.BlockSpec((d, tn), lambda i, j: (0, j + nj)),
                  pl.BlockSpec((CONV_WIDTH, tn), lambda i, j: (0, j)),
                  pl.BlockSpec((1, tn), lambda i, j: (0, j))],
        out_specs=pl.BlockSpec((ROW_TILE, tn), lambda i, j: (i, j)),
        out_shape=jax.ShapeDtypeStruct((n, D_FF), BF16),
        scratch_shapes=[pltpu.VMEM((ROW_TILE, d), BF16), pltpu.VMEM((HALO, d), BF16),
                        pltpu.VMEM((ROW_TILE + HALO, tn), F32)],
        compiler_params=_params(2),
        name="ffn_up",
    )(x2, x2, g.reshape(1, d), w_up_bf, w_up_bf, conv_w, conv_b.reshape(1, D_FF))


def _ffn_down_kernel(a_ref, w_ref, x_ref, g_ref, o_ref, *, final_norm):
    y = x_ref[...] + jnp.dot(a_ref[...], w_ref[...], preferred_element_type=F32)
    if final_norm:
        y = _rmsnorm_rows(y, g_ref[...])
    o_ref[...] = y


def _ffn_down(act, w_bf, x2, final_g, final_norm):
    n, d = x2.shape
    k = act.shape[1]
    return pl.pallas_call(
        functools.partial(_ffn_down_kernel, final_norm=final_norm),
        grid=(n // ROW_TILE,),
        in_specs=[pl.BlockSpec((ROW_TILE, k), lambda i: (i, 0)),
                  pl.BlockSpec((k, d), lambda i: (0, 0)),
                  pl.BlockSpec((ROW_TILE, d), lambda i: (i, 0)),
                  pl.BlockSpec((1, d), lambda i: (0, 0))],
        out_specs=pl.BlockSpec((ROW_TILE, d), lambda i: (i, 0)),
        out_shape=jax.ShapeDtypeStruct((n, d), F32),
        compiler_params=_params(1),
        name="ffn_down",
    )(act, w_bf, x2, final_g.reshape(1, d))


def _cmp_kernel(kc_ref, vc_ref, pos_ref, w1_ref, w2_ref, o_ref):
    out = None
    for c, src in enumerate((kc_ref, vc_ref)):
        r = src[0]
        lo = jnp.dot((r + pos_ref[c, 0:1, :]).astype(BF16), w1_ref[c, 0], preferred_element_type=F32)
        hi = jnp.dot((r + pos_ref[c, 1:2, :]).astype(BF16), w1_ref[c, 1], preferred_element_type=F32)
        nrows = hi.shape[0]
        hid = jax.nn.gelu(lo + pltpu.roll(hi, nrows - 1, 0))
        y = jnp.dot(hid.astype(BF16), w2_ref[c], preferred_element_type=F32)
        out = y if out is None else out + y
    o_ref[0] = out


def _nsa_compress(kc_r, vc_r, pos, w1_bf, w2_pad_bf):
    b, nr, wd = kc_r.shape
    return pl.pallas_call(
        _cmp_kernel,
        grid=(b,),
        in_specs=[pl.BlockSpec((1, nr, wd), lambda i: (i, 0, 0)),
                  pl.BlockSpec((1, nr, wd), lambda i: (i, 0, 0)),
                  pl.BlockSpec(pos.shape, lambda i: (0, 0, 0)),
                  pl.BlockSpec(w1_bf.shape, lambda i: (0, 0, 0, 0)),
                  pl.BlockSpec(w2_pad_bf.shape, lambda i: (0, 0, 0))],
        out_specs=pl.BlockSpec((1, nr, LANES), lambda i: (i, 0, 0)),
        out_shape=jax.ShapeDtypeStruct((b, nr, LANES), F32),
        compiler_params=_params(1),
        name="nsa_compress",
    )(kc_r, vc_r, pos, w1_bf, w2_pad_bf)


def _stack_heads_t(x_t, n_heads, hd):
    z = jnp.zeros((LANES - hd, LANES), F32)
    return jnp.concatenate(
        [jnp.concatenate([x_t[hd * h:hd * (h + 1), :], z], axis=0) for h in range(n_heads)], axis=1)


def _nsa_kernel(q_ref, g_ref, ksvs_ref, kwvw_ref, kvc_ref, c_ref, s1_ref, s2_ref, ovt_ref, cmpb_ref, winb_ref,
                caub_ref, o_ref, ks_scr, vst_scr, kw_scr, vwt_scr, kc_scr, vct_scr, blk_scr, *, n_top):
    i = pl.program_id(1)
    H = NSA_HEADS
    HW = H * LANES
    seq = ks_scr.shape[0]
    lane_q = lax.broadcasted_iota(I32, (QB, LANES), 1)

    @pl.when(i == 0)
    def _init():
        ks_scr[...] = jnp.zeros_like(ks_scr)
        kw_scr[...] = jnp.zeros_like(kw_scr)
        vst_scr[...] = jnp.zeros_like(vst_scr)
        vwt_scr[...] = jnp.zeros_like(vwt_scr)
        kvc = kvc_ref[0]
        lane_c = lax.broadcasted_iota(I32, kvc.shape, 1)
        kc_scr[...] = jnp.where(lane_c < HEAD_DIM, kvc, 0.0).astype(BF16)
        vct_scr[...] = kvc.T[HEAD_DIM:2 * HEAD_DIM, :].astype(BF16)

    c, s1, s2 = c_ref[...], s1_ref[...], s2_ref[...]
    r0 = pl.multiple_of(i * QB, QB)
    for src, kdst, vdst in ((ksvs_ref, ks_scr, vst_scr), (kwvw_ref, kw_scr, vwt_scr)):
        kv = src[...]
        kr = _rope(kv, c, s1, s2, ROPE_DIM // 2)
        kdst[pl.ds(r0, QB), :] = jnp.where(lane_q < HEAD_DIM, kr, 0.0).astype(BF16)
        vdst[:, pl.ds(r0, QB)] = kv.T[HEAD_DIM:2 * HEAD_DIM, :].astype(BF16)

    q = q_ref[...] * (SCALE * LOG2E)
    qr = _rope_wide(q, c, s1, s2, ROPE_DIM // 2)
    q_st = _stack_heads_t(q.T, H, HEAD_DIM).astype(BF16)
    qr_st = _stack_heads_t(qr.T, H, HEAD_DIM).astype(BF16)
    tile_h = lambda b: jnp.concatenate([b] * H, axis=1)

    ncp = kc_scr.shape[0]
    cshift = cmpb_ref.shape[0] - ncp
    cb = cmpb_ref[pl.ds(pl.multiple_of(cshift - (QB // CMP_STRIDE) * i, 8), ncp), :]
    s_c = jnp.dot(kc_scr[...], q_st, preferred_element_type=F32) + tile_h(cb)
    e_c, _, den_c = _softmax2_cols(s_c)
    t_row = i * QB + (lax.broadcasted_iota(I32, (1, HW), 1) & (LANES - 1))
    r_c = jnp.where(t_row >= CMP_BLOCK - 1, 1.0 / den_c, 0.0)
    o_cmp = jnp.dot(vct_scr[...], e_c.astype(BF16), preferred_element_type=F32) * r_c
    psum = e_c[:, 0:LANES] * r_c[:, 0:LANES]
    for h in range(1, H):
        psum = psum + e_c[:, LANES * h:LANES * (h + 1)] * r_c[:, LANES * h:LANES * (h + 1)]

    hi = psum.astype(BF16)
    r1 = psum - hi.astype(F32)
    mid = r1.astype(BF16)
    lo = (r1 - mid.astype(F32)).astype(BF16)
    ovt = ovt_ref[...]
    imp = (jnp.dot(ovt, hi, preferred_element_type=F32) + jnp.dot(ovt, mid, preferred_element_type=F32)
           + jnp.dot(ovt, lo, preferred_element_type=F32))
    n_slc = imp.shape[0]
    jb = lax.broadcasted_iota(I32, (n_slc, LANES), 0)
    tq = i * QB + lax.broadcasted_iota(I32, (n_slc, LANES), 1)
    cur = tq >> (SLC_BLOCK.bit_length() - 1)
    forced = (jb == 0) | (jb == cur) | (jb == cur - 1)
    val = jnp.where(forced, FORCE, jnp.where(jb <= cur, imp, NEG))
    rank = jnp.zeros((n_slc, LANES), I32)
    for jp in range(n_slc):
        row = val[jp:jp + 1, :]
        tie = jnp.where(jb > jp, 1, 0)
        rank = rank + jnp.where(row > val, 1, jnp.where(row == val, tie, 0))
    blk_scr[...] = jnp.where(rank < n_top, jnp.where(val > 0.5 * NEG, 0.0, NEG), NEG)

    bpc = KC // SLC_BLOCK

    def slc_body(ci, carry):
        m, l, acc = carry
        k0 = pl.multiple_of(ci * KC, KC)
        rows = [jnp.broadcast_to(blk_scr[pl.ds(ci * bpc + r, 1), :], (SLC_BLOCK, LANES)) for r in range(bpc)]
        ahead = jnp.minimum(i * QB - k0, KC)
        bias = jnp.concatenate(rows, axis=0) + caub_ref[pl.ds(pl.multiple_of(KC - ahead, QB), KC), :]
        s = jnp.dot(ks_scr[pl.ds(k0, KC), :], qr_st, preferred_element_type=F32) + tile_h(bias)
        m_new = jnp.maximum(m, jnp.max(s, axis=0, keepdims=True))
        alpha = jnp.exp2(m - m_new)
        e = jnp.exp2(s - m_new)
        l = alpha * l + jnp.sum(e, axis=0, keepdims=True)
        acc = alpha * acc + jnp.dot(vst_scr[:, pl.ds(k0, KC)], e.astype(BF16), preferred_element_type=F32)
        return m_new, l, acc

    n_chunks = (i * QB + QB + KC - 1) // KC
    init = (jnp.full((1, HW), NEG, F32), jnp.zeros((1, HW), F32), jnp.zeros((HEAD_DIM, HW), F32))
    m_s, l_s, acc_s = lax.fori_loop(0, n_chunks, slc_body, init)
    o_slc = jnp.where(m_s > 0.5 * NEG, acc_s / l_s, 0.0)

    span = min(WIN + QB, seq)
    w0 = pl.multiple_of(jnp.maximum(i * QB + QB - span, 0), QB)
    wb = winb_ref[pl.ds(pl.multiple_of(span - QB - (i * QB - w0), QB), span), :]
    s_w = jnp.dot(kw_scr[pl.ds(w0, span), :], qr_st, preferred_element_type=F32) + tile_h(wb)
    e_w, _, den_w = _softmax2_cols(s_w)
    o_win = jnp.dot(vwt_scr[:, pl.ds(w0, span)], e_w.astype(BF16), preferred_element_type=F32) * (1.0 / den_w)

    gt = jax.nn.sigmoid(g_ref[...].T)
    outs = []
    for h in range(H):
        sl = slice(LANES * h, LANES * (h + 1))
        outs.append(gt[3 * h:3 * h + 1, :] * o_cmp[:, sl] + gt[3 * h + 1:3 * h + 2, :] * o_slc[:, sl]
                    + gt[3 * h + 2:3 * h + 3, :] * o_win[:, sl])
    outs.append(jnp.zeros((o_ref.shape[1] - H * HEAD_DIM, LANES), F32))
    o_ref[...] = jnp.concatenate(outs, axis=0).T.astype(BF16)


def _nsa(proj, kvcmp, tabs, ovt, batch, seq):
    nb = seq // QB
    n_slc = seq // SLC_BLOCK
    ncp = kvcmp.shape[1]
    col = lambda name, w: _OFF[name] // w
    qmap = lambda cidx: (lambda b, i: (b * nb + i, cidx))
    tmap = lambda k: (lambda b, i: (k, i, 0))
    kern = functools.partial(_nsa_kernel, n_top=min(SLC_TOPN, n_slc))
    cmpb, winb, caub = _cmp_bias_table(seq), _win_bias_table(seq), _causal_bias_table()
    whole = lambda a: pl.BlockSpec(a.shape, lambda b, i: (0, 0))
    return pl.pallas_call(
        kern,
        grid=(batch, nb),
        in_specs=[pl.BlockSpec((QB, 384), qmap(col("a_q", 384))),
                  pl.BlockSpec((QB, LANES), qmap(col("a_g", LANES))),
                  pl.BlockSpec((QB, LANES), qmap(col("a_ksvs", LANES))),
                  pl.BlockSpec((QB, LANES), qmap(col("a_kwvw", LANES))),
                  pl.BlockSpec((1, ncp, LANES), lambda b, i: (b, 0, 0)),
                  pl.BlockSpec((None, QB, LANES), tmap(0)),
                  pl.BlockSpec((None, QB, LANES), tmap(1)),
                  pl.BlockSpec((None, QB, LANES), tmap(2)),
                  whole(ovt), whole(cmpb), whole(winb), whole(caub)],
        out_specs=pl.BlockSpec((QB, 384), lambda b, i: (b * nb + i, 0)),
        out_shape=jax.ShapeDtypeStruct((batch * seq, 384), BF16),
        scratch_shapes=[pltpu.VMEM((seq, LANES), BF16), pltpu.VMEM((HEAD_DIM, seq), BF16),
                        pltpu.VMEM((seq, LANES), BF16), pltpu.VMEM((HEAD_DIM, seq), BF16),
                        pltpu.VMEM((ncp, LANES), BF16), pltpu.VMEM((HEAD_DIM, ncp), BF16),
                        pltpu.VMEM((n_slc, LANES), F32)],
        compiler_params=_params(2),
        name="nsa",
    )(proj, proj, proj, proj, kvcmp, tabs, tabs, tabs, ovt, cmpb, winb, caub)


SEARCH_BISECT_FROM = 24
SEARCH_MAX_PASSES = 64


def _tree_rows(x, op):
    parts = [x[r:r + 8, :] for r in range(0, x.shape[0], 8)]
    while len(parts) > 1:
        nxt = [op(parts[a], parts[a + 1]) for a in range(0, len(parts) - 1, 2)]
        if len(parts) % 2:
            nxt.append(parts[-1])
        parts = nxt
    return parts[0]


def _f2key(v):
    bits = lax.bitcast_convert_type(v, I32)
    return bits ^ ((bits >> 31) & 0x7FFFFFFF)


def _key2f(k):
    return lax.bitcast_convert_type(k ^ ((k >> 31) & 0x7FFFFFFF), F32)


def _dsa_kernel(q_ref, iq_ref, ikw_ref, kv_ref, c_ref, s1_ref, s2_ref, ci_ref, si1_ref, si2_ref, o_ref,
                k_scr, vt_scr, ik_scr, key_scr, *, top, idx_bits):
    i = pl.program_id(1)
    H = DSA_HEADS
    HW = H * LANES
    lane_q = lax.broadcasted_iota(I32, (QB, LANES), 1)

    @pl.when(i == 0)
    def _init():
        k_scr[...] = jnp.zeros_like(k_scr)
        vt_scr[...] = jnp.zeros_like(vt_scr)
        ik_scr[...] = jnp.zeros_like(ik_scr)

    c, s1, s2 = c_ref[...], s1_ref[...], s2_ref[...]
    ci, si1, si2 = ci_ref[...], si1_ref[...], si2_ref[...]
    r0 = pl.multiple_of(i * QB, QB)
    kv = kv_ref[...]
    k_scr[pl.ds(r0, QB), :] = jnp.where(lane_q < HEAD_DIM, _rope(kv, c, s1, s2, ROPE_DIM // 2), 0.0).astype(BF16)
    vt_scr[:, pl.ds(r0, QB)] = kv.T[HEAD_DIM:2 * HEAD_DIM, :].astype(BF16)
    ikw = ikw_ref[...]
    ik_scr[pl.ds(r0, QB), :] = jnp.where(lane_q < IDX_DIM, _rope(ikw, ci, si1, si2, IDX_ROPE_DIM // 2), 0.0).astype(BF16)

    qr = _rope_wide(q_ref[...] * (SCALE * LOG2E), c, s1, s2, ROPE_DIM // 2)
    qr_st = _stack_heads_t(qr.T, H, HEAD_DIM).astype(BF16)
    iqr = _rope_wide(iq_ref[...], ci, si1, si2, IDX_ROPE_DIM // 2)
    iq_st = _stack_heads_t(iqr.T, IDX_HEADS, IDX_DIM).astype(BF16)
    wt = ikw.T * ((IDX_DIM ** -0.5) * (IDX_HEADS ** -0.5))
    w_st = jnp.concatenate([wt[IDX_DIM + h:IDX_DIM + h + 1, :] for h in range(IDX_HEADS)], axis=1)

    n_chunks = (i * QB + QB + KC - 1) // KC
    k_io = lax.broadcasted_iota(I32, (KC, LANES), 0)
    tq = i * QB + lax.broadcasted_iota(I32, (KC, LANES), 1)

    def score_body(cidx, carry):
        mx, mn = carry
        k0 = pl.multiple_of(cidx * KC, KC)
        d = jnp.dot(ik_scr[pl.ds(k0, KC), :], iq_st, preferred_element_type=F32)
        r = jnp.maximum(d, 0.0) * w_st
        sc = r[:, 0:LANES]
        for h in range(1, IDX_HEADS):
            sc = sc + r[:, LANES * h:LANES * (h + 1)]
        sc = jnp.where(sc == 0.0, 0.0, sc)
        causal = k_io + k0 <= tq
        key_scr[pl.ds(k0, KC), :] = jnp.where(causal, _f2key(sc), INT_MIN)
        mx = jnp.maximum(mx, _tree_rows(jnp.where(causal, sc, -jnp.inf), jnp.maximum))
        mn = jnp.minimum(mn, _tree_rows(jnp.where(causal, sc, jnp.inf), jnp.minimum))
        return mx, mn

    mx8, mn8 = lax.fori_loop(0, n_chunks, score_body,
                             (jnp.full((8, LANES), -jnp.inf, F32), jnp.full((8, LANES), jnp.inf, F32)))

    def count(pred):
        def body(cidx, acc):
            k0 = pl.multiple_of(cidx * KC, KC)
            return acc + _tree_rows(pred(key_scr[pl.ds(k0, KC), :], k_io + k0), jnp.add)
        acc8 = lax.fori_loop(0, n_chunks, body, jnp.zeros((8, LANES), F32))
        return jnp.sum(acc8, axis=0, keepdims=True)

    topf = float(top)
    lo0 = _f2key(jnp.min(mn8, axis=0, keepdims=True))
    hi0 = _f2key(jnp.max(mx8, axis=0, keepdims=True)) + 1
    n_causal = (i * QB + 1 + lax.broadcasted_iota(I32, (1, LANES), 1)).astype(F32)
    log_top = float(np.log(top))

    def pending(lo, hi, c_lo):
        return (jnp.max(jnp.where(c_lo > topf, jnp.where(hi > lo + 1, 1.0, 0.0), 0.0)) > 0.5).astype(I32)

    def search_cond(c):
        return jnp.logical_and(c[0] < SEARCH_MAX_PASSES, c[1] > 0)

    def search_body(c):
        it, _, lo, hi, c_lo, c_hi, kept = c
        v_lo, v_hi = _key2f(lo), _key2f(hi)
        l_lo = jnp.log(c_lo)
        frac = (l_lo - log_top) / (l_lo - jnp.log(jnp.maximum(c_hi, 0.5)))
        cand = _f2key(v_lo + frac * (v_hi - v_lo))
        runs = jnp.abs(kept)
        cand = jnp.where(runs >= 2, _f2key(0.5 * v_lo + 0.5 * v_hi), cand)
        key_mid = (lo >> 1) + (hi >> 1) + (lo & hi & 1)
        cand = jnp.where(runs >= 4, key_mid, cand)
        cand = jnp.where(it >= SEARCH_BISECT_FROM, key_mid, cand)
        cand = jnp.where(it < 2, it, cand)
        cand = jnp.minimum(jnp.maximum(cand, lo + 1), hi - 1)
        cnt = count(lambda key, kpos: jnp.where(key >= cand, 1.0, 0.0))
        active = jnp.where(c_lo > topf, jnp.where(hi > lo + 1, 1.0, 0.0), 0.0) > 0.5
        up = jnp.where(active, jnp.where(cnt >= topf, 1.0, 0.0), 0.0) > 0.5
        dn = jnp.where(active, jnp.where(cnt >= topf, 0.0, 1.0), 0.0) > 0.5
        lo, c_lo = jnp.where(up, cand, lo), jnp.where(up, cnt, c_lo)
        hi, c_hi = jnp.where(dn, cand, hi), jnp.where(dn, cnt, c_hi)
        kept = jnp.where(up, jnp.where(kept < 0, kept - 1, -1), jnp.where(dn, jnp.where(kept > 0, kept + 1, 1), kept))
        return it + 1, pending(lo, hi, c_lo), lo, hi, c_lo, c_hi, kept

    zero_f = jnp.zeros((1, LANES), F32)
    _, _, thr, _, c_lo, c_hi, _ = lax.while_loop(
        search_cond, search_body,
        (jnp.int32(0), pending(lo0, hi0, n_causal), lo0, hi0, n_causal, zero_f, jnp.zeros((1, LANES), I32)))

    tied = c_lo > topf
    need1 = (topf - 1.0) - c_hi
    all_pos = jnp.full((1, LANES), 2 ** idx_bits - 1, I32)

    def tie_search():
        def tie_body(bi, cur):
            cand = cur | lax.shift_left(jnp.int32(1), idx_bits - 1 - bi)
            cnt = count(lambda key, kpos: jnp.where(key == thr, jnp.where(kpos < cand, 1.0, 0.0), 0.0))
            return jnp.where(cnt <= need1, cand, cur)
        return lax.fori_loop(0, idx_bits, tie_body, jnp.zeros((1, LANES), I32))

    any_tied = jnp.max(jnp.where(tied, 1.0, 0.0)) > 0.5
    last = jnp.where(tied, lax.cond(any_tied, tie_search, lambda: all_pos), all_pos)

    half_neg = int(np.float32(0.5 * NEG).view(np.int32))
    key_floor = half_neg ^ 0x7FFFFFFF

    def att_body(cidx, carry):
        m, l, acc = carry
        k0 = pl.multiple_of(cidx * KC, KC)
        key = key_scr[pl.ds(k0, KC), :]
        kpos = k_io + k0
        bias = jnp.where(key > thr, 0.0, jnp.where(key == thr, jnp.where(kpos <= last, 0.0, NEG), NEG))
        bias = jnp.where(key > key_floor, bias, NEG)
        s = (jnp.dot(k_scr[pl.ds(k0, KC), :], qr_st, preferred_element_type=F32)
             + jnp.concatenate([bias] * H, axis=1))
        m_new = jnp.maximum(m, jnp.max(s, axis=0, keepdims=True))
        alpha = jnp.exp2(m - m_new)
        e = jnp.exp2(s - m_new)
        l = alpha * l + jnp.sum(e, axis=0, keepdims=True)
        acc = alpha * acc + jnp.dot(vt_scr[:, pl.ds(k0, KC)], e.astype(BF16), preferred_element_type=F32)
        return m_new, l, acc

    init = (jnp.full((1, HW), NEG, F32), jnp.zeros((1, HW), F32), jnp.zeros((HEAD_DIM, HW), F32))
    m_a, l_a, acc_a = lax.fori_loop(0, n_chunks, att_body, init)
    o = jnp.where(m_a > 0.5 * NEG, acc_a / l_a, 0.0)
    outs = [o[:, LANES * h:LANES * (h + 1)] for h in range(H)]
    outs.append(jnp.zeros((o_ref.shape[1] - H * HEAD_DIM, LANES), F32))
    o_ref[...] = jnp.concatenate(outs, axis=0).T.astype(BF16)


def _dsa(proj, tabs, tabs_i, batch, seq):
    nb = seq // QB
    col = lambda name, w: _OFF[name] // w
    qmap = lambda cidx: (lambda b, i: (b * nb + i, cidx))
    tmap = lambda k: (lambda b, i: (k, i, 0))
    top = min(DSA_TOPK_MAX, seq // 4)
    kern = functools.partial(_dsa_kernel, top=top, idx_bits=int(seq).bit_length())
    return pl.pallas_call(
        kern,
        grid=(batch, nb),
        in_specs=[pl.BlockSpec((QB, 384), qmap(col("b_q", 384))),
                  pl.BlockSpec((QB, 256), qmap(col("b_iq", 256))),
                  pl.BlockSpec((QB, LANES), qmap(col("b_ikiw", LANES))),
                  pl.BlockSpec((QB, LANES), qmap(col("b_kv", LANES))),
                  pl.BlockSpec((None, QB, LANES), tmap(0)),
                  pl.BlockSpec((None, QB, LANES), tmap(1)),
                  pl.BlockSpec((None, QB, LANES), tmap(2)),
                  pl.BlockSpec((None, QB, LANES), tmap(0)),
                  pl.BlockSpec((None, QB, LANES), tmap(1)),
                  pl.BlockSpec((None, QB, LANES), tmap(2))],
        out_specs=pl.BlockSpec((QB, 384), lambda b, i: (b * nb + i, 0)),
        out_shape=jax.ShapeDtypeStruct((batch * seq, 384), BF16),
        scratch_shapes=[pltpu.VMEM((seq, LANES), BF16), pltpu.VMEM((HEAD_DIM, seq), BF16),
                        pltpu.VMEM((seq, LANES), BF16), pltpu.VMEM((seq, LANES), I32)],
        compiler_params=_params(2),
        name="dsa",
    )(proj, proj, proj, proj, tabs, tabs, tabs, tabs_i, tabs_i, tabs_i)


DIL_MB = max(d for _, d in DIL_PAIRS) * QB
DIL_VMEM_LIMIT = 56 * 1024 * 1024
DIL_UNROLL = 4


def _dil_layout(g):
    dil = DIL_PAIRS[g][1]
    per = DIL_MB // dil
    return dil, per, per + QB


def _dil_bias_table():
    u = np.arange(2 * QB)[:, None]
    diff = QB + np.arange(LANES)[None, :] - u
    ok = (diff >= 0) & (diff <= QB)
    return _bias_table(np.concatenate([ok, ok & (u >= QB)], axis=0))


def _dil_kernel(*refs):
    G, HG = len(DIL_PAIRS), DIL_HEADS_PER_GROUP
    q_refs, k_refs, v_refs = refs[0:G], refs[G:2 * G], refs[2 * G:3 * G]
    c_ref, s1_ref, s2_ref, bias_ref, o_ref = refs[3 * G:3 * G + 5]
    kds, vds, ogs, lss = (refs[3 * G + 5 + n * G:3 * G + 5 + (n + 1) * G] for n in range(4))
    step = pl.program_id(1)
    half = ROPE_DIM // 2

    @pl.when(step == 0)
    def _zero():
        for g in range(G):
            kds[g][...] = jnp.zeros_like(kds[g])
            vds[g][...] = jnp.zeros_like(vds[g])

    @pl.when(step > 0)
    def _carry_halo():
        for g in range(G):
            dil, per, stride = _dil_layout(g)
            for r in range(dil):
                b0 = r * stride
                kds[g][b0:b0 + QB, :] = kds[g][b0 + per:b0 + per + QB, :]
                vds[g][:, b0:b0 + QB] = vds[g][:, b0 + per:b0 + per + QB]

    for g in range(G):
        dil, per, stride = _dil_layout(g)
        piece = min(per, KC)
        for r in range(dil):
            for p0 in range(0, per, piece):
                rows = pl.ds(r + dil * p0, piece, stride=dil)
                kr = _rope(k_refs[g][rows, :], c_ref[rows, :], s1_ref[rows, :], s2_ref[rows, :], half)
                d0 = r * stride + QB + p0
                kds[g][d0:d0 + piece, :] = kr.astype(BF16)
                vds[g][:, d0:d0 + piece] = v_refs[g][rows, :].T.astype(BF16)

    zero = jnp.zeros((HEAD_DIM, LANES), F32)
    for g in range(G):
        dil, per, stride = _dil_layout(g)
        nblk = per // QB

        def block(jb, _, r=0, g=g, dil=dil, stride=stride):
            p0 = pl.multiple_of(dil * QB * jb, dil * QB)
            win = pl.ds(p0, dil * QB)
            rows = pl.ds(r, QB, stride=dil)
            q = _rope(q_refs[g].at[win, :][rows, :] * (SCALE * LOG2E), c_ref.at[win, :][rows, :],
                      s1_ref.at[win, :][rows, :], s2_ref.at[win, :][rows, :], half)
            qt = q.T
            q2 = jnp.concatenate([jnp.concatenate([qt[:HEAD_DIM], zero], axis=0),
                                  jnp.concatenate([zero, qt[HEAD_DIM:]], axis=0)], axis=1).astype(BF16)
            kb = pl.multiple_of(r * stride + QB * jb, QB)
            first = jnp.logical_and(step == 0, jb == 0)
            bias = bias_ref[pl.ds(pl.multiple_of(jnp.where(first, 2 * QB, 0), QB), 2 * QB), :]
            s = (jnp.dot(kds[g][pl.ds(kb, 2 * QB), :], q2, preferred_element_type=F32)
                 + jnp.concatenate([bias] * HG, axis=1))
            e, m, den = _softmax2_cols(s)
            lse = m + jnp.log2(den)
            rden = 1.0 / den
            parts, lparts = [], []
            for hg in range(HG):
                sl = slice(LANES * hg, LANES * (hg + 1))
                vt = vds[g][HEAD_DIM * hg:HEAD_DIM * (hg + 1), pl.ds(kb, 2 * QB)]
                parts.append(jnp.dot(vt, e[:, sl].astype(BF16), preferred_element_type=F32) * rden[:, sl])
                lparts.append(jnp.broadcast_to(lse[:, sl], (HEAD_DIM, LANES)))
            tile = jnp.concatenate(parts + lparts, axis=0).T
            ogs[g].at[win, :][rows, :] = tile[:, :LANES]
            lss[g].at[win, :][rows, :] = tile[:, LANES:]
            return 0

        for r in range(dil):
            lax.fori_loop(0, nblk, functools.partial(block, r=r), 0, unroll=min(nblk, DIL_UNROLL))

    def mix(ti, _):
        rows = pl.ds(pl.multiple_of(ti * KC, KC), KC)
        ls = [lss[g][rows, :] for g in range(G)]
        mx = functools.reduce(jnp.maximum, ls)
        ex = [jnp.exp2(x - mx) for x in ls]
        rtot = 1.0 / functools.reduce(lambda a, b: a + b, ex)
        o_ref[rows, :] = jnp.concatenate([ex[g] * rtot * ogs[g][rows, :] for g in range(G)], axis=1).astype(BF16)
        return 0

    lax.fori_loop(0, DIL_MB // KC, mix, 0)


def _dil(proj, tabs, batch, seq):
    assert seq % DIL_MB == 0 and all(w // d == QB for w, d in DIL_PAIRS)
    nm = seq // DIL_MB
    G = len(DIL_PAIRS)
    w = DIL_HEADS * HEAD_DIM
    gmap = lambda name, g: (lambda b, j: (b * nm + j, _OFF[name] // LANES + g))
    tmap = lambda k: (lambda b, j: (k, j, 0))
    bias = _dil_bias_table()
    rows = [_dil_layout(g)[0] * _dil_layout(g)[2] for g in range(G)]
    return pl.pallas_call(
        _dil_kernel,
        grid=(batch, nm),
        in_specs=([pl.BlockSpec((DIL_MB, LANES), gmap(name, g)) for name in ("c_q", "c_k", "c_v") for g in range(G)]
                  + [pl.BlockSpec((None, DIL_MB, LANES), tmap(k)) for k in range(3)]
                  + [pl.BlockSpec(bias.shape, lambda b, j: (0, 0))]),
        out_specs=pl.BlockSpec((DIL_MB, w), lambda b, j: (b * nm + j, 0)),
        out_shape=jax.ShapeDtypeStruct((batch * seq, w), BF16),
        scratch_shapes=([pltpu.VMEM((n, LANES), BF16) for n in rows] + [pltpu.VMEM((LANES, n), BF16) for n in rows]
                        + [pltpu.VMEM((DIL_MB, LANES), F32)] * (2 * G)),
        compiler_params=_params(2, DIL_VMEM_LIMIT),
        name="dilated",
    )(*([proj] * (3 * G)), tabs, tabs, tabs, bias)


def _overlap_t(seq):
    n_cmp_pad = seq // CMP_STRIDE
    n_slc = seq // SLC_BLOCK
    c_start = np.arange(n_cmp_pad) * CMP_STRIDE
    s_start = np.arange(n_slc) * SLC_BLOCK
    ov = ((c_start[None, :] < s_start[:, None] + SLC_BLOCK) & (c_start[None, :] + CMP_BLOCK > s_start[:, None]))
    return jnp.asarray(ov.astype(np.float32), dtype=BF16)


def _layer(x2, batch, seq, norm1_g, w_in, cmp_pos, cmp_w1, cmp_w2, w_out, norm2_g, w_up, conv_w, conv_b, w_down,
           final_g, final_norm, tabs, tabs_i, ovt):
    n = batch * seq
    proj = _inproj(x2, norm1_g, _pad_w_in(w_in).astype(BF16))

    o_kc = _OFF["a_kcvc"]
    rows = seq // CMP_STRIDE
    kc_r = proj[:, o_kc:o_kc + HEAD_DIM].reshape(batch, rows, CMP_STRIDE * HEAD_DIM)
    vc_r = proj[:, o_kc + HEAD_DIM:o_kc + 2 * HEAD_DIM].reshape(batch, rows, CMP_STRIDE * HEAD_DIM)
    pos = cmp_pos.reshape(2, 2, CMP_STRIDE * HEAD_DIM)
    w1 = cmp_w1.reshape(2, 2, CMP_STRIDE * HEAD_DIM, CMP_HIDDEN).astype(BF16)
    zpad = jnp.zeros((CMP_HIDDEN, HEAD_DIM), cmp_w2.dtype)
    w2 = jnp.stack([jnp.concatenate([cmp_w2[0], zpad], axis=1),
                    jnp.concatenate([zpad, cmp_w2[1]], axis=1)]).astype(BF16)
    kvcmp = _nsa_compress(kc_r, vc_r, pos, w1, w2)

    o_a = _nsa(proj, kvcmp, tabs, ovt, batch, seq)
    o_b = _dsa(proj, tabs, tabs_i, batch, seq)
    o_c = _dil(proj, tabs, batch, seq)

    hp = NSA_HEADS * HEAD_DIM
    zrow = jnp.zeros((384 - hp, D_MODEL), w_out.dtype)
    w_out_pad = jnp.concatenate([w_out[0:hp], zrow, w_out[hp:2 * hp], zrow, w_out[2 * hp:]], axis=0).astype(BF16)
    x2 = _outproj(o_a, o_b, o_c, w_out_pad, x2)

    act = _ffn_up(x2, norm2_g, w_up.astype(BF16), conv_w, conv_b, seq)
    return _ffn_down(act, w_down.astype(BF16), x2, final_g, final_norm)


def kernel(x, norm1_g, w_in, cmp_pos, cmp_w1, cmp_w2, w_out, norm2_g, w_up, conv_w, conv_b, w_down, final_g):
    batch, seq, d = x.shape
    depth = w_in.shape[0]
    tabs = _rope_lane_tables(seq, HEAD_DIM, ROPE_DIM)
    tabs_i = _rope_lane_tables(seq, IDX_DIM, IDX_ROPE_DIM)
    ovt = _overlap_t(seq)
    x2 = x.reshape(batch * seq, d)
    for li in range(depth):
        x2 = _layer(x2, batch, seq, norm1_g[li], w_in[li], cmp_pos[li], cmp_w1[li], cmp_w2[li], w_out[li],
                    norm2_g[li], w_up[li], conv_w[li], conv_b[li], w_down[li], final_g, li == depth - 1,
                    tabs, tabs_i, ovt)
    return x2.reshape(batch, seq, d)
```

```python
import functools

import numpy as np
import jax
import jax.numpy as jnp
from jax import lax
from jax.experimental import pallas as pl
from jax.experimental.pallas import tpu as pltpu

F32 = jnp.float32
BF16 = jnp.bfloat16
I32 = jnp.int32

D_MODEL = 1024
HEAD_DIM = 64
ROPE_DIM = HEAD_DIM // 4
ROPE_THETA = 500000.0
NORM_EPS = 1e-6
SCALE = HEAD_DIM ** -0.5
LOG2E = 1.4426950408889634
NEG = -1e30
FORCE = 1e9
NSA_HEADS = 5
CMP_BLOCK = 32
CMP_STRIDE = 16
CMP_HIDDEN = 128
SLC_BLOCK = 64
SLC_TOPN = 16
WIN = 512
DSA_HEADS = 5
IDX_HEADS = 8
IDX_DIM = 32
IDX_ROPE_DIM = IDX_DIM // 4
DSA_TOPK_MAX = 256
DIL_PAIRS = ((128, 1), (512, 4), (2048, 16))
DIL_HEADS_PER_GROUP = 2
DIL_HEADS = len(DIL_PAIRS) * DIL_HEADS_PER_GROUP
D_FF = 2816
CONV_WIDTH = 3

LANES = 128
QB = 128
KC = 512
ATT_SPLIT = 1
ROW_TILE = 512
HALO = 16
VMEM_LIMIT = 48 * 1024 * 1024
INT_MIN = -2 ** 31

_A0 = 0
_B0 = 719
_C0 = 1463
_SEGS = (
    ("c_q", _C0, 384, 384), ("c_k", _C0 + 384, 384, 384), ("c_v", _C0 + 768, 384, 384),
    ("a_q", _A0, 320, 384), ("b_q", _B0, 320, 384),
    ("a_kcvc", _A0 + 320, 128, 128),
    ("b_iq", _B0 + 448, 256, 256),
    ("a_ksvs", _A0 + 448, 128, 128), ("a_kwvw", _A0 + 576, 128, 128),
    ("a_g", _A0 + 704, 15, 128),
    ("b_kv", _B0 + 320, 128, 128),
    ("b_ikiw", _B0 + 704, 40, 128),
)
P_COLS = sum(s[3] for s in _SEGS)


def _seg_offsets():
    offs, o = {}, 0
    for name, _, _, pw in _SEGS:
        offs[name] = o
        o += pw
    return offs


_OFF = _seg_offsets()


def _pad_w_in(w):
    parts = []
    for _, src, wdt, pw in _SEGS:
        parts.append(w[:, src:src + wdt])
        if pw > wdt:
            parts.append(jnp.zeros((w.shape[0], pw - wdt), w.dtype))
    return jnp.concatenate(parts, axis=1)


def _rope_lane_tables(L, head_dim, rot_dim):
    half = rot_dim // 2
    inv = 1.0 / (ROPE_THETA ** (np.arange(0, rot_dim, 2, dtype=np.float32) / np.float32(rot_dim)))
    ang = np.arange(L, dtype=np.float32)[:, None] * inv[None, :]
    cos, sin = np.cos(ang).astype(np.float32), np.sin(ang).astype(np.float32)
    d = np.arange(LANES) % head_dim
    lo, hi = d < half, (d >= half) & (d < rot_dim)
    c = np.ones((L, LANES), np.float32)
    s1 = np.zeros((L, LANES), np.float32)
    s2 = np.zeros((L, LANES), np.float32)
    c[:, lo] = cos[:, d[lo]]
    c[:, hi] = cos[:, d[hi] - half]
    s1[:, lo] = -sin[:, d[lo]]
    s2[:, hi] = sin[:, d[hi] - half]
    return jnp.asarray(np.stack([c, s1, s2]))


def _rope(x, c, s1, s2, half):
    xp = pltpu.roll(x, LANES - half, 1)
    xm = pltpu.roll(x, half, 1)
    return x * c + xp * s1 + xm * s2


def _rope_wide(x, c, s1, s2, half):
    n = x.shape[1] // LANES
    return jnp.concatenate([_rope(x[:, LANES * s:LANES * (s + 1)], c, s1, s2, half) for s in range(n)], axis=1)


def _softmax2_cols(s):
    m = jnp.max(s, axis=0, keepdims=True)
    e = jnp.exp2(s - m)
    return e, m, jnp.sum(e, axis=0, keepdims=True)


def _bias_table(ok):
    return jnp.asarray(np.where(ok, 0.0, NEG).astype(np.float32))


def _cmp_bias_table(seq):
    shift = (QB // CMP_STRIDE) * (seq // QB - 1)
    u = np.arange(seq // CMP_STRIDE + shift)[:, None]
    lane = np.arange(LANES)[None, :]
    return _bias_table(CMP_STRIDE * (u - shift) + CMP_BLOCK - 1 <= lane)


def _win_bias_table(seq):
    span = min(WIN + QB, seq)
    cmax = span - QB
    u = np.arange(span + cmax)[:, None]
    diff = cmax + np.arange(LANES)[None, :] - u
    return _bias_table((diff >= 0) & (diff < WIN))


def _causal_bias_table():
    u = np.arange(2 * KC)[:, None]
    return _bias_table(u <= KC + np.arange(LANES)[None, :])


def _params(n_grid, vmem=VMEM_LIMIT):
    return pltpu.CompilerParams(dimension_semantics=("arbitrary",) * n_grid, vmem_limit_bytes=vmem)


def _rmsnorm_rows(x, g):
    return x * lax.rsqrt(jnp.mean(x * x, axis=-1, keepdims=True) + NORM_EPS) * g


def _inproj_kernel(x_ref, g_ref, w_ref, o_ref):
    hn = _rmsnorm_rows(x_ref[...], g_ref[...]).astype(BF16)
    n = w_ref.shape[1]
    for c0 in range(0, n, 512):
        c1 = min(c0 + 512, n)
        o_ref[:, c0:c1] = jnp.dot(hn, w_ref[:, c0:c1], preferred_element_type=F32)


def _inproj(x2, g, w_bf):
    n, d = x2.shape
    pc = w_bf.shape[1]
    return pl.pallas_call(
        _inproj_kernel,
        grid=(n // ROW_TILE,),
        in_specs=[pl.BlockSpec((ROW_TILE, d), lambda i: (i, 0)),
                  pl.BlockSpec((1, d), lambda i: (0, 0)),
                  pl.BlockSpec((d, pc), lambda i: (0, 0))],
        out_specs=pl.BlockSpec((ROW_TILE, pc), lambda i: (i, 0)),
        out_shape=jax.ShapeDtypeStruct((n, pc), F32),
        compiler_params=_params(1),
        name="inproj",
    )(x2, g.reshape(1, d), w_bf)


def _outproj_kernel(oa_ref, ob_ref, oc_ref, w_ref, x_ref, o_ref):
    mix = jnp.concatenate([oa_ref[...], ob_ref[...], oc_ref[...]], axis=1)
    o_ref[...] = x_ref[...] + jnp.dot(mix, w_ref[...], preferred_element_type=F32)


def _outproj(oa, ob, oc, w_bf, x2):
    n, d = x2.shape
    k = w_bf.shape[0]
    mw = oa.shape[1]
    return pl.pallas_call(
        _outproj_kernel,
        grid=(n // ROW_TILE,),
        in_specs=[pl.BlockSpec((ROW_TILE, mw), lambda i: (i, 0)),
                  pl.BlockSpec((ROW_TILE, mw), lambda i: (i, 0)),
                  pl.BlockSpec((ROW_TILE, mw), lambda i: (i, 0)),
                  pl.BlockSpec((k, d), lambda i: (0, 0)),
                  pl.BlockSpec((ROW_TILE, d), lambda i: (i, 0))],
        out_specs=pl.BlockSpec((ROW_TILE, d), lambda i: (i, 0)),
        out_shape=jax.ShapeDtypeStruct((n, d), F32),
        compiler_params=_params(1),
        name="outproj",
    )(oa, ob, oc, w_bf, x2)


def _ffn_up_kernel(x_ref, xh_ref, g_ref, wa_ref, wu_ref, cw_ref, cb_ref, o_ref, hn_scr, hh_scr, a_scr, *, tiles_per_seq):
    i, j = pl.program_id(0), pl.program_id(1)
    tm = x_ref.shape[0]

    @pl.when(j == 0)
    def _norm():
        hn_scr[...] = _rmsnorm_rows(x_ref[...], g_ref[...]).astype(BF16)
        keep = jnp.where(i % tiles_per_seq == 0, 0.0, 1.0)
        hh_scr[...] = (_rmsnorm_rows(xh_ref[...], g_ref[...]) * keep).astype(BF16)

    hn = hn_scr[...]
    a = jnp.dot(hn, wa_ref[...], preferred_element_type=F32)
    u = jnp.dot(hn, wu_ref[...], preferred_element_type=F32)
    a_scr[0:HALO, :] = jnp.dot(hh_scr[...], wa_ref[...], preferred_element_type=F32)
    a_scr[HALO:HALO + tm, :] = a
    a1 = a_scr[pl.ds(HALO - 1, tm), :]
    a2 = a_scr[pl.ds(HALO - 2, tm), :]
    cw = cw_ref[...]
    conv = cw[0:1, :] * a2 + cw[1:2, :] * a1 + cw[2:3, :] * a + cb_ref[...]
    o_ref[...] = (conv * jax.nn.sigmoid(conv) * u).astype(BF16)


def _ffn_up(x2, g, w_up_bf, conv_w, conv_b, seq_len):
    n, d = x2.shape
    tn = D_FF // 2
    nj = D_FF // tn
    hb = ROW_TILE // HALO
    kern = functools.partial(_ffn_up_kernel, tiles_per_seq=seq_len // ROW_TILE)
    return pl.pallas_call(
        kern,
        grid=(n // ROW_TILE, nj),
        in_specs=[pl.BlockSpec((ROW_TILE, d), lambda i, j: (i, 0)),
                  pl.BlockSpec((HALO, d), lambda i, j: (jnp.maximum(i * hb - 1, 0), 0)),
                  pl.BlockSpec((1, d), lambda i, j: (0, 0)),
                  pl.BlockSpec((d, tn), lambda i, j: (0, j)),
                  pl.BlockSpec((d, tn), lambda i, j: (0, j + nj)),
                  pl.BlockSpec((CONV_WIDTH, tn), lambda i, j: (0, j)),
                  pl.BlockSpec((1, tn), lambda i, j: (0, j))],
        out_specs=pl.BlockSpec((ROW_TILE, tn), lambda i, j: (i, j)),
        out_shape=jax.ShapeDtypeStruct((n, D_FF), BF16),
        scratch_shapes=[pltpu.VMEM((ROW_TILE, d), BF16), pltpu.VMEM((HALO, d), BF16),
                        pltpu.VMEM((ROW_TILE + HALO, tn), F32)],
        compiler_params=_params(2),
        name="ffn_up",
    )(x2, x2, g.reshape(1, d), w_up_bf, w_up_bf, conv_w, conv_b.reshape(1, D_FF))


def _ffn_down_kernel(a_ref, w_ref, x_ref, g_ref, o_ref, *, final_norm):
    y = x_ref[...] + jnp.dot(a_ref[...], w_ref[...], preferred_element_type=F32)
    if final_norm:
        y = _rmsnorm_rows(y, g_ref[...])
    o_ref[...] = y


def _ffn_down(act, w_bf, x2, final_g, final_norm):
    n, d = x2.shape
    k = act.shape[1]
    return pl.pallas_call(
        functools.partial(_ffn_down_kernel, final_norm=final_norm),
        grid=(n // ROW_TILE,),
        in_specs=[pl.BlockSpec((ROW_TILE, k), lambda i: (i, 0)),
                  pl.BlockSpec((k, d), lambda i: (0, 0)),
                  pl.BlockSpec((ROW_TILE, d), lambda i: (i, 0)),
                  pl.BlockSpec((1, d), lambda i: (0, 0))],
        out_specs=pl.BlockSpec((ROW_TILE, d), lambda i: (i, 0)),
        out_shape=jax.ShapeDtypeStruct((n, d), F32),
        compiler_params=_params(1),
        name="ffn_down",
    )(act, w_bf, x2, final_g.reshape(1, d))


def _cmp_kernel(kc_ref, vc_ref, pos_ref, w1_ref, w2_ref, o_ref):
    out = None
    for c, src in enumerate((kc_ref, vc_ref)):
        r = src[0]
        lo = jnp.dot((r + pos_ref[c, 0:1, :]).astype(BF16), w1_ref[c, 0], preferred_element_type=F32)
        hi = jnp.dot((r + pos_ref[c, 1:2, :]).astype(BF16), w1_ref[c, 1], preferred_element_type=F32)
        nrows = hi.shape[0]
        hid = jax.nn.gelu(lo + pltpu.roll(hi, nrows - 1, 0))
        y = jnp.dot(hid.astype(BF16), w2_ref[c], preferred_element_type=F32)
        out = y if out is None else out + y
    o_ref[0] = out


def _nsa_compress(kc_r, vc_r, pos, w1_bf, w2_pad_bf):
    b, nr, wd = kc_r.shape
    return pl.pallas_call(
        _cmp_kernel,
        grid=(b,),
        in_specs=[pl.BlockSpec((1, nr, wd), lambda i: (i, 0, 0)),
                  pl.BlockSpec((1, nr, wd), lambda i: (i, 0, 0)),
                  pl.BlockSpec(pos.shape, lambda i: (0, 0, 0)),
                  pl.BlockSpec(w1_bf.shape, lambda i: (0, 0, 0, 0)),
                  pl.BlockSpec(w2_pad_bf.shape, lambda i: (0, 0, 0))],
        out_specs=pl.BlockSpec((1, nr, LANES), lambda i: (i, 0, 0)),
        out_shape=jax.ShapeDtypeStruct((b, nr, LANES), F32),
        compiler_params=_params(1),
        name="nsa_compress",
    )(kc_r, vc_r, pos, w1_bf, w2_pad_bf)


def _stack_heads_t(x_t, n_heads, hd):
    z = jnp.zeros((LANES - hd, LANES), F32)
    return jnp.concatenate(
        [jnp.concatenate([x_t[hd * h:hd * (h + 1), :], z], axis=0) for h in range(n_heads)], axis=1)


def _nsa_kernel(q_ref, g_ref, ksvs_ref, kwvw_ref, kvc_ref, c_ref, s1_ref, s2_ref, ovt_ref, cmpb_ref, winb_ref,
                caub_ref, o_ref, ks_scr, vst_scr, kw_scr, vwt_scr, kc_scr, vct_scr, blk_scr, *, n_top):
    i = pl.program_id(1)
    H = NSA_HEADS
    HW = H * LANES
    seq = ks_scr.shape[0]
    lane_q = lax.broadcasted_iota(I32, (QB, LANES), 1)

    @pl.when(i == 0)
    def _init():
        ks_scr[...] = jnp.zeros_like(ks_scr)
        kw_scr[...] = jnp.zeros_like(kw_scr)
        vst_scr[...] = jnp.zeros_like(vst_scr)
        vwt_scr[...] = jnp.zeros_like(vwt_scr)
        kvc = kvc_ref[0]
        lane_c = lax.broadcasted_iota(I32, kvc.shape, 1)
        kc_scr[...] = jnp.where(lane_c < HEAD_DIM, kvc, 0.0).astype(BF16)
        vct_scr[...] = kvc.T[HEAD_DIM:2 * HEAD_DIM, :].astype(BF16)

    c, s1, s2 = c_ref[...], s1_ref[...], s2_ref[...]
    r0 = pl.multiple_of(i * QB, QB)
    for src, kdst, vdst in ((ksvs_ref, ks_scr, vst_scr), (kwvw_ref, kw_scr, vwt_scr)):
        kv = src[...]
        kr = _rope(kv, c, s1, s2, ROPE_DIM // 2)
        kdst[pl.ds(r0, QB), :] = jnp.where(lane_q < HEAD_DIM, kr, 0.0).astype(BF16)
        vdst[:, pl.ds(r0, QB)] = kv.T[HEAD_DIM:2 * HEAD_DIM, :].astype(BF16)

    q = q_ref[...] * (SCALE * LOG2E)
    qr = _rope_wide(q, c, s1, s2, ROPE_DIM // 2)
    q_st = _stack_heads_t(q.T, H, HEAD_DIM).astype(BF16)
    qr_st = _stack_heads_t(qr.T, H, HEAD_DIM).astype(BF16)
    tile_h = lambda b: jnp.concatenate([b] * H, axis=1)

    ncp = kc_scr.shape[0]
    cshift = cmpb_ref.shape[0] - ncp
    cb = cmpb_ref[pl.ds(pl.multiple_of(cshift - (QB // CMP_STRIDE) * i, 8), ncp), :]
    s_c = jnp.dot(kc_scr[...], q_st, preferred_element_type=F32) + tile_h(cb)
    e_c, _, den_c = _softmax2_cols(s_c)
    t_row = i * QB + (lax.broadcasted_iota(I32, (1, HW), 1) & (LANES - 1))
    r_c = jnp.where(t_row >= CMP_BLOCK - 1, 1.0 / den_c, 0.0)
    o_cmp = jnp.dot(vct_scr[...], e_c.astype(BF16), preferred_element_type=F32) * r_c
    psum = e_c[:, 0:LANES] * r_c[:, 0:LANES]
    for h in range(1, H):
        psum = psum + e_c[:, LANES * h:LANES * (h + 1)] * r_c[:, LANES * h:LANES * (h + 1)]

    hi = psum.astype(BF16)
    r1 = psum - hi.astype(F32)
    mid = r1.astype(BF16)
    lo = (r1 - mid.astype(F32)).astype(BF16)
    ovt = ovt_ref[...]
    imp = (jnp.dot(ovt, hi, preferred_element_type=F32) + jnp.dot(ovt, mid, preferred_element_type=F32)
           + jnp.dot(ovt, lo, preferred_element_type=F32))
    n_slc = imp.shape[0]
    jb = lax.broadcasted_iota(I32, (n_slc, LANES), 0)
    tq = i * QB + lax.broadcasted_iota(I32, (n_slc, LANES), 1)
    cur = tq >> (SLC_BLOCK.bit_length() - 1)
    forced = (jb == 0) | (jb == cur) | (jb == cur - 1)
    val = jnp.where(forced, FORCE, jnp.where(jb <= cur, imp, NEG))
    rank = jnp.zeros((n_slc, LANES), I32)
    for jp in range(n_slc):
        row = val[jp:jp + 1, :]
        tie = jnp.where(jb > jp, 1, 0)
        rank = rank + jnp.where(row > val, 1, jnp.where(row == val, tie, 0))
    blk_scr[...] = jnp.where(rank < n_top, jnp.where(val > 0.5 * NEG, 0.0, NEG), NEG)

    bpc = KC // SLC_BLOCK

    def slc_body(ci, carry):
        m, l, acc = carry
        k0 = pl.multiple_of(ci * KC, KC)
        ahead = jnp.minimum(i * QB - k0, KC)
        c0 = pl.multiple_of(KC - ahead, QB)
        hk = KC // ATT_SPLIT
        ss, m_new = [], m
        for sp in range(ATT_SPLIT):
            kb = pl.multiple_of(k0 + sp * hk, hk)
            rows = [jnp.broadcast_to(blk_scr[pl.ds(ci * bpc + sp * (bpc // ATT_SPLIT) + r, 1), :], (SLC_BLOCK, LANES))
                    for r in range(bpc // ATT_SPLIT)]
            bias = jnp.concatenate(rows, axis=0) + caub_ref[pl.ds(pl.multiple_of(c0 + sp * hk, QB), hk), :]
            s = jnp.dot(ks_scr[pl.ds(kb, hk), :], qr_st, preferred_element_type=F32) + tile_h(bias)
            ss.append(s)
            m_new = jnp.maximum(m_new, jnp.max(s, axis=0, keepdims=True))
        alpha = jnp.exp2(m - m_new)
        l, acc = alpha * l, alpha * acc
        for sp in range(ATT_SPLIT):
            kb = pl.multiple_of(k0 + sp * hk, hk)
            e = jnp.exp2(ss[sp] - m_new)
            l = l + jnp.sum(e, axis=0, keepdims=True)
            acc = acc + jnp.dot(vst_scr[:, pl.ds(kb, hk)], e.astype(BF16), preferred_element_type=F32)
        return m_new, l, acc

    n_chunks = (i * QB + QB + KC - 1) // KC
    init = (jnp.full((1, HW), NEG, F32), jnp.zeros((1, HW), F32), jnp.zeros((HEAD_DIM, HW), F32))
    m_s, l_s, acc_s = lax.fori_loop(0, n_chunks, slc_body, init)
    o_slc = jnp.where(m_s > 0.5 * NEG, acc_s / l_s, 0.0)

    span = min(WIN + QB, seq)
    w0 = pl.multiple_of(jnp.maximum(i * QB + QB - span, 0), QB)
    wb = winb_ref[pl.ds(pl.multiple_of(span - QB - (i * QB - w0), QB), span), :]
    s_w = jnp.dot(kw_scr[pl.ds(w0, span), :], qr_st, preferred_element_type=F32) + tile_h(wb)
    e_w, _, den_w = _softmax2_cols(s_w)
    o_win = jnp.dot(vwt_scr[:, pl.ds(w0, span)], e_w.astype(BF16), preferred_element_type=F32) * (1.0 / den_w)

    gt = jax.nn.sigmoid(g_ref[...].T)
    outs = []
    for h in range(H):
        sl = slice(LANES * h, LANES * (h + 1))
        outs.append(gt[3 * h:3 * h + 1, :] * o_cmp[:, sl] + gt[3 * h + 1:3 * h + 2, :] * o_slc[:, sl]
                    + gt[3 * h + 2:3 * h + 3, :] * o_win[:, sl])
    outs.append(jnp.zeros((o_ref.shape[1] - H * HEAD_DIM, LANES), F32))
    o_ref[...] = jnp.concatenate(outs, axis=0).T.astype(BF16)


def _nsa(proj, kvcmp, tabs, ovt, batch, seq):
    nb = seq // QB
    n_slc = seq // SLC_BLOCK
    ncp = kvcmp.shape[1]
    col = lambda name, w: _OFF[name] // w
    qmap = lambda cidx: (lambda b, i: (b * nb + i, cidx))
    tmap = lambda k: (lambda b, i: (k, i, 0))
    kern = functools.partial(_nsa_kernel, n_top=min(SLC_TOPN, n_slc))
    cmpb, winb, caub = _cmp_bias_table(seq), _win_bias_table(seq), _causal_bias_table()
    whole = lambda a: pl.BlockSpec(a.shape, lambda b, i: (0, 0))
    return pl.pallas_call(
        kern,
        grid=(batch, nb),
        in_specs=[pl.BlockSpec((QB, 384), qmap(col("a_q", 384))),
                  pl.BlockSpec((QB, LANES), qmap(col("a_g", LANES))),
                  pl.BlockSpec((QB, LANES), qmap(col("a_ksvs", LANES))),
                  pl.BlockSpec((QB, LANES), qmap(col("a_kwvw", LANES))),
                  pl.BlockSpec((1, ncp, LANES), lambda b, i: (b, 0, 0)),
                  pl.BlockSpec((None, QB, LANES), tmap(0)),
                  pl.BlockSpec((None, QB, LANES), tmap(1)),
                  pl.BlockSpec((None, QB, LANES), tmap(2)),
                  whole(ovt), whole(cmpb), whole(winb), whole(caub)],
        out_specs=pl.BlockSpec((QB, 384), lambda b, i: (b * nb + i, 0)),
        out_shape=jax.ShapeDtypeStruct((batch * seq, 384), BF16),
        scratch_shapes=[pltpu.VMEM((seq, LANES), BF16), pltpu.VMEM((HEAD_DIM, seq), BF16),
                        pltpu.VMEM((seq, LANES), BF16), pltpu.VMEM((HEAD_DIM, seq), BF16),
                        pltpu.VMEM((ncp, LANES), BF16), pltpu.VMEM((HEAD_DIM, ncp), BF16),
                        pltpu.VMEM((n_slc, LANES), F32)],
        compiler_params=_params(2),
        name="nsa",
    )(proj, proj, proj, proj, kvcmp, tabs, tabs, tabs, ovt, cmpb, winb, caub)


SEARCH_BISECT_FROM = 24
SEARCH_MAX_PASSES = 64
SEARCH_KEPT_LIMIT = 6
INT_MAX = 2 ** 31 - 1
FLT_MIN_NORMAL = 1.17549435e-38
ZERO_BAND_END = 1 << 23


def _tree_rows(x, op):
    parts = [x[r:r + 8, :] for r in range(0, x.shape[0], 8)]
    while len(parts) > 1:
        nxt = [op(parts[a], parts[a + 1]) for a in range(0, len(parts) - 1, 2)]
        if len(parts) % 2:
            nxt.append(parts[-1])
        parts = nxt
    return parts[0]


def _f2key(v):
    bits = lax.bitcast_convert_type(v, I32)
    return bits ^ ((bits >> 31) & 0x7FFFFFFF)


def _key2f(k):
    return lax.bitcast_convert_type(k ^ ((k >> 31) & 0x7FFFFFFF), F32)


def _dsa_kernel(q_ref, iq_ref, ikw_ref, kv_ref, c_ref, s1_ref, s2_ref, ci_ref, si1_ref, si2_ref, o_ref,
                k_scr, vt_scr, ik_scr, key_scr, *, top, idx_bits):
    i = pl.program_id(1)
    H = DSA_HEADS
    HW = H * LANES
    seq = key_scr.shape[0]
    lane_q = lax.broadcasted_iota(I32, (QB, LANES), 1)

    @pl.when(i == 0)
    def _init():
        k_scr[...] = jnp.zeros_like(k_scr)
        vt_scr[...] = jnp.zeros_like(vt_scr)
        ik_scr[...] = jnp.zeros_like(ik_scr)

    c, s1, s2 = c_ref[...], s1_ref[...], s2_ref[...]
    ci, si1, si2 = ci_ref[...], si1_ref[...], si2_ref[...]
    r0 = pl.multiple_of(i * QB, QB)
    kv = kv_ref[...]
    k_scr[pl.ds(r0, QB), :] = jnp.where(lane_q < HEAD_DIM, _rope(kv, c, s1, s2, ROPE_DIM // 2), 0.0).astype(BF16)
    vt_scr[:, pl.ds(r0, QB)] = kv.T[HEAD_DIM:2 * HEAD_DIM, :].astype(BF16)
    ikw = ikw_ref[...]
    ik_scr[pl.ds(r0, QB), :] = jnp.where(lane_q < IDX_DIM, _rope(ikw, ci, si1, si2, IDX_ROPE_DIM // 2), 0.0).astype(BF16)

    qr = _rope_wide(q_ref[...] * (SCALE * LOG2E), c, s1, s2, ROPE_DIM // 2)
    qr_st = _stack_heads_t(qr.T, H, HEAD_DIM).astype(BF16)
    iqr = _rope_wide(iq_ref[...], ci, si1, si2, IDX_ROPE_DIM // 2)
    iq_st = _stack_heads_t(iqr.T, IDX_HEADS, IDX_DIM).astype(BF16)
    wt = ikw.T * ((IDX_DIM ** -0.5) * (IDX_HEADS ** -0.5))
    w_st = jnp.concatenate([wt[IDX_DIM + h:IDX_DIM + h + 1, :] for h in range(IDX_HEADS)], axis=1)

    n_chunks = (i * QB + QB + KC - 1) // KC
    k_io = lax.broadcasted_iota(I32, (KC, LANES), 0)
    tq = i * QB + lax.broadcasted_iota(I32, (KC, LANES), 1)

    sub_io = lax.broadcasted_iota(I32, (QB, LANES), 0)
    sub_tq = i * QB + lax.broadcasted_iota(I32, (QB, LANES), 1)

    def score_body(cidx, carry):
        mx, mn = carry
        k0 = pl.multiple_of(cidx * KC, KC)
        for sb in range(KC // QB):
            kb = pl.multiple_of(k0 + sb * QB, QB)
            d = jnp.dot(ik_scr[pl.ds(kb, QB), :], iq_st, preferred_element_type=F32)
            r = jnp.maximum(d, 0.0) * w_st
            sc = r[:, 0:LANES]
            for h in range(1, IDX_HEADS):
                sc = sc + r[:, LANES * h:LANES * (h + 1)]
            kpos = sub_io + kb
            key = jnp.where(jnp.abs(sc) < FLT_MIN_NORMAL, seq - kpos, _f2key(sc))
            causal = kpos <= sub_tq
            key_scr[pl.ds(kb, QB), :] = jnp.where(causal, key, INT_MIN)
            mx = jnp.maximum(mx, _tree_rows(jnp.where(causal, key, INT_MIN), jnp.maximum))
            mn = jnp.minimum(mn, _tree_rows(jnp.where(causal, key, INT_MAX), jnp.minimum))
        return mx, mn

    mx8, mn8 = lax.fori_loop(0, n_chunks, score_body,
                             (jnp.full((8, LANES), INT_MIN, I32), jnp.full((8, LANES), INT_MAX, I32)))

    def count(pred):
        def body(cidx, acc):
            k0 = pl.multiple_of(cidx * KC, KC)
            return acc + _tree_rows(pred(key_scr[pl.ds(k0, KC), :], k_io + k0), jnp.add)
        acc8 = lax.fori_loop(0, n_chunks, body, jnp.zeros((8, LANES), F32))
        return jnp.sum(acc8, axis=0, keepdims=True)

    topf = float(top)
    lo0 = functools.reduce(jnp.minimum, [mn8[r:r + 1, :] for r in range(8)])
    hi0 = functools.reduce(jnp.maximum, [mx8[r:r + 1, :] for r in range(8)]) + 1
    n_causal = (i * QB + 1 + lax.broadcasted_iota(I32, (1, LANES), 1)).astype(F32)
    log_top = float(np.log(top))

    def in_zero_band(lo, hi):
        return jnp.where(lo >= 1, jnp.where(hi <= ZERO_BAND_END, 1.0, 0.0), 0.0) > 0.5

    def pending(lo, hi, c_lo):
        return (jnp.max(jnp.where(c_lo > topf, jnp.where(hi > lo + 1, 1.0, 0.0), 0.0)) > 0.5).astype(I32)

    def search_cond(c):
        return jnp.logical_and(c[0] < SEARCH_MAX_PASSES, c[1] > 0)

    def search_body(c):
        it, _, lo, hi, c_lo, c_hi, kept, w_lo, w_hi = c
        f_lo = (jnp.log(c_lo) - log_top) * w_lo
        f_hi = (log_top - jnp.log(jnp.maximum(c_hi, 0.5))) * w_hi
        frac = f_lo / (f_lo + f_hi)
        v_lo, v_hi = _key2f(lo), _key2f(hi)
        cand = _f2key(v_lo + frac * (v_hi - v_lo))
        lo_f, hi_f = lo.astype(F32), hi.astype(F32)
        cand = jnp.where(in_zero_band(lo, hi), (lo_f + frac * (hi_f - lo_f)).astype(I32), cand)
        key_mid = (lo >> 1) + (hi >> 1) + (lo & hi & 1)
        cand = jnp.where(jnp.abs(kept) >= SEARCH_KEPT_LIMIT, key_mid, cand)
        cand = jnp.where(it >= SEARCH_BISECT_FROM, key_mid, cand)
        cand = jnp.where(it == 0, 1, jnp.where(it == 1, ZERO_BAND_END, cand))
        cand = jnp.minimum(jnp.maximum(cand, lo + 1), hi - 1)
        cnt = count(lambda key, kpos: jnp.where(key >= cand, 1.0, 0.0))
        active = jnp.where(c_lo > topf, jnp.where(hi > lo + 1, 1.0, 0.0), 0.0) > 0.5
        up = jnp.where(active, jnp.where(cnt >= topf, 1.0, 0.0), 0.0) > 0.5
        dn = jnp.where(active, jnp.where(cnt >= topf, 0.0, 1.0), 0.0) > 0.5
        w_hi = jnp.where(up, jnp.where(kept < 0, 0.5 * w_hi, 1.0), jnp.where(dn, 1.0, w_hi))
        w_lo = jnp.where(dn, jnp.where(kept > 0, 0.5 * w_lo, 1.0), jnp.where(up, 1.0, w_lo))
        kept = jnp.where(up, jnp.where(kept < 0, kept - 1, -1), jnp.where(dn, jnp.where(kept > 0, kept + 1, 1), kept))
        lo, c_lo = jnp.where(up, cand, lo), jnp.where(up, cnt, c_lo)
        hi, c_hi = jnp.where(dn, cand, hi), jnp.where(dn, cnt, c_hi)
        hi = jnp.where(in_zero_band(lo, hi), jnp.minimum(hi, seq + 1), hi)
        return it + 1, pending(lo, hi, c_lo), lo, hi, c_lo, c_hi, kept, w_lo, w_hi

    zero_f, one_f = jnp.zeros((1, LANES), F32), jnp.ones((1, LANES), F32)
    _, _, thr, _, c_lo, c_hi, _, _, _ = lax.while_loop(
        search_cond, search_body,
        (jnp.int32(0), pending(lo0, hi0, n_causal), lo0, hi0, n_causal, zero_f, jnp.zeros((1, LANES), I32),
         one_f, one_f))

    tied = c_lo > topf
    need1 = (topf - 1.0) - c_hi
    all_pos = jnp.full((1, LANES), 2 ** idx_bits - 1, I32)

    def tie_search():
        def tie_body(bi, cur):
            cand = cur | lax.shift_left(jnp.int32(1), idx_bits - 1 - bi)
            cnt = count(lambda key, kpos: jnp.where(key == thr, jnp.where(kpos < cand, 1.0, 0.0), 0.0))
            return jnp.where(cnt <= need1, cand, cur)
        return lax.fori_loop(0, idx_bits, tie_body, jnp.zeros((1, LANES), I32))

    any_tied = jnp.max(jnp.where(tied, 1.0, 0.0)) > 0.5
    last = jnp.where(tied, lax.cond(any_tied, tie_search, lambda: all_pos), all_pos)

    half_neg = int(np.float32(0.5 * NEG).view(np.int32))
    key_floor = half_neg ^ 0x7FFFFFFF

    def att_body(cidx, carry):
        m, l, acc = carry
        k0 = pl.multiple_of(cidx * KC, KC)
        hk = KC // ATT_SPLIT
        ss, m_new = [], m
        for sp in range(ATT_SPLIT):
            kb = pl.multiple_of(k0 + sp * hk, hk)
            key = key_scr[pl.ds(kb, hk), :]
            kpos = k_io[:hk] + kb
            bias = jnp.where(key > thr, 0.0, jnp.where(key == thr, jnp.where(kpos <= last, 0.0, NEG), NEG))
            bias = jnp.where(key > key_floor, bias, NEG)
            s = (jnp.dot(k_scr[pl.ds(kb, hk), :], qr_st, preferred_element_type=F32)
                 + jnp.concatenate([bias] * H, axis=1))
            ss.append(s)
            m_new = jnp.maximum(m_new, jnp.max(s, axis=0, keepdims=True))
        alpha = jnp.exp2(m - m_new)
        l, acc = alpha * l, alpha * acc
        for sp in range(ATT_SPLIT):
            kb = pl.multiple_of(k0 + sp * hk, hk)
            e = jnp.exp2(ss[sp] - m_new)
            l = l + jnp.sum(e, axis=0, keepdims=True)
            acc = acc + jnp.dot(vt_scr[:, pl.ds(kb, hk)], e.astype(BF16), preferred_element_type=F32)
        return m_new, l, acc

    init = (jnp.full((1, HW), NEG, F32), jnp.zeros((1, HW), F32), jnp.zeros((HEAD_DIM, HW), F32))
    m_a, l_a, acc_a = lax.fori_loop(0, n_chunks, att_body, init)
    o = jnp.where(m_a > 0.5 * NEG, acc_a / l_a, 0.0)
    outs = [o[:, LANES * h:LANES * (h + 1)] for h in range(H)]
    outs.append(jnp.zeros((o_ref.shape[1] - H * HEAD_DIM, LANES), F32))
    o_ref[...] = jnp.concatenate(outs, axis=0).T.astype(BF16)


def _dsa(proj, tabs, tabs_i, batch, seq):
    nb = seq // QB
    col = lambda name, w: _OFF[name] // w
    qmap = lambda cidx: (lambda b, i: (b * nb + i, cidx))
    tmap = lambda k: (lambda b, i: (k, i, 0))
    top = min(DSA_TOPK_MAX, seq // 4)
    kern = functools.partial(_dsa_kernel, top=top, idx_bits=int(seq).bit_length())
    return pl.pallas_call(
        kern,
        grid=(batch, nb),
        in_specs=[pl.BlockSpec((QB, 384), qmap(col("b_q", 384))),
                  pl.BlockSpec((QB, 256), qmap(col("b_iq", 256))),
                  pl.BlockSpec((QB, LANES), qmap(col("b_ikiw", LANES))),
                  pl.BlockSpec((QB, LANES), qmap(col("b_kv", LANES))),
                  pl.BlockSpec((None, QB, LANES), tmap(0)),
                  pl.BlockSpec((None, QB, LANES), tmap(1)),
                  pl.BlockSpec((None, QB, LANES), tmap(2)),
                  pl.BlockSpec((None, QB, LANES), tmap(0)),
                  pl.BlockSpec((None, QB, LANES), tmap(1)),
                  pl.BlockSpec((None, QB, LANES), tmap(2))],
        out_specs=pl.BlockSpec((QB, 384), lambda b, i: (b * nb + i, 0)),
        out_shape=jax.ShapeDtypeStruct((batch * seq, 384), BF16),
        scratch_shapes=[pltpu.VMEM((seq, LANES), BF16), pltpu.VMEM((HEAD_DIM, seq), BF16),
                        pltpu.VMEM((seq, LANES), BF16), pltpu.VMEM((seq, LANES), I32)],
        compiler_params=_params(2),
        name="dsa",
    )(proj, proj, proj, proj, tabs, tabs, tabs, tabs_i, tabs_i, tabs_i)


DIL_MB = max(d for _, d in DIL_PAIRS) * QB
DIL_VMEM_LIMIT = 56 * 1024 * 1024
DIL_UNROLL = 4


def _dil_layout(g):
    dil = DIL_PAIRS[g][1]
    per = DIL_MB // dil
    return dil, per, per + QB


def _dil_bias_table():
    u = np.arange(2 * QB)[:, None]
    diff = QB + np.arange(LANES)[None, :] - u
    ok = (diff >= 0) & (diff <= QB)
    return _bias_table(np.concatenate([ok, ok & (u >= QB)], axis=0))


def _dil_kernel(*refs):
    G, HG = len(DIL_PAIRS), DIL_HEADS_PER_GROUP
    q_refs, k_refs, v_refs = refs[0:G], refs[G:2 * G], refs[2 * G:3 * G]
    c_ref, s1_ref, s2_ref, bias_ref, o_ref = refs[3 * G:3 * G + 5]
    kds, vds, ogs, lss = (refs[3 * G + 5 + n * G:3 * G + 5 + (n + 1) * G] for n in range(4))
    step = pl.program_id(1)
    half = ROPE_DIM // 2

    @pl.when(step == 0)
    def _zero():
        for g in range(G):
            kds[g][...] = jnp.zeros_like(kds[g])
            vds[g][...] = jnp.zeros_like(vds[g])

    @pl.when(step > 0)
    def _carry_halo():
        for g in range(G):
            dil, per, stride = _dil_layout(g)
            for r in range(dil):
                b0 = r * stride
                kds[g][b0:b0 + QB, :] = kds[g][b0 + per:b0 + per + QB, :]
                vds[g][:, b0:b0 + QB] = vds[g][:, b0 + per:b0 + per + QB]

    for g in range(G):
        dil, per, stride = _dil_layout(g)
        piece = min(per, KC)
        for r in range(dil):
            for p0 in range(0, per, piece):
                rows = pl.ds(r + dil * p0, piece, stride=dil)
                kr = _rope(k_refs[g][rows, :], c_ref[rows, :], s1_ref[rows, :], s2_ref[rows, :], half)
                d0 = r * stride + QB + p0
                kds[g][d0:d0 + piece, :] = kr.astype(BF16)
                vds[g][:, d0:d0 + piece] = v_refs[g][rows, :].T.astype(BF16)

    zero = jnp.zeros((HEAD_DIM, LANES), F32)
    for g in range(G):
        dil, per, stride = _dil_layout(g)
        nblk = per // QB

        def block(jb, _, r=0, g=g, dil=dil, stride=stride):
            p0 = pl.multiple_of(dil * QB * jb, dil * QB)
            win = pl.ds(p0, dil * QB)
            rows = pl.ds(r, QB, stride=dil)
            q = _rope(q_refs[g].at[win, :][rows, :] * (SCALE * LOG2E), c_ref.at[win, :][rows, :],
                      s1_ref.at[win, :][rows, :], s2_ref.at[win, :][rows, :], half)
            qt = q.T
            q2 = jnp.concatenate([jnp.concatenate([qt[:HEAD_DIM], zero], axis=0),
                                  jnp.concatenate([zero, qt[HEAD_DIM:]], axis=0)], axis=1).astype(BF16)
            kb = pl.multiple_of(r * stride + QB * jb, QB)
            first = jnp.logical_and(step == 0, jb == 0)
            bias = bias_ref[pl.ds(pl.multiple_of(jnp.where(first, 2 * QB, 0), QB), 2 * QB), :]
            s = (jnp.dot(kds[g][pl.ds(kb, 2 * QB), :], q2, preferred_element_type=F32)
                 + jnp.concatenate([bias] * HG, axis=1))
            e, m, den = _softmax2_cols(s)
            lse = m + jnp.log2(den)
            rden = 1.0 / den
            parts, lparts = [], []
            for hg in range(HG):
                sl = slice(LANES * hg, LANES * (hg + 1))
                vt = vds[g][HEAD_DIM * hg:HEAD_DIM * (hg + 1), pl.ds(kb, 2 * QB)]
                parts.append(jnp.dot(vt, e[:, sl].astype(BF16), preferred_element_type=F32) * rden[:, sl])
                lparts.append(jnp.broadcast_to(lse[:, sl], (HEAD_DIM, LANES)))
            tile = jnp.concatenate(parts + lparts, axis=0).T
            ogs[g].at[win, :][rows, :] = tile[:, :LANES]
            lss[g].at[win, :][rows, :] = tile[:, LANES:]
            return 0

        for r in range(dil):
            lax.fori_loop(0, nblk, functools.partial(block, r=r), 0, unroll=min(nblk, DIL_UNROLL))

    def mix(ti, _):
        rows = pl.ds(pl.multiple_of(ti * KC, KC), KC)
        ls = [lss[g][rows, :] for g in range(G)]
        mx = functools.reduce(jnp.maximum, ls)
        ex = [jnp.exp2(x - mx) for x in ls]
        rtot = 1.0 / functools.reduce(lambda a, b: a + b, ex)
        o_ref[rows, :] = jnp.concatenate([ex[g] * rtot * ogs[g][rows, :] for g in range(G)], axis=1).astype(BF16)
        return 0

    lax.fori_loop(0, DIL_MB // KC, mix, 0)


def _dil(proj, tabs, batch, seq):
    assert seq % DIL_MB == 0 and all(w // d == QB for w, d in DIL_PAIRS)
    nm = seq // DIL_MB
    G = len(DIL_PAIRS)
    w = DIL_HEADS * HEAD_DIM
    gmap = lambda name, g: (lambda b, j: (b * nm + j, _OFF[name] // LANES + g))
    tmap = lambda k: (lambda b, j: (k, j, 0))
    bias = _dil_bias_table()
    rows = [_dil_layout(g)[0] * _dil_layout(g)[2] for g in range(G)]
    return pl.pallas_call(
        _dil_kernel,
        grid=(batch, nm),
        in_specs=([pl.BlockSpec((DIL_MB, LANES), gmap(name, g)) for name in ("c_q", "c_k", "c_v") for g in range(G)]
                  + [pl.BlockSpec((None, DIL_MB, LANES), tmap(k)) for k in range(3)]
                  + [pl.BlockSpec(bias.shape, lambda b, j: (0, 0))]),
        out_specs=pl.BlockSpec((DIL_MB, w), lambda b, j: (b * nm + j, 0)),
        out_shape=jax.ShapeDtypeStruct((batch * seq, w), BF16),
        scratch_shapes=([pltpu.VMEM((n, LANES), BF16) for n in rows] + [pltpu.VMEM((LANES, n), BF16) for n in rows]
                        + [pltpu.VMEM((DIL_MB, LANES), F32)] * (2 * G)),
        compiler_params=_params(2, DIL_VMEM_LIMIT),
        name="dilated",
    )(*([proj] * (3 * G)), tabs, tabs, tabs, bias)


def _overlap_t(seq):
    n_cmp_pad = seq // CMP_STRIDE
    n_slc = seq // SLC_BLOCK
    c_start = np.arange(n_cmp_pad) * CMP_STRIDE
    s_start = np.arange(n_slc) * SLC_BLOCK
    ov = ((c_start[None, :] < s_start[:, None] + SLC_BLOCK) & (c_start[None, :] + CMP_BLOCK > s_start[:, None]))
    return jnp.asarray(ov.astype(np.float32), dtype=BF16)


def _layer(x2, batch, seq, norm1_g, w_in, cmp_pos, cmp_w1, cmp_w2, w_out, norm2_g, w_up, conv_w, conv_b, w_down,
           final_g, final_norm, tabs, tabs_i, ovt):
    n = batch * seq
    proj = _inproj(x2, norm1_g, _pad_w_in(w_in).astype(BF16))

    o_kc = _OFF["a_kcvc"]
    rows = seq // CMP_STRIDE
    kc_r = proj[:, o_kc:o_kc + HEAD_DIM].reshape(batch, rows, CMP_STRIDE * HEAD_DIM)
    vc_r = proj[:, o_kc + HEAD_DIM:o_kc + 2 * HEAD_DIM].reshape(batch, rows, CMP_STRIDE * HEAD_DIM)
    pos = cmp_pos.reshape(2, 2, CMP_STRIDE * HEAD_DIM)
    w1 = cmp_w1.reshape(2, 2, CMP_STRIDE * HEAD_DIM, CMP_HIDDEN).astype(BF16)
    zpad = jnp.zeros((CMP_HIDDEN, HEAD_DIM), cmp_w2.dtype)
    w2 = jnp.stack([jnp.concatenate([cmp_w2[0], zpad], axis=1),
                    jnp.concatenate([zpad, cmp_w2[1]], axis=1)]).astype(BF16)
    kvcmp = _nsa_compress(kc_r, vc_r, pos, w1, w2)

    o_a = _nsa(proj, kvcmp, tabs, ovt, batch, seq)
    o_b = _dsa(proj, tabs, tabs_i, batch, seq)
    o_c = _dil(proj, tabs, batch, seq)

    hp = NSA_HEADS * HEAD_DIM
    zrow = jnp.zeros((384 - hp, D_MODEL), w_out.dtype)
    w_out_pad = jnp.concatenate([w_out[0:hp], zrow, w_out[hp:2 * hp], zrow, w_out[2 * hp:]], axis=0).astype(BF16)
    x2 = _outproj(o_a, o_b, o_c, w_out_pad, x2)

    act = _ffn_up(x2, norm2_g, w_up.astype(BF16), conv_w, conv_b, seq)
    return _ffn_down(act, w_down.astype(BF16), x2, final_g, final_norm)


def kernel(x, norm1_g, w_in, cmp_pos, cmp_w1, cmp_w2, w_out, norm2_g, w_up, conv_w, conv_b, w_down, final_g):
    batch, seq, d = x.shape
    depth = w_in.shape[0]
    tabs = _rope_lane_tables(seq, HEAD_DIM, ROPE_DIM)
    tabs_i = _rope_lane_tables(seq, IDX_DIM, IDX_ROPE_DIM)
    ovt = _overlap_t(seq)
    x2 = x.reshape(batch * seq, d)
    for li in range(depth):
        x2 = _layer(x2, batch, seq, norm1_g[li], w_in[li], cmp_pos[li], cmp_w1[li], cmp_w2[li], w_out[li],
                    norm2_g[li], w_up[li], conv_w[li], conv_b[li], w_down[li], final_g, li == depth - 1,
                    tabs, tabs_i, ovt)
    return x2.reshape(batch, seq, d)
```

```python
import functools

import numpy as np
import jax
import jax.numpy as jnp
from jax import lax
from jax.experimental import pallas as pl
from jax.experimental.pallas import tpu as pltpu

F32 = jnp.float32
BF16 = jnp.bfloat16
I32 = jnp.int32

D_MODEL = 1024
HEAD_DIM = 64
ROPE_DIM = HEAD_DIM // 4
ROPE_THETA = 500000.0
NORM_EPS = 1e-6
SCALE = HEAD_DIM ** -0.5
LOG2E = 1.4426950408889634
NEG = -1e30
FORCE = 1e9
NSA_HEADS = 5
CMP_BLOCK = 32
CMP_STRIDE = 16
CMP_HIDDEN = 128
SLC_BLOCK = 64
SLC_TOPN = 16
WIN = 512
DSA_HEADS = 5
IDX_HEADS = 8
IDX_DIM = 32
IDX_ROPE_DIM = IDX_DIM // 4
DSA_TOPK_MAX = 256
DIL_PAIRS = ((128, 1), (512, 4), (2048, 16))
DIL_HEADS_PER_GROUP = 2
DIL_HEADS = len(DIL_PAIRS) * DIL_HEADS_PER_GROUP
D_FF = 2816
CONV_WIDTH = 3

LANES = 128
QB = 128
DSA_QW = 256
KC = 512
ATT_SPLIT = 1
ROW_TILE = 512
HALO = 16
VMEM_LIMIT = 48 * 1024 * 1024
INT_MIN = -2 ** 31

_A0 = 0
_B0 = 719
_C0 = 1463
_SEGS = (
    ("c_q", _C0, 384, 384), ("c_k", _C0 + 384, 384, 384), ("c_v", _C0 + 768, 384, 384),
    ("a_q", _A0, 320, 384), ("b_q", _B0, 320, 384),
    ("a_kcvc", _A0 + 320, 128, 128),
    ("b_iq", _B0 + 448, 256, 256),
    ("a_ksvs", _A0 + 448, 128, 128), ("a_kwvw", _A0 + 576, 128, 128),
    ("a_g", _A0 + 704, 15, 128),
    ("b_kv", _B0 + 320, 128, 128),
    ("b_ikiw", _B0 + 704, 40, 128),
)
P_COLS = sum(s[3] for s in _SEGS)


def _seg_offsets():
    offs, o = {}, 0
    for name, _, _, pw in _SEGS:
        offs[name] = o
        o += pw
    return offs


_OFF = _seg_offsets()


def _pad_w_in(w):
    parts = []
    for _, src, wdt, pw in _SEGS:
        parts.append(w[:, src:src + wdt])
        if pw > wdt:
            parts.append(jnp.zeros((w.shape[0], pw - wdt), w.dtype))
    return jnp.concatenate(parts, axis=1)


def _rope_lane_tables(L, head_dim, rot_dim):
    half = rot_dim // 2
    inv = 1.0 / (ROPE_THETA ** (np.arange(0, rot_dim, 2, dtype=np.float32) / np.float32(rot_dim)))
    ang = np.arange(L, dtype=np.float32)[:, None] * inv[None, :]
    cos, sin = np.cos(ang).astype(np.float32), np.sin(ang).astype(np.float32)
    d = np.arange(LANES) % head_dim
    lo, hi = d < half, (d >= half) & (d < rot_dim)
    c = np.ones((L, LANES), np.float32)
    s1 = np.zeros((L, LANES), np.float32)
    s2 = np.zeros((L, LANES), np.float32)
    c[:, lo] = cos[:, d[lo]]
    c[:, hi] = cos[:, d[hi] - half]
    s1[:, lo] = -sin[:, d[lo]]
    s2[:, hi] = sin[:, d[hi] - half]
    return jnp.asarray(np.stack([c, s1, s2]))


def _rope(x, c, s1, s2, half):
    xp = pltpu.roll(x, LANES - half, 1)
    xm = pltpu.roll(x, half, 1)
    return x * c + xp * s1 + xm * s2


def _rope_wide(x, c, s1, s2, half):
    n = x.shape[1] // LANES
    return jnp.concatenate([_rope(x[:, LANES * s:LANES * (s + 1)], c, s1, s2, half) for s in range(n)], axis=1)


def _softmax2_cols(s):
    m = jnp.max(s, axis=0, keepdims=True)
    e = jnp.exp2(s - m)
    return e, m, jnp.sum(e, axis=0, keepdims=True)


def _bias_table(ok):
    return jnp.asarray(np.where(ok, 0.0, NEG).astype(np.float32))


def _cmp_bias_table(seq):
    shift = (QB // CMP_STRIDE) * (seq // QB - 1)
    u = np.arange(seq // CMP_STRIDE + shift)[:, None]
    lane = np.arange(LANES)[None, :]
    return _bias_table(CMP_STRIDE * (u - shift) + CMP_BLOCK - 1 <= lane)


def _win_bias_table(seq):
    span = min(WIN + QB, seq)
    cmax = span - QB
    u = np.arange(span + cmax)[:, None]
    diff = cmax + np.arange(LANES)[None, :] - u
    return _bias_table((diff >= 0) & (diff < WIN))


def _causal_bias_table():
    u = np.arange(2 * KC)[:, None]
    return _bias_table(u <= KC + np.arange(LANES)[None, :])


def _params(n_grid, vmem=VMEM_LIMIT):
    return pltpu.CompilerParams(dimension_semantics=("arbitrary",) * n_grid, vmem_limit_bytes=vmem)


def _rmsnorm_rows(x, g):
    return x * lax.rsqrt(jnp.mean(x * x, axis=-1, keepdims=True) + NORM_EPS) * g


def _inproj_kernel(x_ref, g_ref, w_ref, o_ref):
    hn = _rmsnorm_rows(x_ref[...], g_ref[...]).astype(BF16)
    n = w_ref.shape[1]
    for c0 in range(0, n, 512):
        c1 = min(c0 + 512, n)
        o_ref[:, c0:c1] = jnp.dot(hn, w_ref[:, c0:c1], preferred_element_type=F32)


def _inproj(x2, g, w_bf):
    n, d = x2.shape
    pc = w_bf.shape[1]
    return pl.pallas_call(
        _inproj_kernel,
        grid=(n // ROW_TILE,),
        in_specs=[pl.BlockSpec((ROW_TILE, d), lambda i: (i, 0)),
                  pl.BlockSpec((1, d), lambda i: (0, 0)),
                  pl.BlockSpec((d, pc), lambda i: (0, 0))],
        out_specs=pl.BlockSpec((ROW_TILE, pc), lambda i: (i, 0)),
        out_shape=jax.ShapeDtypeStruct((n, pc), F32),
        compiler_params=_params(1),
        name="inproj",
    )(x2, g.reshape(1, d), w_bf)


def _outproj_kernel(oa_ref, ob_ref, oc_ref, w_ref, x_ref, o_ref):
    mix = jnp.concatenate([oa_ref[...], ob_ref[...], oc_ref[...]], axis=1)
    o_ref[...] = x_ref[...] + jnp.dot(mix, w_ref[...], preferred_element_type=F32)


def _outproj(oa, ob, oc, w_bf, x2):
    n, d = x2.shape
    k = w_bf.shape[0]
    mw = oa.shape[1]
    return pl.pallas_call(
        _outproj_kernel,
        grid=(n // ROW_TILE,),
        in_specs=[pl.BlockSpec((ROW_TILE, mw), lambda i: (i, 0)),
                  pl.BlockSpec((ROW_TILE, mw), lambda i: (i, 0)),
                  pl.BlockSpec((ROW_TILE, mw), lambda i: (i, 0)),
                  pl.BlockSpec((k, d), lambda i: (0, 0)),
                  pl.BlockSpec((ROW_TILE, d), lambda i: (i, 0))],
        out_specs=pl.BlockSpec((ROW_TILE, d), lambda i: (i, 0)),
        out_shape=jax.ShapeDtypeStruct((n, d), F32),
        compiler_params=_params(1),
        name="outproj",
    )(oa, ob, oc, w_bf, x2)


def _ffn_up_kernel(x_ref, xh_ref, g_ref, wa_ref, wu_ref, cw_ref, cb_ref, o_ref, hn_scr, hh_scr, a_scr, *, tiles_per_seq):
    i, j = pl.program_id(0), pl.program_id(1)
    tm = x_ref.shape[0]

    @pl.when(j == 0)
    def _norm():
        hn_scr[...] = _rmsnorm_rows(x_ref[...], g_ref[...]).astype(BF16)
        keep = jnp.where(i % tiles_per_seq == 0, 0.0, 1.0)
        hh_scr[...] = (_rmsnorm_rows(xh_ref[...], g_ref[...]) * keep).astype(BF16)

    hn = hn_scr[...]
    a = jnp.dot(hn, wa_ref[...], preferred_element_type=F32)
    u = jnp.dot(hn, wu_ref[...], preferred_element_type=F32)
    a_scr[0:HALO, :] = jnp.dot(hh_scr[...], wa_ref[...], preferred_element_type=F32)
    a_scr[HALO:HALO + tm, :] = a
    a1 = a_scr[pl.ds(HALO - 1, tm), :]
    a2 = a_scr[pl.ds(HALO - 2, tm), :]
    cw = cw_ref[...]
    conv = cw[0:1, :] * a2 + cw[1:2, :] * a1 + cw[2:3, :] * a + cb_ref[...]
    o_ref[...] = (conv * jax.nn.sigmoid(conv) * u).astype(BF16)


def _ffn_up(x2, g, w_up_bf, conv_w, conv_b, seq_len):
    n, d = x2.shape
    tn = D_FF // 2
    nj = D_FF // tn
    hb = ROW_TILE // HALO
    kern = functools.partial(_ffn_up_kernel, tiles_per_seq=seq_len // ROW_TILE)
    return pl.pallas_call(
        kern,
        grid=(n // ROW_TILE, nj),
        in_specs=[pl.BlockSpec((ROW_TILE, d), lambda i, j: (i, 0)),
                  pl.BlockSpec((HALO, d), lambda i, j: (jnp.maximum(i * hb - 1, 0), 0)),
                  pl.BlockSpec((1, d), lambda i, j: (0, 0)),
                  pl.BlockSpec((d, tn), lambda i, j: (0, j)),
                  pl.BlockSpec((d, tn), lambda i, j: (0, j + nj)),
                  pl.BlockSpec((CONV_WIDTH, tn), lambda i, j: (0, j)),
                  pl.BlockSpec((1, tn), lambda i, j: (0, j))],
        out_specs=pl.BlockSpec((ROW_TILE, tn), lambda i, j: (i, j)),
        out_shape=jax.ShapeDtypeStruct((n, D_FF), BF16),
        scratch_shapes=[pltpu.VMEM((ROW_TILE, d), BF16), pltpu.VMEM((HALO, d), BF16),
                        pltpu.VMEM((ROW_TILE + HALO, tn), F32)],
        compiler_params=_params(2),
        name="ffn_up",
    )(x2, x2, g.reshape(1, d), w_up_bf, w_up_bf, conv_w, conv_b.reshape(1, D_FF))


def _ffn_down_kernel(a_ref, w_ref, x_ref, g_ref, o_ref, *, final_norm):
    y = x_ref[...] + jnp.dot(a_ref[...], w_ref[...], preferred_element_type=F32)
    if final_norm:
        y = _rmsnorm_rows(y, g_ref[...])
    o_ref[...] = y


def _ffn_down(act, w_bf, x2, final_g, final_norm):
    n, d = x2.shape
    k = act.shape[1]
    return pl.pallas_call(
        functools.partial(_ffn_down_kernel, final_norm=final_norm),
        grid=(n // ROW_TILE,),
        in_specs=[pl.BlockSpec((ROW_TILE, k), lambda i: (i, 0)),
                  pl.BlockSpec((k, d), lambda i: (0, 0)),
                  pl.BlockSpec((ROW_TILE, d), lambda i: (i, 0)),
                  pl.BlockSpec((1, d), lambda i: (0, 0))],
        out_specs=pl.BlockSpec((ROW_TILE, d), lambda i: (i, 0)),
        out_shape=jax.ShapeDtypeStruct((n, d), F32),
        compiler_params=_params(1),
        name="ffn_down",
    )(act, w_bf, x2, final_g.reshape(1, d))


def _cmp_kernel(kc_ref, vc_ref, pos_ref, w1_ref, w2_ref, o_ref):
    out = None
    for c, src in enumerate((kc_ref, vc_ref)):
        r = src[0]
        lo = jnp.dot((r + pos_ref[c, 0:1, :]).astype(BF16), w1_ref[c, 0], preferred_element_type=F32)
        hi = jnp.dot((r + pos_ref[c, 1:2, :]).astype(BF16), w1_ref[c, 1], preferred_element_type=F32)
        nrows = hi.shape[0]
        hid = jax.nn.gelu(lo + pltpu.roll(hi, nrows - 1, 0))
        y = jnp.dot(hid.astype(BF16), w2_ref[c], preferred_element_type=F32)
        out = y if out is None else out + y
    o_ref[0] = out


def _nsa_compress(kc_r, vc_r, pos, w1_bf, w2_pad_bf):
    b, nr, wd = kc_r.shape
    return pl.pallas_call(
        _cmp_kernel,
        grid=(b,),
        in_specs=[pl.BlockSpec((1, nr, wd), lambda i: (i, 0, 0)),
                  pl.BlockSpec((1, nr, wd), lambda i: (i, 0, 0)),
                  pl.BlockSpec(pos.shape, lambda i: (0, 0, 0)),
                  pl.BlockSpec(w1_bf.shape, lambda i: (0, 0, 0, 0)),
                  pl.BlockSpec(w2_pad_bf.shape, lambda i: (0, 0, 0))],
        out_specs=pl.BlockSpec((1, nr, LANES), lambda i: (i, 0, 0)),
        out_shape=jax.ShapeDtypeStruct((b, nr, LANES), F32),
        compiler_params=_params(1),
        name="nsa_compress",
    )(kc_r, vc_r, pos, w1_bf, w2_pad_bf)


def _stack_heads_t(x_t, n_heads, hd):
    z = jnp.zeros((LANES - hd, x_t.shape[1]), F32)
    return jnp.concatenate(
        [jnp.concatenate([x_t[hd * h:hd * (h + 1), :], z], axis=0) for h in range(n_heads)], axis=1)


def _nsa_kernel(q_ref, g_ref, ksvs_ref, kwvw_ref, kvc_ref, c_ref, s1_ref, s2_ref, ovt_ref, cmpb_ref, winb_ref,
                caub_ref, o_ref, ks_scr, vst_scr, kw_scr, vwt_scr, kc_scr, vct_scr, blk_scr, *, n_top):
    i = pl.program_id(1)
    H = NSA_HEADS
    HW = H * LANES
    seq = ks_scr.shape[0]
    lane_q = lax.broadcasted_iota(I32, (QB, LANES), 1)

    @pl.when(i == 0)
    def _init():
        ks_scr[...] = jnp.zeros_like(ks_scr)
        kw_scr[...] = jnp.zeros_like(kw_scr)
        vst_scr[...] = jnp.zeros_like(vst_scr)
        vwt_scr[...] = jnp.zeros_like(vwt_scr)
        kvc = kvc_ref[0]
        lane_c = lax.broadcasted_iota(I32, kvc.shape, 1)
        kc_scr[...] = jnp.where(lane_c < HEAD_DIM, kvc, 0.0).astype(BF16)
        vct_scr[...] = kvc.T[HEAD_DIM:2 * HEAD_DIM, :].astype(BF16)

    c, s1, s2 = c_ref[...], s1_ref[...], s2_ref[...]
    r0 = pl.multiple_of(i * QB, QB)
    for src, kdst, vdst in ((ksvs_ref, ks_scr, vst_scr), (kwvw_ref, kw_scr, vwt_scr)):
        kv = src[...]
        kr = _rope(kv, c, s1, s2, ROPE_DIM // 2)
        kdst[pl.ds(r0, QB), :] = jnp.where(lane_q < HEAD_DIM, kr, 0.0).astype(BF16)
        vdst[:, pl.ds(r0, QB)] = kv.T[HEAD_DIM:2 * HEAD_DIM, :].astype(BF16)

    q = q_ref[...] * (SCALE * LOG2E)
    qr = _rope_wide(q, c, s1, s2, ROPE_DIM // 2)
    q_st = _stack_heads_t(q.T, H, HEAD_DIM).astype(BF16)
    qr_st = _stack_heads_t(qr.T, H, HEAD_DIM).astype(BF16)
    tile_h = lambda b: jnp.concatenate([b] * H, axis=1)

    ncp = kc_scr.shape[0]
    cshift = cmpb_ref.shape[0] - ncp
    cb = cmpb_ref[pl.ds(pl.multiple_of(cshift - (QB // CMP_STRIDE) * i, 8), ncp), :]
    s_c = jnp.dot(kc_scr[...], q_st, preferred_element_type=F32) + tile_h(cb)
    e_c, _, den_c = _softmax2_cols(s_c)
    t_row = i * QB + (lax.broadcasted_iota(I32, (1, HW), 1) & (LANES - 1))
    r_c = jnp.where(t_row >= CMP_BLOCK - 1, 1.0 / den_c, 0.0)
    o_cmp = jnp.dot(vct_scr[...], e_c.astype(BF16), preferred_element_type=F32) * r_c
    psum = e_c[:, 0:LANES] * r_c[:, 0:LANES]
    for h in range(1, H):
        psum = psum + e_c[:, LANES * h:LANES * (h + 1)] * r_c[:, LANES * h:LANES * (h + 1)]

    hi = psum.astype(BF16)
    r1 = psum - hi.astype(F32)
    mid = r1.astype(BF16)
    lo = (r1 - mid.astype(F32)).astype(BF16)
    ovt = ovt_ref[...]
    imp = (jnp.dot(ovt, hi, preferred_element_type=F32) + jnp.dot(ovt, mid, preferred_element_type=F32)
           + jnp.dot(ovt, lo, preferred_element_type=F32))
    n_slc = imp.shape[0]
    jb = lax.broadcasted_iota(I32, (n_slc, LANES), 0)
    tq = i * QB + lax.broadcasted_iota(I32, (n_slc, LANES), 1)
    cur = tq >> (SLC_BLOCK.bit_length() - 1)
    forced = (jb == 0) | (jb == cur) | (jb == cur - 1)
    val = jnp.where(forced, FORCE, jnp.where(jb <= cur, imp, NEG))
    rank = jnp.zeros((n_slc, LANES), I32)
    for jp in range(n_slc):
        row = val[jp:jp + 1, :]
        tie = jnp.where(jb > jp, 1, 0)
        rank = rank + jnp.where(row > val, 1, jnp.where(row == val, tie, 0))
    blk_scr[...] = jnp.where(rank < n_top, jnp.where(val > 0.5 * NEG, 0.0, NEG), NEG)

    bpc = KC // SLC_BLOCK

    def slc_body(ci, carry):
        m, l, acc = carry
        k0 = pl.multiple_of(ci * KC, KC)
        ahead = jnp.minimum(i * QB - k0, KC)
        c0 = pl.multiple_of(KC - ahead, QB)
        hk = KC // ATT_SPLIT
        ss, m_new = [], m
        for sp in range(ATT_SPLIT):
            kb = pl.multiple_of(k0 + sp * hk, hk)
            rows = [jnp.broadcast_to(blk_scr[pl.ds(ci * bpc + sp * (bpc // ATT_SPLIT) + r, 1), :], (SLC_BLOCK, LANES))
                    for r in range(bpc // ATT_SPLIT)]
            bias = jnp.concatenate(rows, axis=0) + caub_ref[pl.ds(pl.multiple_of(c0 + sp * hk, QB), hk), :]
            s = jnp.dot(ks_scr[pl.ds(kb, hk), :], qr_st, preferred_element_type=F32) + tile_h(bias)
            ss.append(s)
            m_new = jnp.maximum(m_new, jnp.max(s, axis=0, keepdims=True))
        alpha = jnp.exp2(m - m_new)
        l, acc = alpha * l, alpha * acc
        for sp in range(ATT_SPLIT):
            kb = pl.multiple_of(k0 + sp * hk, hk)
            e = jnp.exp2(ss[sp] - m_new)
            l = l + jnp.sum(e, axis=0, keepdims=True)
            acc = acc + jnp.dot(vst_scr[:, pl.ds(kb, hk)], e.astype(BF16), preferred_element_type=F32)
        return m_new, l, acc

    n_chunks = (i * QB + QB + KC - 1) // KC
    init = (jnp.full((1, HW), NEG, F32), jnp.zeros((1, HW), F32), jnp.zeros((HEAD_DIM, HW), F32))
    m_s, l_s, acc_s = lax.fori_loop(0, n_chunks, slc_body, init)
    o_slc = jnp.where(m_s > 0.5 * NEG, acc_s / l_s, 0.0)

    span = min(WIN + QB, seq)
    w0 = pl.multiple_of(jnp.maximum(i * QB + QB - span, 0), QB)
    wb = winb_ref[pl.ds(pl.multiple_of(span - QB - (i * QB - w0), QB), span), :]
    s_w = jnp.dot(kw_scr[pl.ds(w0, span), :], qr_st, preferred_element_type=F32) + tile_h(wb)
    e_w, _, den_w = _softmax2_cols(s_w)
    o_win = jnp.dot(vwt_scr[:, pl.ds(w0, span)], e_w.astype(BF16), preferred_element_type=F32) * (1.0 / den_w)

    gt = jax.nn.sigmoid(g_ref[...].T)
    outs = []
    for h in range(H):
        sl = slice(LANES * h, LANES * (h + 1))
        outs.append(gt[3 * h:3 * h + 1, :] * o_cmp[:, sl] + gt[3 * h + 1:3 * h + 2, :] * o_slc[:, sl]
                    + gt[3 * h + 2:3 * h + 3, :] * o_win[:, sl])
    outs.append(jnp.zeros((o_ref.shape[1] - H * HEAD_DIM, LANES), F32))
    o_ref[...] = jnp.concatenate(outs, axis=0).T.astype(BF16)


def _nsa(proj, kvcmp, tabs, ovt, batch, seq):
    nb = seq // QB
    n_slc = seq // SLC_BLOCK
    ncp = kvcmp.shape[1]
    col = lambda name, w: _OFF[name] // w
    qmap = lambda cidx: (lambda b, i: (b * nb + i, cidx))
    tmap = lambda k: (lambda b, i: (k, i, 0))
    kern = functools.partial(_nsa_kernel, n_top=min(SLC_TOPN, n_slc))
    cmpb, winb, caub = _cmp_bias_table(seq), _win_bias_table(seq), _causal_bias_table()
    whole = lambda a: pl.BlockSpec(a.shape, lambda b, i: (0, 0))
    return pl.pallas_call(
        kern,
        grid=(batch, nb),
        in_specs=[pl.BlockSpec((QB, 384), qmap(col("a_q", 384))),
                  pl.BlockSpec((QB, LANES), qmap(col("a_g", LANES))),
                  pl.BlockSpec((QB, LANES), qmap(col("a_ksvs", LANES))),
                  pl.BlockSpec((QB, LANES), qmap(col("a_kwvw", LANES))),
                  pl.BlockSpec((1, ncp, LANES), lambda b, i: (b, 0, 0)),
                  pl.BlockSpec((None, QB, LANES), tmap(0)),
                  pl.BlockSpec((None, QB, LANES), tmap(1)),
                  pl.BlockSpec((None, QB, LANES), tmap(2)),
                  whole(ovt), whole(cmpb), whole(winb), whole(caub)],
        out_specs=pl.BlockSpec((QB, 384), lambda b, i: (b * nb + i, 0)),
        out_shape=jax.ShapeDtypeStruct((batch * seq, 384), BF16),
        scratch_shapes=[pltpu.VMEM((seq, LANES), BF16), pltpu.VMEM((HEAD_DIM, seq), BF16),
                        pltpu.VMEM((seq, LANES), BF16), pltpu.VMEM((HEAD_DIM, seq), BF16),
                        pltpu.VMEM((ncp, LANES), BF16), pltpu.VMEM((HEAD_DIM, ncp), BF16),
                        pltpu.VMEM((n_slc, LANES), F32)],
        compiler_params=_params(2),
        name="nsa",
    )(proj, proj, proj, proj, kvcmp, tabs, tabs, tabs, ovt, cmpb, winb, caub)


SEARCH_BISECT_FROM = 24
SEARCH_MAX_PASSES = 64
SEARCH_KEPT_LIMIT = 6
INT_MAX = 2 ** 31 - 1
FLT_MIN_NORMAL = 1.17549435e-38
ZERO_BAND_END = 1 << 23


def _tree_rows(x, op):
    parts = [x[r:r + 8, :] for r in range(0, x.shape[0], 8)]
    while len(parts) > 1:
        nxt = [op(parts[a], parts[a + 1]) for a in range(0, len(parts) - 1, 2)]
        if len(parts) % 2:
            nxt.append(parts[-1])
        parts = nxt
    return parts[0]


def _f2key(v):
    bits = lax.bitcast_convert_type(v, I32)
    return bits ^ ((bits >> 31) & 0x7FFFFFFF)


def _key2f(k):
    return lax.bitcast_convert_type(k ^ ((k >> 31) & 0x7FFFFFFF), F32)


def _dsa_kernel(q_ref, iq_ref, ikw_ref, kv_ref, c_ref, s1_ref, s2_ref, ci_ref, si1_ref, si2_ref, o_ref,
                k_scr, vt_scr, ik_scr, key_scr, *, top, idx_bits):
    i = pl.program_id(1)
    H = DSA_HEADS
    QW = q_ref.shape[0]
    HW = H * QW
    seq = key_scr.shape[0]
    lane_q = lax.broadcasted_iota(I32, (QW, LANES), 1)

    @pl.when(i == 0)
    def _init():
        k_scr[...] = jnp.zeros_like(k_scr)
        vt_scr[...] = jnp.zeros_like(vt_scr)
        ik_scr[...] = jnp.zeros_like(ik_scr)

    c, s1, s2 = c_ref[...], s1_ref[...], s2_ref[...]
    ci, si1, si2 = ci_ref[...], si1_ref[...], si2_ref[...]
    r0 = pl.multiple_of(i * QW, QW)
    kv = kv_ref[...]
    k_scr[pl.ds(r0, QW), :] = jnp.where(lane_q < HEAD_DIM, _rope(kv, c, s1, s2, ROPE_DIM // 2), 0.0).astype(BF16)
    vt_scr[:, pl.ds(r0, QW)] = kv.T[HEAD_DIM:2 * HEAD_DIM, :].astype(BF16)
    ikw = ikw_ref[...]
    ik_scr[pl.ds(r0, QW), :] = jnp.where(lane_q < IDX_DIM, _rope(ikw, ci, si1, si2, IDX_ROPE_DIM // 2), 0.0).astype(BF16)

    qr = _rope_wide(q_ref[...] * (SCALE * LOG2E), c, s1, s2, ROPE_DIM // 2)
    qr_st = _stack_heads_t(qr.T, H, HEAD_DIM).astype(BF16)
    iqr = _rope_wide(iq_ref[...], ci, si1, si2, IDX_ROPE_DIM // 2)
    iq_st = _stack_heads_t(iqr.T, IDX_HEADS, IDX_DIM).astype(BF16)
    wt = ikw.T * ((IDX_DIM ** -0.5) * (IDX_HEADS ** -0.5))
    w_st = jnp.concatenate([wt[IDX_DIM + h:IDX_DIM + h + 1, :] for h in range(IDX_HEADS)], axis=1)

    n_chunks = (i * QW + QW + KC - 1) // KC
    k_io = lax.broadcasted_iota(I32, (KC, QW), 0)
    tq = i * QW + lax.broadcasted_iota(I32, (KC, QW), 1)

    sub_io = lax.broadcasted_iota(I32, (QB, QW), 0)
    sub_tq = i * QW + lax.broadcasted_iota(I32, (QB, QW), 1)

    def score_body(cidx, carry):
        mx, mn = carry
        k0 = pl.multiple_of(cidx * KC, KC)
        for sb in range(KC // QB):
            kb = pl.multiple_of(k0 + sb * QB, QB)
            d = jnp.dot(ik_scr[pl.ds(kb, QB), :], iq_st, preferred_element_type=F32)
            r = jnp.maximum(d, 0.0) * w_st
            sc = r[:, 0:QW]
            for h in range(1, IDX_HEADS):
                sc = sc + r[:, QW * h:QW * (h + 1)]
            kpos = sub_io + kb
            key = jnp.where(jnp.abs(sc) < FLT_MIN_NORMAL, seq - kpos, _f2key(sc))
            causal = kpos <= sub_tq
            key_scr[pl.ds(kb, QB), :] = jnp.where(causal, key, INT_MIN)
            mx = jnp.maximum(mx, _tree_rows(jnp.where(causal, key, INT_MIN), jnp.maximum))
            mn = jnp.minimum(mn, _tree_rows(jnp.where(causal, key, INT_MAX), jnp.minimum))
        return mx, mn

    mx8, mn8 = lax.fori_loop(0, n_chunks, score_body,
                             (jnp.full((8, QW), INT_MIN, I32), jnp.full((8, QW), INT_MAX, I32)))

    def count(pred):
        def body(cidx, acc):
            k0 = pl.multiple_of(cidx * KC, KC)
            return acc + _tree_rows(pred(key_scr[pl.ds(k0, KC), :], k_io + k0), jnp.add)
        acc8 = lax.fori_loop(0, n_chunks, body, jnp.zeros((8, QW), F32))
        return jnp.sum(acc8, axis=0, keepdims=True)

    topf = float(top)
    lo0 = functools.reduce(jnp.minimum, [mn8[r:r + 1, :] for r in range(8)])
    hi0 = functools.reduce(jnp.maximum, [mx8[r:r + 1, :] for r in range(8)]) + 1
    n_causal = (i * QW + 1 + lax.broadcasted_iota(I32, (1, QW), 1)).astype(F32)
    log_top = float(np.log(top))

    def in_zero_band(lo, hi):
        return jnp.where(lo >= 1, jnp.where(hi <= ZERO_BAND_END, 1.0, 0.0), 0.0) > 0.5

    def pending(lo, hi, c_lo):
        return (jnp.max(jnp.where(c_lo > topf, jnp.where(hi > lo + 1, 1.0, 0.0), 0.0)) > 0.5).astype(I32)

    def search_cond(c):
        return jnp.logical_and(c[0] < SEARCH_MAX_PASSES, c[1] > 0)

    def search_body(c):
        it, _, lo, hi, c_lo, c_hi, kept, w_lo, w_hi = c
        f_lo = (jnp.log(c_lo) - log_top) * w_lo
        f_hi = (log_top - jnp.log(jnp.maximum(c_hi, 0.5))) * w_hi
        frac = f_lo / (f_lo + f_hi)
        v_lo, v_hi = _key2f(lo), _key2f(hi)
        cand = _f2key(v_lo + frac * (v_hi - v_lo))
        lo_f, hi_f = lo.astype(F32), hi.astype(F32)
        cand = jnp.where(in_zero_band(lo, hi), (lo_f + frac * (hi_f - lo_f)).astype(I32), cand)
        key_mid = (lo >> 1) + (hi >> 1) + (lo & hi & 1)
        cand = jnp.where(jnp.abs(kept) >= SEARCH_KEPT_LIMIT, key_mid, cand)
        cand = jnp.where(it >= SEARCH_BISECT_FROM, key_mid, cand)
        cand = jnp.where(it == 0, 1, jnp.where(it == 1, ZERO_BAND_END, cand))
        cand = jnp.minimum(jnp.maximum(cand, lo + 1), hi - 1)
        cnt = count(lambda key, kpos: jnp.where(key >= cand, 1.0, 0.0))
        active = jnp.where(c_lo > topf, jnp.where(hi > lo + 1, 1.0, 0.0), 0.0) > 0.5
        up = jnp.where(active, jnp.where(cnt >= topf, 1.0, 0.0), 0.0) > 0.5
        dn = jnp.where(active, jnp.where(cnt >= topf, 0.0, 1.0), 0.0) > 0.5
        w_hi = jnp.where(up, jnp.where(kept < 0, 0.5 * w_hi, 1.0), jnp.where(dn, 1.0, w_hi))
        w_lo = jnp.where(dn, jnp.where(kept > 0, 0.5 * w_lo, 1.0), jnp.where(up, 1.0, w_lo))
        kept = jnp.where(up, jnp.where(kept < 0, kept - 1, -1), jnp.where(dn, jnp.where(kept > 0, kept + 1, 1), kept))
        lo, c_lo = jnp.where(up, cand, lo), jnp.where(up, cnt, c_lo)
        hi, c_hi = jnp.where(dn, cand, hi), jnp.where(dn, cnt, c_hi)
        hi = jnp.where(in_zero_band(lo, hi), jnp.minimum(hi, seq + 1), hi)
        return it + 1, pending(lo, hi, c_lo), lo, hi, c_lo, c_hi, kept, w_lo, w_hi

    zero_f, one_f = jnp.zeros((1, QW), F32), jnp.ones((1, QW), F32)
    _, _, thr, _, c_lo, c_hi, _, _, _ = lax.while_loop(
        search_cond, search_body,
        (jnp.int32(0), pending(lo0, hi0, n_causal), lo0, hi0, n_causal, zero_f, jnp.zeros((1, QW), I32),
         one_f, one_f))

    tied = c_lo > topf
    need1 = (topf - 1.0) - c_hi
    all_pos = jnp.full((1, QW), 2 ** idx_bits - 1, I32)

    def tie_search():
        def tie_body(bi, cur):
            cand = cur | lax.shift_left(jnp.int32(1), idx_bits - 1 - bi)
            cnt = count(lambda key, kpos: jnp.where(key == thr, jnp.where(kpos < cand, 1.0, 0.0), 0.0))
            return jnp.where(cnt <= need1, cand, cur)
        return lax.fori_loop(0, idx_bits, tie_body, jnp.zeros((1, QW), I32))

    any_tied = jnp.max(jnp.where(tied, 1.0, 0.0)) > 0.5
    last = jnp.where(tied, lax.cond(any_tied, tie_search, lambda: all_pos), all_pos)

    half_neg = int(np.float32(0.5 * NEG).view(np.int32))
    key_floor = half_neg ^ 0x7FFFFFFF

    def att_body(cidx, carry):
        m, l, acc = carry
        k0 = pl.multiple_of(cidx * KC, KC)
        hk = KC // ATT_SPLIT
        ss, m_new = [], m
        for sp in range(ATT_SPLIT):
            kb = pl.multiple_of(k0 + sp * hk, hk)
            key = key_scr[pl.ds(kb, hk), :]
            kpos = k_io[:hk] + kb
            bias = jnp.where(key > thr, 0.0, jnp.where(key == thr, jnp.where(kpos <= last, 0.0, NEG), NEG))
            bias = jnp.where(key > key_floor, bias, NEG)
            s = (jnp.dot(k_scr[pl.ds(kb, hk), :], qr_st, preferred_element_type=F32)
                 + jnp.concatenate([bias] * H, axis=1))
            ss.append(s)
            m_new = jnp.maximum(m_new, jnp.max(s, axis=0, keepdims=True))
        alpha = jnp.exp2(m - m_new)
        l, acc = alpha * l, alpha * acc
        for sp in range(ATT_SPLIT):
            kb = pl.multiple_of(k0 + sp * hk, hk)
            e = jnp.exp2(ss[sp] - m_new)
            l = l + jnp.sum(e, axis=0, keepdims=True)
            acc = acc + jnp.dot(vt_scr[:, pl.ds(kb, hk)], e.astype(BF16), preferred_element_type=F32)
        return m_new, l, acc

    init = (jnp.full((1, HW), NEG, F32), jnp.zeros((1, HW), F32), jnp.zeros((HEAD_DIM, HW), F32))
    m_a, l_a, acc_a = lax.fori_loop(0, n_chunks, att_body, init)
    o = jnp.where(m_a > 0.5 * NEG, acc_a / l_a, 0.0)
    outs = [o[:, QW * h:QW * (h + 1)] for h in range(H)]
    outs.append(jnp.zeros((o_ref.shape[1] - H * HEAD_DIM, QW), F32))
    o_ref[...] = jnp.concatenate(outs, axis=0).T.astype(BF16)


def _dsa(proj, tabs, tabs_i, batch, seq):
    qw = DSA_QW
    nb = seq // qw
    col = lambda name, w: _OFF[name] // w
    qmap = lambda cidx: (lambda b, i: (b * nb + i, cidx))
    tmap = lambda k: (lambda b, i: (k, i, 0))
    top = min(DSA_TOPK_MAX, seq // 4)
    kern = functools.partial(_dsa_kernel, top=top, idx_bits=int(seq).bit_length())
    return pl.pallas_call(
        kern,
        grid=(batch, nb),
        in_specs=[pl.BlockSpec((qw, 384), qmap(col("b_q", 384))),
                  pl.BlockSpec((qw, 256), qmap(col("b_iq", 256))),
                  pl.BlockSpec((qw, LANES), qmap(col("b_ikiw", LANES))),
                  pl.BlockSpec((qw, LANES), qmap(col("b_kv", LANES))),
                  pl.BlockSpec((None, qw, LANES), tmap(0)),
                  pl.BlockSpec((None, qw, LANES), tmap(1)),
                  pl.BlockSpec((None, qw, LANES), tmap(2)),
                  pl.BlockSpec((None, qw, LANES), tmap(0)),
                  pl.BlockSpec((None, qw, LANES), tmap(1)),
                  pl.BlockSpec((None, qw, LANES), tmap(2))],
        out_specs=pl.BlockSpec((qw, 384), lambda b, i: (b * nb + i, 0)),
        out_shape=jax.ShapeDtypeStruct((batch * seq, 384), BF16),
        scratch_shapes=[pltpu.VMEM((seq, LANES), BF16), pltpu.VMEM((HEAD_DIM, seq), BF16),
                        pltpu.VMEM((seq, LANES), BF16), pltpu.VMEM((seq, qw), I32)],
        compiler_params=_params(2),
        name="dsa",
    )(proj, proj, proj, proj, tabs, tabs, tabs, tabs_i, tabs_i, tabs_i)


DIL_MB = max(d for _, d in DIL_PAIRS) * QB
DIL_VMEM_LIMIT = 56 * 1024 * 1024
DIL_UNROLL = 4


def _dil_layout(g):
    dil = DIL_PAIRS[g][1]
    per = DIL_MB // dil
    return dil, per, per + QB


def _dil_bias_table():
    u = np.arange(2 * QB)[:, None]
    diff = QB + np.arange(LANES)[None, :] - u
    ok = (diff >= 0) & (diff <= QB)
    return _bias_table(np.concatenate([ok, ok & (u >= QB)], axis=0))


def _dil_kernel(*refs):
    G, HG = len(DIL_PAIRS), DIL_HEADS_PER_GROUP
    q_refs, k_refs, v_refs = refs[0:G], refs[G:2 * G], refs[2 * G:3 * G]
    c_ref, s1_ref, s2_ref, bias_ref, o_ref = refs[3 * G:3 * G + 5]
    kds, vds, ogs, lss = (refs[3 * G + 5 + n * G:3 * G + 5 + (n + 1) * G] for n in range(4))
    step = pl.program_id(1)
    half = ROPE_DIM // 2

    @pl.when(step == 0)
    def _zero():
        for g in range(G):
            kds[g][...] = jnp.zeros_like(kds[g])
            vds[g][...] = jnp.zeros_like(vds[g])

    @pl.when(step > 0)
    def _carry_halo():
        for g in range(G):
            dil, per, stride = _dil_layout(g)
            for r in range(dil):
                b0 = r * stride
                kds[g][b0:b0 + QB, :] = kds[g][b0 + per:b0 + per + QB, :]
                vds[g][:, b0:b0 + QB] = vds[g][:, b0 + per:b0 + per + QB]

    for g in range(G):
        dil, per, stride = _dil_layout(g)
        piece = min(per, KC)
        for r in range(dil):
            for p0 in range(0, per, piece):
                rows = pl.ds(r + dil * p0, piece, stride=dil)
                kr = _rope(k_refs[g][rows, :], c_ref[rows, :], s1_ref[rows, :], s2_ref[rows, :], half)
                d0 = r * stride + QB + p0
                kds[g][d0:d0 + piece, :] = kr.astype(BF16)
                vds[g][:, d0:d0 + piece] = v_refs[g][rows, :].T.astype(BF16)

    zero = jnp.zeros((HEAD_DIM, LANES), F32)
    for g in range(G):
        dil, per, stride = _dil_layout(g)
        nblk = per // QB

        def block(jb, _, r=0, g=g, dil=dil, stride=stride):
            p0 = pl.multiple_of(dil * QB * jb, dil * QB)
            win = pl.ds(p0, dil * QB)
            rows = pl.ds(r, QB, stride=dil)
            q = _rope(q_refs[g].at[win, :][rows, :] * (SCALE * LOG2E), c_ref.at[win, :][rows, :],
                      s1_ref.at[win, :][rows, :], s2_ref.at[win, :][rows, :], half)
            qt = q.T
            q2 = jnp.concatenate([jnp.concatenate([qt[:HEAD_DIM], zero], axis=0),
                                  jnp.concatenate([zero, qt[HEAD_DIM:]], axis=0)], axis=1).astype(BF16)
            kb = pl.multiple_of(r * stride + QB * jb, QB)
            first = jnp.logical_and(step == 0, jb == 0)
            bias = bias_ref[pl.ds(pl.multiple_of(jnp.where(first, 2 * QB, 0), QB), 2 * QB), :]
            s = (jnp.dot(kds[g][pl.ds(kb, 2 * QB), :], q2, preferred_element_type=F32)
                 + jnp.concatenate([bias] * HG, axis=1))
            e, m, den = _softmax2_cols(s)
            lse = m + jnp.log2(den)
            rden = 1.0 / den
            parts, lparts = [], []
            for hg in range(HG):
                sl = slice(LANES * hg, LANES * (hg + 1))
                vt = vds[g][HEAD_DIM * hg:HEAD_DIM * (hg + 1), pl.ds(kb, 2 * QB)]
                parts.append(jnp.dot(vt, e[:, sl].astype(BF16), preferred_element_type=F32) * rden[:, sl])
                lparts.append(jnp.broadcast_to(lse[:, sl], (HEAD_DIM, LANES)))
            tile = jnp.concatenate(parts + lparts, axis=0).T
            ogs[g].at[win, :][rows, :] = tile[:, :LANES]
            lss[g].at[win, :][rows, :] = tile[:, LANES:]
            return 0

        for r in range(dil):
            lax.fori_loop(0, nblk, functools.partial(block, r=r), 0, unroll=min(nblk, DIL_UNROLL))

    def mix(ti, _):
        rows = pl.ds(pl.multiple_of(ti * KC, KC), KC)
        ls = [lss[g][rows, :] for g in range(G)]
        mx = functools.reduce(jnp.maximum, ls)
        ex = [jnp.exp2(x - mx) for x in ls]
        rtot = 1.0 / functools.reduce(lambda a, b: a + b, ex)
        o_ref[rows, :] = jnp.concatenate([ex[g] * rtot * ogs[g][rows, :] for g in range(G)], axis=1).astype(BF16)
        return 0

    lax.fori_loop(0, DIL_MB // KC, mix, 0)


def _dil(proj, tabs, batch, seq):
    assert seq % DIL_MB == 0 and all(w // d == QB for w, d in DIL_PAIRS)
    nm = seq // DIL_MB
    G = len(DIL_PAIRS)
    w = DIL_HEADS * HEAD_DIM
    gmap = lambda name, g: (lambda b, j: (b * nm + j, _OFF[name] // LANES + g))
    tmap = lambda k: (lambda b, j: (k, j, 0))
    bias = _dil_bias_table()
    rows = [_dil_layout(g)[0] * _dil_layout(g)[2] for g in range(G)]
    return pl.pallas_call(
        _dil_kernel,
        grid=(batch, nm),
        in_specs=([pl.BlockSpec((DIL_MB, LANES), gmap(name, g)) for name in ("c_q", "c_k", "c_v") for g in range(G)]
                  + [pl.BlockSpec((None, DIL_MB, LANES), tmap(k)) for k in range(3)]
                  + [pl.BlockSpec(bias.shape, lambda b, j: (0, 0))]),
        out_specs=pl.BlockSpec((DIL_MB, w), lambda b, j: (b * nm + j, 0)),
        out_shape=jax.ShapeDtypeStruct((batch * seq, w), BF16),
        scratch_shapes=([pltpu.VMEM((n, LANES), BF16) for n in rows] + [pltpu.VMEM((LANES, n), BF16) for n in rows]
                        + [pltpu.VMEM((DIL_MB, LANES), F32)] * (2 * G)),
        compiler_params=_params(2, DIL_VMEM_LIMIT),
        name="dilated",
    )(*([proj] * (3 * G)), tabs, tabs, tabs, bias)


def _overlap_t(seq):
    n_cmp_pad = seq // CMP_STRIDE
    n_slc = seq // SLC_BLOCK
    c_start = np.arange(n_cmp_pad) * CMP_STRIDE
    s_start = np.arange(n_slc) * SLC_BLOCK
    ov = ((c_start[None, :] < s_start[:, None] + SLC_BLOCK) & (c_start[None, :] + CMP_BLOCK > s_start[:, None]))
    return jnp.asarray(ov.astype(np.float32), dtype=BF16)


def _layer(x2, batch, seq, norm1_g, w_in, cmp_pos, cmp_w1, cmp_w2, w_out, norm2_g, w_up, conv_w, conv_b, w_down,
           final_g, final_norm, tabs, tabs_i, ovt):
    n = batch * seq
    proj = _inproj(x2, norm1_g, _pad_w_in(w_in).astype(BF16))

    o_kc = _OFF["a_kcvc"]
    rows = seq // CMP_STRIDE
    kc_r = proj[:, o_kc:o_kc + HEAD_DIM].reshape(batch, rows, CMP_STRIDE * HEAD_DIM)
    vc_r = proj[:, o_kc + HEAD_DIM:o_kc + 2 * HEAD_DIM].reshape(batch, rows, CMP_STRIDE * HEAD_DIM)
    pos = cmp_pos.reshape(2, 2, CMP_STRIDE * HEAD_DIM)
    w1 = cmp_w1.reshape(2, 2, CMP_STRIDE * HEAD_DIM, CMP_HIDDEN).astype(BF16)
    zpad = jnp.zeros((CMP_HIDDEN, HEAD_DIM), cmp_w2.dtype)
    w2 = jnp.stack([jnp.concatenate([cmp_w2[0], zpad], axis=1),
                    jnp.concatenate([zpad, cmp_w2[1]], axis=1)]).astype(BF16)
    kvcmp = _nsa_compress(kc_r, vc_r, pos, w1, w2)

    o_a = _nsa(proj, kvcmp, tabs, ovt, batch, seq)
    o_b = _dsa(proj, tabs, tabs_i, batch, seq)
    o_c = _dil(proj, tabs, batch, seq)

    hp = NSA_HEADS * HEAD_DIM
    zrow = jnp.zeros((384 - hp, D_MODEL), w_out.dtype)
    w_out_pad = jnp.concatenate([w_out[0:hp], zrow, w_out[hp:2 * hp], zrow, w_out[2 * hp:]], axis=0).astype(BF16)
    x2 = _outproj(o_a, o_b, o_c, w_out_pad, x2)

    act = _ffn_up(x2, norm2_g, w_up.astype(BF16), conv_w, conv_b, seq)
    return _ffn_down(act, w_down.astype(BF16), x2, final_g, final_norm)


def kernel(x, norm1_g, w_in, cmp_pos, cmp_w1, cmp_w2, w_out, norm2_g, w_up, conv_w, conv_b, w_down, final_g):
    batch, seq, d = x.shape
    depth = w_in.shape[0]
    tabs = _rope_lane_tables(seq, HEAD_DIM, ROPE_DIM)
    tabs_i = _rope_lane_tables(seq, IDX_DIM, IDX_ROPE_DIM)
    ovt = _overlap_t(seq)
    x2 = x.reshape(batch * seq, d)
    for li in range(depth):
        x2 = _layer(x2, batch, seq, norm1_g[li], w_in[li], cmp_pos[li], cmp_w1[li], cmp_w2[li], w_out[li],
                    norm2_g[li], w_up[li], conv_w[li], conv_b[li], w_down[li], final_g, li == depth - 1,
                    tabs, tabs_i, ovt)
    return x2.reshape(batch, seq, d)
```

```python
import functools

import numpy as np
import jax
import jax.numpy as jnp
from jax import lax
from jax.experimental import pallas as pl
from jax.experimental.pallas import tpu as pltpu

F32 = jnp.float32
BF16 = jnp.bfloat16
I32 = jnp.int32

D_MODEL = 1024
HEAD_DIM = 64
ROPE_DIM = HEAD_DIM // 4
ROPE_THETA = 500000.0
NORM_EPS = 1e-6
SCALE = HEAD_DIM ** -0.5
LOG2E = 1.4426950408889634
NEG = -1e30
FORCE = 1e9
NSA_HEADS = 5
CMP_BLOCK = 32
CMP_STRIDE = 16
CMP_HIDDEN = 128
SLC_BLOCK = 64
SLC_TOPN = 16
WIN = 512
DSA_HEADS = 5
IDX_HEADS = 8
IDX_DIM = 32
IDX_ROPE_DIM = IDX_DIM // 4
DSA_TOPK_MAX = 256
DIL_PAIRS = ((128, 1), (512, 4), (2048, 16))
DIL_HEADS_PER_GROUP = 2
DIL_HEADS = len(DIL_PAIRS) * DIL_HEADS_PER_GROUP
D_FF = 2816
CONV_WIDTH = 3

LANES = 128
QB = 128
DSA_QW = 256
NSA_QW = 256
KC = 512
ATT_SPLIT = 1
ROW_TILE = 512
HALO = 16
VMEM_LIMIT = 48 * 1024 * 1024
INT_MIN = -2 ** 31

_A0 = 0
_B0 = 719
_C0 = 1463
_SEGS = (
    ("c_q", _C0, 384, 384), ("c_k", _C0 + 384, 384, 384), ("c_v", _C0 + 768, 384, 384),
    ("a_q", _A0, 320, 384), ("b_q", _B0, 320, 384),
    ("a_kcvc", _A0 + 320, 128, 128),
    ("b_iq", _B0 + 448, 256, 256),
    ("a_ksvs", _A0 + 448, 128, 128), ("a_kwvw", _A0 + 576, 128, 128),
    ("a_g", _A0 + 704, 15, 128),
    ("b_kv", _B0 + 320, 128, 128),
    ("b_ikiw", _B0 + 704, 40, 128),
)
P_COLS = sum(s[3] for s in _SEGS)


def _seg_offsets():
    offs, o = {}, 0
    for name, _, _, pw in _SEGS:
        offs[name] = o
        o += pw
    return offs


_OFF = _seg_offsets()


def _pad_w_in(w):
    parts = []
    for _, src, wdt, pw in _SEGS:
        parts.append(w[:, src:src + wdt])
        if pw > wdt:
            parts.append(jnp.zeros((w.shape[0], pw - wdt), w.dtype))
    return jnp.concatenate(parts, axis=1)


def _rope_lane_tables(L, head_dim, rot_dim):
    half = rot_dim // 2
    inv = 1.0 / (ROPE_THETA ** (np.arange(0, rot_dim, 2, dtype=np.float32) / np.float32(rot_dim)))
    ang = np.arange(L, dtype=np.float32)[:, None] * inv[None, :]
    cos, sin = np.cos(ang).astype(np.float32), np.sin(ang).astype(np.float32)
    d = np.arange(LANES) % head_dim
    lo, hi = d < half, (d >= half) & (d < rot_dim)
    c = np.ones((L, LANES), np.float32)
    s1 = np.zeros((L, LANES), np.float32)
    s2 = np.zeros((L, LANES), np.float32)
    c[:, lo] = cos[:, d[lo]]
    c[:, hi] = cos[:, d[hi] - half]
    s1[:, lo] = -sin[:, d[lo]]
    s2[:, hi] = sin[:, d[hi] - half]
    return jnp.asarray(np.stack([c, s1, s2]))


def _rope(x, c, s1, s2, half):
    xp = pltpu.roll(x, LANES - half, 1)
    xm = pltpu.roll(x, half, 1)
    return x * c + xp * s1 + xm * s2


def _rope_wide(x, c, s1, s2, half):
    n = x.shape[1] // LANES
    return jnp.concatenate([_rope(x[:, LANES * s:LANES * (s + 1)], c, s1, s2, half) for s in range(n)], axis=1)


def _softmax2_cols(s):
    m = jnp.max(s, axis=0, keepdims=True)
    e = jnp.exp2(s - m)
    return e, m, jnp.sum(e, axis=0, keepdims=True)


def _bias_table(ok):
    return jnp.asarray(np.where(ok, 0.0, NEG).astype(np.float32))


def _cmp_bias_table(seq, qw):
    shift = (qw // CMP_STRIDE) * (seq // qw - 1)
    u = np.arange(seq // CMP_STRIDE + shift)[:, None]
    lane = np.arange(qw)[None, :]
    return _bias_table(CMP_STRIDE * (u - shift) + CMP_BLOCK - 1 <= lane)


def _win_bias_table(seq, qw):
    span = min(WIN + qw, seq)
    cmax = span - qw
    u = np.arange(span + cmax)[:, None]
    diff = cmax + np.arange(qw)[None, :] - u
    return _bias_table((diff >= 0) & (diff < WIN))


def _causal_bias_table(qw):
    u = np.arange(2 * KC)[:, None]
    return _bias_table(u <= KC + np.arange(qw)[None, :])


def _params(n_grid, vmem=VMEM_LIMIT):
    return pltpu.CompilerParams(dimension_semantics=("arbitrary",) * n_grid, vmem_limit_bytes=vmem)


def _rmsnorm_rows(x, g):
    return x * lax.rsqrt(jnp.mean(x * x, axis=-1, keepdims=True) + NORM_EPS) * g


def _inproj_kernel(x_ref, g_ref, w_ref, o_ref):
    hn = _rmsnorm_rows(x_ref[...], g_ref[...]).astype(BF16)
    n = w_ref.shape[1]
    for c0 in range(0, n, 512):
        c1 = min(c0 + 512, n)
        o_ref[:, c0:c1] = jnp.dot(hn, w_ref[:, c0:c1], preferred_element_type=F32)


def _inproj(x2, g, w_bf):
    n, d = x2.shape
    pc = w_bf.shape[1]
    return pl.pallas_call(
        _inproj_kernel,
        grid=(n // ROW_TILE,),
        in_specs=[pl.BlockSpec((ROW_TILE, d), lambda i: (i, 0)),
                  pl.BlockSpec((1, d), lambda i: (0, 0)),
                  pl.BlockSpec((d, pc), lambda i: (0, 0))],
        out_specs=pl.BlockSpec((ROW_TILE, pc), lambda i: (i, 0)),
        out_shape=jax.ShapeDtypeStruct((n, pc), F32),
        compiler_params=_params(1),
        name="inproj",
    )(x2, g.reshape(1, d), w_bf)


def _outproj_kernel(oa_ref, ob_ref, oc_ref, w_ref, x_ref, o_ref):
    mix = jnp.concatenate([oa_ref[...], ob_ref[...], oc_ref[...]], axis=1)
    o_ref[...] = x_ref[...] + jnp.dot(mix, w_ref[...], preferred_element_type=F32)


def _outproj(oa, ob, oc, w_bf, x2):
    n, d = x2.shape
    k = w_bf.shape[0]
    mw = oa.shape[1]
    return pl.pallas_call(
        _outproj_kernel,
        grid=(n // ROW_TILE,),
        in_specs=[pl.BlockSpec((ROW_TILE, mw), lambda i: (i, 0)),
                  pl.BlockSpec((ROW_TILE, mw), lambda i: (i, 0)),
                  pl.BlockSpec((ROW_TILE, mw), lambda i: (i, 0)),
                  pl.BlockSpec((k, d), lambda i: (0, 0)),
                  pl.BlockSpec((ROW_TILE, d), lambda i: (i, 0))],
        out_specs=pl.BlockSpec((ROW_TILE, d), lambda i: (i, 0)),
        out_shape=jax.ShapeDtypeStruct((n, d), F32),
        compiler_params=_params(1),
        name="outproj",
    )(oa, ob, oc, w_bf, x2)


def _ffn_up_kernel(x_ref, xh_ref, g_ref, wa_ref, wu_ref, cw_ref, cb_ref, o_ref, hn_scr, hh_scr, a_scr, *, tiles_per_seq):
    i, j = pl.program_id(0), pl.program_id(1)
    tm = x_ref.shape[0]

    @pl.when(j == 0)
    def _norm():
        hn_scr[...] = _rmsnorm_rows(x_ref[...], g_ref[...]).astype(BF16)
        keep = jnp.where(i % tiles_per_seq == 0, 0.0, 1.0)
        hh_scr[...] = (_rmsnorm_rows(xh_ref[...], g_ref[...]) * keep).astype(BF16)

    hn = hn_scr[...]
    a = jnp.dot(hn, wa_ref[...], preferred_element_type=F32)
    u = jnp.dot(hn, wu_ref[...], preferred_element_type=F32)
    a_scr[0:HALO, :] = jnp.dot(hh_scr[...], wa_ref[...], preferred_element_type=F32)
    a_scr[HALO:HALO + tm, :] = a
    a1 = a_scr[pl.ds(HALO - 1, tm), :]
    a2 = a_scr[pl.ds(HALO - 2, tm), :]
    cw = cw_ref[...]
    conv = cw[0:1, :] * a2 + cw[1:2, :] * a1 + cw[2:3, :] * a + cb_ref[...]
    o_ref[...] = (conv * jax.nn.sigmoid(conv) * u).astype(BF16)


def _ffn_up(x2, g, w_up_bf, conv_w, conv_b, seq_len):
    n, d = x2.shape
    tn = D_FF // 2
    nj = D_FF // tn
    hb = ROW_TILE // HALO
    kern = functools.partial(_ffn_up_kernel, tiles_per_seq=seq_len // ROW_TILE)
    return pl.pallas_call(
        kern,
        grid=(n // ROW_TILE, nj),
        in_specs=[pl.BlockSpec((ROW_TILE, d), lambda i, j: (i, 0)),
                  pl.BlockSpec((HALO, d), lambda i, j: (jnp.maximum(i * hb - 1, 0), 0)),
                  pl.BlockSpec((1, d), lambda i, j: (0, 0)),
                  pl.BlockSpec((d, tn), lambda i, j: (0, j)),
                  pl.BlockSpec((d, tn), lambda i, j: (0, j + nj)),
                  pl.BlockSpec((CONV_WIDTH, tn), lambda i, j: (0, j)),
                  pl.BlockSpec((1, tn), lambda i, j: (0, j))],
        out_specs=pl.BlockSpec((ROW_TILE, tn), lambda i, j: (i, j)),
        out_shape=jax.ShapeDtypeStruct((n, D_FF), BF16),
        scratch_shapes=[pltpu.VMEM((ROW_TILE, d), BF16), pltpu.VMEM((HALO, d), BF16),
                        pltpu.VMEM((ROW_TILE + HALO, tn), F32)],
        compiler_params=_params(2),
        name="ffn_up",
    )(x2, x2, g.reshape(1, d), w_up_bf, w_up_bf, conv_w, conv_b.reshape(1, D_FF))


def _ffn_down_kernel(a_ref, w_ref, x_ref, g_ref, o_ref, *, final_norm):
    y = x_ref[...] + jnp.dot(a_ref[...], w_ref[...], preferred_element_type=F32)
    if final_norm:
        y = _rmsnorm_rows(y, g_ref[...])
    o_ref[...] = y


def _ffn_down(act, w_bf, x2, final_g, final_norm):
    n, d = x2.shape
    k = act.shape[1]
    return pl.pallas_call(
        functools.partial(_ffn_down_kernel, final_norm=final_norm),
        grid=(n // ROW_TILE,),
        in_specs=[pl.BlockSpec((ROW_TILE, k), lambda i: (i, 0)),
                  pl.BlockSpec((k, d), lambda i: (0, 0)),
                  pl.BlockSpec((ROW_TILE, d), lambda i: (i, 0)),
                  pl.BlockSpec((1, d), lambda i: (0, 0))],
        out_specs=pl.BlockSpec((ROW_TILE, d), lambda i: (i, 0)),
        out_shape=jax.ShapeDtypeStruct((n, d), F32),
        compiler_params=_params(1),
        name="ffn_down",
    )(act, w_bf, x2, final_g.reshape(1, d))


def _cmp_kernel(kc_ref, vc_ref, pos_ref, w1_ref, w2_ref, o_ref):
    out = None
    for c, src in enumerate((kc_ref, vc_ref)):
        r = src[0]
        lo = jnp.dot((r + pos_ref[c, 0:1, :]).astype(BF16), w1_ref[c, 0], preferred_element_type=F32)
        hi = jnp.dot((r + pos_ref[c, 1:2, :]).astype(BF16), w1_ref[c, 1], preferred_element_type=F32)
        nrows = hi.shape[0]
        hid = jax.nn.gelu(lo + pltpu.roll(hi, nrows - 1, 0))
        y = jnp.dot(hid.astype(BF16), w2_ref[c], preferred_element_type=F32)
        out = y if out is None else out + y
    o_ref[0] = out


def _nsa_compress(kc_r, vc_r, pos, w1_bf, w2_pad_bf):
    b, nr, wd = kc_r.shape
    return pl.pallas_call(
        _cmp_kernel,
        grid=(b,),
        in_specs=[pl.BlockSpec((1, nr, wd), lambda i: (i, 0, 0)),
                  pl.BlockSpec((1, nr, wd), lambda i: (i, 0, 0)),
                  pl.BlockSpec(pos.shape, lambda i: (0, 0, 0)),
                  pl.BlockSpec(w1_bf.shape, lambda i: (0, 0, 0, 0)),
                  pl.BlockSpec(w2_pad_bf.shape, lambda i: (0, 0, 0))],
        out_specs=pl.BlockSpec((1, nr, LANES), lambda i: (i, 0, 0)),
        out_shape=jax.ShapeDtypeStruct((b, nr, LANES), F32),
        compiler_params=_params(1),
        name="nsa_compress",
    )(kc_r, vc_r, pos, w1_bf, w2_pad_bf)


def _stack_heads_t(x_t, n_heads, hd):
    z = jnp.zeros((LANES - hd, x_t.shape[1]), F32)
    return jnp.concatenate(
        [jnp.concatenate([x_t[hd * h:hd * (h + 1), :], z], axis=0) for h in range(n_heads)], axis=1)


def _nsa_kernel(q_ref, g_ref, ksvs_ref, kwvw_ref, kvc_ref, c_ref, s1_ref, s2_ref, ovt_ref, cmpb_ref, winb_ref,
                caub_ref, o_ref, ks_scr, vst_scr, kw_scr, vwt_scr, kc_scr, vct_scr, blk_scr, *, n_top):
    i = pl.program_id(1)
    H = NSA_HEADS
    QW = q_ref.shape[0]
    HW = H * QW
    seq = ks_scr.shape[0]
    lane_q = lax.broadcasted_iota(I32, (QW, LANES), 1)

    @pl.when(i == 0)
    def _init():
        ks_scr[...] = jnp.zeros_like(ks_scr)
        kw_scr[...] = jnp.zeros_like(kw_scr)
        vst_scr[...] = jnp.zeros_like(vst_scr)
        vwt_scr[...] = jnp.zeros_like(vwt_scr)
        kvc = kvc_ref[0]
        lane_c = lax.broadcasted_iota(I32, kvc.shape, 1)
        kc_scr[...] = jnp.where(lane_c < HEAD_DIM, kvc, 0.0).astype(BF16)
        vct_scr[...] = kvc.T[HEAD_DIM:2 * HEAD_DIM, :].astype(BF16)

    c, s1, s2 = c_ref[...], s1_ref[...], s2_ref[...]
    r0 = pl.multiple_of(i * QW, QW)
    for src, kdst, vdst in ((ksvs_ref, ks_scr, vst_scr), (kwvw_ref, kw_scr, vwt_scr)):
        kv = src[...]
        kr = _rope(kv, c, s1, s2, ROPE_DIM // 2)
        kdst[pl.ds(r0, QW), :] = jnp.where(lane_q < HEAD_DIM, kr, 0.0).astype(BF16)
        vdst[:, pl.ds(r0, QW)] = kv.T[HEAD_DIM:2 * HEAD_DIM, :].astype(BF16)

    q = q_ref[...] * (SCALE * LOG2E)
    qr = _rope_wide(q, c, s1, s2, ROPE_DIM // 2)
    q_st = _stack_heads_t(q.T, H, HEAD_DIM).astype(BF16)
    qr_st = _stack_heads_t(qr.T, H, HEAD_DIM).astype(BF16)
    tile_h = lambda b: jnp.concatenate([b] * H, axis=1)

    ncp = kc_scr.shape[0]
    cshift = cmpb_ref.shape[0] - ncp
    cb = cmpb_ref[pl.ds(pl.multiple_of(cshift - (QW // CMP_STRIDE) * i, 8), ncp), :]
    s_c = jnp.dot(kc_scr[...], q_st, preferred_element_type=F32) + tile_h(cb)
    e_c, _, den_c = _softmax2_cols(s_c)
    t_row = i * QW + (lax.broadcasted_iota(I32, (1, HW), 1) & (QW - 1))
    r_c = jnp.where(t_row >= CMP_BLOCK - 1, 1.0 / den_c, 0.0)
    o_cmp = jnp.dot(vct_scr[...], e_c.astype(BF16), preferred_element_type=F32) * r_c
    psum = e_c[:, 0:QW] * r_c[:, 0:QW]
    for h in range(1, H):
        psum = psum + e_c[:, QW * h:QW * (h + 1)] * r_c[:, QW * h:QW * (h + 1)]

    hi = psum.astype(BF16)
    r1 = psum - hi.astype(F32)
    mid = r1.astype(BF16)
    lo = (r1 - mid.astype(F32)).astype(BF16)
    ovt = ovt_ref[...]
    imp = (jnp.dot(ovt, hi, preferred_element_type=F32) + jnp.dot(ovt, mid, preferred_element_type=F32)
           + jnp.dot(ovt, lo, preferred_element_type=F32))
    n_slc = imp.shape[0]
    jb = lax.broadcasted_iota(I32, (n_slc, QW), 0)
    tq = i * QW + lax.broadcasted_iota(I32, (n_slc, QW), 1)
    cur = tq >> (SLC_BLOCK.bit_length() - 1)
    forced = (jb == 0) | (jb == cur) | (jb == cur - 1)
    val = jnp.where(forced, FORCE, jnp.where(jb <= cur, imp, NEG))
    rank = jnp.zeros((n_slc, QW), I32)
    for jp in range(n_slc):
        row = val[jp:jp + 1, :]
        tie = jnp.where(jb > jp, 1, 0)
        rank = rank + jnp.where(row > val, 1, jnp.where(row == val, tie, 0))
    blk_scr[...] = jnp.where(rank < n_top, jnp.where(val > 0.5 * NEG, 0.0, NEG), NEG)

    bpc = KC // SLC_BLOCK

    def slc_body(ci, carry):
        m, l, acc = carry
        k0 = pl.multiple_of(ci * KC, KC)
        ahead = jnp.minimum(i * QW - k0, KC)
        c0 = pl.multiple_of(KC - ahead, QB)
        hk = KC // ATT_SPLIT
        ss, m_new = [], m
        for sp in range(ATT_SPLIT):
            kb = pl.multiple_of(k0 + sp * hk, hk)
            rows = [jnp.broadcast_to(blk_scr[pl.ds(ci * bpc + sp * (bpc // ATT_SPLIT) + r, 1), :], (SLC_BLOCK, QW))
                    for r in range(bpc // ATT_SPLIT)]
            bias = jnp.concatenate(rows, axis=0) + caub_ref[pl.ds(pl.multiple_of(c0 + sp * hk, QB), hk), :]
            s = jnp.dot(ks_scr[pl.ds(kb, hk), :], qr_st, preferred_element_type=F32) + tile_h(bias)
            ss.append(s)
            m_new = jnp.maximum(m_new, jnp.max(s, axis=0, keepdims=True))
        alpha = jnp.exp2(m - m_new)
        l, acc = alpha * l, alpha * acc
        for sp in range(ATT_SPLIT):
            kb = pl.multiple_of(k0 + sp * hk, hk)
            e = jnp.exp2(ss[sp] - m_new)
            l = l + jnp.sum(e, axis=0, keepdims=True)
            acc = acc + jnp.dot(vst_scr[:, pl.ds(kb, hk)], e.astype(BF16), preferred_element_type=F32)
        return m_new, l, acc

    n_chunks = (i * QW + QW + KC - 1) // KC
    init = (jnp.full((1, HW), NEG, F32), jnp.zeros((1, HW), F32), jnp.zeros((HEAD_DIM, HW), F32))
    m_s, l_s, acc_s = lax.fori_loop(0, n_chunks, slc_body, init)
    o_slc = jnp.where(m_s > 0.5 * NEG, acc_s / l_s, 0.0)

    span = min(WIN + QW, seq)
    w0 = pl.multiple_of(jnp.maximum(i * QW + QW - span, 0), QB)
    wb = winb_ref[pl.ds(pl.multiple_of(span - QW - (i * QW - w0), QB), span), :]
    s_w = jnp.dot(kw_scr[pl.ds(w0, span), :], qr_st, preferred_element_type=F32) + tile_h(wb)
    e_w, _, den_w = _softmax2_cols(s_w)
    o_win = jnp.dot(vwt_scr[:, pl.ds(w0, span)], e_w.astype(BF16), preferred_element_type=F32) * (1.0 / den_w)

    gt = jax.nn.sigmoid(g_ref[...].T)
    outs = []
    for h in range(H):
        sl = slice(QW * h, QW * (h + 1))
        outs.append(gt[3 * h:3 * h + 1, :] * o_cmp[:, sl] + gt[3 * h + 1:3 * h + 2, :] * o_slc[:, sl]
                    + gt[3 * h + 2:3 * h + 3, :] * o_win[:, sl])
    outs.append(jnp.zeros((o_ref.shape[1] - H * HEAD_DIM, QW), F32))
    o_ref[...] = jnp.concatenate(outs, axis=0).T.astype(BF16)


def _nsa(proj, kvcmp, tabs, ovt, batch, seq):
    qw = NSA_QW
    nb = seq // qw
    n_slc = seq // SLC_BLOCK
    ncp = kvcmp.shape[1]
    col = lambda name, w: _OFF[name] // w
    qmap = lambda cidx: (lambda b, i: (b * nb + i, cidx))
    tmap = lambda k: (lambda b, i: (k, i, 0))
    kern = functools.partial(_nsa_kernel, n_top=min(SLC_TOPN, n_slc))
    cmpb, winb, caub = _cmp_bias_table(seq, qw), _win_bias_table(seq, qw), _causal_bias_table(qw)
    whole = lambda a: pl.BlockSpec(a.shape, lambda b, i: (0, 0))
    return pl.pallas_call(
        kern,
        grid=(batch, nb),
        in_specs=[pl.BlockSpec((qw, 384), qmap(col("a_q", 384))),
                  pl.BlockSpec((qw, LANES), qmap(col("a_g", LANES))),
                  pl.BlockSpec((qw, LANES), qmap(col("a_ksvs", LANES))),
                  pl.BlockSpec((qw, LANES), qmap(col("a_kwvw", LANES))),
                  pl.BlockSpec((1, ncp, LANES), lambda b, i: (b, 0, 0)),
                  pl.BlockSpec((None, qw, LANES), tmap(0)),
                  pl.BlockSpec((None, qw, LANES), tmap(1)),
                  pl.BlockSpec((None, qw, LANES), tmap(2)),
                  whole(ovt), whole(cmpb), whole(winb), whole(caub)],
        out_specs=pl.BlockSpec((qw, 384), lambda b, i: (b * nb + i, 0)),
        out_shape=jax.ShapeDtypeStruct((batch * seq, 384), BF16),
        scratch_shapes=[pltpu.VMEM((seq, LANES), BF16), pltpu.VMEM((HEAD_DIM, seq), BF16),
                        pltpu.VMEM((seq, LANES), BF16), pltpu.VMEM((HEAD_DIM, seq), BF16),
                        pltpu.VMEM((ncp, LANES), BF16), pltpu.VMEM((HEAD_DIM, ncp), BF16),
                        pltpu.VMEM((n_slc, qw), F32)],
        compiler_params=_params(2),
        name="nsa",
    )(proj, proj, proj, proj, kvcmp, tabs, tabs, tabs, ovt, cmpb, winb, caub)


SEARCH_BISECT_FROM = 24
SEARCH_MAX_PASSES = 64
SEARCH_KEPT_LIMIT = 6
INT_MAX = 2 ** 31 - 1
FLT_MIN_NORMAL = 1.17549435e-38
ZERO_BAND_END = 1 << 23


def _tree_rows(x, op):
    parts = [x[r:r + 8, :] for r in range(0, x.shape[0], 8)]
    while len(parts) > 1:
        nxt = [op(parts[a], parts[a + 1]) for a in range(0, len(parts) - 1, 2)]
        if len(parts) % 2:
            nxt.append(parts[-1])
        parts = nxt
    return parts[0]


def _f2key(v):
    bits = lax.bitcast_convert_type(v, I32)
    return bits ^ ((bits >> 31) & 0x7FFFFFFF)


def _key2f(k):
    return lax.bitcast_convert_type(k ^ ((k >> 31) & 0x7FFFFFFF), F32)


def _dsa_kernel(q_ref, iq_ref, ikw_ref, kv_ref, c_ref, s1_ref, s2_ref, ci_ref, si1_ref, si2_ref, o_ref,
                k_scr, vt_scr, ik_scr, key_scr, *, top, idx_bits):
    i = pl.program_id(1)
    H = DSA_HEADS
    QW = q_ref.shape[0]
    HW = H * QW
    seq = key_scr.shape[0]
    lane_q = lax.broadcasted_iota(I32, (QW, LANES), 1)

    @pl.when(i == 0)
    def _init():
        k_scr[...] = jnp.zeros_like(k_scr)
        vt_scr[...] = jnp.zeros_like(vt_scr)
        ik_scr[...] = jnp.zeros_like(ik_scr)

    c, s1, s2 = c_ref[...], s1_ref[...], s2_ref[...]
    ci, si1, si2 = ci_ref[...], si1_ref[...], si2_ref[...]
    r0 = pl.multiple_of(i * QW, QW)
    kv = kv_ref[...]
    k_scr[pl.ds(r0, QW), :] = jnp.where(lane_q < HEAD_DIM, _rope(kv, c, s1, s2, ROPE_DIM // 2), 0.0).astype(BF16)
    vt_scr[:, pl.ds(r0, QW)] = kv.T[HEAD_DIM:2 * HEAD_DIM, :].astype(BF16)
    ikw = ikw_ref[...]
    ik_scr[pl.ds(r0, QW), :] = jnp.where(lane_q < IDX_DIM, _rope(ikw, ci, si1, si2, IDX_ROPE_DIM // 2), 0.0).astype(BF16)

    qr = _rope_wide(q_ref[...] * (SCALE * LOG2E), c, s1, s2, ROPE_DIM // 2)
    qr_st = _stack_heads_t(qr.T, H, HEAD_DIM).astype(BF16)
    iqr = _rope_wide(iq_ref[...], ci, si1, si2, IDX_ROPE_DIM // 2)
    iq_st = _stack_heads_t(iqr.T, IDX_HEADS, IDX_DIM).astype(BF16)
    wt = ikw.T * ((IDX_DIM ** -0.5) * (IDX_HEADS ** -0.5))
    w_st = jnp.concatenate([wt[IDX_DIM + h:IDX_DIM + h + 1, :] for h in range(IDX_HEADS)], axis=1)

    n_chunks = (i * QW + QW + KC - 1) // KC
    k_io = lax.broadcasted_iota(I32, (KC, QW), 0)
    tq = i * QW + lax.broadcasted_iota(I32, (KC, QW), 1)

    sub_io = lax.broadcasted_iota(I32, (QB, QW), 0)
    sub_tq = i * QW + lax.broadcasted_iota(I32, (QB, QW), 1)

    def score_body(cidx, carry):
        mx, mn = carry
        k0 = pl.multiple_of(cidx * KC, KC)
        for sb in range(KC // QB):
            kb = pl.multiple_of(k0 + sb * QB, QB)
            d = jnp.dot(ik_scr[pl.ds(kb, QB), :], iq_st, preferred_element_type=F32)
            r = jnp.maximum(d, 0.0) * w_st
            sc = r[:, 0:QW]
            for h in range(1, IDX_HEADS):
                sc = sc + r[:, QW * h:QW * (h + 1)]
            kpos = sub_io + kb
            key = jnp.where(jnp.abs(sc) < FLT_MIN_NORMAL, seq - kpos, _f2key(sc))
            causal = kpos <= sub_tq
            key_scr[pl.ds(kb, QB), :] = jnp.where(causal, key, INT_MIN)
            mx = jnp.maximum(mx, _tree_rows(jnp.where(causal, key, INT_MIN), jnp.maximum))
            mn = jnp.minimum(mn, _tree_rows(jnp.where(causal, key, INT_MAX), jnp.minimum))
        return mx, mn

    mx8, mn8 = lax.fori_loop(0, n_chunks, score_body,
                             (jnp.full((8, QW), INT_MIN, I32), jnp.full((8, QW), INT_MAX, I32)))

    def count(pred):
        def body(cidx, acc):
            k0 = pl.multiple_of(cidx * KC, KC)
            return acc + _tree_rows(pred(key_scr[pl.ds(k0, KC), :], k_io + k0), jnp.add)
        acc8 = lax.fori_loop(0, n_chunks, body, jnp.zeros((8, QW), F32))
        return jnp.sum(acc8, axis=0, keepdims=True)

    topf = float(top)
    lo0 = functools.reduce(jnp.minimum, [mn8[r:r + 1, :] for r in range(8)])
    hi0 = functools.reduce(jnp.maximum, [mx8[r:r + 1, :] for r in range(8)]) + 1
    n_causal = (i * QW + 1 + lax.broadcasted_iota(I32, (1, QW), 1)).astype(F32)
    log_top = float(np.log(top))

    def in_zero_band(lo, hi):
        return jnp.where(lo >= 1, jnp.where(hi <= ZERO_BAND_END, 1.0, 0.0), 0.0) > 0.5

    def pending(lo, hi, c_lo):
        return (jnp.max(jnp.where(c_lo > topf, jnp.where(hi > lo + 1, 1.0, 0.0), 0.0)) > 0.5).astype(I32)

    def search_cond(c):
        return jnp.logical_and(c[0] < SEARCH_MAX_PASSES, c[1] > 0)

    def search_body(c):
        it, _, lo, hi, c_lo, c_hi, kept, w_lo, w_hi = c
        f_lo = (jnp.log(c_lo) - log_top) * w_lo
        f_hi = (log_top - jnp.log(jnp.maximum(c_hi, 0.5))) * w_hi
        frac = f_lo / (f_lo + f_hi)
        v_lo, v_hi = _key2f(lo), _key2f(hi)
        cand = _f2key(v_lo + frac * (v_hi - v_lo))
        lo_f, hi_f = lo.astype(F32), hi.astype(F32)
        cand = jnp.where(in_zero_band(lo, hi), (lo_f + frac * (hi_f - lo_f)).astype(I32), cand)
        key_mid = (lo >> 1) + (hi >> 1) + (lo & hi & 1)
        cand = jnp.where(jnp.abs(kept) >= SEARCH_KEPT_LIMIT, key_mid, cand)
        cand = jnp.where(it >= SEARCH_BISECT_FROM, key_mid, cand)
        cand = jnp.where(it == 0, 1, jnp.where(it == 1, ZERO_BAND_END, cand))
        cand = jnp.minimum(jnp.maximum(cand, lo + 1), hi - 1)
        cnt = count(lambda key, kpos: jnp.where(key >= cand, 1.0, 0.0))
        active = jnp.where(c_lo > topf, jnp.where(hi > lo + 1, 1.0, 0.0), 0.0) > 0.5
        up = jnp.where(active, jnp.where(cnt >= topf, 1.0, 0.0), 0.0) > 0.5
        dn = jnp.where(active, jnp.where(cnt >= topf, 0.0, 1.0), 0.0) > 0.5
        w_hi = jnp.where(up, jnp.where(kept < 0, 0.5 * w_hi, 1.0), jnp.where(dn, 1.0, w_hi))
        w_lo = jnp.where(dn, jnp.where(kept > 0, 0.5 * w_lo, 1.0), jnp.where(up, 1.0, w_lo))
        kept = jnp.where(up, jnp.where(kept < 0, kept - 1, -1), jnp.where(dn, jnp.where(kept > 0, kept + 1, 1), kept))
        lo, c_lo = jnp.where(up, cand, lo), jnp.where(up, cnt, c_lo)
        hi, c_hi = jnp.where(dn, cand, hi), jnp.where(dn, cnt, c_hi)
        hi = jnp.where(in_zero_band(lo, hi), jnp.minimum(hi, seq + 1), hi)
        return it + 1, pending(lo, hi, c_lo), lo, hi, c_lo, c_hi, kept, w_lo, w_hi

    zero_f, one_f = jnp.zeros((1, QW), F32), jnp.ones((1, QW), F32)
    _, _, thr, _, c_lo, c_hi, _, _, _ = lax.while_loop(
        search_cond, search_body,
        (jnp.int32(0), pending(lo0, hi0, n_causal), lo0, hi0, n_causal, zero_f, jnp.zeros((1, QW), I32),
         one_f, one_f))

    tied = c_lo > topf
    need1 = (topf - 1.0) - c_hi
    all_pos = jnp.full((1, QW), 2 ** idx_bits - 1, I32)

    def tie_search():
        def tie_body(bi, cur):
            cand = cur | lax.shift_left(jnp.int32(1), idx_bits - 1 - bi)
            cnt = count(lambda key, kpos: jnp.where(key == thr, jnp.where(kpos < cand, 1.0, 0.0), 0.0))
            return jnp.where(cnt <= need1, cand, cur)
        return lax.fori_loop(0, idx_bits, tie_body, jnp.zeros((1, QW), I32))

    any_tied = jnp.max(jnp.where(tied, 1.0, 0.0)) > 0.5
    last = jnp.where(tied, lax.cond(any_tied, tie_search, lambda: all_pos), all_pos)

    half_neg = int(np.float32(0.5 * NEG).view(np.int32))
    key_floor = half_neg ^ 0x7FFFFFFF

    def att_body(cidx, carry):
        m, l, acc = carry
        k0 = pl.multiple_of(cidx * KC, KC)
        hk = KC // ATT_SPLIT
        ss, m_new = [], m
        for sp in range(ATT_SPLIT):
            kb = pl.multiple_of(k0 + sp * hk, hk)
            key = key_scr[pl.ds(kb, hk), :]
            kpos = k_io[:hk] + kb
            bias = jnp.where(key > thr, 0.0, jnp.where(key == thr, jnp.where(kpos <= last, 0.0, NEG), NEG))
            bias = jnp.where(key > key_floor, bias, NEG)
            s = (jnp.dot(k_scr[pl.ds(kb, hk), :], qr_st, preferred_element_type=F32)
                 + jnp.concatenate([bias] * H, axis=1))
            ss.append(s)
            m_new = jnp.maximum(m_new, jnp.max(s, axis=0, keepdims=True))
        alpha = jnp.exp2(m - m_new)
        l, acc = alpha * l, alpha * acc
        for sp in range(ATT_SPLIT):
            kb = pl.multiple_of(k0 + sp * hk, hk)
            e = jnp.exp2(ss[sp] - m_new)
            l = l + jnp.sum(e, axis=0, keepdims=True)
            acc = acc + jnp.dot(vt_scr[:, pl.ds(kb, hk)], e.astype(BF16), preferred_element_type=F32)
        return m_new, l, acc

    init = (jnp.full((1, HW), NEG, F32), jnp.zeros((1, HW), F32), jnp.zeros((HEAD_DIM, HW), F32))
    m_a, l_a, acc_a = lax.fori_loop(0, n_chunks, att_body, init)
    o = jnp.where(m_a > 0.5 * NEG, acc_a / l_a, 0.0)
    outs = [o[:, QW * h:QW * (h + 1)] for h in range(H)]
    outs.append(jnp.zeros((o_ref.shape[1] - H * HEAD_DIM, QW), F32))
    o_ref[...] = jnp.concatenate(outs, axis=0).T.astype(BF16)


def _dsa(proj, tabs, tabs_i, batch, seq):
    qw = DSA_QW
    nb = seq // qw
    col = lambda name, w: _OFF[name] // w
    qmap = lambda cidx: (lambda b, i: (b * nb + i, cidx))
    tmap = lambda k: (lambda b, i: (k, i, 0))
    top = min(DSA_TOPK_MAX, seq // 4)
    kern = functools.partial(_dsa_kernel, top=top, idx_bits=int(seq).bit_length())
    return pl.pallas_call(
        kern,
        grid=(batch, nb),
        in_specs=[pl.BlockSpec((qw, 384), qmap(col("b_q", 384))),
                  pl.BlockSpec((qw, 256), qmap(col("b_iq", 256))),
                  pl.BlockSpec((qw, LANES), qmap(col("b_ikiw", LANES))),
                  pl.BlockSpec((qw, LANES), qmap(col("b_kv", LANES))),
                  pl.BlockSpec((None, qw, LANES), tmap(0)),
                  pl.BlockSpec((None, qw, LANES), tmap(1)),
                  pl.BlockSpec((None, qw, LANES), tmap(2)),
                  pl.BlockSpec((None, qw, LANES), tmap(0)),
                  pl.BlockSpec((None, qw, LANES), tmap(1)),
                  pl.BlockSpec((None, qw, LANES), tmap(2))],
        out_specs=pl.BlockSpec((qw, 384), lambda b, i: (b * nb + i, 0)),
        out_shape=jax.ShapeDtypeStruct((batch * seq, 384), BF16),
        scratch_shapes=[pltpu.VMEM((seq, LANES), BF16), pltpu.VMEM((HEAD_DIM, seq), BF16),
                        pltpu.VMEM((seq, LANES), BF16), pltpu.VMEM((seq, qw), I32)],
        compiler_params=_params(2),
        name="dsa",
    )(proj, proj, proj, proj, tabs, tabs, tabs, tabs_i, tabs_i, tabs_i)


DIL_MB = max(d for _, d in DIL_PAIRS) * QB
DIL_VMEM_LIMIT = 56 * 1024 * 1024
DIL_UNROLL = 4


def _dil_layout(g):
    dil = DIL_PAIRS[g][1]
    per = DIL_MB // dil
    return dil, per, per + QB


def _dil_bias_table():
    u = np.arange(2 * QB)[:, None]
    diff = QB + np.arange(LANES)[None, :] - u
    ok = (diff >= 0) & (diff <= QB)
    return _bias_table(np.concatenate([ok, ok & (u >= QB)], axis=0))


def _dil_kernel(*refs):
    G, HG = len(DIL_PAIRS), DIL_HEADS_PER_GROUP
    q_refs, k_refs, v_refs = refs[0:G], refs[G:2 * G], refs[2 * G:3 * G]
    c_ref, s1_ref, s2_ref, bias_ref, o_ref = refs[3 * G:3 * G + 5]
    kds, vds, ogs, lss = (refs[3 * G + 5 + n * G:3 * G + 5 + (n + 1) * G] for n in range(4))
    step = pl.program_id(1)
    half = ROPE_DIM // 2

    @pl.when(step == 0)
    def _zero():
        for g in range(G):
            kds[g][...] = jnp.zeros_like(kds[g])
            vds[g][...] = jnp.zeros_like(vds[g])

    @pl.when(step > 0)
    def _carry_halo():
        for g in range(G):
            dil, per, stride = _dil_layout(g)
            for r in range(dil):
                b0 = r * stride
                kds[g][b0:b0 + QB, :] = kds[g][b0 + per:b0 + per + QB, :]
                vds[g][:, b0:b0 + QB] = vds[g][:, b0 + per:b0 + per + QB]

    for g in range(G):
        dil, per, stride = _dil_layout(g)
        piece = min(per, KC)
        for r in range(dil):
            for p0 in range(0, per, piece):
                rows = pl.ds(r + dil * p0, piece, stride=dil)
                kr = _rope(k_refs[g][rows, :], c_ref[rows, :], s1_ref[rows, :], s2_ref[rows, :], half)
                d0 = r * stride + QB + p0
                kds[g][d0:d0 + piece, :] = kr.astype(BF16)
                vds[g][:, d0:d0 + piece] = v_refs[g][rows, :].T.astype(BF16)

    zero = jnp.zeros((HEAD_DIM, LANES), F32)
    for g in range(G):
        dil, per, stride = _dil_layout(g)
        nblk = per // QB

        def block(jb, _, r=0, g=g, dil=dil, stride=stride):
            p0 = pl.multiple_of(dil * QB * jb, dil * QB)
            win = pl.ds(p0, dil * QB)
            rows = pl.ds(r, QB, stride=dil)
            q = _rope(q_refs[g].at[win, :][rows, :] * (SCALE * LOG2E), c_ref.at[win, :][rows, :],
                      s1_ref.at[win, :][rows, :], s2_ref.at[win, :][rows, :], half)
            qt = q.T
            q2 = jnp.concatenate([jnp.concatenate([qt[:HEAD_DIM], zero], axis=0),
                                  jnp.concatenate([zero, qt[HEAD_DIM:]], axis=0)], axis=1).astype(BF16)
            kb = pl.multiple_of(r * stride + QB * jb, QB)
            first = jnp.logical_and(step == 0, jb == 0)
            bias = bias_ref[pl.ds(pl.multiple_of(jnp.where(first, 2 * QB, 0), QB), 2 * QB), :]
            s = (jnp.dot(kds[g][pl.ds(kb, 2 * QB), :], q2, preferred_element_type=F32)
                 + jnp.concatenate([bias] * HG, axis=1))
            e, m, den = _softmax2_cols(s)
            lse = m + jnp.log2(den)
            rden = 1.0 / den
            parts, lparts = [], []
            for hg in range(HG):
                sl = slice(LANES * hg, LANES * (hg + 1))
                vt = vds[g][HEAD_DIM * hg:HEAD_DIM * (hg + 1), pl.ds(kb, 2 * QB)]
                parts.append(jnp.dot(vt, e[:, sl].astype(BF16), preferred_element_type=F32) * rden[:, sl])
                lparts.append(jnp.broadcast_to(lse[:, sl], (HEAD_DIM, LANES)))
            tile = jnp.concatenate(parts + lparts, axis=0).T
            ogs[g].at[win, :][rows, :] = tile[:, :LANES]
            lss[g].at[win, :][rows, :] = tile[:, LANES:]
            return 0

        for r in range(dil):
            lax.fori_loop(0, nblk, functools.partial(block, r=r), 0, unroll=min(nblk, DIL_UNROLL))

    def mix(ti, _):
        rows = pl.ds(pl.multiple_of(ti * KC, KC), KC)
        ls = [lss[g][rows, :] for g in range(G)]
        mx = functools.reduce(jnp.maximum, ls)
        ex = [jnp.exp2(x - mx) for x in ls]
        rtot = 1.0 / functools.reduce(lambda a, b: a + b, ex)
        o_ref[rows, :] = jnp.concatenate([ex[g] * rtot * ogs[g][rows, :] for g in range(G)], axis=1).astype(BF16)
        return 0

    lax.fori_loop(0, DIL_MB // KC, mix, 0)


def _dil(proj, tabs, batch, seq):
    assert seq % DIL_MB == 0 and all(w // d == QB for w, d in DIL_PAIRS)
    nm = seq // DIL_MB
    G = len(DIL_PAIRS)
    w = DIL_HEADS * HEAD_DIM
    gmap = lambda name, g: (lambda b, j: (b * nm + j, _OFF[name] // LANES + g))
    tmap = lambda k: (lambda b, j: (k, j, 0))
    bias = _dil_bias_table()
    rows = [_dil_layout(g)[0] * _dil_layout(g)[2] for g in range(G)]
    return pl.pallas_call(
        _dil_kernel,
        grid=(batch, nm),
        in_specs=([pl.BlockSpec((DIL_MB, LANES), gmap(name, g)) for name in ("c_q", "c_k", "c_v") for g in range(G)]
                  + [pl.BlockSpec((None, DIL_MB, LANES), tmap(k)) for k in range(3)]
                  + [pl.BlockSpec(bias.shape, lambda b, j: (0, 0))]),
        out_specs=pl.BlockSpec((DIL_MB, w), lambda b, j: (b * nm + j, 0)),
        out_shape=jax.ShapeDtypeStruct((batch * seq, w), BF16),
        scratch_shapes=([pltpu.VMEM((n, LANES), BF16) for n in rows] + [pltpu.VMEM((LANES, n), BF16) for n in rows]
                        + [pltpu.VMEM((DIL_MB, LANES), F32)] * (2 * G)),
        compiler_params=_params(2, DIL_VMEM_LIMIT),
        name="dilated",
    )(*([proj] * (3 * G)), tabs, tabs, tabs, bias)


def _overlap_t(seq):
    n_cmp_pad = seq // CMP_STRIDE
    n_slc = seq // SLC_BLOCK
    c_start = np.arange(n_cmp_pad) * CMP_STRIDE
    s_start = np.arange(n_slc) * SLC_BLOCK
    ov = ((c_start[None, :] < s_start[:, None] + SLC_BLOCK) & (c_start[None, :] + CMP_BLOCK > s_start[:, None]))
    return jnp.asarray(ov.astype(np.float32), dtype=BF16)


def _layer(x2, batch, seq, norm1_g, w_in, cmp_pos, cmp_w1, cmp_w2, w_out, norm2_g, w_up, conv_w, conv_b, w_down,
           final_g, final_norm, tabs, tabs_i, ovt):
    n = batch * seq
    proj = _inproj(x2, norm1_g, _pad_w_in(w_in).astype(BF16))

    o_kc = _OFF["a_kcvc"]
    rows = seq // CMP_STRIDE
    kc_r = proj[:, o_kc:o_kc + HEAD_DIM].reshape(batch, rows, CMP_STRIDE * HEAD_DIM)
    vc_r = proj[:, o_kc + HEAD_DIM:o_kc + 2 * HEAD_DIM].reshape(batch, rows, CMP_STRIDE * HEAD_DIM)
    pos = cmp_pos.reshape(2, 2, CMP_STRIDE * HEAD_DIM)
    w1 = cmp_w1.reshape(2, 2, CMP_STRIDE * HEAD_DIM, CMP_HIDDEN).astype(BF16)
    zpad = jnp.zeros((CMP_HIDDEN, HEAD_DIM), cmp_w2.dtype)
    w2 = jnp.stack([jnp.concatenate([cmp_w2[0], zpad], axis=1),
                    jnp.concatenate([zpad, cmp_w2[1]], axis=1)]).astype(BF16)
    kvcmp = _nsa_compress(kc_r, vc_r, pos, w1, w2)

    o_a = _nsa(proj, kvcmp, tabs, ovt, batch, seq)
    o_b = _dsa(proj, tabs, tabs_i, batch, seq)
    o_c = _dil(proj, tabs, batch, seq)

    hp = NSA_HEADS * HEAD_DIM
    zrow = jnp.zeros((384 - hp, D_MODEL), w_out.dtype)
    w_out_pad = jnp.concatenate([w_out[0:hp], zrow, w_out[hp:2 * hp], zrow, w_out[2 * hp:]], axis=0).astype(BF16)
    x2 = _outproj(o_a, o_b, o_c, w_out_pad, x2)

    act = _ffn_up(x2, norm2_g, w_up.astype(BF16), conv_w, conv_b, seq)
    return _ffn_down(act, w_down.astype(BF16), x2, final_g, final_norm)


def kernel(x, norm1_g, w_in, cmp_pos, cmp_w1, cmp_w2, w_out, norm2_g, w_up, conv_w, conv_b, w_down, final_g):
    batch, seq, d = x.shape
    depth = w_in.shape[0]
    tabs = _rope_lane_tables(seq, HEAD_DIM, ROPE_DIM)
    tabs_i = _rope_lane_tables(seq, IDX_DIM, IDX_ROPE_DIM)
    ovt = _overlap_t(seq)
    x2 = x.reshape(batch * seq, d)
    for li in range(depth):
        x2 = _layer(x2, batch, seq, norm1_g[li], w_in[li], cmp_pos[li], cmp_w1[li], cmp_w2[li], w_out[li],
                    norm2_g[li], w_up[li], conv_w[li], conv_b[li], w_down[li], final_g, li == depth - 1,
                    tabs, tabs_i, ovt)
    return x2.reshape(batch, seq, d)
```

```python
import functools

import numpy as np
import jax
import jax.numpy as jnp
from jax import lax
from jax.experimental import pallas as pl
from jax.experimental.pallas import tpu as pltpu

F32 = jnp.float32
BF16 = jnp.bfloat16
I32 = jnp.int32

D_MODEL = 1024
HEAD_DIM = 64
ROPE_DIM = HEAD_DIM // 4
ROPE_THETA = 500000.0
NORM_EPS = 1e-6
SCALE = HEAD_DIM ** -0.5
LOG2E = 1.4426950408889634
NEG = -1e30
FORCE = 1e9
NSA_HEADS = 5
CMP_BLOCK = 32
CMP_STRIDE = 16
CMP_HIDDEN = 128
SLC_BLOCK = 64
SLC_TOPN = 16
WIN = 512
DSA_HEADS = 5
IDX_HEADS = 8
IDX_DIM = 32
IDX_ROPE_DIM = IDX_DIM // 4
DSA_TOPK_MAX = 256
DIL_PAIRS = ((128, 1), (512, 4), (2048, 16))
DIL_HEADS_PER_GROUP = 2
DIL_HEADS = len(DIL_PAIRS) * DIL_HEADS_PER_GROUP
D_FF = 2816
CONV_WIDTH = 3

LANES = 128
QB = 128
DSA_QW = 256
NSA_QW = 256
KC = 512
ATT_SPLIT = 1
ROW_TILE = 512
HALO = 16
VMEM_LIMIT = 48 * 1024 * 1024
INT_MIN = -2 ** 31

_A0 = 0
_B0 = 719
_C0 = 1463
_SEGS = (
    ("c_q", _C0, 384, 384), ("c_k", _C0 + 384, 384, 384), ("c_v", _C0 + 768, 384, 384),
    ("a_q", _A0, 320, 384), ("b_q", _B0, 320, 384),
    ("a_kcvc", _A0 + 320, 128, 128),
    ("b_iq", _B0 + 448, 256, 256),
    ("a_ksvs", _A0 + 448, 128, 128), ("a_kwvw", _A0 + 576, 128, 128),
    ("a_g", _A0 + 704, 15, 128),
    ("b_kv", _B0 + 320, 128, 128),
    ("b_ikiw", _B0 + 704, 40, 128),
)
P_COLS = sum(s[3] for s in _SEGS)


def _seg_offsets():
    offs, o = {}, 0
    for name, _, _, pw in _SEGS:
        offs[name] = o
        o += pw
    return offs


_OFF = _seg_offsets()


def _pad_w_in(w):
    parts = []
    for _, src, wdt, pw in _SEGS:
        parts.append(w[:, src:src + wdt])
        if pw > wdt:
            parts.append(jnp.zeros((w.shape[0], pw - wdt), w.dtype))
    return jnp.concatenate(parts, axis=1)


def _rope_lane_tables(L, head_dim, rot_dim):
    half = rot_dim // 2
    inv = 1.0 / (ROPE_THETA ** (np.arange(0, rot_dim, 2, dtype=np.float32) / np.float32(rot_dim)))
    ang = np.arange(L, dtype=np.float32)[:, None] * inv[None, :]
    cos, sin = np.cos(ang).astype(np.float32), np.sin(ang).astype(np.float32)
    d = np.arange(LANES) % head_dim
    lo, hi = d < half, (d >= half) & (d < rot_dim)
    c = np.ones((L, LANES), np.float32)
    s1 = np.zeros((L, LANES), np.float32)
    s2 = np.zeros((L, LANES), np.float32)
    c[:, lo] = cos[:, d[lo]]
    c[:, hi] = cos[:, d[hi] - half]
    s1[:, lo] = -sin[:, d[lo]]
    s2[:, hi] = sin[:, d[hi] - half]
    return jnp.asarray(np.stack([c, s1, s2]))


def _rope(x, c, s1, s2, half):
    xp = pltpu.roll(x, LANES - half, 1)
    xm = pltpu.roll(x, half, 1)
    return x * c + xp * s1 + xm * s2


def _rope_wide(x, c, s1, s2, half):
    n = x.shape[1] // LANES
    return jnp.concatenate([_rope(x[:, LANES * s:LANES * (s + 1)], c, s1, s2, half) for s in range(n)], axis=1)


def _softmax2_cols(s):
    m = jnp.max(s, axis=0, keepdims=True)
    e = jnp.exp2(s - m)
    return e, m, jnp.sum(e, axis=0, keepdims=True)


def _bias_table(ok):
    return jnp.asarray(np.where(ok, 0.0, NEG).astype(np.float32))


def _cmp_bias_table(seq, qw):
    shift = (qw // CMP_STRIDE) * (seq // qw - 1)
    u = np.arange(seq // CMP_STRIDE + shift)[:, None]
    lane = np.arange(qw)[None, :]
    return _bias_table(CMP_STRIDE * (u - shift) + CMP_BLOCK - 1 <= lane)


def _win_bias_table(seq, qw):
    span = min(WIN + qw, seq)
    cmax = span - qw
    u = np.arange(span + cmax)[:, None]
    diff = cmax + np.arange(qw)[None, :] - u
    return _bias_table((diff >= 0) & (diff < WIN))


def _causal_bias_table(qw):
    u = np.arange(2 * KC)[:, None]
    return _bias_table(u <= KC + np.arange(qw)[None, :])


def _params(n_grid, vmem=VMEM_LIMIT):
    return pltpu.CompilerParams(dimension_semantics=("arbitrary",) * n_grid, vmem_limit_bytes=vmem)


def _rmsnorm_rows(x, g):
    return x * lax.rsqrt(jnp.mean(x * x, axis=-1, keepdims=True) + NORM_EPS) * g


def _inproj_kernel(x_ref, g_ref, w_ref, o_ref):
    hn = _rmsnorm_rows(x_ref[...], g_ref[...]).astype(BF16)
    n = w_ref.shape[1]
    for c0 in range(0, n, 512):
        c1 = min(c0 + 512, n)
        o_ref[:, c0:c1] = jnp.dot(hn, w_ref[:, c0:c1], preferred_element_type=F32)


def _inproj(x2, g, w_bf):
    n, d = x2.shape
    pc = w_bf.shape[1]
    return pl.pallas_call(
        _inproj_kernel,
        grid=(n // ROW_TILE,),
        in_specs=[pl.BlockSpec((ROW_TILE, d), lambda i: (i, 0)),
                  pl.BlockSpec((1, d), lambda i: (0, 0)),
                  pl.BlockSpec((d, pc), lambda i: (0, 0))],
        out_specs=pl.BlockSpec((ROW_TILE, pc), lambda i: (i, 0)),
        out_shape=jax.ShapeDtypeStruct((n, pc), F32),
        compiler_params=_params(1),
        name="inproj",
    )(x2, g.reshape(1, d), w_bf)


def _outproj_kernel(oa_ref, ob_ref, oc_ref, w_ref, x_ref, o_ref):
    mix = jnp.concatenate([oa_ref[...], ob_ref[...], oc_ref[...]], axis=1)
    o_ref[...] = x_ref[...] + jnp.dot(mix, w_ref[...], preferred_element_type=F32)


def _outproj(oa, ob, oc, w_bf, x2):
    n, d = x2.shape
    k = w_bf.shape[0]
    mw = oa.shape[1]
    return pl.pallas_call(
        _outproj_kernel,
        grid=(n // ROW_TILE,),
        in_specs=[pl.BlockSpec((ROW_TILE, mw), lambda i: (i, 0)),
                  pl.BlockSpec((ROW_TILE, mw), lambda i: (i, 0)),
                  pl.BlockSpec((ROW_TILE, mw), lambda i: (i, 0)),
                  pl.BlockSpec((k, d), lambda i: (0, 0)),
                  pl.BlockSpec((ROW_TILE, d), lambda i: (i, 0))],
        out_specs=pl.BlockSpec((ROW_TILE, d), lambda i: (i, 0)),
        out_shape=jax.ShapeDtypeStruct((n, d), F32),
        compiler_params=_params(1),
        name="outproj",
    )(oa, ob, oc, w_bf, x2)


def _ffn_up_kernel(x_ref, xh_ref, g_ref, wa_ref, wu_ref, cw_ref, cb_ref, o_ref, hn_scr, hh_scr, a_scr, *, tiles_per_seq):
    i, j = pl.program_id(0), pl.program_id(1)
    tm = x_ref.shape[0]

    @pl.when(j == 0)
    def _norm():
        hn_scr[...] = _rmsnorm_rows(x_ref[...], g_ref[...]).astype(BF16)
        keep = jnp.where(i % tiles_per_seq == 0, 0.0, 1.0)
        hh_scr[...] = (_rmsnorm_rows(xh_ref[...], g_ref[...]) * keep).astype(BF16)

    hn = hn_scr[...]
    a = jnp.dot(hn, wa_ref[...], preferred_element_type=F32)
    u = jnp.dot(hn, wu_ref[...], preferred_element_type=F32)
    a_scr[0:HALO, :] = jnp.dot(hh_scr[...], wa_ref[...], preferred_element_type=F32)
    a_scr[HALO:HALO + tm, :] = a
    a1 = a_scr[pl.ds(HALO - 1, tm), :]
    a2 = a_scr[pl.ds(HALO - 2, tm), :]
    cw = cw_ref[...]
    conv = cw[0:1, :] * a2 + cw[1:2, :] * a1 + cw[2:3, :] * a + cb_ref[...]
    o_ref[...] = (conv * jax.nn.sigmoid(conv) * u).astype(BF16)


def _ffn_up(x2, g, w_up_bf, conv_w, conv_b, seq_len):
    n, d = x2.shape
    tn = D_FF // 2
    nj = D_FF // tn
    hb = ROW_TILE // HALO
    kern = functools.partial(_ffn_up_kernel, tiles_per_seq=seq_len // ROW_TILE)
    return pl.pallas_call(
        kern,
        grid=(n // ROW_TILE, nj),
        in_specs=[pl.BlockSpec((ROW_TILE, d), lambda i, j: (i, 0)),
                  pl.BlockSpec((HALO, d), lambda i, j: (jnp.maximum(i * hb - 1, 0), 0)),
                  pl.BlockSpec((1, d), lambda i, j: (0, 0)),
                  pl.BlockSpec((d, tn), lambda i, j: (0, j)),
                  pl.BlockSpec((d, tn), lambda i, j: (0, j + nj)),
                  pl.BlockSpec((CONV_WIDTH, tn), lambda i, j: (0, j)),
                  pl.BlockSpec((1, tn), lambda i, j: (0, j))],
        out_specs=pl.BlockSpec((ROW_TILE, tn), lambda i, j: (i, j)),
        out_shape=jax.ShapeDtypeStruct((n, D_FF), BF16),
        scratch_shapes=[pltpu.VMEM((ROW_TILE, d), BF16), pltpu.VMEM((HALO, d), BF16),
                        pltpu.VMEM((ROW_TILE + HALO, tn), F32)],
        compiler_params=_params(2),
        name="ffn_up",
    )(x2, x2, g.reshape(1, d), w_up_bf, w_up_bf, conv_w, conv_b.reshape(1, D_FF))


def _ffn_down_kernel(a_ref, w_ref, x_ref, g_ref, o_ref, *, final_norm):
    y = x_ref[...] + jnp.dot(a_ref[...], w_ref[...], preferred_element_type=F32)
    if final_norm:
        y = _rmsnorm_rows(y, g_ref[...])
    o_ref[...] = y


def _ffn_down(act, w_bf, x2, final_g, final_norm):
    n, d = x2.shape
    k = act.shape[1]
    return pl.pallas_call(
        functools.partial(_ffn_down_kernel, final_norm=final_norm),
        grid=(n // ROW_TILE,),
        in_specs=[pl.BlockSpec((ROW_TILE, k), lambda i: (i, 0)),
                  pl.BlockSpec((k, d), lambda i: (0, 0)),
                  pl.BlockSpec((ROW_TILE, d), lambda i: (i, 0)),
                  pl.BlockSpec((1, d), lambda i: (0, 0))],
        out_specs=pl.BlockSpec((ROW_TILE, d), lambda i: (i, 0)),
        out_shape=jax.ShapeDtypeStruct((n, d), F32),
        compiler_params=_params(1),
        name="ffn_down",
    )(act, w_bf, x2, final_g.reshape(1, d))


def _cmp_kernel(kc_ref, vc_ref, pos_ref, w1_ref, w2_ref, o_ref):
    out = None
    for c, src in enumerate((kc_ref, vc_ref)):
        r = src[0]
        lo = jnp.dot((r + pos_ref[c, 0:1, :]).astype(BF16), w1_ref[c, 0], preferred_element_type=F32)
        hi = jnp.dot((r + pos_ref[c, 1:2, :]).astype(BF16), w1_ref[c, 1], preferred_element_type=F32)
        nrows = hi.shape[0]
        hid = jax.nn.gelu(lo + pltpu.roll(hi, nrows - 1, 0))
        y = jnp.dot(hid.astype(BF16), w2_ref[c], preferred_element_type=F32)
        out = y if out is None else out + y
    o_ref[0] = out


def _nsa_compress(kc_r, vc_r, pos, w1_bf, w2_pad_bf):
    b, nr, wd = kc_r.shape
    return pl.pallas_call(
        _cmp_kernel,
        grid=(b,),
        in_specs=[pl.BlockSpec((1, nr, wd), lambda i: (i, 0, 0)),
                  pl.BlockSpec((1, nr, wd), lambda i: (i, 0, 0)),
                  pl.BlockSpec(pos.shape, lambda i: (0, 0, 0)),
                  pl.BlockSpec(w1_bf.shape, lambda i: (0, 0, 0, 0)),
                  pl.BlockSpec(w2_pad_bf.shape, lambda i: (0, 0, 0))],
        out_specs=pl.BlockSpec((1, nr, LANES), lambda i: (i, 0, 0)),
        out_shape=jax.ShapeDtypeStruct((b, nr, LANES), F32),
        compiler_params=_params(1),
        name="nsa_compress",
    )(kc_r, vc_r, pos, w1_bf, w2_pad_bf)


def _stack_heads_t(x_t, n_heads, hd):
    z = jnp.zeros((LANES - hd, x_t.shape[1]), F32)
    return jnp.concatenate(
        [jnp.concatenate([x_t[hd * h:hd * (h + 1), :], z], axis=0) for h in range(n_heads)], axis=1)


def _nsa_kernel(q_ref, g_ref, ksvs_ref, kwvw_ref, kvc_ref, c_ref, s1_ref, s2_ref, ovt_ref, cmpb_ref, winb_ref,
                caub_ref, o_ref, ks_scr, vst_scr, kw_scr, vwt_scr, kc_scr, vct_scr, blk_scr, *, n_top):
    i = pl.program_id(1)
    H = NSA_HEADS
    QW = q_ref.shape[0]
    HW = H * QW
    seq = ks_scr.shape[0]
    lane_q = lax.broadcasted_iota(I32, (QW, LANES), 1)

    @pl.when(i == 0)
    def _init():
        ks_scr[...] = jnp.zeros_like(ks_scr)
        kw_scr[...] = jnp.zeros_like(kw_scr)
        vst_scr[...] = jnp.zeros_like(vst_scr)
        vwt_scr[...] = jnp.zeros_like(vwt_scr)
        kvc = kvc_ref[0]
        lane_c = lax.broadcasted_iota(I32, kvc.shape, 1)
        kc_scr[...] = jnp.where(lane_c < HEAD_DIM, kvc, 0.0).astype(BF16)
        vct_scr[...] = kvc.T[HEAD_DIM:2 * HEAD_DIM, :].astype(BF16)

    c, s1, s2 = c_ref[...], s1_ref[...], s2_ref[...]
    r0 = pl.multiple_of(i * QW, QW)
    for src, kdst, vdst in ((ksvs_ref, ks_scr, vst_scr), (kwvw_ref, kw_scr, vwt_scr)):
        kv = src[...]
        kr = _rope(kv, c, s1, s2, ROPE_DIM // 2)
        kdst[pl.ds(r0, QW), :] = jnp.where(lane_q < HEAD_DIM, kr, 0.0).astype(BF16)
        vdst[:, pl.ds(r0, QW)] = kv.T[HEAD_DIM:2 * HEAD_DIM, :].astype(BF16)

    q = q_ref[...] * (SCALE * LOG2E)
    qr = _rope_wide(q, c, s1, s2, ROPE_DIM // 2)
    q_st = _stack_heads_t(q.T, H, HEAD_DIM).astype(BF16)
    qr_st = _stack_heads_t(qr.T, H, HEAD_DIM).astype(BF16)
    tile_h = lambda b: jnp.concatenate([b] * H, axis=1)

    ncp = kc_scr.shape[0]
    cshift = cmpb_ref.shape[0] - ncp
    cb = cmpb_ref[pl.ds(pl.multiple_of(cshift - (QW // CMP_STRIDE) * i, 8), ncp), :]
    s_c = jnp.dot(kc_scr[...], q_st, preferred_element_type=F32) + tile_h(cb)
    e_c, _, den_c = _softmax2_cols(s_c)
    t_row = i * QW + (lax.broadcasted_iota(I32, (1, HW), 1) & (QW - 1))
    r_c = jnp.where(t_row >= CMP_BLOCK - 1, 1.0 / den_c, 0.0)
    o_cmp = jnp.dot(vct_scr[...], e_c.astype(BF16), preferred_element_type=F32) * r_c
    psum = e_c[:, 0:QW] * r_c[:, 0:QW]
    for h in range(1, H):
        psum = psum + e_c[:, QW * h:QW * (h + 1)] * r_c[:, QW * h:QW * (h + 1)]

    hi = psum.astype(BF16)
    r1 = psum - hi.astype(F32)
    mid = r1.astype(BF16)
    lo = (r1 - mid.astype(F32)).astype(BF16)
    ovt = ovt_ref[...]
    imp = (jnp.dot(ovt, hi, preferred_element_type=F32) + jnp.dot(ovt, mid, preferred_element_type=F32)
           + jnp.dot(ovt, lo, preferred_element_type=F32))
    n_slc = imp.shape[0]
    jb = lax.broadcasted_iota(I32, (n_slc, QW), 0)
    tq = i * QW + lax.broadcasted_iota(I32, (n_slc, QW), 1)
    cur = tq >> (SLC_BLOCK.bit_length() - 1)
    forced = (jb == 0) | (jb == cur) | (jb == cur - 1)
    val = jnp.where(forced, FORCE, jnp.where(jb <= cur, imp, NEG))
    rank = jnp.zeros((n_slc, QW), I32)
    for jp in range(n_slc):
        row = val[jp:jp + 1, :]
        tie = jnp.where(jb > jp, 1, 0)
        rank = rank + jnp.where(row > val, 1, jnp.where(row == val, tie, 0))
    blk_scr[...] = jnp.where(rank < n_top, jnp.where(val > 0.5 * NEG, 0.0, NEG), NEG)

    bpc = KC // SLC_BLOCK

    def slc_body(ci, carry):
        m, l, acc = carry
        k0 = pl.multiple_of(ci * KC, KC)
        ahead = jnp.minimum(i * QW - k0, KC)
        c0 = pl.multiple_of(KC - ahead, QB)
        hk = KC // ATT_SPLIT
        ss, m_new = [], m
        for sp in range(ATT_SPLIT):
            kb = pl.multiple_of(k0 + sp * hk, hk)
            rows = [jnp.broadcast_to(blk_scr[pl.ds(ci * bpc + sp * (bpc // ATT_SPLIT) + r, 1), :], (SLC_BLOCK, QW))
                    for r in range(bpc // ATT_SPLIT)]
            bias = jnp.concatenate(rows, axis=0) + caub_ref[pl.ds(pl.multiple_of(c0 + sp * hk, QB), hk), :]
            s = jnp.dot(ks_scr[pl.ds(kb, hk), :], qr_st, preferred_element_type=F32) + tile_h(bias)
            ss.append(s)
            m_new = jnp.maximum(m_new, jnp.max(s, axis=0, keepdims=True))
        alpha = jnp.exp2(m - m_new)
        l, acc = alpha * l, alpha * acc
        for sp in range(ATT_SPLIT):
            kb = pl.multiple_of(k0 + sp * hk, hk)
            e = jnp.exp2(ss[sp] - m_new)
            l = l + jnp.sum(e, axis=0, keepdims=True)
            acc = acc + jnp.dot(vst_scr[:, pl.ds(kb, hk)], e.astype(BF16), preferred_element_type=F32)
        return m_new, l, acc

    n_chunks = (i * QW + QW + KC - 1) // KC
    init = (jnp.full((1, HW), NEG, F32), jnp.zeros((1, HW), F32), jnp.zeros((HEAD_DIM, HW), F32))
    m_s, l_s, acc_s = lax.fori_loop(0, n_chunks, slc_body, init)
    o_slc = jnp.where(m_s > 0.5 * NEG, acc_s / l_s, 0.0)

    span = min(WIN + QW, seq)
    w0 = pl.multiple_of(jnp.maximum(i * QW + QW - span, 0), QB)
    wb = winb_ref[pl.ds(pl.multiple_of(span - QW - (i * QW - w0), QB), span), :]
    s_w = jnp.dot(kw_scr[pl.ds(w0, span), :], qr_st, preferred_element_type=F32) + tile_h(wb)
    e_w, _, den_w = _softmax2_cols(s_w)
    o_win = jnp.dot(vwt_scr[:, pl.ds(w0, span)], e_w.astype(BF16), preferred_element_type=F32) * (1.0 / den_w)

    gt = jax.nn.sigmoid(g_ref[...].T)
    outs = []
    for h in range(H):
        sl = slice(QW * h, QW * (h + 1))
        outs.append(gt[3 * h:3 * h + 1, :] * o_cmp[:, sl] + gt[3 * h + 1:3 * h + 2, :] * o_slc[:, sl]
                    + gt[3 * h + 2:3 * h + 3, :] * o_win[:, sl])
    outs.append(jnp.zeros((o_ref.shape[1] - H * HEAD_DIM, QW), F32))
    o_ref[...] = jnp.concatenate(outs, axis=0).T.astype(BF16)


def _nsa(proj, kvcmp, tabs, ovt, batch, seq):
    qw = NSA_QW
    nb = seq // qw
    n_slc = seq // SLC_BLOCK
    ncp = kvcmp.shape[1]
    col = lambda name, w: _OFF[name] // w
    qmap = lambda cidx: (lambda b, i: (b * nb + i, cidx))
    tmap = lambda k: (lambda b, i: (k, i, 0))
    kern = functools.partial(_nsa_kernel, n_top=min(SLC_TOPN, n_slc))
    cmpb, winb, caub = _cmp_bias_table(seq, qw), _win_bias_table(seq, qw), _causal_bias_table(qw)
    whole = lambda a: pl.BlockSpec(a.shape, lambda b, i: (0, 0))
    return pl.pallas_call(
        kern,
        grid=(batch, nb),
        in_specs=[pl.BlockSpec((qw, 384), qmap(col("a_q", 384))),
                  pl.BlockSpec((qw, LANES), qmap(col("a_g", LANES))),
                  pl.BlockSpec((qw, LANES), qmap(col("a_ksvs", LANES))),
                  pl.BlockSpec((qw, LANES), qmap(col("a_kwvw", LANES))),
                  pl.BlockSpec((1, ncp, LANES), lambda b, i: (b, 0, 0)),
                  pl.BlockSpec((None, qw, LANES), tmap(0)),
                  pl.BlockSpec((None, qw, LANES), tmap(1)),
                  pl.BlockSpec((None, qw, LANES), tmap(2)),
                  whole(ovt), whole(cmpb), whole(winb), whole(caub)],
        out_specs=pl.BlockSpec((qw, 384), lambda b, i: (b * nb + i, 0)),
        out_shape=jax.ShapeDtypeStruct((batch * seq, 384), BF16),
        scratch_shapes=[pltpu.VMEM((seq, LANES), BF16), pltpu.VMEM((HEAD_DIM, seq), BF16),
                        pltpu.VMEM((seq, LANES), BF16), pltpu.VMEM((HEAD_DIM, seq), BF16),
                        pltpu.VMEM((ncp, LANES), BF16), pltpu.VMEM((HEAD_DIM, ncp), BF16),
                        pltpu.VMEM((n_slc, qw), F32)],
        compiler_params=_params(2),
        name="nsa",
    )(proj, proj, proj, proj, kvcmp, tabs, tabs, tabs, ovt, cmpb, winb, caub)


SEARCH_BISECT_FROM = 24
SEARCH_MAX_PASSES = 64
SEARCH_KEPT_LIMIT = 6
SEARCH_BLIND_PASSES = 12
INT_MAX = 2 ** 31 - 1
FLT_MIN_NORMAL = 1.17549435e-38
ZERO_BAND_END = 1 << 23


def _tree_rows(x, op):
    parts = [x[r:r + 8, :] for r in range(0, x.shape[0], 8)]
    while len(parts) > 1:
        nxt = [op(parts[a], parts[a + 1]) for a in range(0, len(parts) - 1, 2)]
        if len(parts) % 2:
            nxt.append(parts[-1])
        parts = nxt
    return parts[0]


def _f2key(v):
    bits = lax.bitcast_convert_type(v, I32)
    return bits ^ ((bits >> 31) & 0x7FFFFFFF)


def _key2f(k):
    return lax.bitcast_convert_type(k ^ ((k >> 31) & 0x7FFFFFFF), F32)


def _dsa_kernel(q_ref, iq_ref, ikw_ref, kv_ref, c_ref, s1_ref, s2_ref, ci_ref, si1_ref, si2_ref, o_ref,
                k_scr, vt_scr, ik_scr, key_scr, *, top, idx_bits):
    i = pl.program_id(1)
    H = DSA_HEADS
    QW = q_ref.shape[0]
    HW = H * QW
    seq = key_scr.shape[0]
    lane_q = lax.broadcasted_iota(I32, (QW, LANES), 1)

    @pl.when(i == 0)
    def _init():
        k_scr[...] = jnp.zeros_like(k_scr)
        vt_scr[...] = jnp.zeros_like(vt_scr)
        ik_scr[...] = jnp.zeros_like(ik_scr)

    c, s1, s2 = c_ref[...], s1_ref[...], s2_ref[...]
    ci, si1, si2 = ci_ref[...], si1_ref[...], si2_ref[...]
    r0 = pl.multiple_of(i * QW, QW)
    kv = kv_ref[...]
    k_scr[pl.ds(r0, QW), :] = jnp.where(lane_q < HEAD_DIM, _rope(kv, c, s1, s2, ROPE_DIM // 2), 0.0).astype(BF16)
    vt_scr[:, pl.ds(r0, QW)] = kv.T[HEAD_DIM:2 * HEAD_DIM, :].astype(BF16)
    ikw = ikw_ref[...]
    ik_scr[pl.ds(r0, QW), :] = jnp.where(lane_q < IDX_DIM, _rope(ikw, ci, si1, si2, IDX_ROPE_DIM // 2), 0.0).astype(BF16)

    qr = _rope_wide(q_ref[...] * (SCALE * LOG2E), c, s1, s2, ROPE_DIM // 2)
    qr_st = _stack_heads_t(qr.T, H, HEAD_DIM).astype(BF16)
    iqr = _rope_wide(iq_ref[...], ci, si1, si2, IDX_ROPE_DIM // 2)
    iq_st = _stack_heads_t(iqr.T, IDX_HEADS, IDX_DIM).astype(BF16)
    wt = ikw.T * ((IDX_DIM ** -0.5) * (IDX_HEADS ** -0.5))
    w_st = jnp.concatenate([wt[IDX_DIM + h:IDX_DIM + h + 1, :] for h in range(IDX_HEADS)], axis=1)

    n_chunks = (i * QW + QW + KC - 1) // KC
    k_io = lax.broadcasted_iota(I32, (KC, QW), 0)
    tq = i * QW + lax.broadcasted_iota(I32, (KC, QW), 1)

    sub_io = lax.broadcasted_iota(I32, (QB, QW), 0)
    sub_tq = i * QW + lax.broadcasted_iota(I32, (QB, QW), 1)

    def score_body(cidx, carry):
        mx, mn = carry
        k0 = pl.multiple_of(cidx * KC, KC)
        for sb in range(KC // QB):
            kb = pl.multiple_of(k0 + sb * QB, QB)
            d = jnp.dot(ik_scr[pl.ds(kb, QB), :], iq_st, preferred_element_type=F32)
            r = jnp.maximum(d, 0.0) * w_st
            sc = r[:, 0:QW]
            for h in range(1, IDX_HEADS):
                sc = sc + r[:, QW * h:QW * (h + 1)]
            kpos = sub_io + kb
            key = jnp.where(jnp.abs(sc) < FLT_MIN_NORMAL, seq - kpos, _f2key(sc))
            causal = kpos <= sub_tq
            key_scr[pl.ds(kb, QB), :] = jnp.where(causal, key, INT_MIN)
            mx = jnp.maximum(mx, _tree_rows(jnp.where(causal, key, INT_MIN), jnp.maximum))
            mn = jnp.minimum(mn, _tree_rows(jnp.where(causal, key, INT_MAX), jnp.minimum))
        return mx, mn

    mx8, mn8 = lax.fori_loop(0, n_chunks, score_body,
                             (jnp.full((8, QW), INT_MIN, I32), jnp.full((8, QW), INT_MAX, I32)))

    def count(pred):
        def body(cidx, acc):
            k0 = pl.multiple_of(cidx * KC, KC)
            return acc + _tree_rows(pred(key_scr[pl.ds(k0, KC), :], k_io + k0), jnp.add)
        acc8 = lax.fori_loop(0, n_chunks, body, jnp.zeros((8, QW), F32))
        return jnp.sum(acc8, axis=0, keepdims=True)

    topf = float(top)
    lo0 = functools.reduce(jnp.minimum, [mn8[r:r + 1, :] for r in range(8)])
    hi0 = functools.reduce(jnp.maximum, [mx8[r:r + 1, :] for r in range(8)]) + 1
    n_causal = (i * QW + 1 + lax.broadcasted_iota(I32, (1, QW), 1)).astype(F32)
    log_top = float(np.log(top))

    def in_zero_band(lo, hi):
        return jnp.where(lo >= 1, jnp.where(hi <= ZERO_BAND_END, 1.0, 0.0), 0.0) > 0.5

    def pending(lo, hi, c_lo):
        return (jnp.max(jnp.where(c_lo > topf, jnp.where(hi > lo + 1, 1.0, 0.0), 0.0)) > 0.5).astype(I32)

    def search_cond(c):
        return jnp.logical_and(c[0] < SEARCH_MAX_PASSES, c[1] > 0)

    def search_pass(it, st):
        lo, hi, c_lo, c_hi, kept, w_lo, w_hi = st
        f_lo = (jnp.log(c_lo) - log_top) * w_lo
        f_hi = (log_top - jnp.log(jnp.maximum(c_hi, 0.5))) * w_hi
        frac = f_lo / (f_lo + f_hi)
        v_lo, v_hi = _key2f(lo), _key2f(hi)
        cand = _f2key(v_lo + frac * (v_hi - v_lo))
        lo_f, hi_f = lo.astype(F32), hi.astype(F32)
        cand = jnp.where(in_zero_band(lo, hi), (lo_f + frac * (hi_f - lo_f)).astype(I32), cand)
        key_mid = (lo >> 1) + (hi >> 1) + (lo & hi & 1)
        cand = jnp.where(jnp.abs(kept) >= SEARCH_KEPT_LIMIT, key_mid, cand)
        cand = jnp.where(it >= SEARCH_BISECT_FROM, key_mid, cand)
        cand = jnp.where(it == 0, 1, jnp.where(it == 1, ZERO_BAND_END, cand))
        cand = jnp.minimum(jnp.maximum(cand, lo + 1), hi - 1)
        cnt = count(lambda key, kpos: jnp.where(key >= cand, 1.0, 0.0))
        active = jnp.where(c_lo > topf, jnp.where(hi > lo + 1, 1.0, 0.0), 0.0) > 0.5
        up = jnp.where(active, jnp.where(cnt >= topf, 1.0, 0.0), 0.0) > 0.5
        dn = jnp.where(active, jnp.where(cnt >= topf, 0.0, 1.0), 0.0) > 0.5
        w_hi = jnp.where(up, jnp.where(kept < 0, 0.5 * w_hi, 1.0), jnp.where(dn, 1.0, w_hi))
        w_lo = jnp.where(dn, jnp.where(kept > 0, 0.5 * w_lo, 1.0), jnp.where(up, 1.0, w_lo))
        kept = jnp.where(up, jnp.where(kept < 0, kept - 1, -1), jnp.where(dn, jnp.where(kept > 0, kept + 1, 1), kept))
        lo, c_lo = jnp.where(up, cand, lo), jnp.where(up, cnt, c_lo)
        hi, c_hi = jnp.where(dn, cand, hi), jnp.where(dn, cnt, c_hi)
        hi = jnp.where(in_zero_band(lo, hi), jnp.minimum(hi, seq + 1), hi)
        return lo, hi, c_lo, c_hi, kept, w_lo, w_hi

    def search_body(c):
        st = search_pass(c[0], c[2])
        return c[0] + 1, pending(st[0], st[1], st[2]), st

    zero_f, one_f = jnp.zeros((1, QW), F32), jnp.ones((1, QW), F32)
    st0 = (lo0, hi0, n_causal, zero_f, jnp.zeros((1, QW), I32), one_f, one_f)
    n_blind = jnp.where(i * QW + QW > top, SEARCH_BLIND_PASSES, 0)
    st1 = lax.fori_loop(0, n_blind, search_pass, st0)
    _, _, (thr, _, c_lo, c_hi, _, _, _) = lax.while_loop(
        search_cond, search_body, (n_blind, pending(st1[0], st1[1], st1[2]), st1))

    tied = c_lo > topf
    need1 = (topf - 1.0) - c_hi
    all_pos = jnp.full((1, QW), 2 ** idx_bits - 1, I32)

    def tie_search():
        def tie_body(bi, cur):
            cand = cur | lax.shift_left(jnp.int32(1), idx_bits - 1 - bi)
            cnt = count(lambda key, kpos: jnp.where(key == thr, jnp.where(kpos < cand, 1.0, 0.0), 0.0))
            return jnp.where(cnt <= need1, cand, cur)
        return lax.fori_loop(0, idx_bits, tie_body, jnp.zeros((1, QW), I32))

    any_tied = jnp.max(jnp.where(tied, 1.0, 0.0)) > 0.5
    last = jnp.where(tied, lax.cond(any_tied, tie_search, lambda: all_pos), all_pos)

    half_neg = int(np.float32(0.5 * NEG).view(np.int32))
    key_floor = half_neg ^ 0x7FFFFFFF

    def att_chunk(cidx, state):
        m, l, acc = state
        k0 = pl.multiple_of(cidx * KC, KC)
        key = key_scr[pl.ds(k0, KC), :]
        kpos = k_io + k0
        bias = jnp.where(key > thr, 0.0, jnp.where(key == thr, jnp.where(kpos <= last, 0.0, NEG), NEG))
        bias = jnp.where(key > key_floor, bias, NEG)
        s = (jnp.dot(k_scr[pl.ds(k0, KC), :], qr_st, preferred_element_type=F32)
             + jnp.concatenate([bias] * H, axis=1))
        m_new = jnp.maximum(m, jnp.max(s, axis=0, keepdims=True))
        alpha = jnp.exp2(m - m_new)
        e = jnp.exp2(s - m_new)
        l = alpha * l + jnp.sum(e, axis=0, keepdims=True)
        acc = alpha * acc + jnp.dot(vt_scr[:, pl.ds(k0, KC)], e.astype(BF16), preferred_element_type=F32)
        return m_new, l, acc

    init = (jnp.full((1, HW), NEG, F32), jnp.zeros((1, HW), F32), jnp.zeros((HEAD_DIM, HW), F32))
    m_a, l_a, acc_a = lax.fori_loop(0, n_chunks, att_chunk, init)
    o = jnp.where(m_a > 0.5 * NEG, acc_a / l_a, 0.0)
    outs = [o[:, QW * h:QW * (h + 1)] for h in range(H)]
    outs.append(jnp.zeros((o_ref.shape[1] - H * HEAD_DIM, QW), F32))
    o_ref[...] = jnp.concatenate(outs, axis=0).T.astype(BF16)


def _dsa(proj, tabs, tabs_i, batch, seq):
    qw = DSA_QW
    nb = seq // qw
    col = lambda name, w: _OFF[name] // w
    qmap = lambda cidx: (lambda b, i: (b * nb + i, cidx))
    tmap = lambda k: (lambda b, i: (k, i, 0))
    top = min(DSA_TOPK_MAX, seq // 4)
    kern = functools.partial(_dsa_kernel, top=top, idx_bits=int(seq).bit_length())
    return pl.pallas_call(
        kern,
        grid=(batch, nb),
        in_specs=[pl.BlockSpec((qw, 384), qmap(col("b_q", 384))),
                  pl.BlockSpec((qw, 256), qmap(col("b_iq", 256))),
                  pl.BlockSpec((qw, LANES), qmap(col("b_ikiw", LANES))),
                  pl.BlockSpec((qw, LANES), qmap(col("b_kv", LANES))),
                  pl.BlockSpec((None, qw, LANES), tmap(0)),
                  pl.BlockSpec((None, qw, LANES), tmap(1)),
                  pl.BlockSpec((None, qw, LANES), tmap(2)),
                  pl.BlockSpec((None, qw, LANES), tmap(0)),
                  pl.BlockSpec((None, qw, LANES), tmap(1)),
                  pl.BlockSpec((None, qw, LANES), tmap(2))],
        out_specs=pl.BlockSpec((qw, 384), lambda b, i: (b * nb + i, 0)),
        out_shape=jax.ShapeDtypeStruct((batch * seq, 384), BF16),
        scratch_shapes=[pltpu.VMEM((seq, LANES), BF16), pltpu.VMEM((HEAD_DIM, seq), BF16),
                        pltpu.VMEM((seq, LANES), BF16), pltpu.VMEM((seq, qw), I32)],
        compiler_params=_params(2),
        name="dsa",
    )(proj, proj, proj, proj, tabs, tabs, tabs, tabs_i, tabs_i, tabs_i)


DIL_MB = max(d for _, d in DIL_PAIRS) * QB
DIL_VMEM_LIMIT = 56 * 1024 * 1024
DIL_UNROLL = 4


def _dil_layout(g):
    dil = DIL_PAIRS[g][1]
    per = DIL_MB // dil
    return dil, per, per + QB


def _dil_bias_table():
    u = np.arange(2 * QB)[:, None]
    diff = QB + np.arange(LANES)[None, :] - u
    ok = (diff >= 0) & (diff <= QB)
    return _bias_table(np.concatenate([ok, ok & (u >= QB)], axis=0))


def _dil_kernel(*refs):
    G, HG = len(DIL_PAIRS), DIL_HEADS_PER_GROUP
    q_refs, k_refs, v_refs = refs[0:G], refs[G:2 * G], refs[2 * G:3 * G]
    c_ref, s1_ref, s2_ref, bias_ref, o_ref = refs[3 * G:3 * G + 5]
    kds, vds, ogs, lss = (refs[3 * G + 5 + n * G:3 * G + 5 + (n + 1) * G] for n in range(4))
    step = pl.program_id(1)
    half = ROPE_DIM // 2

    @pl.when(step == 0)
    def _zero():
        for g in range(G):
            kds[g][...] = jnp.zeros_like(kds[g])
            vds[g][...] = jnp.zeros_like(vds[g])

    @pl.when(step > 0)
    def _carry_halo():
        for g in range(G):
            dil, per, stride = _dil_layout(g)
            for r in range(dil):
                b0 = r * stride
                kds[g][b0:b0 + QB, :] = kds[g][b0 + per:b0 + per + QB, :]
                vds[g][:, b0:b0 + QB] = vds[g][:, b0 + per:b0 + per + QB]

    for g in range(G):
        dil, per, stride = _dil_layout(g)
        piece = min(per, KC)
        for r in range(dil):
            for p0 in range(0, per, piece):
                rows = pl.ds(r + dil * p0, piece, stride=dil)
                kr = _rope(k_refs[g][rows, :], c_ref[rows, :], s1_ref[rows, :], s2_ref[rows, :], half)
                d0 = r * stride + QB + p0
                kds[g][d0:d0 + piece, :] = kr.astype(BF16)
                vds[g][:, d0:d0 + piece] = v_refs[g][rows, :].T.astype(BF16)

    zero = jnp.zeros((HEAD_DIM, LANES), F32)
    for g in range(G):
        dil, per, stride = _dil_layout(g)
        nblk = per // QB

        def block(jb, _, r=0, g=g, dil=dil, stride=stride):
            p0 = pl.multiple_of(dil * QB * jb, dil * QB)
            win = pl.ds(p0, dil * QB)
            rows = pl.ds(r, QB, stride=dil)
            q = _rope(q_refs[g].at[win, :][rows, :] * (SCALE * LOG2E), c_ref.at[win, :][rows, :],
                      s1_ref.at[win, :][rows, :], s2_ref.at[win, :][rows, :], half)
            qt = q.T
            q2 = jnp.concatenate([jnp.concatenate([qt[:HEAD_DIM], zero], axis=0),
                                  jnp.concatenate([zero, qt[HEAD_DIM:]], axis=0)], axis=1).astype(BF16)
            kb = pl.multiple_of(r * stride + QB * jb, QB)
            first = jnp.logical_and(step == 0, jb == 0)
            bias = bias_ref[pl.ds(pl.multiple_of(jnp.where(first, 2 * QB, 0), QB), 2 * QB), :]
            s = (jnp.dot(kds[g][pl.ds(kb, 2 * QB), :], q2, preferred_element_type=F32)
                 + jnp.concatenate([bias] * HG, axis=1))
            e, m, den = _softmax2_cols(s)
            lse = m + jnp.log2(den)
            rden = 1.0 / den
            parts, lparts = [], []
            for hg in range(HG):
                sl = slice(LANES * hg, LANES * (hg + 1))
                vt = vds[g][HEAD_DIM * hg:HEAD_DIM * (hg + 1), pl.ds(kb, 2 * QB)]
                parts.append(jnp.dot(vt, e[:, sl].astype(BF16), preferred_element_type=F32) * rden[:, sl])
                lparts.append(jnp.broadcast_to(lse[:, sl], (HEAD_DIM, LANES)))
            tile = jnp.concatenate(parts + lparts, axis=0).T
            ogs[g].at[win, :][rows, :] = tile[:, :LANES]
            lss[g].at[win, :][rows, :] = tile[:, LANES:]
            return 0

        for r in range(dil):
            lax.fori_loop(0, nblk, functools.partial(block, r=r), 0, unroll=min(nblk, DIL_UNROLL))

    def mix(ti, _):
        rows = pl.ds(pl.multiple_of(ti * KC, KC), KC)
        ls = [lss[g][rows, :] for g in range(G)]
        mx = functools.reduce(jnp.maximum, ls)
        ex = [jnp.exp2(x - mx) for x in ls]
        rtot = 1.0 / functools.reduce(lambda a, b: a + b, ex)
        o_ref[rows, :] = jnp.concatenate([ex[g] * rtot * ogs[g][rows, :] for g in range(G)], axis=1).astype(BF16)
        return 0

    lax.fori_loop(0, DIL_MB // KC, mix, 0)


def _dil(proj, tabs, batch, seq):
    assert seq % DIL_MB == 0 and all(w // d == QB for w, d in DIL_PAIRS)
    nm = seq // DIL_MB
    G = len(DIL_PAIRS)
    w = DIL_HEADS * HEAD_DIM
    gmap = lambda name, g: (lambda b, j: (b * nm + j, _OFF[name] // LANES + g))
    tmap = lambda k: (lambda b, j: (k, j, 0))
    bias = _dil_bias_table()
    rows = [_dil_layout(g)[0] * _dil_layout(g)[2] for g in range(G)]
    return pl.pallas_call(
        _dil_kernel,
        grid=(batch, nm),
        in_specs=([pl.BlockSpec((DIL_MB, LANES), gmap(name, g)) for name in ("c_q", "c_k", "c_v") for g in range(G)]
                  + [pl.BlockSpec((None, DIL_MB, LANES), tmap(k)) for k in range(3)]
                  + [pl.BlockSpec(bias.shape, lambda b, j: (0, 0))]),
        out_specs=pl.BlockSpec((DIL_MB, w), lambda b, j: (b * nm + j, 0)),
        out_shape=jax.ShapeDtypeStruct((batch * seq, w), BF16),
        scratch_shapes=([pltpu.VMEM((n, LANES), BF16) for n in rows] + [pltpu.VMEM((LANES, n), BF16) for n in rows]
                        + [pltpu.VMEM((DIL_MB, LANES), F32)] * (2 * G)),
        compiler_params=_params(2, DIL_VMEM_LIMIT),
        name="dilated",
    )(*([proj] * (3 * G)), tabs, tabs, tabs, bias)


def _overlap_t(seq):
    n_cmp_pad = seq // CMP_STRIDE
    n_slc = seq // SLC_BLOCK
    c_start = np.arange(n_cmp_pad) * CMP_STRIDE
    s_start = np.arange(n_slc) * SLC_BLOCK
    ov = ((c_start[None, :] < s_start[:, None] + SLC_BLOCK) & (c_start[None, :] + CMP_BLOCK > s_start[:, None]))
    return jnp.asarray(ov.astype(np.float32), dtype=BF16)


def _layer(x2, batch, seq, norm1_g, w_in, cmp_pos, cmp_w1, cmp_w2, w_out, norm2_g, w_up, conv_w, conv_b, w_down,
           final_g, final_norm, tabs, tabs_i, ovt):
    n = batch * seq
    proj = _inproj(x2, norm1_g, _pad_w_in(w_in).astype(BF16))

    o_kc = _OFF["a_kcvc"]
    rows = seq // CMP_STRIDE
    kc_r = proj[:, o_kc:o_kc + HEAD_DIM].reshape(batch, rows, CMP_STRIDE * HEAD_DIM)
    vc_r = proj[:, o_kc + HEAD_DIM:o_kc + 2 * HEAD_DIM].reshape(batch, rows, CMP_STRIDE * HEAD_DIM)
    pos = cmp_pos.reshape(2, 2, CMP_STRIDE * HEAD_DIM)
    w1 = cmp_w1.reshape(2, 2, CMP_STRIDE * HEAD_DIM, CMP_HIDDEN).astype(BF16)
    zpad = jnp.zeros((CMP_HIDDEN, HEAD_DIM), cmp_w2.dtype)
    w2 = jnp.stack([jnp.concatenate([cmp_w2[0], zpad], axis=1),
                    jnp.concatenate([zpad, cmp_w2[1]], axis=1)]).astype(BF16)
    kvcmp = _nsa_compress(kc_r, vc_r, pos, w1, w2)

    o_a = _nsa(proj, kvcmp, tabs, ovt, batch, seq)
    o_b = _dsa(proj, tabs, tabs_i, batch, seq)
    o_c = _dil(proj, tabs, batch, seq)

    hp = NSA_HEADS * HEAD_DIM
    zrow = jnp.zeros((384 - hp, D_MODEL), w_out.dtype)
    w_out_pad = jnp.concatenate([w_out[0:hp], zrow, w_out[hp:2 * hp], zrow, w_out[2 * hp:]], axis=0).astype(BF16)
    x2 = _outproj(o_a, o_b, o_c, w_out_pad, x2)

    act = _ffn_up(x2, norm2_g, w_up.astype(BF16), conv_w, conv_b, seq)
    return _ffn_down(act, w_down.astype(BF16), x2, final_g, final_norm)


def kernel(x, norm1_g, w_in, cmp_pos, cmp_w1, cmp_w2, w_out, norm2_g, w_up, conv_w, conv_b, w_down, final_g):
    batch, seq, d = x.shape
    depth = w_in.shape[0]
    tabs = _rope_lane_tables(seq, HEAD_DIM, ROPE_DIM)
    tabs_i = _rope_lane_tables(seq, IDX_DIM, IDX_ROPE_DIM)
    ovt = _overlap_t(seq)
    x2 = x.reshape(batch * seq, d)
    for li in range(depth):
        x2 = _layer(x2, batch, seq, norm1_g[li], w_in[li], cmp_pos[li], cmp_w1[li], cmp_w2[li], w_out[li],
                    norm2_g[li], w_up[li], conv_w[li], conv_b[li], w_down[li], final_g, li == depth - 1,
                    tabs, tabs_i, ovt)
    return x2.reshape(batch, seq, d)
```

```python
import functools

import numpy as np
import jax
import jax.numpy as jnp
from jax import lax
from jax.experimental import pallas as pl
from jax.experimental.pallas import tpu as pltpu

F32 = jnp.float32
BF16 = jnp.bfloat16
I32 = jnp.int32

D_MODEL = 1024
HEAD_DIM = 64
ROPE_DIM = HEAD_DIM // 4
ROPE_THETA = 500000.0
NORM_EPS = 1e-6
SCALE = HEAD_DIM ** -0.5
LOG2E = 1.4426950408889634
NEG = -1e30
FORCE = 1e9
NSA_HEADS = 5
CMP_BLOCK = 32
CMP_STRIDE = 16
CMP_HIDDEN = 128
SLC_BLOCK = 64
SLC_TOPN = 16
WIN = 512
DSA_HEADS = 5
IDX_HEADS = 8
IDX_DIM = 32
IDX_ROPE_DIM = IDX_DIM // 4
DSA_TOPK_MAX = 256
DIL_PAIRS = ((128, 1), (512, 4), (2048, 16))
DIL_HEADS_PER_GROUP = 2
DIL_HEADS = len(DIL_PAIRS) * DIL_HEADS_PER_GROUP
D_FF = 2816
CONV_WIDTH = 3

LANES = 128
QB = 128
DSA_QW = 256
NSA_QW = 256
KC = 512
ATT_SPLIT = 1
ROW_TILE = 512
HALO = 16
VMEM_LIMIT = 48 * 1024 * 1024
INT_MIN = -2 ** 31

_A0 = 0
_B0 = 719
_C0 = 1463
_SEGS = (
    ("c_q", _C0, 384, 384), ("c_k", _C0 + 384, 384, 384), ("c_v", _C0 + 768, 384, 384),
    ("a_q", _A0, 320, 384), ("b_q", _B0, 320, 384),
    ("a_kcvc", _A0 + 320, 128, 128),
    ("b_iq", _B0 + 448, 256, 256),
    ("a_ksvs", _A0 + 448, 128, 128), ("a_kwvw", _A0 + 576, 128, 128),
    ("a_g", _A0 + 704, 15, 128),
    ("b_kv", _B0 + 320, 128, 128),
    ("b_ikiw", _B0 + 704, 40, 128),
)
P_COLS = sum(s[3] for s in _SEGS)


def _seg_offsets():
    offs, o = {}, 0
    for name, _, _, pw in _SEGS:
        offs[name] = o
        o += pw
    return offs


_OFF = _seg_offsets()


def _pad_w_in(w):
    parts = []
    for _, src, wdt, pw in _SEGS:
        parts.append(w[:, src:src + wdt])
        if pw > wdt:
            parts.append(jnp.zeros((w.shape[0], pw - wdt), w.dtype))
    return jnp.concatenate(parts, axis=1)


def _rope_lane_tables(L, head_dim, rot_dim):
    half = rot_dim // 2
    inv = 1.0 / (ROPE_THETA ** (np.arange(0, rot_dim, 2, dtype=np.float32) / np.float32(rot_dim)))
    ang = np.arange(L, dtype=np.float32)[:, None] * inv[None, :]
    cos, sin = np.cos(ang).astype(np.float32), np.sin(ang).astype(np.float32)
    d = np.arange(LANES) % head_dim
    lo, hi = d < half, (d >= half) & (d < rot_dim)
    c = np.ones((L, LANES), np.float32)
    s1 = np.zeros((L, LANES), np.float32)
    s2 = np.zeros((L, LANES), np.float32)
    c[:, lo] = cos[:, d[lo]]
    c[:, hi] = cos[:, d[hi] - half]
    s1[:, lo] = -sin[:, d[lo]]
    s2[:, hi] = sin[:, d[hi] - half]
    return jnp.asarray(np.stack([c, s1, s2]))


def _rope(x, c, s1, s2, half):
    xp = pltpu.roll(x, LANES - half, 1)
    xm = pltpu.roll(x, half, 1)
    return x * c + xp * s1 + xm * s2


def _rope_wide(x, c, s1, s2, half):
    n = x.shape[1] // LANES
    return jnp.concatenate([_rope(x[:, LANES * s:LANES * (s + 1)], c, s1, s2, half) for s in range(n)], axis=1)


def _softmax2_cols(s):
    m = jnp.max(s, axis=0, keepdims=True)
    e = jnp.exp2(s - m)
    return e, m, jnp.sum(e, axis=0, keepdims=True)


def _bias_table(ok):
    return jnp.asarray(np.where(ok, 0.0, NEG).astype(np.float32))


def _cmp_bias_table(seq, qw):
    shift = (qw // CMP_STRIDE) * (seq // qw - 1)
    u = np.arange(seq // CMP_STRIDE + shift)[:, None]
    lane = np.arange(qw)[None, :]
    return _bias_table(CMP_STRIDE * (u - shift) + CMP_BLOCK - 1 <= lane)


def _win_bias_table(seq, qw):
    span = min(WIN + qw, seq)
    cmax = span - qw
    u = np.arange(span + cmax)[:, None]
    diff = cmax + np.arange(qw)[None, :] - u
    return _bias_table((diff >= 0) & (diff < WIN))


def _causal_bias_table(qw):
    u = np.arange(2 * KC)[:, None]
    return _bias_table(u <= KC + np.arange(qw)[None, :])


def _params(n_grid, vmem=VMEM_LIMIT):
    return pltpu.CompilerParams(dimension_semantics=("arbitrary",) * n_grid, vmem_limit_bytes=vmem)


def _rmsnorm_rows(x, g):
    return x * lax.rsqrt(jnp.mean(x * x, axis=-1, keepdims=True) + NORM_EPS) * g


def _inproj_kernel(x_ref, g_ref, w_ref, o_ref):
    hn = _rmsnorm_rows(x_ref[...], g_ref[...]).astype(BF16)
    n = w_ref.shape[1]
    for c0 in range(0, n, 512):
        c1 = min(c0 + 512, n)
        o_ref[:, c0:c1] = jnp.dot(hn, w_ref[:, c0:c1], preferred_element_type=F32)


def _inproj(x2, g, w_bf):
    n, d = x2.shape
    pc = w_bf.shape[1]
    return pl.pallas_call(
        _inproj_kernel,
        grid=(n // ROW_TILE,),
        in_specs=[pl.BlockSpec((ROW_TILE, d), lambda i: (i, 0)),
                  pl.BlockSpec((1, d), lambda i: (0, 0)),
                  pl.BlockSpec((d, pc), lambda i: (0, 0))],
        out_specs=pl.BlockSpec((ROW_TILE, pc), lambda i: (i, 0)),
        out_shape=jax.ShapeDtypeStruct((n, pc), F32),
        compiler_params=_params(1),
        name="inproj",
    )(x2, g.reshape(1, d), w_bf)


def _outproj_kernel(oa_ref, ob_ref, oc_ref, w_ref, x_ref, o_ref):
    mix = jnp.concatenate([oa_ref[...], ob_ref[...], oc_ref[...]], axis=1)
    o_ref[...] = x_ref[...] + jnp.dot(mix, w_ref[...], preferred_element_type=F32)


def _outproj(oa, ob, oc, w_bf, x2):
    n, d = x2.shape
    k = w_bf.shape[0]
    mw = oa.shape[1]
    return pl.pallas_call(
        _outproj_kernel,
        grid=(n // ROW_TILE,),
        in_specs=[pl.BlockSpec((ROW_TILE, mw), lambda i: (i, 0)),
                  pl.BlockSpec((ROW_TILE, mw), lambda i: (i, 0)),
                  pl.BlockSpec((ROW_TILE, mw), lambda i: (i, 0)),
                  pl.BlockSpec((k, d), lambda i: (0, 0)),
                  pl.BlockSpec((ROW_TILE, d), lambda i: (i, 0))],
        out_specs=pl.BlockSpec((ROW_TILE, d), lambda i: (i, 0)),
        out_shape=jax.ShapeDtypeStruct((n, d), F32),
        compiler_params=_params(1),
        name="outproj",
    )(oa, ob, oc, w_bf, x2)


def _ffn_up_kernel(x_ref, xh_ref, g_ref, wa_ref, wu_ref, cw_ref, cb_ref, o_ref, hn_scr, hh_scr, a_scr, *, tiles_per_seq):
    i, j = pl.program_id(0), pl.program_id(1)
    tm = x_ref.shape[0]

    @pl.when(j == 0)
    def _norm():
        hn_scr[...] = _rmsnorm_rows(x_ref[...], g_ref[...]).astype(BF16)
        keep = jnp.where(i % tiles_per_seq == 0, 0.0, 1.0)
        hh_scr[...] = (_rmsnorm_rows(xh_ref[...], g_ref[...]) * keep).astype(BF16)

    hn = hn_scr[...]
    a = jnp.dot(hn, wa_ref[...], preferred_element_type=F32)
    u = jnp.dot(hn, wu_ref[...], preferred_element_type=F32)
    a_scr[0:HALO, :] = jnp.dot(hh_scr[...], wa_ref[...], preferred_element_type=F32)
    a_scr[HALO:HALO + tm, :] = a
    a1 = a_scr[pl.ds(HALO - 1, tm), :]
    a2 = a_scr[pl.ds(HALO - 2, tm), :]
    cw = cw_ref[...]
    conv = cw[0:1, :] * a2 + cw[1:2, :] * a1 + cw[2:3, :] * a + cb_ref[...]
    o_ref[...] = (conv * jax.nn.sigmoid(conv) * u).astype(BF16)


def _ffn_up(x2, g, w_up_bf, conv_w, conv_b, seq_len):
    n, d = x2.shape
    tn = D_FF // 2
    nj = D_FF // tn
    hb = ROW_TILE // HALO
    kern = functools.partial(_ffn_up_kernel, tiles_per_seq=seq_len // ROW_TILE)
    return pl.pallas_call(
        kern,
        grid=(n // ROW_TILE, nj),
        in_specs=[pl.BlockSpec((ROW_TILE, d), lambda i, j: (i, 0)),
                  pl.BlockSpec((HALO, d), lambda i, j: (jnp.maximum(i * hb - 1, 0), 0)),
                  pl.BlockSpec((1, d), lambda i, j: (0, 0)),
                  pl.BlockSpec((d, tn), lambda i, j: (0, j)),
                  pl.BlockSpec((d, tn), lambda i, j: (0, j + nj)),
                  pl.BlockSpec((CONV_WIDTH, tn), lambda i, j: (0, j)),
                  pl.BlockSpec((1, tn), lambda i, j: (0, j))],
        out_specs=pl.BlockSpec((ROW_TILE, tn), lambda i, j: (i, j)),
        out_shape=jax.ShapeDtypeStruct((n, D_FF), BF16),
        scratch_shapes=[pltpu.VMEM((ROW_TILE, d), BF16), pltpu.VMEM((HALO, d), BF16),
                        pltpu.VMEM((ROW_TILE + HALO, tn), F32)],
        compiler_params=_params(2),
        name="ffn_up",
    )(x2, x2, g.reshape(1, d), w_up_bf, w_up_bf, conv_w, conv_b.reshape(1, D_FF))


def _ffn_down_kernel(a_ref, w_ref, x_ref, g_ref, o_ref, *, final_norm):
    y = x_ref[...] + jnp.dot(a_ref[...], w_ref[...], preferred_element_type=F32)
    if final_norm:
        y = _rmsnorm_rows(y, g_ref[...])
    o_ref[...] = y


def _ffn_down(act, w_bf, x2, final_g, final_norm):
    n, d = x2.shape
    k = act.shape[1]
    return pl.pallas_call(
        functools.partial(_ffn_down_kernel, final_norm=final_norm),
        grid=(n // ROW_TILE,),
        in_specs=[pl.BlockSpec((ROW_TILE, k), lambda i: (i, 0)),
                  pl.BlockSpec((k, d), lambda i: (0, 0)),
                  pl.BlockSpec((ROW_TILE, d), lambda i: (i, 0)),
                  pl.BlockSpec((1, d), lambda i: (0, 0))],
        out_specs=pl.BlockSpec((ROW_TILE, d), lambda i: (i, 0)),
        out_shape=jax.ShapeDtypeStruct((n, d), F32),
        compiler_params=_params(1),
        name="ffn_down",
    )(act, w_bf, x2, final_g.reshape(1, d))


def _cmp_kernel(kv_ref, pos_ref, w1_ref, w2_ref, o_ref):
    ngrp = kv_ref.shape[0] // CMP_STRIDE
    lo = hi = None
    for p in range(CMP_STRIDE):
        x = kv_ref[pl.ds(p, ngrp, stride=CMP_STRIDE), :]
        a = jnp.dot((x + pos_ref[p:p + 1, :]).astype(BF16), w1_ref[p], preferred_element_type=F32)
        b = jnp.dot((x + pos_ref[CMP_STRIDE + p:CMP_STRIDE + p + 1, :]).astype(BF16), w1_ref[CMP_STRIDE + p],
                    preferred_element_type=F32)
        lo = a if lo is None else lo + a
        hi = b if hi is None else hi + b
    hid = jax.nn.gelu(lo + pltpu.roll(hi, ngrp - 1, 0))
    o_ref[0] = jnp.dot(hid.astype(BF16), w2_ref[...], preferred_element_type=F32)


def _nsa_compress(proj, pos_cat, w1_cat, w2_cat, batch, seq):
    ngrp = seq // CMP_STRIDE
    return pl.pallas_call(
        _cmp_kernel,
        grid=(batch,),
        in_specs=[pl.BlockSpec((seq, LANES), lambda i: (i, _OFF["a_kcvc"] // LANES)),
                  pl.BlockSpec(pos_cat.shape, lambda i: (0, 0)),
                  pl.BlockSpec(w1_cat.shape, lambda i: (0, 0, 0)),
                  pl.BlockSpec(w2_cat.shape, lambda i: (0, 0))],
        out_specs=pl.BlockSpec((1, ngrp, LANES), lambda i: (i, 0, 0)),
        out_shape=jax.ShapeDtypeStruct((batch, ngrp, LANES), F32),
        compiler_params=_params(1),
        name="nsa_compress",
    )(proj, pos_cat, w1_cat, w2_cat)


def _stack_heads_t(x_t, n_heads, hd):
    z = jnp.zeros((LANES - hd, x_t.shape[1]), F32)
    return jnp.concatenate(
        [jnp.concatenate([x_t[hd * h:hd * (h + 1), :], z], axis=0) for h in range(n_heads)], axis=1)


def _nsa_kernel(q_ref, g_ref, ksvs_ref, kwvw_ref, kvc_ref, c_ref, s1_ref, s2_ref, ovt_ref, cmpb_ref, winb_ref,
                caub_ref, o_ref, ks_scr, vst_scr, kw_scr, vwt_scr, kc_scr, vct_scr, blk_scr, *, n_top):
    i = pl.program_id(1)
    H = NSA_HEADS
    QW = q_ref.shape[0]
    HW = H * QW
    seq = ks_scr.shape[0]
    lane_q = lax.broadcasted_iota(I32, (QW, LANES), 1)

    @pl.when(i == 0)
    def _init():
        ks_scr[...] = jnp.zeros_like(ks_scr)
        kw_scr[...] = jnp.zeros_like(kw_scr)
        vst_scr[...] = jnp.zeros_like(vst_scr)
        vwt_scr[...] = jnp.zeros_like(vwt_scr)
        kvc = kvc_ref[0]
        lane_c = lax.broadcasted_iota(I32, kvc.shape, 1)
        kc_scr[...] = jnp.where(lane_c < HEAD_DIM, kvc, 0.0).astype(BF16)
        vct_scr[...] = kvc.T[HEAD_DIM:2 * HEAD_DIM, :].astype(BF16)

    c, s1, s2 = c_ref[...], s1_ref[...], s2_ref[...]
    r0 = pl.multiple_of(i * QW, QW)
    for src, kdst, vdst in ((ksvs_ref, ks_scr, vst_scr), (kwvw_ref, kw_scr, vwt_scr)):
        kv = src[...]
        kr = _rope(kv, c, s1, s2, ROPE_DIM // 2)
        kdst[pl.ds(r0, QW), :] = jnp.where(lane_q < HEAD_DIM, kr, 0.0).astype(BF16)
        vdst[:, pl.ds(r0, QW)] = kv.T[HEAD_DIM:2 * HEAD_DIM, :].astype(BF16)

    q = q_ref[...] * (SCALE * LOG2E)
    qr = _rope_wide(q, c, s1, s2, ROPE_DIM // 2)
    q_st = _stack_heads_t(q.T, H, HEAD_DIM).astype(BF16)
    qr_st = _stack_heads_t(qr.T, H, HEAD_DIM).astype(BF16)
    tile_h = lambda b: jnp.concatenate([b] * H, axis=1)

    ncp = kc_scr.shape[0]
    cshift = cmpb_ref.shape[0] - ncp
    cb = cmpb_ref[pl.ds(pl.multiple_of(cshift - (QW // CMP_STRIDE) * i, 8), ncp), :]
    s_c = jnp.dot(kc_scr[...], q_st, preferred_element_type=F32) + tile_h(cb)
    e_c, _, den_c = _softmax2_cols(s_c)
    t_row = i * QW + (lax.broadcasted_iota(I32, (1, HW), 1) & (QW - 1))
    r_c = jnp.where(t_row >= CMP_BLOCK - 1, 1.0 / den_c, 0.0)
    o_cmp = jnp.dot(vct_scr[...], e_c.astype(BF16), preferred_element_type=F32) * r_c
    psum = e_c[:, 0:QW] * r_c[:, 0:QW]
    for h in range(1, H):
        psum = psum + e_c[:, QW * h:QW * (h + 1)] * r_c[:, QW * h:QW * (h + 1)]

    hi = psum.astype(BF16)
    r1 = psum - hi.astype(F32)
    mid = r1.astype(BF16)
    lo = (r1 - mid.astype(F32)).astype(BF16)
    ovt = ovt_ref[...]
    imp = (jnp.dot(ovt, hi, preferred_element_type=F32) + jnp.dot(ovt, mid, preferred_element_type=F32)
           + jnp.dot(ovt, lo, preferred_element_type=F32))
    n_slc = imp.shape[0]
    jb = lax.broadcasted_iota(I32, (n_slc, QW), 0)
    tq = i * QW + lax.broadcasted_iota(I32, (n_slc, QW), 1)
    cur = tq >> (SLC_BLOCK.bit_length() - 1)
    forced = (jb == 0) | (jb == cur) | (jb == cur - 1)
    val = jnp.where(forced, FORCE, jnp.where(jb <= cur, imp, NEG))
    rank = jnp.zeros((n_slc, QW), I32)
    for jp in range(n_slc):
        row = val[jp:jp + 1, :]
        tie = jnp.where(jb > jp, 1, 0)
        rank = rank + jnp.where(row > val, 1, jnp.where(row == val, tie, 0))
    blk_scr[...] = jnp.where(rank < n_top, jnp.where(val > 0.5 * NEG, 0.0, NEG), NEG)

    bpc = KC // SLC_BLOCK

    def slc_body(ci, carry):
        m, l, acc = carry
        k0 = pl.multiple_of(ci * KC, KC)
        ahead = jnp.minimum(i * QW - k0, KC)
        c0 = pl.multiple_of(KC - ahead, QB)
        hk = KC // ATT_SPLIT
        ss, m_new = [], m
        for sp in range(ATT_SPLIT):
            kb = pl.multiple_of(k0 + sp * hk, hk)
            rows = [jnp.broadcast_to(blk_scr[pl.ds(ci * bpc + sp * (bpc // ATT_SPLIT) + r, 1), :], (SLC_BLOCK, QW))
                    for r in range(bpc // ATT_SPLIT)]
            bias = jnp.concatenate(rows, axis=0) + caub_ref[pl.ds(pl.multiple_of(c0 + sp * hk, QB), hk), :]
            s = jnp.dot(ks_scr[pl.ds(kb, hk), :], qr_st, preferred_element_type=F32) + tile_h(bias)
            ss.append(s)
            m_new = jnp.maximum(m_new, jnp.max(s, axis=0, keepdims=True))
        alpha = jnp.exp2(m - m_new)
        l, acc = alpha * l, alpha * acc
        for sp in range(ATT_SPLIT):
            kb = pl.multiple_of(k0 + sp * hk, hk)
            e = jnp.exp2(ss[sp] - m_new)
            l = l + jnp.sum(e, axis=0, keepdims=True)
            acc = acc + jnp.dot(vst_scr[:, pl.ds(kb, hk)], e.astype(BF16), preferred_element_type=F32)
        return m_new, l, acc

    n_chunks = (i * QW + QW + KC - 1) // KC
    init = (jnp.full((1, HW), NEG, F32), jnp.zeros((1, HW), F32), jnp.zeros((HEAD_DIM, HW), F32))
    m_s, l_s, acc_s = lax.fori_loop(0, n_chunks, slc_body, init)
    o_slc = jnp.where(m_s > 0.5 * NEG, acc_s / l_s, 0.0)

    span = min(WIN + QW, seq)
    w0 = pl.multiple_of(jnp.maximum(i * QW + QW - span, 0), QB)
    wb = winb_ref[pl.ds(pl.multiple_of(span - QW - (i * QW - w0), QB), span), :]
    s_w = jnp.dot(kw_scr[pl.ds(w0, span), :], qr_st, preferred_element_type=F32) + tile_h(wb)
    e_w, _, den_w = _softmax2_cols(s_w)
    o_win = jnp.dot(vwt_scr[:, pl.ds(w0, span)], e_w.astype(BF16), preferred_element_type=F32) * (1.0 / den_w)

    gt = jax.nn.sigmoid(g_ref[...].T)
    outs = []
    for h in range(H):
        sl = slice(QW * h, QW * (h + 1))
        outs.append(gt[3 * h:3 * h + 1, :] * o_cmp[:, sl] + gt[3 * h + 1:3 * h + 2, :] * o_slc[:, sl]
                    + gt[3 * h + 2:3 * h + 3, :] * o_win[:, sl])
    outs.append(jnp.zeros((o_ref.shape[1] - H * HEAD_DIM, QW), F32))
    o_ref[...] = jnp.concatenate(outs, axis=0).T.astype(BF16)


def _nsa(proj, kvcmp, tabs, ovt, batch, seq):
    qw = NSA_QW
    nb = seq // qw
    n_slc = seq // SLC_BLOCK
    ncp = kvcmp.shape[1]
    col = lambda name, w: _OFF[name] // w
    qmap = lambda cidx: (lambda b, i: (b * nb + i, cidx))
    tmap = lambda k: (lambda b, i: (k, i, 0))
    kern = functools.partial(_nsa_kernel, n_top=min(SLC_TOPN, n_slc))
    cmpb, winb, caub = _cmp_bias_table(seq, qw), _win_bias_table(seq, qw), _causal_bias_table(qw)
    whole = lambda a: pl.BlockSpec(a.shape, lambda b, i: (0, 0))
    return pl.pallas_call(
        kern,
        grid=(batch, nb),
        in_specs=[pl.BlockSpec((qw, 384), qmap(col("a_q", 384))),
                  pl.BlockSpec((qw, LANES), qmap(col("a_g", LANES))),
                  pl.BlockSpec((qw, LANES), qmap(col("a_ksvs", LANES))),
                  pl.BlockSpec((qw, LANES), qmap(col("a_kwvw", LANES))),
                  pl.BlockSpec((1, ncp, LANES), lambda b, i: (b, 0, 0)),
                  pl.BlockSpec((None, qw, LANES), tmap(0)),
                  pl.BlockSpec((None, qw, LANES), tmap(1)),
                  pl.BlockSpec((None, qw, LANES), tmap(2)),
                  whole(ovt), whole(cmpb), whole(winb), whole(caub)],
        out_specs=pl.BlockSpec((qw, 384), lambda b, i: (b * nb + i, 0)),
        out_shape=jax.ShapeDtypeStruct((batch * seq, 384), BF16),
        scratch_shapes=[pltpu.VMEM((seq, LANES), BF16), pltpu.VMEM((HEAD_DIM, seq), BF16),
                        pltpu.VMEM((seq, LANES), BF16), pltpu.VMEM((HEAD_DIM, seq), BF16),
                        pltpu.VMEM((ncp, LANES), BF16), pltpu.VMEM((HEAD_DIM, ncp), BF16),
                        pltpu.VMEM((n_slc, qw), F32)],
        compiler_params=_params(2),
        name="nsa",
    )(proj, proj, proj, proj, kvcmp, tabs, tabs, tabs, ovt, cmpb, winb, caub)


SEARCH_BISECT_FROM = 24
SEARCH_MAX_PASSES = 64
SEARCH_KEPT_LIMIT = 6
SEARCH_BLIND_PASSES = 12
INT_MAX = 2 ** 31 - 1
FLT_MIN_NORMAL = 1.17549435e-38
ZERO_BAND_END = 1 << 23


def _tree_rows(x, op):
    parts = [x[r:r + 8, :] for r in range(0, x.shape[0], 8)]
    while len(parts) > 1:
        nxt = [op(parts[a], parts[a + 1]) for a in range(0, len(parts) - 1, 2)]
        if len(parts) % 2:
            nxt.append(parts[-1])
        parts = nxt
    return parts[0]


def _f2key(v):
    bits = lax.bitcast_convert_type(v, I32)
    return bits ^ ((bits >> 31) & 0x7FFFFFFF)


def _key2f(k):
    return lax.bitcast_convert_type(k ^ ((k >> 31) & 0x7FFFFFFF), F32)


def _dsa_kernel(q_ref, iq_ref, ikw_ref, kv_ref, c_ref, s1_ref, s2_ref, ci_ref, si1_ref, si2_ref, o_ref,
                k_scr, vt_scr, ik_scr, key_scr, *, top, idx_bits):
    i = pl.program_id(1)
    H = DSA_HEADS
    QW = q_ref.shape[0]
    HW = H * QW
    seq = key_scr.shape[0]
    lane_q = lax.broadcasted_iota(I32, (QW, LANES), 1)

    @pl.when(i == 0)
    def _init():
        k_scr[...] = jnp.zeros_like(k_scr)
        vt_scr[...] = jnp.zeros_like(vt_scr)
        ik_scr[...] = jnp.zeros_like(ik_scr)

    c, s1, s2 = c_ref[...], s1_ref[...], s2_ref[...]
    ci, si1, si2 = ci_ref[...], si1_ref[...], si2_ref[...]
    r0 = pl.multiple_of(i * QW, QW)
    kv = kv_ref[...]
    k_scr[pl.ds(r0, QW), :] = jnp.where(lane_q < HEAD_DIM, _rope(kv, c, s1, s2, ROPE_DIM // 2), 0.0).astype(BF16)
    vt_scr[:, pl.ds(r0, QW)] = kv.T[HEAD_DIM:2 * HEAD_DIM, :].astype(BF16)
    ikw = ikw_ref[...]
    ik_scr[pl.ds(r0, QW), :] = jnp.where(lane_q < IDX_DIM, _rope(ikw, ci, si1, si2, IDX_ROPE_DIM // 2), 0.0).astype(BF16)

    qr = _rope_wide(q_ref[...] * (SCALE * LOG2E), c, s1, s2, ROPE_DIM // 2)
    qr_st = _stack_heads_t(qr.T, H, HEAD_DIM).astype(BF16)
    iqr = _rope_wide(iq_ref[...], ci, si1, si2, IDX_ROPE_DIM // 2)
    iq_st = _stack_heads_t(iqr.T, IDX_HEADS, IDX_DIM).astype(BF16)
    wt = ikw.T * ((IDX_DIM ** -0.5) * (IDX_HEADS ** -0.5))
    w_st = jnp.concatenate([wt[IDX_DIM + h:IDX_DIM + h + 1, :] for h in range(IDX_HEADS)], axis=1)

    n_chunks = (i * QW + QW + KC - 1) // KC
    k_io = lax.broadcasted_iota(I32, (KC, QW), 0)
    tq = i * QW + lax.broadcasted_iota(I32, (KC, QW), 1)

    sub_io = lax.broadcasted_iota(I32, (QB, QW), 0)
    sub_tq = i * QW + lax.broadcasted_iota(I32, (QB, QW), 1)

    def score_body(cidx, carry):
        mx, mn = carry
        k0 = pl.multiple_of(cidx * KC, KC)
        for sb in range(KC // QB):
            kb = pl.multiple_of(k0 + sb * QB, QB)
            d = jnp.dot(ik_scr[pl.ds(kb, QB), :], iq_st, preferred_element_type=F32)
            r = jnp.maximum(d, 0.0) * w_st
            sc = r[:, 0:QW]
            for h in range(1, IDX_HEADS):
                sc = sc + r[:, QW * h:QW * (h + 1)]
            kpos = sub_io + kb
            key = jnp.where(jnp.abs(sc) < FLT_MIN_NORMAL, seq - kpos, _f2key(sc))
            causal = kpos <= sub_tq
            key_scr[pl.ds(kb, QB), :] = jnp.where(causal, key, INT_MIN)
            mx = jnp.maximum(mx, _tree_rows(jnp.where(causal, key, INT_MIN), jnp.maximum))
            mn = jnp.minimum(mn, _tree_rows(jnp.where(causal, key, INT_MAX), jnp.minimum))
        return mx, mn

    mx8, mn8 = lax.fori_loop(0, n_chunks, score_body,
                             (jnp.full((8, QW), INT_MIN, I32), jnp.full((8, QW), INT_MAX, I32)))

    def count(pred):
        def body(cidx, acc):
            k0 = pl.multiple_of(cidx * KC, KC)
            return acc + _tree_rows(pred(key_scr[pl.ds(k0, KC), :], k_io + k0), jnp.add)
        acc8 = lax.fori_loop(0, n_chunks, body, jnp.zeros((8, QW), F32))
        return jnp.sum(acc8, axis=0, keepdims=True)

    topf = float(top)
    lo0 = functools.reduce(jnp.minimum, [mn8[r:r + 1, :] for r in range(8)])
    hi0 = functools.reduce(jnp.maximum, [mx8[r:r + 1, :] for r in range(8)]) + 1
    n_causal = (i * QW + 1 + lax.broadcasted_iota(I32, (1, QW), 1)).astype(F32)
    log_top = float(np.log(top))

    def in_zero_band(lo, hi):
        return jnp.where(lo >= 1, jnp.where(hi <= ZERO_BAND_END, 1.0, 0.0), 0.0) > 0.5

    def pending(lo, hi, c_lo):
        return (jnp.max(jnp.where(c_lo > topf, jnp.where(hi > lo + 1, 1.0, 0.0), 0.0)) > 0.5).astype(I32)

    def search_cond(c):
        return jnp.logical_and(c[0] < SEARCH_MAX_PASSES, c[1] > 0)

    def search_pass(it, st):
        lo, hi, c_lo, c_hi, kept, w_lo, w_hi = st
        f_lo = (jnp.log(c_lo) - log_top) * w_lo
        f_hi = (log_top - jnp.log(jnp.maximum(c_hi, 0.5))) * w_hi
        frac = f_lo / (f_lo + f_hi)
        v_lo, v_hi = _key2f(lo), _key2f(hi)
        cand = _f2key(v_lo + frac * (v_hi - v_lo))
        lo_f, hi_f = lo.astype(F32), hi.astype(F32)
        cand = jnp.where(in_zero_band(lo, hi), (lo_f + frac * (hi_f - lo_f)).astype(I32), cand)
        key_mid = (lo >> 1) + (hi >> 1) + (lo & hi & 1)
        cand = jnp.where(jnp.abs(kept) >= SEARCH_KEPT_LIMIT, key_mid, cand)
        cand = jnp.where(it >= SEARCH_BISECT_FROM, key_mid, cand)
        cand = jnp.where(it == 0, 1, jnp.where(it == 1, ZERO_BAND_END, cand))
        cand = jnp.minimum(jnp.maximum(cand, lo + 1), hi - 1)
        cnt = count(lambda key, kpos: jnp.where(key >= cand, 1.0, 0.0))
        active = jnp.where(c_lo > topf, jnp.where(hi > lo + 1, 1.0, 0.0), 0.0) > 0.5
        up = jnp.where(active, jnp.where(cnt >= topf, 1.0, 0.0), 0.0) > 0.5
        dn = jnp.where(active, jnp.where(cnt >= topf, 0.0, 1.0), 0.0) > 0.5
        w_hi = jnp.where(up, jnp.where(kept < 0, 0.5 * w_hi, 1.0), jnp.where(dn, 1.0, w_hi))
        w_lo = jnp.where(dn, jnp.where(kept > 0, 0.5 * w_lo, 1.0), jnp.where(up, 1.0, w_lo))
        kept = jnp.where(up, jnp.where(kept < 0, kept - 1, -1), jnp.where(dn, jnp.where(kept > 0, kept + 1, 1), kept))
        lo, c_lo = jnp.where(up, cand, lo), jnp.where(up, cnt, c_lo)
        hi, c_hi = jnp.where(dn, cand, hi), jnp.where(dn, cnt, c_hi)
        hi = jnp.where(in_zero_band(lo, hi), jnp.minimum(hi, seq + 1), hi)
        return lo, hi, c_lo, c_hi, kept, w_lo, w_hi

    def search_body(c):
        st = search_pass(c[0], c[2])
        return c[0] + 1, pending(st[0], st[1], st[2]), st

    zero_f, one_f = jnp.zeros((1, QW), F32), jnp.ones((1, QW), F32)
    st0 = (lo0, hi0, n_causal, zero_f, jnp.zeros((1, QW), I32), one_f, one_f)
    n_blind = jnp.where(i * QW + QW > top, SEARCH_BLIND_PASSES, 0)
    st1 = lax.fori_loop(0, n_blind, search_pass, st0)
    _, _, (thr, _, c_lo, c_hi, _, _, _) = lax.while_loop(
        search_cond, search_body, (n_blind, pending(st1[0], st1[1], st1[2]), st1))

    tied = c_lo > topf
    need1 = (topf - 1.0) - c_hi
    all_pos = jnp.full((1, QW), 2 ** idx_bits - 1, I32)

    def tie_search():
        def tie_body(bi, cur):
            cand = cur | lax.shift_left(jnp.int32(1), idx_bits - 1 - bi)
            cnt = count(lambda key, kpos: jnp.where(key == thr, jnp.where(kpos < cand, 1.0, 0.0), 0.0))
            return jnp.where(cnt <= need1, cand, cur)
        return lax.fori_loop(0, idx_bits, tie_body, jnp.zeros((1, QW), I32))

    any_tied = jnp.max(jnp.where(tied, 1.0, 0.0)) > 0.5
    last = jnp.where(tied, lax.cond(any_tied, tie_search, lambda: all_pos), all_pos)

    half_neg = int(np.float32(0.5 * NEG).view(np.int32))
    key_floor = half_neg ^ 0x7FFFFFFF

    def att_chunk(cidx, state):
        m, l, acc = state
        k0 = pl.multiple_of(cidx * KC, KC)
        key = key_scr[pl.ds(k0, KC), :]
        kpos = k_io + k0
        bias = jnp.where(key > thr, 0.0, jnp.where(key == thr, jnp.where(kpos <= last, 0.0, NEG), NEG))
        bias = jnp.where(key > key_floor, bias, NEG)
        s = (jnp.dot(k_scr[pl.ds(k0, KC), :], qr_st, preferred_element_type=F32)
             + jnp.concatenate([bias] * H, axis=1))
        m_new = jnp.maximum(m, jnp.max(s, axis=0, keepdims=True))
        alpha = jnp.exp2(m - m_new)
        e = jnp.exp2(s - m_new)
        l = alpha * l + jnp.sum(e, axis=0, keepdims=True)
        acc = alpha * acc + jnp.dot(vt_scr[:, pl.ds(k0, KC)], e.astype(BF16), preferred_element_type=F32)
        return m_new, l, acc

    init = (jnp.full((1, HW), NEG, F32), jnp.zeros((1, HW), F32), jnp.zeros((HEAD_DIM, HW), F32))
    m_a, l_a, acc_a = lax.fori_loop(0, n_chunks, att_chunk, init)
    o = jnp.where(m_a > 0.5 * NEG, acc_a / l_a, 0.0)
    outs = [o[:, QW * h:QW * (h + 1)] for h in range(H)]
    outs.append(jnp.zeros((o_ref.shape[1] - H * HEAD_DIM, QW), F32))
    o_ref[...] = jnp.concatenate(outs, axis=0).T.astype(BF16)


def _dsa(proj, tabs, tabs_i, batch, seq):
    qw = DSA_QW
    nb = seq // qw
    col = lambda name, w: _OFF[name] // w
    qmap = lambda cidx: (lambda b, i: (b * nb + i, cidx))
    tmap = lambda k: (lambda b, i: (k, i, 0))
    top = min(DSA_TOPK_MAX, seq // 4)
    kern = functools.partial(_dsa_kernel, top=top, idx_bits=int(seq).bit_length())
    return pl.pallas_call(
        kern,
        grid=(batch, nb),
        in_specs=[pl.BlockSpec((qw, 384), qmap(col("b_q", 384))),
                  pl.BlockSpec((qw, 256), qmap(col("b_iq", 256))),
                  pl.BlockSpec((qw, LANES), qmap(col("b_ikiw", LANES))),
                  pl.BlockSpec((qw, LANES), qmap(col("b_kv", LANES))),
                  pl.BlockSpec((None, qw, LANES), tmap(0)),
                  pl.BlockSpec((None, qw, LANES), tmap(1)),
                  pl.BlockSpec((None, qw, LANES), tmap(2)),
                  pl.BlockSpec((None, qw, LANES), tmap(0)),
                  pl.BlockSpec((None, qw, LANES), tmap(1)),
                  pl.BlockSpec((None, qw, LANES), tmap(2))],
        out_specs=pl.BlockSpec((qw, 384), lambda b, i: (b * nb + i, 0)),
        out_shape=jax.ShapeDtypeStruct((batch * seq, 384), BF16),
        scratch_shapes=[pltpu.VMEM((seq, LANES), BF16), pltpu.VMEM((HEAD_DIM, seq), BF16),
                        pltpu.VMEM((seq, LANES), BF16), pltpu.VMEM((seq, qw), I32)],
        compiler_params=_params(2),
        name="dsa",
    )(proj, proj, proj, proj, tabs, tabs, tabs, tabs_i, tabs_i, tabs_i)


DIL_MB = max(d for _, d in DIL_PAIRS) * QB
DIL_VMEM_LIMIT = 56 * 1024 * 1024
DIL_UNROLL = 4


def _dil_layout(g):
    dil = DIL_PAIRS[g][1]
    per = DIL_MB // dil
    return dil, per, per + QB


def _dil_bias_table():
    u = np.arange(2 * QB)[:, None]
    diff = QB + np.arange(LANES)[None, :] - u
    ok = (diff >= 0) & (diff <= QB)
    return _bias_table(np.concatenate([ok, ok & (u >= QB)], axis=0))


def _dil_kernel(*refs):
    G, HG = len(DIL_PAIRS), DIL_HEADS_PER_GROUP
    q_refs, k_refs, v_refs = refs[0:G], refs[G:2 * G], refs[2 * G:3 * G]
    c_ref, s1_ref, s2_ref, bias_ref, o_ref = refs[3 * G:3 * G + 5]
    kds, vds, ogs, lss = (refs[3 * G + 5 + n * G:3 * G + 5 + (n + 1) * G] for n in range(4))
    step = pl.program_id(1)
    half = ROPE_DIM // 2

    @pl.when(step == 0)
    def _zero():
        for g in range(G):
            kds[g][...] = jnp.zeros_like(kds[g])
            vds[g][...] = jnp.zeros_like(vds[g])

    @pl.when(step > 0)
    def _carry_halo():
        for g in range(G):
            dil, per, stride = _dil_layout(g)
            for r in range(dil):
                b0 = r * stride
                kds[g][b0:b0 + QB, :] = kds[g][b0 + per:b0 + per + QB, :]
                vds[g][:, b0:b0 + QB] = vds[g][:, b0 + per:b0 + per + QB]

    for g in range(G):
        dil, per, stride = _dil_layout(g)
        piece = min(per, KC)
        for r in range(dil):
            for p0 in range(0, per, piece):
                rows = pl.ds(r + dil * p0, piece, stride=dil)
                kr = _rope(k_refs[g][rows, :], c_ref[rows, :], s1_ref[rows, :], s2_ref[rows, :], half)
                d0 = r * stride + QB + p0
                kds[g][d0:d0 + piece, :] = kr.astype(BF16)
                vds[g][:, d0:d0 + piece] = v_refs[g][rows, :].T.astype(BF16)

    zero = jnp.zeros((HEAD_DIM, LANES), F32)
    for g in range(G):
        dil, per, stride = _dil_layout(g)
        nblk = per // QB

        def block(jb, _, r=0, g=g, dil=dil, stride=stride):
            p0 = pl.multiple_of(dil * QB * jb, dil * QB)
            win = pl.ds(p0, dil * QB)
            rows = pl.ds(r, QB, stride=dil)
            q = _rope(q_refs[g].at[win, :][rows, :] * (SCALE * LOG2E), c_ref.at[win, :][rows, :],
                      s1_ref.at[win, :][rows, :], s2_ref.at[win, :][rows, :], half)
            qt = q.T
            q2 = jnp.concatenate([jnp.concatenate([qt[:HEAD_DIM], zero], axis=0),
                                  jnp.concatenate([zero, qt[HEAD_DIM:]], axis=0)], axis=1).astype(BF16)
            kb = pl.multiple_of(r * stride + QB * jb, QB)
            first = jnp.logical_and(step == 0, jb == 0)
            bias = bias_ref[pl.ds(pl.multiple_of(jnp.where(first, 2 * QB, 0), QB), 2 * QB), :]
            s = (jnp.dot(kds[g][pl.ds(kb, 2 * QB), :], q2, preferred_element_type=F32)
                 + jnp.concatenate([bias] * HG, axis=1))
            e, m, den = _softmax2_cols(s)
            lse = m + jnp.log2(den)
            rden = 1.0 / den
            parts, lparts = [], []
            for hg in range(HG):
                sl = slice(LANES * hg, LANES * (hg + 1))
                vt = vds[g][HEAD_DIM * hg:HEAD_DIM * (hg + 1), pl.ds(kb, 2 * QB)]
                parts.append(jnp.dot(vt, e[:, sl].astype(BF16), preferred_element_type=F32) * rden[:, sl])
                lparts.append(jnp.broadcast_to(lse[:, sl], (HEAD_DIM, LANES)))
            tile = jnp.concatenate(parts + lparts, axis=0).T
            ogs[g].at[win, :][rows, :] = tile[:, :LANES]
            lss[g].at[win, :][rows, :] = tile[:, LANES:]
            return 0

        for r in range(dil):
            lax.fori_loop(0, nblk, functools.partial(block, r=r), 0, unroll=min(nblk, DIL_UNROLL))

    def mix(ti, _):
        rows = pl.ds(pl.multiple_of(ti * KC, KC), KC)
        ls = [lss[g][rows, :] for g in range(G)]
        mx = functools.reduce(jnp.maximum, ls)
        ex = [jnp.exp2(x - mx) for x in ls]
        rtot = 1.0 / functools.reduce(lambda a, b: a + b, ex)
        o_ref[rows, :] = jnp.concatenate([ex[g] * rtot * ogs[g][rows, :] for g in range(G)], axis=1).astype(BF16)
        return 0

    lax.fori_loop(0, DIL_MB // KC, mix, 0)


def _dil(proj, tabs, batch, seq):
    assert seq % DIL_MB == 0 and all(w // d == QB for w, d in DIL_PAIRS)
    nm = seq // DIL_MB
    G = len(DIL_PAIRS)
    w = DIL_HEADS * HEAD_DIM
    gmap = lambda name, g: (lambda b, j: (b * nm + j, _OFF[name] // LANES + g))
    tmap = lambda k: (lambda b, j: (k, j, 0))
    bias = _dil_bias_table()
    rows = [_dil_layout(g)[0] * _dil_layout(g)[2] for g in range(G)]
    return pl.pallas_call(
        _dil_kernel,
        grid=(batch, nm),
        in_specs=([pl.BlockSpec((DIL_MB, LANES), gmap(name, g)) for name in ("c_q", "c_k", "c_v") for g in range(G)]
                  + [pl.BlockSpec((None, DIL_MB, LANES), tmap(k)) for k in range(3)]
                  + [pl.BlockSpec(bias.shape, lambda b, j: (0, 0))]),
        out_specs=pl.BlockSpec((DIL_MB, w), lambda b, j: (b * nm + j, 0)),
        out_shape=jax.ShapeDtypeStruct((batch * seq, w), BF16),
        scratch_shapes=([pltpu.VMEM((n, LANES), BF16) for n in rows] + [pltpu.VMEM((LANES, n), BF16) for n in rows]
                        + [pltpu.VMEM((DIL_MB, LANES), F32)] * (2 * G)),
        compiler_params=_params(2, DIL_VMEM_LIMIT),
        name="dilated",
    )(*([proj] * (3 * G)), tabs, tabs, tabs, bias)


def _overlap_t(seq):
    n_cmp_pad = seq // CMP_STRIDE
    n_slc = seq // SLC_BLOCK
    c_start = np.arange(n_cmp_pad) * CMP_STRIDE
    s_start = np.arange(n_slc) * SLC_BLOCK
    ov = ((c_start[None, :] < s_start[:, None] + SLC_BLOCK) & (c_start[None, :] + CMP_BLOCK > s_start[:, None]))
    return jnp.asarray(ov.astype(np.float32), dtype=BF16)


def _layer(x2, batch, seq, norm1_g, w_in, cmp_pos, cmp_w1, cmp_w2, w_out, norm2_g, w_up, conv_w, conv_b, w_down,
           final_g, final_norm, tabs, tabs_i, ovt):
    n = batch * seq
    proj = _inproj(x2, norm1_g, _pad_w_in(w_in).astype(BF16))

    w1 = cmp_w1.reshape(2, CMP_BLOCK, HEAD_DIM, CMP_HIDDEN)
    z1 = jnp.zeros_like(w1[0])
    w1_cat = jnp.concatenate([jnp.concatenate([w1[0], z1], axis=2),
                              jnp.concatenate([z1, w1[1]], axis=2)], axis=1).astype(BF16)
    z2 = jnp.zeros((CMP_HIDDEN, HEAD_DIM), cmp_w2.dtype)
    w2_cat = jnp.concatenate([jnp.concatenate([cmp_w2[0], z2], axis=1),
                              jnp.concatenate([z2, cmp_w2[1]], axis=1)], axis=0).astype(BF16)
    pos_cat = jnp.concatenate([cmp_pos[0], cmp_pos[1]], axis=1)
    kvcmp = _nsa_compress(proj, pos_cat, w1_cat, w2_cat, batch, seq)

    o_a = _nsa(proj, kvcmp, tabs, ovt, batch, seq)
    o_b = _dsa(proj, tabs, tabs_i, batch, seq)
    o_c = _dil(proj, tabs, batch, seq)

    hp = NSA_HEADS * HEAD_DIM
    zrow = jnp.zeros((384 - hp, D_MODEL), w_out.dtype)
    w_out_pad = jnp.concatenate([w_out[0:hp], zrow, w_out[hp:2 * hp], zrow, w_out[2 * hp:]], axis=0).astype(BF16)
    x2 = _outproj(o_a, o_b, o_c, w_out_pad, x2)

    act = _ffn_up(x2, norm2_g, w_up.astype(BF16), conv_w, conv_b, seq)
    return _ffn_down(act, w_down.astype(BF16), x2, final_g, final_norm)


def kernel(x, norm1_g, w_in, cmp_pos, cmp_w1, cmp_w2, w_out, norm2_g, w_up, conv_w, conv_b, w_down, final_g):
    batch, seq, d = x.shape
    depth = w_in.shape[0]
    tabs = _rope_lane_tables(seq, HEAD_DIM, ROPE_DIM)
    tabs_i = _rope_lane_tables(seq, IDX_DIM, IDX_ROPE_DIM)
    ovt = _overlap_t(seq)
    x2 = x.reshape(batch * seq, d)
    for li in range(depth):
        x2 = _layer(x2, batch, seq, norm1_g[li], w_in[li], cmp_pos[li], cmp_w1[li], cmp_w2[li], w_out[li],
                    norm2_g[li], w_up[li], conv_w[li], conv_b[li], w_down[li], final_g, li == depth - 1,
                    tabs, tabs_i, ovt)
    return x2.reshape(batch, seq, d)
```

```python
import functools

import numpy as np
import jax
import jax.numpy as jnp
from jax import lax
from jax.experimental import pallas as pl
from jax.experimental.pallas import tpu as pltpu

F32 = jnp.float32
BF16 = jnp.bfloat16
I32 = jnp.int32

D_MODEL = 1024
HEAD_DIM = 64
ROPE_DIM = HEAD_DIM // 4
ROPE_THETA = 500000.0
NORM_EPS = 1e-6
SCALE = HEAD_DIM ** -0.5
LOG2E = 1.4426950408889634
NEG = -1e30
FORCE = 1e9
NSA_HEADS = 5
CMP_BLOCK = 32
CMP_STRIDE = 16
CMP_HIDDEN = 128
SLC_BLOCK = 64
SLC_TOPN = 16
WIN = 512
DSA_HEADS = 5
IDX_HEADS = 8
IDX_DIM = 32
IDX_ROPE_DIM = IDX_DIM // 4
DSA_TOPK_MAX = 256
DIL_PAIRS = ((128, 1), (512, 4), (2048, 16))
DIL_HEADS_PER_GROUP = 2
DIL_HEADS = len(DIL_PAIRS) * DIL_HEADS_PER_GROUP
D_FF = 2816
CONV_WIDTH = 3

LANES = 128
QB = 128
DSA_QW = 256
NSA_QW = 256
KC = 512
ATT_SPLIT = 1
ROW_TILE = 512
HALO = 16
VMEM_LIMIT = 48 * 1024 * 1024
INT_MIN = -2 ** 31

_A0 = 0
_B0 = 719
_C0 = 1463
_SEGS = (
    ("c_q", _C0, 384, 384), ("c_k", _C0 + 384, 384, 384), ("c_v", _C0 + 768, 384, 384),
    ("a_q", _A0, 320, 384), ("b_q", _B0, 320, 384),
    ("a_kcvc", _A0 + 320, 128, 128),
    ("b_iq", _B0 + 448, 256, 256),
    ("a_ksvs", _A0 + 448, 128, 128), ("a_kwvw", _A0 + 576, 128, 128),
    ("a_g", _A0 + 704, 15, 128),
    ("b_kv", _B0 + 320, 128, 128),
    ("b_ikiw", _B0 + 704, 40, 128),
)
P_COLS = sum(s[3] for s in _SEGS)


def _seg_offsets():
    offs, o = {}, 0
    for name, _, _, pw in _SEGS:
        offs[name] = o
        o += pw
    return offs


_OFF = _seg_offsets()


def _regroup_kernel(w_ref, o_ref):
    for name, src, wdt, pw in _SEGS:
        dst = _OFF[name]
        o_ref[:, dst:dst + wdt] = w_ref[:, src:src + wdt].astype(BF16)
        if pw > wdt:
            o_ref[:, dst + wdt:dst + pw] = jnp.zeros((o_ref.shape[0], pw - wdt), BF16)


def _regroup_w_in(w, layer):
    _, d, n_in = w.shape
    rt = 256
    return pl.pallas_call(
        _regroup_kernel,
        grid=(d // rt,),
        in_specs=[pl.BlockSpec((None, rt, n_in), lambda i: (layer, i, 0))],
        out_specs=pl.BlockSpec((rt, P_COLS), lambda i: (i, 0)),
        out_shape=jax.ShapeDtypeStruct((d, P_COLS), BF16),
        compiler_params=_params(1),
        name="regroup_w_in",
    )(w)


def _rope_lane_tables(L, head_dim, rot_dim):
    half = rot_dim // 2
    inv = 1.0 / (ROPE_THETA ** (np.arange(0, rot_dim, 2, dtype=np.float32) / np.float32(rot_dim)))
    ang = np.arange(L, dtype=np.float32)[:, None] * inv[None, :]
    cos, sin = np.cos(ang).astype(np.float32), np.sin(ang).astype(np.float32)
    d = np.arange(LANES) % head_dim
    lo, hi = d < half, (d >= half) & (d < rot_dim)
    c = np.ones((L, LANES), np.float32)
    s1 = np.zeros((L, LANES), np.float32)
    s2 = np.zeros((L, LANES), np.float32)
    c[:, lo] = cos[:, d[lo]]
    c[:, hi] = cos[:, d[hi] - half]
    s1[:, lo] = -sin[:, d[lo]]
    s2[:, hi] = sin[:, d[hi] - half]
    return jnp.asarray(np.stack([c, s1, s2]))


def _rope(x, c, s1, s2, half):
    xp = pltpu.roll(x, LANES - half, 1)
    xm = pltpu.roll(x, half, 1)
    return x * c + xp * s1 + xm * s2


def _rope_wide(x, c, s1, s2, half):
    n = x.shape[1] // LANES
    return jnp.concatenate([_rope(x[:, LANES * s:LANES * (s + 1)], c, s1, s2, half) for s in range(n)], axis=1)


def _softmax2_cols(s):
    m = jnp.max(s, axis=0, keepdims=True)
    e = jnp.exp2(s - m)
    return e, m, jnp.sum(e, axis=0, keepdims=True)


def _bias_table(ok):
    return jnp.asarray(np.where(ok, 0.0, NEG).astype(np.float32))


def _cmp_bias_table(seq, qw):
    shift = (qw // CMP_STRIDE) * (seq // qw - 1)
    u = np.arange(seq // CMP_STRIDE + shift)[:, None]
    lane = np.arange(qw)[None, :]
    return _bias_table(CMP_STRIDE * (u - shift) + CMP_BLOCK - 1 <= lane)


def _win_bias_table(seq, qw):
    span = min(WIN + qw, seq)
    cmax = span - qw
    u = np.arange(span + cmax)[:, None]
    diff = cmax + np.arange(qw)[None, :] - u
    return _bias_table((diff >= 0) & (diff < WIN))


def _causal_bias_table(qw):
    u = np.arange(2 * KC)[:, None]
    return _bias_table(u <= KC + np.arange(qw)[None, :])


def _params(n_grid, vmem=VMEM_LIMIT):
    return pltpu.CompilerParams(dimension_semantics=("arbitrary",) * n_grid, vmem_limit_bytes=vmem)


def _rmsnorm_rows(x, g):
    return x * lax.rsqrt(jnp.mean(x * x, axis=-1, keepdims=True) + NORM_EPS) * g


def _inproj_kernel(x_ref, g_ref, w_ref, o_ref):
    hn = _rmsnorm_rows(x_ref[...], g_ref[...]).astype(BF16)
    n = w_ref.shape[1]
    for c0 in range(0, n, 512):
        c1 = min(c0 + 512, n)
        o_ref[:, c0:c1] = jnp.dot(hn, w_ref[:, c0:c1], preferred_element_type=F32)


def _inproj(x2, g, w_bf):
    n, d = x2.shape
    pc = w_bf.shape[1]
    return pl.pallas_call(
        _inproj_kernel,
        grid=(n // ROW_TILE,),
        in_specs=[pl.BlockSpec((ROW_TILE, d), lambda i: (i, 0)),
                  pl.BlockSpec((1, d), lambda i: (0, 0)),
                  pl.BlockSpec((d, pc), lambda i: (0, 0))],
        out_specs=pl.BlockSpec((ROW_TILE, pc), lambda i: (i, 0)),
        out_shape=jax.ShapeDtypeStruct((n, pc), F32),
        compiler_params=_params(1),
        name="inproj",
    )(x2, g.reshape(1, d), w_bf)


def _outproj_kernel(oa_ref, ob_ref, oc_ref, w_ref, x_ref, o_ref):
    mix = jnp.concatenate([oa_ref[...], ob_ref[...], oc_ref[...]], axis=1)
    o_ref[...] = x_ref[...] + jnp.dot(mix, w_ref[...], preferred_element_type=F32)


def _outproj(oa, ob, oc, w_bf, x2):
    n, d = x2.shape
    k = w_bf.shape[0]
    mw = oa.shape[1]
    return pl.pallas_call(
        _outproj_kernel,
        grid=(n // ROW_TILE,),
        in_specs=[pl.BlockSpec((ROW_TILE, mw), lambda i: (i, 0)),
                  pl.BlockSpec((ROW_TILE, mw), lambda i: (i, 0)),
                  pl.BlockSpec((ROW_TILE, mw), lambda i: (i, 0)),
                  pl.BlockSpec((k, d), lambda i: (0, 0)),
                  pl.BlockSpec((ROW_TILE, d), lambda i: (i, 0))],
        out_specs=pl.BlockSpec((ROW_TILE, d), lambda i: (i, 0)),
        out_shape=jax.ShapeDtypeStruct((n, d), F32),
        compiler_params=_params(1),
        name="outproj",
    )(oa, ob, oc, w_bf, x2)


def _ffn_up_kernel(x_ref, xh_ref, g_ref, wa_ref, wu_ref, cw_ref, cb_ref, o_ref, hn_scr, hh_scr, a_scr, *, tiles_per_seq):
    i, j = pl.program_id(0), pl.program_id(1)
    tm = x_ref.shape[0]

    @pl.when(j == 0)
    def _norm():
        hn_scr[...] = _rmsnorm_rows(x_ref[...], g_ref[...]).astype(BF16)
        keep = jnp.where(i % tiles_per_seq == 0, 0.0, 1.0)
        hh_scr[...] = (_rmsnorm_rows(xh_ref[...], g_ref[...]) * keep).astype(BF16)

    hn = hn_scr[...]
    a = jnp.dot(hn, wa_ref[...], preferred_element_type=F32)
    u = jnp.dot(hn, wu_ref[...], preferred_element_type=F32)
    a_scr[0:HALO, :] = jnp.dot(hh_scr[...], wa_ref[...], preferred_element_type=F32)
    a_scr[HALO:HALO + tm, :] = a
    a1 = a_scr[pl.ds(HALO - 1, tm), :]
    a2 = a_scr[pl.ds(HALO - 2, tm), :]
    cw = cw_ref[...]
    conv = cw[0:1, :] * a2 + cw[1:2, :] * a1 + cw[2:3, :] * a + cb_ref[...]
    o_ref[...] = (conv * jax.nn.sigmoid(conv) * u).astype(BF16)


def _ffn_up(x2, g, w_up_bf, layer, conv_w, conv_b, seq_len):
    n, d = x2.shape
    tn = D_FF // 2
    nj = D_FF // tn
    hb = ROW_TILE // HALO
    kern = functools.partial(_ffn_up_kernel, tiles_per_seq=seq_len // ROW_TILE)
    return pl.pallas_call(
        kern,
        grid=(n // ROW_TILE, nj),
        in_specs=[pl.BlockSpec((ROW_TILE, d), lambda i, j: (i, 0)),
                  pl.BlockSpec((HALO, d), lambda i, j: (jnp.maximum(i * hb - 1, 0), 0)),
                  pl.BlockSpec((1, d), lambda i, j: (0, 0)),
                  pl.BlockSpec((None, d, tn), lambda i, j: (layer, 0, j)),
                  pl.BlockSpec((None, d, tn), lambda i, j: (layer, 0, j + nj)),
                  pl.BlockSpec((CONV_WIDTH, tn), lambda i, j: (0, j)),
                  pl.BlockSpec((1, tn), lambda i, j: (0, j))],
        out_specs=pl.BlockSpec((ROW_TILE, tn), lambda i, j: (i, j)),
        out_shape=jax.ShapeDtypeStruct((n, D_FF), BF16),
        scratch_shapes=[pltpu.VMEM((ROW_TILE, d), BF16), pltpu.VMEM((HALO, d), BF16),
                        pltpu.VMEM((ROW_TILE + HALO, tn), F32)],
        compiler_params=_params(2),
        name="ffn_up",
    )(x2, x2, g.reshape(1, d), w_up_bf, w_up_bf, conv_w, conv_b.reshape(1, D_FF))


def _ffn_down_kernel(a_ref, w_ref, x_ref, g_ref, o_ref, *, final_norm):
    y = x_ref[...] + jnp.dot(a_ref[...], w_ref[...], preferred_element_type=F32)
    if final_norm:
        y = _rmsnorm_rows(y, g_ref[...])
    o_ref[...] = y


def _ffn_down(act, w_bf, layer, x2, final_g, final_norm):
    n, d = x2.shape
    k = act.shape[1]
    return pl.pallas_call(
        functools.partial(_ffn_down_kernel, final_norm=final_norm),
        grid=(n // ROW_TILE,),
        in_specs=[pl.BlockSpec((ROW_TILE, k), lambda i: (i, 0)),
                  pl.BlockSpec((None, k, d), lambda i: (layer, 0, 0)),
                  pl.BlockSpec((ROW_TILE, d), lambda i: (i, 0)),
                  pl.BlockSpec((1, d), lambda i: (0, 0))],
        out_specs=pl.BlockSpec((ROW_TILE, d), lambda i: (i, 0)),
        out_shape=jax.ShapeDtypeStruct((n, d), F32),
        compiler_params=_params(1),
        name="ffn_down",
    )(act, w_bf, x2, final_g.reshape(1, d))


def _cmp_kernel(kv_ref, pos_ref, w1_ref, w2_ref, o_ref):
    ngrp = kv_ref.shape[0] // CMP_STRIDE
    lo = hi = None
    for p in range(CMP_STRIDE):
        x = kv_ref[pl.ds(p, ngrp, stride=CMP_STRIDE), :]
        a = jnp.dot((x + pos_ref[p:p + 1, :]).astype(BF16), w1_ref[p], preferred_element_type=F32)
        b = jnp.dot((x + pos_ref[CMP_STRIDE + p:CMP_STRIDE + p + 1, :]).astype(BF16), w1_ref[CMP_STRIDE + p],
                    preferred_element_type=F32)
        lo = a if lo is None else lo + a
        hi = b if hi is None else hi + b
    hid = jax.nn.gelu(lo + pltpu.roll(hi, ngrp - 1, 0))
    o_ref[0] = jnp.dot(hid.astype(BF16), w2_ref[...], preferred_element_type=F32)


def _nsa_compress(proj, pos_cat, w1_cat, w2_cat, batch, seq):
    ngrp = seq // CMP_STRIDE
    return pl.pallas_call(
        _cmp_kernel,
        grid=(batch,),
        in_specs=[pl.BlockSpec((seq, LANES), lambda i: (i, _OFF["a_kcvc"] // LANES)),
                  pl.BlockSpec(pos_cat.shape, lambda i: (0, 0)),
                  pl.BlockSpec(w1_cat.shape, lambda i: (0, 0, 0)),
                  pl.BlockSpec(w2_cat.shape, lambda i: (0, 0))],
        out_specs=pl.BlockSpec((1, ngrp, LANES), lambda i: (i, 0, 0)),
        out_shape=jax.ShapeDtypeStruct((batch, ngrp, LANES), F32),
        compiler_params=_params(1),
        name="nsa_compress",
    )(proj, pos_cat, w1_cat, w2_cat)


def _stack_heads_t(x_t, n_heads, hd):
    z = jnp.zeros((LANES - hd, x_t.shape[1]), F32)
    return jnp.concatenate(
        [jnp.concatenate([x_t[hd * h:hd * (h + 1), :], z], axis=0) for h in range(n_heads)], axis=1)


def _nsa_kernel(q_ref, g_ref, ksvs_ref, kwvw_ref, kvc_ref, c_ref, s1_ref, s2_ref, ovt_ref, cmpb_ref, winb_ref,
                caub_ref, o_ref, ks_scr, vst_scr, kw_scr, vwt_scr, kc_scr, vct_scr, blk_scr, *, n_top):
    i = pl.program_id(1)
    H = NSA_HEADS
    QW = q_ref.shape[0]
    HW = H * QW
    seq = ks_scr.shape[0]
    lane_q = lax.broadcasted_iota(I32, (QW, LANES), 1)

    @pl.when(i == 0)
    def _init():
        ks_scr[...] = jnp.zeros_like(ks_scr)
        kw_scr[...] = jnp.zeros_like(kw_scr)
        vst_scr[...] = jnp.zeros_like(vst_scr)
        vwt_scr[...] = jnp.zeros_like(vwt_scr)
        kvc = kvc_ref[0]
        lane_c = lax.broadcasted_iota(I32, kvc.shape, 1)
        kc_scr[...] = jnp.where(lane_c < HEAD_DIM, kvc, 0.0).astype(BF16)
        vct_scr[...] = kvc.T[HEAD_DIM:2 * HEAD_DIM, :].astype(BF16)

    c, s1, s2 = c_ref[...], s1_ref[...], s2_ref[...]
    r0 = pl.multiple_of(i * QW, QW)
    for src, kdst, vdst in ((ksvs_ref, ks_scr, vst_scr), (kwvw_ref, kw_scr, vwt_scr)):
        kv = src[...]
        kr = _rope(kv, c, s1, s2, ROPE_DIM // 2)
        kdst[pl.ds(r0, QW), :] = jnp.where(lane_q < HEAD_DIM, kr, 0.0).astype(BF16)
        vdst[:, pl.ds(r0, QW)] = kv.T[HEAD_DIM:2 * HEAD_DIM, :].astype(BF16)

    q = q_ref[...] * (SCALE * LOG2E)
    qr = _rope_wide(q, c, s1, s2, ROPE_DIM // 2)
    q_st = _stack_heads_t(q.T, H, HEAD_DIM).astype(BF16)
    qr_st = _stack_heads_t(qr.T, H, HEAD_DIM).astype(BF16)
    tile_h = lambda b: jnp.concatenate([b] * H, axis=1)

    ncp = kc_scr.shape[0]
    cshift = cmpb_ref.shape[0] - ncp
    cb = cmpb_ref[pl.ds(pl.multiple_of(cshift - (QW // CMP_STRIDE) * i, 8), ncp), :]
    s_c = jnp.dot(kc_scr[...], q_st, preferred_element_type=F32) + tile_h(cb)
    e_c, _, den_c = _softmax2_cols(s_c)
    t_row = i * QW + (lax.broadcasted_iota(I32, (1, HW), 1) & (QW - 1))
    r_c = jnp.where(t_row >= CMP_BLOCK - 1, 1.0 / den_c, 0.0)
    o_cmp = jnp.dot(vct_scr[...], e_c.astype(BF16), preferred_element_type=F32) * r_c
    psum = e_c[:, 0:QW] * r_c[:, 0:QW]
    for h in range(1, H):
        psum = psum + e_c[:, QW * h:QW * (h + 1)] * r_c[:, QW * h:QW * (h + 1)]

    hi = psum.astype(BF16)
    r1 = psum - hi.astype(F32)
    mid = r1.astype(BF16)
    lo = (r1 - mid.astype(F32)).astype(BF16)
    ovt = ovt_ref[...]
    imp = (jnp.dot(ovt, hi, preferred_element_type=F32) + jnp.dot(ovt, mid, preferred_element_type=F32)
           + jnp.dot(ovt, lo, preferred_element_type=F32))
    n_slc = imp.shape[0]
    jb = lax.broadcasted_iota(I32, (n_slc, QW), 0)
    tq = i * QW + lax.broadcasted_iota(I32, (n_slc, QW), 1)
    cur = tq >> (SLC_BLOCK.bit_length() - 1)
    forced = (jb == 0) | (jb == cur) | (jb == cur - 1)
    val = jnp.where(forced, FORCE, jnp.where(jb <= cur, imp, NEG))
    rank = jnp.zeros((n_slc, QW), I32)
    for jp in range(n_slc):
        row = val[jp:jp + 1, :]
        tie = jnp.where(jb > jp, 1, 0)
        rank = rank + jnp.where(row > val, 1, jnp.where(row == val, tie, 0))
    blk_scr[...] = jnp.where(rank < n_top, jnp.where(val > 0.5 * NEG, 0.0, NEG), NEG)

    bpc = KC // SLC_BLOCK

    def slc_body(ci, carry):
        m, l, acc = carry
        k0 = pl.multiple_of(ci * KC, KC)
        ahead = jnp.minimum(i * QW - k0, KC)
        c0 = pl.multiple_of(KC - ahead, QB)
        hk = KC // ATT_SPLIT
        ss, m_new = [], m
        for sp in range(ATT_SPLIT):
            kb = pl.multiple_of(k0 + sp * hk, hk)
            rows = [jnp.broadcast_to(blk_scr[pl.ds(ci * bpc + sp * (bpc // ATT_SPLIT) + r, 1), :], (SLC_BLOCK, QW))
                    for r in range(bpc // ATT_SPLIT)]
            bias = jnp.concatenate(rows, axis=0) + caub_ref[pl.ds(pl.multiple_of(c0 + sp * hk, QB), hk), :]
            s = jnp.dot(ks_scr[pl.ds(kb, hk), :], qr_st, preferred_element_type=F32) + tile_h(bias)
            ss.append(s)
            m_new = jnp.maximum(m_new, jnp.max(s, axis=0, keepdims=True))
        alpha = jnp.exp2(m - m_new)
        l, acc = alpha * l, alpha * acc
        for sp in range(ATT_SPLIT):
            kb = pl.multiple_of(k0 + sp * hk, hk)
            e = jnp.exp2(ss[sp] - m_new)
            l = l + jnp.sum(e, axis=0, keepdims=True)
            acc = acc + jnp.dot(vst_scr[:, pl.ds(kb, hk)], e.astype(BF16), preferred_element_type=F32)
        return m_new, l, acc

    n_chunks = (i * QW + QW + KC - 1) // KC
    init = (jnp.full((1, HW), NEG, F32), jnp.zeros((1, HW), F32), jnp.zeros((HEAD_DIM, HW), F32))
    m_s, l_s, acc_s = lax.fori_loop(0, n_chunks, slc_body, init)
    o_slc = jnp.where(m_s > 0.5 * NEG, acc_s / l_s, 0.0)

    span = min(WIN + QW, seq)
    w0 = pl.multiple_of(jnp.maximum(i * QW + QW - span, 0), QB)
    wb = winb_ref[pl.ds(pl.multiple_of(span - QW - (i * QW - w0), QB), span), :]
    s_w = jnp.dot(kw_scr[pl.ds(w0, span), :], qr_st, preferred_element_type=F32) + tile_h(wb)
    e_w, _, den_w = _softmax2_cols(s_w)
    o_win = jnp.dot(vwt_scr[:, pl.ds(w0, span)], e_w.astype(BF16), preferred_element_type=F32) * (1.0 / den_w)

    gt = jax.nn.sigmoid(g_ref[...].T)
    outs = []
    for h in range(H):
        sl = slice(QW * h, QW * (h + 1))
        outs.append(gt[3 * h:3 * h + 1, :] * o_cmp[:, sl] + gt[3 * h + 1:3 * h + 2, :] * o_slc[:, sl]
                    + gt[3 * h + 2:3 * h + 3, :] * o_win[:, sl])
    outs.append(jnp.zeros((o_ref.shape[1] - H * HEAD_DIM, QW), F32))
    o_ref[...] = jnp.concatenate(outs, axis=0).T.astype(BF16)


def _nsa(proj, kvcmp, tabs, ovt, batch, seq):
    qw = NSA_QW
    nb = seq // qw
    n_slc = seq // SLC_BLOCK
    ncp = kvcmp.shape[1]
    col = lambda name, w: _OFF[name] // w
    qmap = lambda cidx: (lambda b, i: (b * nb + i, cidx))
    tmap = lambda k: (lambda b, i: (k, i, 0))
    kern = functools.partial(_nsa_kernel, n_top=min(SLC_TOPN, n_slc))
    cmpb, winb, caub = _cmp_bias_table(seq, qw), _win_bias_table(seq, qw), _causal_bias_table(qw)
    whole = lambda a: pl.BlockSpec(a.shape, lambda b, i: (0, 0))
    return pl.pallas_call(
        kern,
        grid=(batch, nb),
        in_specs=[pl.BlockSpec((qw, 384), qmap(col("a_q", 384))),
                  pl.BlockSpec((qw, LANES), qmap(col("a_g", LANES))),
                  pl.BlockSpec((qw, LANES), qmap(col("a_ksvs", LANES))),
                  pl.BlockSpec((qw, LANES), qmap(col("a_kwvw", LANES))),
                  pl.BlockSpec((1, ncp, LANES), lambda b, i: (b, 0, 0)),
                  pl.BlockSpec((None, qw, LANES), tmap(0)),
                  pl.BlockSpec((None, qw, LANES), tmap(1)),
                  pl.BlockSpec((None, qw, LANES), tmap(2)),
                  whole(ovt), whole(cmpb), whole(winb), whole(caub)],
        out_specs=pl.BlockSpec((qw, 384), lambda b, i: (b * nb + i, 0)),
        out_shape=jax.ShapeDtypeStruct((batch * seq, 384), BF16),
        scratch_shapes=[pltpu.VMEM((seq, LANES), BF16), pltpu.VMEM((HEAD_DIM, seq), BF16),
                        pltpu.VMEM((seq, LANES), BF16), pltpu.VMEM((HEAD_DIM, seq), BF16),
                        pltpu.VMEM((ncp, LANES), BF16), pltpu.VMEM((HEAD_DIM, ncp), BF16),
                        pltpu.VMEM((n_slc, qw), F32)],
        compiler_params=_params(2),
        name="nsa",
    )(proj, proj, proj, proj, kvcmp, tabs, tabs, tabs, ovt, cmpb, winb, caub)


SEARCH_BISECT_FROM = 24
SEARCH_MAX_PASSES = 64
SEARCH_KEPT_LIMIT = 6
SEARCH_BLIND_PASSES = 12
INT_MAX = 2 ** 31 - 1
FLT_MIN_NORMAL = 1.17549435e-38
ZERO_BAND_END = 1 << 23


def _tree_rows(x, op):
    parts = [x[r:r + 8, :] for r in range(0, x.shape[0], 8)]
    while len(parts) > 1:
        nxt = [op(parts[a], parts[a + 1]) for a in range(0, len(parts) - 1, 2)]
        if len(parts) % 2:
            nxt.append(parts[-1])
        parts = nxt
    return parts[0]


def _f2key(v):
    bits = lax.bitcast_convert_type(v, I32)
    return bits ^ ((bits >> 31) & 0x7FFFFFFF)


def _key2f(k):
    return lax.bitcast_convert_type(k ^ ((k >> 31) & 0x7FFFFFFF), F32)


def _dsa_kernel(q_ref, iq_ref, ikw_ref, kv_ref, c_ref, s1_ref, s2_ref, ci_ref, si1_ref, si2_ref, o_ref,
                k_scr, vt_scr, ik_scr, key_scr, *, top, idx_bits):
    i = pl.program_id(1)
    H = DSA_HEADS
    QW = q_ref.shape[0]
    HW = H * QW
    seq = key_scr.shape[0]
    lane_q = lax.broadcasted_iota(I32, (QW, LANES), 1)

    @pl.when(i == 0)
    def _init():
        k_scr[...] = jnp.zeros_like(k_scr)
        vt_scr[...] = jnp.zeros_like(vt_scr)
        ik_scr[...] = jnp.zeros_like(ik_scr)

    c, s1, s2 = c_ref[...], s1_ref[...], s2_ref[...]
    ci, si1, si2 = ci_ref[...], si1_ref[...], si2_ref[...]
    r0 = pl.multiple_of(i * QW, QW)
    kv = kv_ref[...]
    k_scr[pl.ds(r0, QW), :] = jnp.where(lane_q < HEAD_DIM, _rope(kv, c, s1, s2, ROPE_DIM // 2), 0.0).astype(BF16)
    vt_scr[:, pl.ds(r0, QW)] = kv.T[HEAD_DIM:2 * HEAD_DIM, :].astype(BF16)
    ikw = ikw_ref[...]
    ik_scr[pl.ds(r0, QW), :] = jnp.where(lane_q < IDX_DIM, _rope(ikw, ci, si1, si2, IDX_ROPE_DIM // 2), 0.0).astype(BF16)

    qr = _rope_wide(q_ref[...] * (SCALE * LOG2E), c, s1, s2, ROPE_DIM // 2)
    qr_st = _stack_heads_t(qr.T, H, HEAD_DIM).astype(BF16)
    iqr = _rope_wide(iq_ref[...], ci, si1, si2, IDX_ROPE_DIM // 2)
    iq_st = _stack_heads_t(iqr.T, IDX_HEADS, IDX_DIM).astype(BF16)
    wt = ikw.T * ((IDX_DIM ** -0.5) * (IDX_HEADS ** -0.5))
    w_st = jnp.concatenate([wt[IDX_DIM + h:IDX_DIM + h + 1, :] for h in range(IDX_HEADS)], axis=1)

    n_chunks = (i * QW + QW + KC - 1) // KC
    k_io = lax.broadcasted_iota(I32, (KC, QW), 0)
    tq = i * QW + lax.broadcasted_iota(I32, (KC, QW), 1)

    sub_io = lax.broadcasted_iota(I32, (QB, QW), 0)
    sub_tq = i * QW + lax.broadcasted_iota(I32, (QB, QW), 1)

    def score_body(cidx, carry):
        mx, mn = carry
        k0 = pl.multiple_of(cidx * KC, KC)
        for sb in range(KC // QB):
            kb = pl.multiple_of(k0 + sb * QB, QB)
            d = jnp.dot(ik_scr[pl.ds(kb, QB), :], iq_st, preferred_element_type=F32)
            r = jnp.maximum(d, 0.0) * w_st
            sc = r[:, 0:QW]
            for h in range(1, IDX_HEADS):
                sc = sc + r[:, QW * h:QW * (h + 1)]
            kpos = sub_io + kb
            key = jnp.where(jnp.abs(sc) < FLT_MIN_NORMAL, seq - kpos, _f2key(sc))
            causal = kpos <= sub_tq
            key_scr[pl.ds(kb, QB), :] = jnp.where(causal, key, INT_MIN)
            mx = jnp.maximum(mx, _tree_rows(jnp.where(causal, key, INT_MIN), jnp.maximum))
            mn = jnp.minimum(mn, _tree_rows(jnp.where(causal, key, INT_MAX), jnp.minimum))
        return mx, mn

    mx8, mn8 = lax.fori_loop(0, n_chunks, score_body,
                             (jnp.full((8, QW), INT_MIN, I32), jnp.full((8, QW), INT_MAX, I32)))

    def count(pred):
        def body(cidx, acc):
            k0 = pl.multiple_of(cidx * KC, KC)
            return acc + _tree_rows(pred(key_scr[pl.ds(k0, KC), :], k_io + k0), jnp.add)
        acc8 = lax.fori_loop(0, n_chunks, body, jnp.zeros((8, QW), F32))
        return jnp.sum(acc8, axis=0, keepdims=True)

    topf = float(top)
    lo0 = functools.reduce(jnp.minimum, [mn8[r:r + 1, :] for r in range(8)])
    hi0 = functools.reduce(jnp.maximum, [mx8[r:r + 1, :] for r in range(8)]) + 1
    n_causal = (i * QW + 1 + lax.broadcasted_iota(I32, (1, QW), 1)).astype(F32)
    log_top = float(np.log(top))

    def in_zero_band(lo, hi):
        return jnp.where(lo >= 1, jnp.where(hi <= ZERO_BAND_END, 1.0, 0.0), 0.0) > 0.5

    def pending(lo, hi, c_lo):
        return (jnp.max(jnp.where(c_lo > topf, jnp.where(hi > lo + 1, 1.0, 0.0), 0.0)) > 0.5).astype(I32)

    def search_cond(c):
        return jnp.logical_and(c[0] < SEARCH_MAX_PASSES, c[1] > 0)

    def search_pass(it, st):
        lo, hi, c_lo, c_hi, kept, w_lo, w_hi = st
        f_lo = (jnp.log(c_lo) - log_top) * w_lo
        f_hi = (log_top - jnp.log(jnp.maximum(c_hi, 0.5))) * w_hi
        frac = f_lo / (f_lo + f_hi)
        v_lo, v_hi = _key2f(lo), _key2f(hi)
        cand = _f2key(v_lo + frac * (v_hi - v_lo))
        lo_f, hi_f = lo.astype(F32), hi.astype(F32)
        cand = jnp.where(in_zero_band(lo, hi), (lo_f + frac * (hi_f - lo_f)).astype(I32), cand)
        key_mid = (lo >> 1) + (hi >> 1) + (lo & hi & 1)
        cand = jnp.where(jnp.abs(kept) >= SEARCH_KEPT_LIMIT, key_mid, cand)
        cand = jnp.where(it >= SEARCH_BISECT_FROM, key_mid, cand)
        cand = jnp.where(it == 0, 1, jnp.where(it == 1, ZERO_BAND_END, cand))
        cand = jnp.minimum(jnp.maximum(cand, lo + 1), hi - 1)
        cnt = count(lambda key, kpos: jnp.where(key >= cand, 1.0, 0.0))
        active = jnp.where(c_lo > topf, jnp.where(hi > lo + 1, 1.0, 0.0), 0.0) > 0.5
        up = jnp.where(active, jnp.where(cnt >= topf, 1.0, 0.0), 0.0) > 0.5
        dn = jnp.where(active, jnp.where(cnt >= topf, 0.0, 1.0), 0.0) > 0.5
        w_hi = jnp.where(up, jnp.where(kept < 0, 0.5 * w_hi, 1.0), jnp.where(dn, 1.0, w_hi))
        w_lo = jnp.where(dn, jnp.where(kept > 0, 0.5 * w_lo, 1.0), jnp.where(up, 1.0, w_lo))
        kept = jnp.where(up, jnp.where(kept < 0, kept - 1, -1), jnp.where(dn, jnp.where(kept > 0, kept + 1, 1), kept))
        lo, c_lo = jnp.where(up, cand, lo), jnp.where(up, cnt, c_lo)
        hi, c_hi = jnp.where(dn, cand, hi), jnp.where(dn, cnt, c_hi)
        hi = jnp.where(in_zero_band(lo, hi), jnp.minimum(hi, seq + 1), hi)
        return lo, hi, c_lo, c_hi, kept, w_lo, w_hi

    def search_body(c):
        st = search_pass(c[0], c[2])
        return c[0] + 1, pending(st[0], st[1], st[2]), st

    zero_f, one_f = jnp.zeros((1, QW), F32), jnp.ones((1, QW), F32)
    st0 = (lo0, hi0, n_causal, zero_f, jnp.zeros((1, QW), I32), one_f, one_f)
    n_blind = jnp.where(i * QW + QW > top, SEARCH_BLIND_PASSES, 0)
    st1 = lax.fori_loop(0, n_blind, search_pass, st0)
    _, _, (thr, _, c_lo, c_hi, _, _, _) = lax.while_loop(
        search_cond, search_body, (n_blind, pending(st1[0], st1[1], st1[2]), st1))

    tied = c_lo > topf
    need1 = (topf - 1.0) - c_hi
    all_pos = jnp.full((1, QW), 2 ** idx_bits - 1, I32)

    def tie_search():
        def tie_body(bi, cur):
            cand = cur | lax.shift_left(jnp.int32(1), idx_bits - 1 - bi)
            cnt = count(lambda key, kpos: jnp.where(key == thr, jnp.where(kpos < cand, 1.0, 0.0), 0.0))
            return jnp.where(cnt <= need1, cand, cur)
        return lax.fori_loop(0, idx_bits, tie_body, jnp.zeros((1, QW), I32))

    any_tied = jnp.max(jnp.where(tied, 1.0, 0.0)) > 0.5
    last = jnp.where(tied, lax.cond(any_tied, tie_search, lambda: all_pos), all_pos)

    half_neg = int(np.float32(0.5 * NEG).view(np.int32))
    key_floor = half_neg ^ 0x7FFFFFFF

    def att_chunk(cidx, state):
        m, l, acc = state
        k0 = pl.multiple_of(cidx * KC, KC)
        key = key_scr[pl.ds(k0, KC), :]
        kpos = k_io + k0
        bias = jnp.where(key > thr, 0.0, jnp.where(key == thr, jnp.where(kpos <= last, 0.0, NEG), NEG))
        bias = jnp.where(key > key_floor, bias, NEG)
        s = (jnp.dot(k_scr[pl.ds(k0, KC), :], qr_st, preferred_element_type=F32)
             + jnp.concatenate([bias] * H, axis=1))
        m_new = jnp.maximum(m, jnp.max(s, axis=0, keepdims=True))
        alpha = jnp.exp2(m - m_new)
        e = jnp.exp2(s - m_new)
        l = alpha * l + jnp.sum(e, axis=0, keepdims=True)
        acc = alpha * acc + jnp.dot(vt_scr[:, pl.ds(k0, KC)], e.astype(BF16), preferred_element_type=F32)
        return m_new, l, acc

    init = (jnp.full((1, HW), NEG, F32), jnp.zeros((1, HW), F32), jnp.zeros((HEAD_DIM, HW), F32))
    m_a, l_a, acc_a = lax.fori_loop(0, n_chunks, att_chunk, init)
    o = jnp.where(m_a > 0.5 * NEG, acc_a / l_a, 0.0)
    outs = [o[:, QW * h:QW * (h + 1)] for h in range(H)]
    outs.append(jnp.zeros((o_ref.shape[1] - H * HEAD_DIM, QW), F32))
    o_ref[...] = jnp.concatenate(outs, axis=0).T.astype(BF16)


def _dsa(proj, tabs, tabs_i, batch, seq):
    qw = DSA_QW
    nb = seq // qw
    col = lambda name, w: _OFF[name] // w
    qmap = lambda cidx: (lambda b, i: (b * nb + i, cidx))
    tmap = lambda k: (lambda b, i: (k, i, 0))
    top = min(DSA_TOPK_MAX, seq // 4)
    kern = functools.partial(_dsa_kernel, top=top, idx_bits=int(seq).bit_length())
    return pl.pallas_call(
        kern,
        grid=(batch, nb),
        in_specs=[pl.BlockSpec((qw, 384), qmap(col("b_q", 384))),
                  pl.BlockSpec((qw, 256), qmap(col("b_iq", 256))),
                  pl.BlockSpec((qw, LANES), qmap(col("b_ikiw", LANES))),
                  pl.BlockSpec((qw, LANES), qmap(col("b_kv", LANES))),
                  pl.BlockSpec((None, qw, LANES), tmap(0)),
                  pl.BlockSpec((None, qw, LANES), tmap(1)),
                  pl.BlockSpec((None, qw, LANES), tmap(2)),
                  pl.BlockSpec((None, qw, LANES), tmap(0)),
                  pl.BlockSpec((None, qw, LANES), tmap(1)),
                  pl.BlockSpec((None, qw, LANES), tmap(2))],
        out_specs=pl.BlockSpec((qw, 384), lambda b, i: (b * nb + i, 0)),
        out_shape=jax.ShapeDtypeStruct((batch * seq, 384), BF16),
        scratch_shapes=[pltpu.VMEM((seq, LANES), BF16), pltpu.VMEM((HEAD_DIM, seq), BF16),
                        pltpu.VMEM((seq, LANES), BF16), pltpu.VMEM((seq, qw), I32)],
        compiler_params=_params(2),
        name="dsa",
    )(proj, proj, proj, proj, tabs, tabs, tabs, tabs_i, tabs_i, tabs_i)


DIL_MB = max(d for _, d in DIL_PAIRS) * QB
DIL_VMEM_LIMIT = 56 * 1024 * 1024
DIL_UNROLL = 4


def _dil_layout(g):
    dil = DIL_PAIRS[g][1]
    per = DIL_MB // dil
    return dil, per, per + QB


def _dil_bias_table():
    u = np.arange(2 * QB)[:, None]
    diff = QB + np.arange(LANES)[None, :] - u
    ok = (diff >= 0) & (diff <= QB)
    return _bias_table(np.concatenate([ok, ok & (u >= QB)], axis=0))


def _dil_kernel(*refs):
    G, HG = len(DIL_PAIRS), DIL_HEADS_PER_GROUP
    q_refs, k_refs, v_refs = refs[0:G], refs[G:2 * G], refs[2 * G:3 * G]
    c_ref, s1_ref, s2_ref, bias_ref, o_ref = refs[3 * G:3 * G + 5]
    kds, vds, ogs, lss = (refs[3 * G + 5 + n * G:3 * G + 5 + (n + 1) * G] for n in range(4))
    step = pl.program_id(1)
    half = ROPE_DIM // 2

    @pl.when(step == 0)
    def _zero():
        for g in range(G):
            kds[g][...] = jnp.zeros_like(kds[g])
            vds[g][...] = jnp.zeros_like(vds[g])

    @pl.when(step > 0)
    def _carry_halo():
        for g in range(G):
            dil, per, stride = _dil_layout(g)
            for r in range(dil):
                b0 = r * stride
                kds[g][b0:b0 + QB, :] = kds[g][b0 + per:b0 + per + QB, :]
                vds[g][:, b0:b0 + QB] = vds[g][:, b0 + per:b0 + per + QB]

    for g in range(G):
        dil, per, stride = _dil_layout(g)
        piece = min(per, KC)
        for r in range(dil):
            for p0 in range(0, per, piece):
                rows = pl.ds(r + dil * p0, piece, stride=dil)
                kr = _rope(k_refs[g][rows, :], c_ref[rows, :], s1_ref[rows, :], s2_ref[rows, :], half)
                d0 = r * stride + QB + p0
                kds[g][d0:d0 + piece, :] = kr.astype(BF16)
                vds[g][:, d0:d0 + piece] = v_refs[g][rows, :].T.astype(BF16)

    zero = jnp.zeros((HEAD_DIM, LANES), F32)
    for g in range(G):
        dil, per, stride = _dil_layout(g)
        nblk = per // QB

        def block(jb, _, r=0, g=g, dil=dil, stride=stride):
            p0 = pl.multiple_of(dil * QB * jb, dil * QB)
            win = pl.ds(p0, dil * QB)
            rows = pl.ds(r, QB, stride=dil)
            q = _rope(q_refs[g].at[win, :][rows, :] * (SCALE * LOG2E), c_ref.at[win, :][rows, :],
                      s1_ref.at[win, :][rows, :], s2_ref.at[win, :][rows, :], half)
            qt = q.T
            q2 = jnp.concatenate([jnp.concatenate([qt[:HEAD_DIM], zero], axis=0),
                                  jnp.concatenate([zero, qt[HEAD_DIM:]], axis=0)], axis=1).astype(BF16)
            kb = pl.multiple_of(r * stride + QB * jb, QB)
            first = jnp.logical_and(step == 0, jb == 0)
            bias = bias_ref[pl.ds(pl.multiple_of(jnp.where(first, 2 * QB, 0), QB), 2 * QB), :]
            s = (jnp.dot(kds[g][pl.ds(kb, 2 * QB), :], q2, preferred_element_type=F32)
                 + jnp.concatenate([bias] * HG, axis=1))
            e, m, den = _softmax2_cols(s)
            lse = m + jnp.log2(den)
            rden = 1.0 / den
            parts, lparts = [], []
            for hg in range(HG):
                sl = slice(LANES * hg, LANES * (hg + 1))
                vt = vds[g][HEAD_DIM * hg:HEAD_DIM * (hg + 1), pl.ds(kb, 2 * QB)]
                parts.append(jnp.dot(vt, e[:, sl].astype(BF16), preferred_element_type=F32) * rden[:, sl])
                lparts.append(jnp.broadcast_to(lse[:, sl], (HEAD_DIM, LANES)))
            tile = jnp.concatenate(parts + lparts, axis=0).T
            ogs[g].at[win, :][rows, :] = tile[:, :LANES]
            lss[g].at[win, :][rows, :] = tile[:, LANES:]
            return 0

        for r in range(dil):
            lax.fori_loop(0, nblk, functools.partial(block, r=r), 0, unroll=min(nblk, DIL_UNROLL))

    def mix(ti, _):
        rows = pl.ds(pl.multiple_of(ti * KC, KC), KC)
        ls = [lss[g][rows, :] for g in range(G)]
        mx = functools.reduce(jnp.maximum, ls)
        ex = [jnp.exp2(x - mx) for x in ls]
        rtot = 1.0 / functools.reduce(lambda a, b: a + b, ex)
        o_ref[rows, :] = jnp.concatenate([ex[g] * rtot * ogs[g][rows, :] for g in range(G)], axis=1).astype(BF16)
        return 0

    lax.fori_loop(0, DIL_MB // KC, mix, 0)


def _dil(proj, tabs, batch, seq):
    assert seq % DIL_MB == 0 and all(w // d == QB for w, d in DIL_PAIRS)
    nm = seq // DIL_MB
    G = len(DIL_PAIRS)
    w = DIL_HEADS * HEAD_DIM
    gmap = lambda name, g: (lambda b, j: (b * nm + j, _OFF[name] // LANES + g))
    tmap = lambda k: (lambda b, j: (k, j, 0))
    bias = _dil_bias_table()
    rows = [_dil_layout(g)[0] * _dil_layout(g)[2] for g in range(G)]
    return pl.pallas_call(
        _dil_kernel,
        grid=(batch, nm),
        in_specs=([pl.BlockSpec((DIL_MB, LANES), gmap(name, g)) for name in ("c_q", "c_k", "c_v") for g in range(G)]
                  + [pl.BlockSpec((None, DIL_MB, LANES), tmap(k)) for k in range(3)]
                  + [pl.BlockSpec(bias.shape, lambda b, j: (0, 0))]),
        out_specs=pl.BlockSpec((DIL_MB, w), lambda b, j: (b * nm + j, 0)),
        out_shape=jax.ShapeDtypeStruct((batch * seq, w), BF16),
        scratch_shapes=([pltpu.VMEM((n, LANES), BF16) for n in rows] + [pltpu.VMEM((LANES, n), BF16) for n in rows]
                        + [pltpu.VMEM((DIL_MB, LANES), F32)] * (2 * G)),
        compiler_params=_params(2, DIL_VMEM_LIMIT),
        name="dilated",
    )(*([proj] * (3 * G)), tabs, tabs, tabs, bias)


def _overlap_t(seq):
    n_cmp_pad = seq // CMP_STRIDE
    n_slc = seq // SLC_BLOCK
    c_start = np.arange(n_cmp_pad) * CMP_STRIDE
    s_start = np.arange(n_slc) * SLC_BLOCK
    ov = ((c_start[None, :] < s_start[:, None] + SLC_BLOCK) & (c_start[None, :] + CMP_BLOCK > s_start[:, None]))
    return jnp.asarray(ov.astype(np.float32), dtype=BF16)


def _layer(x2, batch, seq, layer, norm1_g, w_in, cmp_pos, cmp_w1, cmp_w2, w_out, norm2_g, w_up_bf, conv_w, conv_b,
           w_down_bf, final_g, final_norm, tabs, tabs_i, ovt):
    n = batch * seq
    proj = _inproj(x2, norm1_g, _regroup_w_in(w_in, layer))

    w1 = cmp_w1.reshape(2, CMP_BLOCK, HEAD_DIM, CMP_HIDDEN)
    z1 = jnp.zeros_like(w1[0])
    w1_cat = jnp.concatenate([jnp.concatenate([w1[0], z1], axis=2),
                              jnp.concatenate([z1, w1[1]], axis=2)], axis=1).astype(BF16)
    z2 = jnp.zeros((CMP_HIDDEN, HEAD_DIM), cmp_w2.dtype)
    w2_cat = jnp.concatenate([jnp.concatenate([cmp_w2[0], z2], axis=1),
                              jnp.concatenate([z2, cmp_w2[1]], axis=1)], axis=0).astype(BF16)
    pos_cat = jnp.concatenate([cmp_pos[0], cmp_pos[1]], axis=1)
    kvcmp = _nsa_compress(proj, pos_cat, w1_cat, w2_cat, batch, seq)

    o_a = _nsa(proj, kvcmp, tabs, ovt, batch, seq)
    o_b = _dsa(proj, tabs, tabs_i, batch, seq)
    o_c = _dil(proj, tabs, batch, seq)

    hp = NSA_HEADS * HEAD_DIM
    zrow = jnp.zeros((384 - hp, D_MODEL), w_out.dtype)
    w_out_pad = jnp.concatenate([w_out[0:hp], zrow, w_out[hp:2 * hp], zrow, w_out[2 * hp:]], axis=0).astype(BF16)
    x2 = _outproj(o_a, o_b, o_c, w_out_pad, x2)

    act = _ffn_up(x2, norm2_g, w_up_bf, layer, conv_w, conv_b, seq)
    return _ffn_down(act, w_down_bf, layer, x2, final_g, final_norm)


def kernel(x, norm1_g, w_in, cmp_pos, cmp_w1, cmp_w2, w_out, norm2_g, w_up, conv_w, conv_b, w_down, final_g):
    batch, seq, d = x.shape
    depth = w_in.shape[0]
    tabs = _rope_lane_tables(seq, HEAD_DIM, ROPE_DIM)
    tabs_i = _rope_lane_tables(seq, IDX_DIM, IDX_ROPE_DIM)
    ovt = _overlap_t(seq)
    x2 = x.reshape(batch * seq, d)
    w_up_bf, w_down_bf = w_up.astype(BF16), w_down.astype(BF16)
    for li in range(depth):
        x2 = _layer(x2, batch, seq, li, norm1_g[li], w_in, cmp_pos[li], cmp_w1[li], cmp_w2[li], w_out[li],
                    norm2_g[li], w_up_bf, conv_w[li], conv_b[li], w_down_bf, final_g, li == depth - 1,
                    tabs, tabs_i, ovt)
    return x2.reshape(batch, seq, d)
```

```python
import functools

import numpy as np
import jax
import jax.numpy as jnp
from jax import lax
from jax.experimental import pallas as pl
from jax.experimental.pallas import tpu as pltpu

F32 = jnp.float32
BF16 = jnp.bfloat16
I32 = jnp.int32

D_MODEL = 1024
HEAD_DIM = 64
ROPE_DIM = HEAD_DIM // 4
ROPE_THETA = 500000.0
NORM_EPS = 1e-6
SCALE = HEAD_DIM ** -0.5
LOG2E = 1.4426950408889634
NEG = -1e30
FORCE = 1e9
NSA_HEADS = 5
CMP_BLOCK = 32
CMP_STRIDE = 16
CMP_HIDDEN = 128
SLC_BLOCK = 64
SLC_TOPN = 16
WIN = 512
DSA_HEADS = 5
IDX_HEADS = 8
IDX_DIM = 32
IDX_ROPE_DIM = IDX_DIM // 4
DSA_TOPK_MAX = 256
DIL_PAIRS = ((128, 1), (512, 4), (2048, 16))
DIL_HEADS_PER_GROUP = 2
DIL_HEADS = len(DIL_PAIRS) * DIL_HEADS_PER_GROUP
D_FF = 2816
CONV_WIDTH = 3

LANES = 128
QB = 128
DSA_QW = 256
NSA_QW = 256
KC = 512
ATT_LONG = 2
ROW_TILE = 512
HALO = 16
VMEM_LIMIT = 48 * 1024 * 1024
INT_MIN = -2 ** 31

_A0 = 0
_B0 = 719
_C0 = 1463
_SEGS = (
    ("c_q", _C0, 384, 384), ("c_k", _C0 + 384, 384, 384), ("c_v", _C0 + 768, 384, 384),
    ("a_q", _A0, 320, 384), ("b_q", _B0, 320, 384),
    ("a_kcvc", _A0 + 320, 128, 128),
    ("b_iq", _B0 + 448, 256, 256),
    ("a_ksvs", _A0 + 448, 128, 128), ("a_kwvw", _A0 + 576, 128, 128),
    ("a_g", _A0 + 704, 15, 128),
    ("b_kv", _B0 + 320, 128, 128),
    ("b_ikiw", _B0 + 704, 40, 128),
)
P_COLS = sum(s[3] for s in _SEGS)


def _seg_offsets():
    offs, o = {}, 0
    for name, _, _, pw in _SEGS:
        offs[name] = o
        o += pw
    return offs


_OFF = _seg_offsets()


def _regroup_kernel(w_ref, o_ref):
    for name, src, wdt, pw in _SEGS:
        dst = _OFF[name]
        o_ref[:, dst:dst + wdt] = w_ref[:, src:src + wdt].astype(BF16)
        if pw > wdt:
            o_ref[:, dst + wdt:dst + pw] = jnp.zeros((o_ref.shape[0], pw - wdt), BF16)


def _regroup_w_in(w, layer):
    _, d, n_in = w.shape
    rt = 256
    return pl.pallas_call(
        _regroup_kernel,
        grid=(d // rt,),
        in_specs=[pl.BlockSpec((None, rt, n_in), lambda i: (layer, i, 0))],
        out_specs=pl.BlockSpec((rt, P_COLS), lambda i: (i, 0)),
        out_shape=jax.ShapeDtypeStruct((d, P_COLS), BF16),
        compiler_params=_params(1),
        name="regroup_w_in",
    )(w)


def _rope_lane_tables(L, head_dim, rot_dim):
    half = rot_dim // 2
    inv = 1.0 / (ROPE_THETA ** (np.arange(0, rot_dim, 2, dtype=np.float32) / np.float32(rot_dim)))
    ang = np.arange(L, dtype=np.float32)[:, None] * inv[None, :]
    cos, sin = np.cos(ang).astype(np.float32), np.sin(ang).astype(np.float32)
    d = np.arange(LANES) % head_dim
    lo, hi = d < half, (d >= half) & (d < rot_dim)
    c = np.ones((L, LANES), np.float32)
    s1 = np.zeros((L, LANES), np.float32)
    s2 = np.zeros((L, LANES), np.float32)
    c[:, lo] = cos[:, d[lo]]
    c[:, hi] = cos[:, d[hi] - half]
    s1[:, lo] = -sin[:, d[lo]]
    s2[:, hi] = sin[:, d[hi] - half]
    return jnp.asarray(np.stack([c, s1, s2]))


def _rope(x, c, s1, s2, half):
    xp = pltpu.roll(x, LANES - half, 1)
    xm = pltpu.roll(x, half, 1)
    return x * c + xp * s1 + xm * s2


def _rope_wide(x, c, s1, s2, half):
    n = x.shape[1] // LANES
    return jnp.concatenate([_rope(x[:, LANES * s:LANES * (s + 1)], c, s1, s2, half) for s in range(n)], axis=1)


def _softmax2_cols(s):
    m = jnp.max(s, axis=0, keepdims=True)
    e = jnp.exp2(s - m)
    return e, m, jnp.sum(e, axis=0, keepdims=True)


def _bias_table(ok):
    return jnp.asarray(np.where(ok, 0.0, NEG).astype(np.float32))


def _cmp_bias_table(seq, qw):
    shift = (qw // CMP_STRIDE) * (seq // qw - 1)
    u = np.arange(seq // CMP_STRIDE + shift)[:, None]
    lane = np.arange(qw)[None, :]
    return _bias_table(CMP_STRIDE * (u - shift) + CMP_BLOCK - 1 <= lane)


def _win_bias_table(seq, qw):
    span = min(WIN + qw, seq)
    cmax = span - qw
    u = np.arange(span + cmax)[:, None]
    diff = cmax + np.arange(qw)[None, :] - u
    return _bias_table((diff >= 0) & (diff < WIN))


def _causal_bias_table(qw):
    kmax = ATT_LONG * KC
    u = np.arange(2 * kmax)[:, None]
    return _bias_table(u <= kmax + np.arange(qw)[None, :])


def _params(n_grid, vmem=VMEM_LIMIT):
    return pltpu.CompilerParams(dimension_semantics=("arbitrary",) * n_grid, vmem_limit_bytes=vmem)


def _rmsnorm_rows(x, g):
    return x * lax.rsqrt(jnp.mean(x * x, axis=-1, keepdims=True) + NORM_EPS) * g


def _inproj_kernel(x_ref, g_ref, w_ref, o_ref):
    hn = _rmsnorm_rows(x_ref[...], g_ref[...]).astype(BF16)
    n = w_ref.shape[1]
    for c0 in range(0, n, 512):
        c1 = min(c0 + 512, n)
        o_ref[:, c0:c1] = jnp.dot(hn, w_ref[:, c0:c1], preferred_element_type=F32)


def _inproj(x2, g, w_bf):
    n, d = x2.shape
    pc = w_bf.shape[1]
    return pl.pallas_call(
        _inproj_kernel,
        grid=(n // ROW_TILE,),
        in_specs=[pl.BlockSpec((ROW_TILE, d), lambda i: (i, 0)),
                  pl.BlockSpec((1, d), lambda i: (0, 0)),
                  pl.BlockSpec((d, pc), lambda i: (0, 0))],
        out_specs=pl.BlockSpec((ROW_TILE, pc), lambda i: (i, 0)),
        out_shape=jax.ShapeDtypeStruct((n, pc), F32),
        compiler_params=_params(1),
        name="inproj",
    )(x2, g.reshape(1, d), w_bf)


def _outproj_kernel(oa_ref, ob_ref, oc_ref, w_ref, x_ref, o_ref):
    mix = jnp.concatenate([oa_ref[...], ob_ref[...], oc_ref[...]], axis=1)
    o_ref[...] = x_ref[...] + jnp.dot(mix, w_ref[...], preferred_element_type=F32)


def _outproj(oa, ob, oc, w_bf, x2):
    n, d = x2.shape
    k = w_bf.shape[0]
    mw = oa.shape[1]
    return pl.pallas_call(
        _outproj_kernel,
        grid=(n // ROW_TILE,),
        in_specs=[pl.BlockSpec((ROW_TILE, mw), lambda i: (i, 0)),
                  pl.BlockSpec((ROW_TILE, mw), lambda i: (i, 0)),
                  pl.BlockSpec((ROW_TILE, mw), lambda i: (i, 0)),
                  pl.BlockSpec((k, d), lambda i: (0, 0)),
                  pl.BlockSpec((ROW_TILE, d), lambda i: (i, 0))],
        out_specs=pl.BlockSpec((ROW_TILE, d), lambda i: (i, 0)),
        out_shape=jax.ShapeDtypeStruct((n, d), F32),
        compiler_params=_params(1),
        name="outproj",
    )(oa, ob, oc, w_bf, x2)


def _ffn_up_kernel(x_ref, xh_ref, g_ref, wa_ref, wu_ref, cw_ref, cb_ref, o_ref, hn_scr, hh_scr, a_scr, *, tiles_per_seq):
    i, j = pl.program_id(0), pl.program_id(1)
    tm = x_ref.shape[0]

    @pl.when(j == 0)
    def _norm():
        hn_scr[...] = _rmsnorm_rows(x_ref[...], g_ref[...]).astype(BF16)
        keep = jnp.where(i % tiles_per_seq == 0, 0.0, 1.0)
        hh_scr[...] = (_rmsnorm_rows(xh_ref[...], g_ref[...]) * keep).astype(BF16)

    hn = hn_scr[...]
    a = jnp.dot(hn, wa_ref[...], preferred_element_type=F32)
    u = jnp.dot(hn, wu_ref[...], preferred_element_type=F32)
    a_scr[0:HALO, :] = jnp.dot(hh_scr[...], wa_ref[...], preferred_element_type=F32)
    a_scr[HALO:HALO + tm, :] = a
    a1 = a_scr[pl.ds(HALO - 1, tm), :]
    a2 = a_scr[pl.ds(HALO - 2, tm), :]
    cw = cw_ref[...]
    conv = cw[0:1, :] * a2 + cw[1:2, :] * a1 + cw[2:3, :] * a + cb_ref[...]
    o_ref[...] = (conv * jax.nn.sigmoid(conv) * u).astype(BF16)


def _ffn_up(x2, g, w_up_bf, layer, conv_w, conv_b, seq_len):
    n, d = x2.shape
    tn = D_FF // 2
    nj = D_FF // tn
    hb = ROW_TILE // HALO
    kern = functools.partial(_ffn_up_kernel, tiles_per_seq=seq_len // ROW_TILE)
    return pl.pallas_call(
        kern,
        grid=(n // ROW_TILE, nj),
        in_specs=[pl.BlockSpec((ROW_TILE, d), lambda i, j: (i, 0)),
                  pl.BlockSpec((HALO, d), lambda i, j: (jnp.maximum(i * hb - 1, 0), 0)),
                  pl.BlockSpec((1, d), lambda i, j: (0, 0)),
                  pl.BlockSpec((None, d, tn), lambda i, j: (layer, 0, j)),
                  pl.BlockSpec((None, d, tn), lambda i, j: (layer, 0, j + nj)),
                  pl.BlockSpec((CONV_WIDTH, tn), lambda i, j: (0, j)),
                  pl.BlockSpec((1, tn), lambda i, j: (0, j))],
        out_specs=pl.BlockSpec((ROW_TILE, tn), lambda i, j: (i, j)),
        out_shape=jax.ShapeDtypeStruct((n, D_FF), BF16),
        scratch_shapes=[pltpu.VMEM((ROW_TILE, d), BF16), pltpu.VMEM((HALO, d), BF16),
                        pltpu.VMEM((ROW_TILE + HALO, tn), F32)],
        compiler_params=_params(2),
        name="ffn_up",
    )(x2, x2, g.reshape(1, d), w_up_bf, w_up_bf, conv_w, conv_b.reshape(1, D_FF))


def _ffn_down_kernel(a_ref, w_ref, x_ref, g_ref, o_ref, *, final_norm):
    y = x_ref[...] + jnp.dot(a_ref[...], w_ref[...], preferred_element_type=F32)
    if final_norm:
        y = _rmsnorm_rows(y, g_ref[...])
    o_ref[...] = y


def _ffn_down(act, w_bf, layer, x2, final_g, final_norm):
    n, d = x2.shape
    k = act.shape[1]
    return pl.pallas_call(
        functools.partial(_ffn_down_kernel, final_norm=final_norm),
        grid=(n // ROW_TILE,),
        in_specs=[pl.BlockSpec((ROW_TILE, k), lambda i: (i, 0)),
                  pl.BlockSpec((None, k, d), lambda i: (layer, 0, 0)),
                  pl.BlockSpec((ROW_TILE, d), lambda i: (i, 0)),
                  pl.BlockSpec((1, d), lambda i: (0, 0))],
        out_specs=pl.BlockSpec((ROW_TILE, d), lambda i: (i, 0)),
        out_shape=jax.ShapeDtypeStruct((n, d), F32),
        compiler_params=_params(1),
        name="ffn_down",
    )(act, w_bf, x2, final_g.reshape(1, d))


def _cmp_kernel(kv_ref, pos_ref, w1_ref, w2_ref, o_ref):
    ngrp = kv_ref.shape[0] // CMP_STRIDE
    lo = hi = None
    for p in range(CMP_STRIDE):
        x = kv_ref[pl.ds(p, ngrp, stride=CMP_STRIDE), :]
        a = jnp.dot((x + pos_ref[p:p + 1, :]).astype(BF16), w1_ref[p], preferred_element_type=F32)
        b = jnp.dot((x + pos_ref[CMP_STRIDE + p:CMP_STRIDE + p + 1, :]).astype(BF16), w1_ref[CMP_STRIDE + p],
                    preferred_element_type=F32)
        lo = a if lo is None else lo + a
        hi = b if hi is None else hi + b
    hid = jax.nn.gelu(lo + pltpu.roll(hi, ngrp - 1, 0))
    o_ref[0] = jnp.dot(hid.astype(BF16), w2_ref[...], preferred_element_type=F32)


def _nsa_compress(proj, pos_cat, w1_cat, w2_cat, batch, seq):
    ngrp = seq // CMP_STRIDE
    return pl.pallas_call(
        _cmp_kernel,
        grid=(batch,),
        in_specs=[pl.BlockSpec((seq, LANES), lambda i: (i, _OFF["a_kcvc"] // LANES)),
                  pl.BlockSpec(pos_cat.shape, lambda i: (0, 0)),
                  pl.BlockSpec(w1_cat.shape, lambda i: (0, 0, 0)),
                  pl.BlockSpec(w2_cat.shape, lambda i: (0, 0))],
        out_specs=pl.BlockSpec((1, ngrp, LANES), lambda i: (i, 0, 0)),
        out_shape=jax.ShapeDtypeStruct((batch, ngrp, LANES), F32),
        compiler_params=_params(1),
        name="nsa_compress",
    )(proj, pos_cat, w1_cat, w2_cat)


def _stack_heads_t(x_t, n_heads, hd):
    z = jnp.zeros((LANES - hd, x_t.shape[1]), F32)
    return jnp.concatenate(
        [jnp.concatenate([x_t[hd * h:hd * (h + 1), :], z], axis=0) for h in range(n_heads)], axis=1)


def _nsa_kernel(q_ref, g_ref, ksvs_ref, kwvw_ref, kvc_ref, c_ref, s1_ref, s2_ref, ovt_ref, cmpb_ref, winb_ref,
                caub_ref, o_ref, ks_scr, vst_scr, kw_scr, vwt_scr, kc_scr, vct_scr, blk_scr, *, n_top):
    i = pl.program_id(1)
    H = NSA_HEADS
    QW = q_ref.shape[0]
    HW = H * QW
    seq = ks_scr.shape[0]
    lane_q = lax.broadcasted_iota(I32, (QW, LANES), 1)

    @pl.when(i == 0)
    def _init():
        ks_scr[...] = jnp.zeros_like(ks_scr)
        kw_scr[...] = jnp.zeros_like(kw_scr)
        vst_scr[...] = jnp.zeros_like(vst_scr)
        vwt_scr[...] = jnp.zeros_like(vwt_scr)
        kvc = kvc_ref[0]
        lane_c = lax.broadcasted_iota(I32, kvc.shape, 1)
        kc_scr[...] = jnp.where(lane_c < HEAD_DIM, kvc, 0.0).astype(BF16)
        vct_scr[...] = kvc.T[HEAD_DIM:2 * HEAD_DIM, :].astype(BF16)

    c, s1, s2 = c_ref[...], s1_ref[...], s2_ref[...]
    r0 = pl.multiple_of(i * QW, QW)
    for src, kdst, vdst in ((ksvs_ref, ks_scr, vst_scr), (kwvw_ref, kw_scr, vwt_scr)):
        kv = src[...]
        kr = _rope(kv, c, s1, s2, ROPE_DIM // 2)
        kdst[pl.ds(r0, QW), :] = jnp.where(lane_q < HEAD_DIM, kr, 0.0).astype(BF16)
        vdst[:, pl.ds(r0, QW)] = kv.T[HEAD_DIM:2 * HEAD_DIM, :].astype(BF16)

    q = q_ref[...] * (SCALE * LOG2E)
    qr = _rope_wide(q, c, s1, s2, ROPE_DIM // 2)
    q_st = _stack_heads_t(q.T, H, HEAD_DIM).astype(BF16)
    qr_st = _stack_heads_t(qr.T, H, HEAD_DIM).astype(BF16)
    tile_h = lambda b: jnp.concatenate([b] * H, axis=1)

    ncp = kc_scr.shape[0]
    cshift = cmpb_ref.shape[0] - ncp
    cb = cmpb_ref[pl.ds(pl.multiple_of(cshift - (QW // CMP_STRIDE) * i, 8), ncp), :]
    s_c = jnp.dot(kc_scr[...], q_st, preferred_element_type=F32) + tile_h(cb)
    e_c, _, den_c = _softmax2_cols(s_c)
    t_row = i * QW + (lax.broadcasted_iota(I32, (1, HW), 1) & (QW - 1))
    r_c = jnp.where(t_row >= CMP_BLOCK - 1, 1.0 / den_c, 0.0)
    o_cmp = jnp.dot(vct_scr[...], e_c.astype(BF16), preferred_element_type=F32) * r_c
    psum = e_c[:, 0:QW] * r_c[:, 0:QW]
    for h in range(1, H):
        psum = psum + e_c[:, QW * h:QW * (h + 1)] * r_c[:, QW * h:QW * (h + 1)]

    hi = psum.astype(BF16)
    r1 = psum - hi.astype(F32)
    mid = r1.astype(BF16)
    lo = (r1 - mid.astype(F32)).astype(BF16)
    ovt = ovt_ref[...]
    imp = (jnp.dot(ovt, hi, preferred_element_type=F32) + jnp.dot(ovt, mid, preferred_element_type=F32)
           + jnp.dot(ovt, lo, preferred_element_type=F32))
    n_slc = imp.shape[0]
    jb = lax.broadcasted_iota(I32, (n_slc, QW), 0)
    tq = i * QW + lax.broadcasted_iota(I32, (n_slc, QW), 1)
    cur = tq >> (SLC_BLOCK.bit_length() - 1)
    forced = (jb == 0) | (jb == cur) | (jb == cur - 1)
    val = jnp.where(forced, FORCE, jnp.where(jb <= cur, imp, NEG))
    rank = jnp.zeros((n_slc, QW), I32)
    for jp in range(n_slc):
        row = val[jp:jp + 1, :]
        tie = jnp.where(jb > jp, 1, 0)
        rank = rank + jnp.where(row > val, 1, jnp.where(row == val, tie, 0))
    blk_scr[...] = jnp.where(rank < n_top, jnp.where(val > 0.5 * NEG, 0.0, NEG), NEG)

    kmax = caub_ref.shape[0] // 2

    def slc_chunk(size, base):
        def body(ci, carry):
            m, l, acc = carry
            k0 = pl.multiple_of(base + ci * size, KC)
            ahead = jnp.minimum(i * QW - k0, kmax)
            b0 = k0 // SLC_BLOCK
            rows = [jnp.broadcast_to(blk_scr[pl.ds(b0 + r, 1), :], (SLC_BLOCK, QW)) for r in range(size // SLC_BLOCK)]
            bias = jnp.concatenate(rows, axis=0) + caub_ref[pl.ds(pl.multiple_of(kmax - ahead, QB), size), :]
            s = jnp.dot(ks_scr[pl.ds(k0, size), :], qr_st, preferred_element_type=F32) + tile_h(bias)
            m_new = jnp.maximum(m, jnp.max(s, axis=0, keepdims=True))
            alpha = jnp.exp2(m - m_new)
            e = jnp.exp2(s - m_new)
            l = alpha * l + jnp.sum(e, axis=0, keepdims=True)
            acc = alpha * acc + jnp.dot(vst_scr[:, pl.ds(k0, size)], e.astype(BF16), preferred_element_type=F32)
            return m_new, l, acc
        return body

    n_chunks = (i * QW + QW + KC - 1) // KC
    n_long = n_chunks // ATT_LONG
    init = (jnp.full((1, HW), NEG, F32), jnp.zeros((1, HW), F32), jnp.zeros((HEAD_DIM, HW), F32))
    state = lax.fori_loop(0, n_long, slc_chunk(ATT_LONG * KC, 0), init)
    m_s, l_s, acc_s = lax.fori_loop(0, n_chunks - n_long * ATT_LONG, slc_chunk(KC, n_long * (ATT_LONG * KC)), state)
    o_slc = jnp.where(m_s > 0.5 * NEG, acc_s / l_s, 0.0)

    span = min(WIN + QW, seq)
    w0 = pl.multiple_of(jnp.maximum(i * QW + QW - span, 0), QB)
    wb = winb_ref[pl.ds(pl.multiple_of(span - QW - (i * QW - w0), QB), span), :]
    s_w = jnp.dot(kw_scr[pl.ds(w0, span), :], qr_st, preferred_element_type=F32) + tile_h(wb)
    e_w, _, den_w = _softmax2_cols(s_w)
    o_win = jnp.dot(vwt_scr[:, pl.ds(w0, span)], e_w.astype(BF16), preferred_element_type=F32) * (1.0 / den_w)

    gt = jax.nn.sigmoid(g_ref[...].T)
    outs = []
    for h in range(H):
        sl = slice(QW * h, QW * (h + 1))
        outs.append(gt[3 * h:3 * h + 1, :] * o_cmp[:, sl] + gt[3 * h + 1:3 * h + 2, :] * o_slc[:, sl]
                    + gt[3 * h + 2:3 * h + 3, :] * o_win[:, sl])
    outs.append(jnp.zeros((o_ref.shape[1] - H * HEAD_DIM, QW), F32))
    o_ref[...] = jnp.concatenate(outs, axis=0).T.astype(BF16)


def _nsa(proj, kvcmp, tabs, ovt, batch, seq):
    qw = NSA_QW
    nb = seq // qw
    n_slc = seq // SLC_BLOCK
    ncp = kvcmp.shape[1]
    col = lambda name, w: _OFF[name] // w
    qmap = lambda cidx: (lambda b, i: (b * nb + i, cidx))
    tmap = lambda k: (lambda b, i: (k, i, 0))
    kern = functools.partial(_nsa_kernel, n_top=min(SLC_TOPN, n_slc))
    cmpb, winb, caub = _cmp_bias_table(seq, qw), _win_bias_table(seq, qw), _causal_bias_table(qw)
    whole = lambda a: pl.BlockSpec(a.shape, lambda b, i: (0, 0))
    return pl.pallas_call(
        kern,
        grid=(batch, nb),
        in_specs=[pl.BlockSpec((qw, 384), qmap(col("a_q", 384))),
                  pl.BlockSpec((qw, LANES), qmap(col("a_g", LANES))),
                  pl.BlockSpec((qw, LANES), qmap(col("a_ksvs", LANES))),
                  pl.BlockSpec((qw, LANES), qmap(col("a_kwvw", LANES))),
                  pl.BlockSpec((1, ncp, LANES), lambda b, i: (b, 0, 0)),
                  pl.BlockSpec((None, qw, LANES), tmap(0)),
                  pl.BlockSpec((None, qw, LANES), tmap(1)),
                  pl.BlockSpec((None, qw, LANES), tmap(2)),
                  whole(ovt), whole(cmpb), whole(winb), whole(caub)],
        out_specs=pl.BlockSpec((qw, 384), lambda b, i: (b * nb + i, 0)),
        out_shape=jax.ShapeDtypeStruct((batch * seq, 384), BF16),
        scratch_shapes=[pltpu.VMEM((seq, LANES), BF16), pltpu.VMEM((HEAD_DIM, seq), BF16),
                        pltpu.VMEM((seq, LANES), BF16), pltpu.VMEM((HEAD_DIM, seq), BF16),
                        pltpu.VMEM((ncp, LANES), BF16), pltpu.VMEM((HEAD_DIM, ncp), BF16),
                        pltpu.VMEM((n_slc, qw), F32)],
        compiler_params=_params(2),
        name="nsa",
    )(proj, proj, proj, proj, kvcmp, tabs, tabs, tabs, ovt, cmpb, winb, caub)


SEARCH_BISECT_FROM = 24
SEARCH_MAX_PASSES = 64
SEARCH_KEPT_LIMIT = 6
SEARCH_BLIND_PASSES = 12
INT_MAX = 2 ** 31 - 1
FLT_MIN_NORMAL = 1.17549435e-38
ZERO_BAND_END = 1 << 23


def _tree_rows(x, op):
    parts = [x[r:r + 8, :] for r in range(0, x.shape[0], 8)]
    while len(parts) > 1:
        nxt = [op(parts[a], parts[a + 1]) for a in range(0, len(parts) - 1, 2)]
        if len(parts) % 2:
            nxt.append(parts[-1])
        parts = nxt
    return parts[0]


def _f2key(v):
    bits = lax.bitcast_convert_type(v, I32)
    return bits ^ ((bits >> 31) & 0x7FFFFFFF)


def _key2f(k):
    return lax.bitcast_convert_type(k ^ ((k >> 31) & 0x7FFFFFFF), F32)


def _dsa_kernel(q_ref, iq_ref, ikw_ref, kv_ref, c_ref, s1_ref, s2_ref, ci_ref, si1_ref, si2_ref, o_ref,
                k_scr, vt_scr, ik_scr, key_scr, *, top, idx_bits):
    i = pl.program_id(1)
    H = DSA_HEADS
    QW = q_ref.shape[0]
    HW = H * QW
    seq = key_scr.shape[0]
    lane_q = lax.broadcasted_iota(I32, (QW, LANES), 1)

    @pl.when(i == 0)
    def _init():
        k_scr[...] = jnp.zeros_like(k_scr)
        vt_scr[...] = jnp.zeros_like(vt_scr)
        ik_scr[...] = jnp.zeros_like(ik_scr)

    c, s1, s2 = c_ref[...], s1_ref[...], s2_ref[...]
    ci, si1, si2 = ci_ref[...], si1_ref[...], si2_ref[...]
    r0 = pl.multiple_of(i * QW, QW)
    kv = kv_ref[...]
    k_scr[pl.ds(r0, QW), :] = jnp.where(lane_q < HEAD_DIM, _rope(kv, c, s1, s2, ROPE_DIM // 2), 0.0).astype(BF16)
    vt_scr[:, pl.ds(r0, QW)] = kv.T[HEAD_DIM:2 * HEAD_DIM, :].astype(BF16)
    ikw = ikw_ref[...]
    ik_scr[pl.ds(r0, QW), :] = jnp.where(lane_q < IDX_DIM, _rope(ikw, ci, si1, si2, IDX_ROPE_DIM // 2), 0.0).astype(BF16)

    qr = _rope_wide(q_ref[...] * (SCALE * LOG2E), c, s1, s2, ROPE_DIM // 2)
    qr_st = _stack_heads_t(qr.T, H, HEAD_DIM).astype(BF16)
    iqr = _rope_wide(iq_ref[...], ci, si1, si2, IDX_ROPE_DIM // 2)
    iq_st = _stack_heads_t(iqr.T, IDX_HEADS, IDX_DIM).astype(BF16)
    wt = ikw.T * ((IDX_DIM ** -0.5) * (IDX_HEADS ** -0.5))
    w_st = jnp.concatenate([wt[IDX_DIM + h:IDX_DIM + h + 1, :] for h in range(IDX_HEADS)], axis=1)

    n_chunks = (i * QW + QW + KC - 1) // KC
    k_io = lax.broadcasted_iota(I32, (KC, QW), 0)
    tq = i * QW + lax.broadcasted_iota(I32, (KC, QW), 1)

    sub_io = lax.broadcasted_iota(I32, (QB, QW), 0)
    sub_tq = i * QW + lax.broadcasted_iota(I32, (QB, QW), 1)
    zkey0 = seq - sub_io

    def score_body(cidx, carry):
        mx, mn = carry
        k0 = pl.multiple_of(cidx * KC, KC)
        for sb in range(KC // QB):
            kb = pl.multiple_of(k0 + sb * QB, QB)
            d = jnp.dot(ik_scr[pl.ds(kb, QB), :], iq_st, preferred_element_type=F32)
            r = jnp.maximum(d, 0.0) * w_st
            sc = r[:, 0:QW]
            for h in range(1, IDX_HEADS):
                sc = sc + r[:, QW * h:QW * (h + 1)]
            key = jnp.where(jnp.abs(sc) < FLT_MIN_NORMAL, zkey0 - kb, _f2key(sc))
            key_scr[pl.ds(kb, QB), :] = jnp.where(sub_io + kb <= sub_tq, key, INT_MIN)
            mx = jnp.maximum(mx, _tree_rows(key, jnp.maximum))
            mn = jnp.minimum(mn, _tree_rows(key, jnp.minimum))
        return mx, mn

    mx8, mn8 = lax.fori_loop(0, n_chunks, score_body,
                             (jnp.full((8, QW), INT_MIN, I32), jnp.full((8, QW), INT_MAX, I32)))

    def count(pred):
        def body(cidx, acc):
            k0 = pl.multiple_of(cidx * KC, KC)
            return acc + _tree_rows(pred(key_scr[pl.ds(k0, KC), :], k_io + k0), jnp.add)
        acc8 = lax.fori_loop(0, n_chunks, body, jnp.zeros((8, QW), F32))
        return jnp.sum(acc8, axis=0, keepdims=True)

    topf = float(top)
    lo0 = functools.reduce(jnp.minimum, [mn8[r:r + 1, :] for r in range(8)])
    hi0 = functools.reduce(jnp.maximum, [mx8[r:r + 1, :] for r in range(8)]) + 1
    n_causal = (i * QW + 1 + lax.broadcasted_iota(I32, (1, QW), 1)).astype(F32)
    log_top = float(np.log(top))

    def in_zero_band(lo, hi):
        return jnp.where(lo >= 1, jnp.where(hi <= ZERO_BAND_END, 1.0, 0.0), 0.0) > 0.5

    def pending(lo, hi, c_lo):
        return (jnp.max(jnp.where(c_lo > topf, jnp.where(hi > lo + 1, 1.0, 0.0), 0.0)) > 0.5).astype(I32)

    def search_cond(c):
        return jnp.logical_and(c[0] < SEARCH_MAX_PASSES, c[1] > 0)

    def search_pass(it, st):
        lo, hi, c_lo, c_hi, kept, w_lo, w_hi = st
        f_lo = (jnp.log(c_lo) - log_top) * w_lo
        f_hi = (log_top - jnp.log(jnp.maximum(c_hi, 0.5))) * w_hi
        frac = f_lo / (f_lo + f_hi)
        v_lo, v_hi = _key2f(lo), _key2f(hi)
        cand = _f2key(v_lo + frac * (v_hi - v_lo))
        lo_f, hi_f = lo.astype(F32), hi.astype(F32)
        cand = jnp.where(in_zero_band(lo, hi), (lo_f + frac * (hi_f - lo_f)).astype(I32), cand)
        key_mid = (lo >> 1) + (hi >> 1) + (lo & hi & 1)
        cand = jnp.where(jnp.abs(kept) >= SEARCH_KEPT_LIMIT, key_mid, cand)
        cand = jnp.where(it >= SEARCH_BISECT_FROM, key_mid, cand)
        cand = jnp.where(it == 0, 1, jnp.where(it == 1, ZERO_BAND_END, cand))
        cand = jnp.minimum(jnp.maximum(cand, lo + 1), hi - 1)
        cnt = count(lambda key, kpos: jnp.where(key >= cand, 1.0, 0.0))
        active = jnp.where(c_lo > topf, jnp.where(hi > lo + 1, 1.0, 0.0), 0.0) > 0.5
        up = jnp.where(active, jnp.where(cnt >= topf, 1.0, 0.0), 0.0) > 0.5
        dn = jnp.where(active, jnp.where(cnt >= topf, 0.0, 1.0), 0.0) > 0.5
        w_hi = jnp.where(up, jnp.where(kept < 0, 0.5 * w_hi, 1.0), jnp.where(dn, 1.0, w_hi))
        w_lo = jnp.where(dn, jnp.where(kept > 0, 0.5 * w_lo, 1.0), jnp.where(up, 1.0, w_lo))
        kept = jnp.where(up, jnp.where(kept < 0, kept - 1, -1), jnp.where(dn, jnp.where(kept > 0, kept + 1, 1), kept))
        lo, c_lo = jnp.where(up, cand, lo), jnp.where(up, cnt, c_lo)
        hi, c_hi = jnp.where(dn, cand, hi), jnp.where(dn, cnt, c_hi)
        hi = jnp.where(in_zero_band(lo, hi), jnp.minimum(hi, seq + 1), hi)
        return lo, hi, c_lo, c_hi, kept, w_lo, w_hi

    def search_body(c):
        st = search_pass(c[0], c[2])
        return c[0] + 1, pending(st[0], st[1], st[2]), st

    zero_f, one_f = jnp.zeros((1, QW), F32), jnp.ones((1, QW), F32)
    st0 = (lo0, hi0, n_causal, zero_f, jnp.zeros((1, QW), I32), one_f, one_f)
    n_blind = jnp.where(i * QW + QW > top, SEARCH_BLIND_PASSES, 0)
    st1 = lax.fori_loop(0, n_blind, search_pass, st0)
    _, _, (thr, _, c_lo, c_hi, _, _, _) = lax.while_loop(
        search_cond, search_body, (n_blind, pending(st1[0], st1[1], st1[2]), st1))

    tied = c_lo > topf
    need1 = (topf - 1.0) - c_hi
    all_pos = jnp.full((1, QW), 2 ** idx_bits - 1, I32)

    def tie_search():
        def tie_body(bi, cur):
            cand = cur | lax.shift_left(jnp.int32(1), idx_bits - 1 - bi)
            cnt = count(lambda key, kpos: jnp.where(key == thr, jnp.where(kpos < cand, 1.0, 0.0), 0.0))
            return jnp.where(cnt <= need1, cand, cur)
        return lax.fori_loop(0, idx_bits, tie_body, jnp.zeros((1, QW), I32))

    any_tied = jnp.max(jnp.where(tied, 1.0, 0.0)) > 0.5
    last = jnp.where(tied, lax.cond(any_tied, tie_search, lambda: all_pos), all_pos)

    half_neg = int(np.float32(0.5 * NEG).view(np.int32))
    key_floor = half_neg ^ 0x7FFFFFFF

    def att_chunk(size, base):
        pos_io = lax.broadcasted_iota(I32, (size, QW), 0)

        def body(cidx, state):
            m, l, acc = state
            k0 = pl.multiple_of(base + cidx * size, KC)
            key = key_scr[pl.ds(k0, size), :]
            kpos = pos_io + k0
            bias = jnp.where(key > thr, 0.0, jnp.where(key == thr, jnp.where(kpos <= last, 0.0, NEG), NEG))
            bias = jnp.where(key > key_floor, bias, NEG)
            s = (jnp.dot(k_scr[pl.ds(k0, size), :], qr_st, preferred_element_type=F32)
                 + jnp.concatenate([bias] * H, axis=1))
            m_new = jnp.maximum(m, jnp.max(s, axis=0, keepdims=True))
            alpha = jnp.exp2(m - m_new)
            e = jnp.exp2(s - m_new)
            l = alpha * l + jnp.sum(e, axis=0, keepdims=True)
            acc = alpha * acc + jnp.dot(vt_scr[:, pl.ds(k0, size)], e.astype(BF16), preferred_element_type=F32)
            return m_new, l, acc
        return body

    init = (jnp.full((1, HW), NEG, F32), jnp.zeros((1, HW), F32), jnp.zeros((HEAD_DIM, HW), F32))
    n_long = n_chunks // ATT_LONG
    state = lax.fori_loop(0, n_long, att_chunk(ATT_LONG * KC, 0), init)
    m_a, l_a, acc_a = lax.fori_loop(0, n_chunks - n_long * ATT_LONG, att_chunk(KC, n_long * (ATT_LONG * KC)), state)
    o = jnp.where(m_a > 0.5 * NEG, acc_a / l_a, 0.0)
    outs = [o[:, QW * h:QW * (h + 1)] for h in range(H)]
    outs.append(jnp.zeros((o_ref.shape[1] - H * HEAD_DIM, QW), F32))
    o_ref[...] = jnp.concatenate(outs, axis=0).T.astype(BF16)


def _dsa(proj, tabs, tabs_i, batch, seq):
    qw = DSA_QW
    nb = seq // qw
    col = lambda name, w: _OFF[name] // w
    qmap = lambda cidx: (lambda b, i: (b * nb + i, cidx))
    tmap = lambda k: (lambda b, i: (k, i, 0))
    top = min(DSA_TOPK_MAX, seq // 4)
    kern = functools.partial(_dsa_kernel, top=top, idx_bits=int(seq).bit_length())
    return pl.pallas_call(
        kern,
        grid=(batch, nb),
        in_specs=[pl.BlockSpec((qw, 384), qmap(col("b_q", 384))),
                  pl.BlockSpec((qw, 256), qmap(col("b_iq", 256))),
                  pl.BlockSpec((qw, LANES), qmap(col("b_ikiw", LANES))),
                  pl.BlockSpec((qw, LANES), qmap(col("b_kv", LANES))),
                  pl.BlockSpec((None, qw, LANES), tmap(0)),
                  pl.BlockSpec((None, qw, LANES), tmap(1)),
                  pl.BlockSpec((None, qw, LANES), tmap(2)),
                  pl.BlockSpec((None, qw, LANES), tmap(0)),
                  pl.BlockSpec((None, qw, LANES), tmap(1)),
                  pl.BlockSpec((None, qw, LANES), tmap(2))],
        out_specs=pl.BlockSpec((qw, 384), lambda b, i: (b * nb + i, 0)),
        out_shape=jax.ShapeDtypeStruct((batch * seq, 384), BF16),
        scratch_shapes=[pltpu.VMEM((seq, LANES), BF16), pltpu.VMEM((HEAD_DIM, seq), BF16),
                        pltpu.VMEM((seq, LANES), BF16), pltpu.VMEM((seq, qw), I32)],
        compiler_params=_params(2),
        name="dsa",
    )(proj, proj, proj, proj, tabs, tabs, tabs, tabs_i, tabs_i, tabs_i)


DIL_MB = max(d for _, d in DIL_PAIRS) * QB
DIL_VMEM_LIMIT = 56 * 1024 * 1024
DIL_UNROLL = 4


def _dil_layout(g):
    dil = DIL_PAIRS[g][1]
    per = DIL_MB // dil
    return dil, per, per + QB


def _dil_bias_table():
    u = np.arange(2 * QB)[:, None]
    diff = QB + np.arange(LANES)[None, :] - u
    ok = (diff >= 0) & (diff <= QB)
    return _bias_table(np.concatenate([ok, ok & (u >= QB)], axis=0))


def _dil_kernel(*refs):
    G, HG = len(DIL_PAIRS), DIL_HEADS_PER_GROUP
    q_refs, k_refs, v_refs = refs[0:G], refs[G:2 * G], refs[2 * G:3 * G]
    c_ref, s1_ref, s2_ref, bias_ref, o_ref = refs[3 * G:3 * G + 5]
    kds, vds, ogs, lss = (refs[3 * G + 5 + n * G:3 * G + 5 + (n + 1) * G] for n in range(4))
    step = pl.program_id(1)
    half = ROPE_DIM // 2

    @pl.when(step == 0)
    def _zero():
        for g in range(G):
            kds[g][...] = jnp.zeros_like(kds[g])
            vds[g][...] = jnp.zeros_like(vds[g])

    @pl.when(step > 0)
    def _carry_halo():
        for g in range(G):
            dil, per, stride = _dil_layout(g)
            for r in range(dil):
                b0 = r * stride
                kds[g][b0:b0 + QB, :] = kds[g][b0 + per:b0 + per + QB, :]
                vds[g][:, b0:b0 + QB] = vds[g][:, b0 + per:b0 + per + QB]

    for g in range(G):
        dil, per, stride = _dil_layout(g)
        piece = min(per, KC)
        for r in range(dil):
            for p0 in range(0, per, piece):
                rows = pl.ds(r + dil * p0, piece, stride=dil)
                kr = _rope(k_refs[g][rows, :], c_ref[rows, :], s1_ref[rows, :], s2_ref[rows, :], half)
                d0 = r * stride + QB + p0
                kds[g][d0:d0 + piece, :] = kr.astype(BF16)
                vds[g][:, d0:d0 + piece] = v_refs[g][rows, :].T.astype(BF16)

    zero = jnp.zeros((HEAD_DIM, LANES), F32)
    for g in range(G):
        dil, per, stride = _dil_layout(g)
        nblk = per // QB

        def block(jb, _, r=0, g=g, dil=dil, stride=stride):
            p0 = pl.multiple_of(dil * QB * jb, dil * QB)
            win = pl.ds(p0, dil * QB)
            rows = pl.ds(r, QB, stride=dil)
            q = _rope(q_refs[g].at[win, :][rows, :] * (SCALE * LOG2E), c_ref.at[win, :][rows, :],
                      s1_ref.at[win, :][rows, :], s2_ref.at[win, :][rows, :], half)
            qt = q.T
            q2 = jnp.concatenate([jnp.concatenate([qt[:HEAD_DIM], zero], axis=0),
                                  jnp.concatenate([zero, qt[HEAD_DIM:]], axis=0)], axis=1).astype(BF16)
            kb = pl.multiple_of(r * stride + QB * jb, QB)
            first = jnp.logical_and(step == 0, jb == 0)
            bias = bias_ref[pl.ds(pl.multiple_of(jnp.where(first, 2 * QB, 0), QB), 2 * QB), :]
            s = (jnp.dot(kds[g][pl.ds(kb, 2 * QB), :], q2, preferred_element_type=F32)
                 + jnp.concatenate([bias] * HG, axis=1))
            e, m, den = _softmax2_cols(s)
            lse = m + jnp.log2(den)
            rden = 1.0 / den
            parts, lparts = [], []
            for hg in range(HG):
                sl = slice(LANES * hg, LANES * (hg + 1))
                vt = vds[g][HEAD_DIM * hg:HEAD_DIM * (hg + 1), pl.ds(kb, 2 * QB)]
                parts.append(jnp.dot(vt, e[:, sl].astype(BF16), preferred_element_type=F32) * rden[:, sl])
                lparts.append(jnp.broadcast_to(lse[:, sl], (HEAD_DIM, LANES)))
            tile = jnp.concatenate(parts + lparts, axis=0).T
            ogs[g].at[win, :][rows, :] = tile[:, :LANES]
            lss[g].at[win, :][rows, :] = tile[:, LANES:]
            return 0

        for r in range(dil):
            lax.fori_loop(0, nblk, functools.partial(block, r=r), 0, unroll=min(nblk, DIL_UNROLL))

    def mix(ti, _):
        rows = pl.ds(pl.multiple_of(ti * KC, KC), KC)
        ls = [lss[g][rows, :] for g in range(G)]
        mx = functools.reduce(jnp.maximum, ls)
        ex = [jnp.exp2(x - mx) for x in ls]
        rtot = 1.0 / functools.reduce(lambda a, b: a + b, ex)
        o_ref[rows, :] = jnp.concatenate([ex[g] * rtot * ogs[g][rows, :] for g in range(G)], axis=1).astype(BF16)
        return 0

    lax.fori_loop(0, DIL_MB // KC, mix, 0)


def _dil(proj, tabs, batch, seq):
    assert seq % DIL_MB == 0 and all(w // d == QB for w, d in DIL_PAIRS)
    nm = seq // DIL_MB
    G = len(DIL_PAIRS)
    w = DIL_HEADS * HEAD_DIM
    gmap = lambda name, g: (lambda b, j: (b * nm + j, _OFF[name] // LANES + g))
    tmap = lambda k: (lambda b, j: (k, j, 0))
    bias = _dil_bias_table()
    rows = [_dil_layout(g)[0] * _dil_layout(g)[2] for g in range(G)]
    return pl.pallas_call(
        _dil_kernel,
        grid=(batch, nm),
        in_specs=([pl.BlockSpec((DIL_MB, LANES), gmap(name, g)) for name in ("c_q", "c_k", "c_v") for g in range(G)]
                  + [pl.BlockSpec((None, DIL_MB, LANES), tmap(k)) for k in range(3)]
                  + [pl.BlockSpec(bias.shape, lambda b, j: (0, 0))]),
        out_specs=pl.BlockSpec((DIL_MB, w), lambda b, j: (b * nm + j, 0)),
        out_shape=jax.ShapeDtypeStruct((batch * seq, w), BF16),
        scratch_shapes=([pltpu.VMEM((n, LANES), BF16) for n in rows] + [pltpu.VMEM((LANES, n), BF16) for n in rows]
                        + [pltpu.VMEM((DIL_MB, LANES), F32)] * (2 * G)),
        compiler_params=_params(2, DIL_VMEM_LIMIT),
        name="dilated",
    )(*([proj] * (3 * G)), tabs, tabs, tabs, bias)


def _overlap_t(seq):
    n_cmp_pad = seq // CMP_STRIDE
    n_slc = seq // SLC_BLOCK
    c_start = np.arange(n_cmp_pad) * CMP_STRIDE
    s_start = np.arange(n_slc) * SLC_BLOCK
    ov = ((c_start[None, :] < s_start[:, None] + SLC_BLOCK) & (c_start[None, :] + CMP_BLOCK > s_start[:, None]))
    return jnp.asarray(ov.astype(np.float32), dtype=BF16)


def _layer(x2, batch, seq, layer, norm1_g, w_in, cmp_pos, cmp_w1, cmp_w2, w_out, norm2_g, w_up_bf, conv_w, conv_b,
           w_down_bf, final_g, final_norm, tabs, tabs_i, ovt):
    n = batch * seq
    proj = _inproj(x2, norm1_g, _regroup_w_in(w_in, layer))

    w1 = cmp_w1.reshape(2, CMP_BLOCK, HEAD_DIM, CMP_HIDDEN)
    z1 = jnp.zeros_like(w1[0])
    w1_cat = jnp.concatenate([jnp.concatenate([w1[0], z1], axis=2),
                              jnp.concatenate([z1, w1[1]], axis=2)], axis=1).astype(BF16)
    z2 = jnp.zeros((CMP_HIDDEN, HEAD_DIM), cmp_w2.dtype)
    w2_cat = jnp.concatenate([jnp.concatenate([cmp_w2[0], z2], axis=1),
                              jnp.concatenate([z2, cmp_w2[1]], axis=1)], axis=0).astype(BF16)
    pos_cat = jnp.concatenate([cmp_pos[0], cmp_pos[1]], axis=1)
    kvcmp = _nsa_compress(proj, pos_cat, w1_cat, w2_cat, batch, seq)

    o_a = _nsa(proj, kvcmp, tabs, ovt, batch, seq)
    o_b = _dsa(proj, tabs, tabs_i, batch, seq)
    o_c = _dil(proj, tabs, batch, seq)

    hp = NSA_HEADS * HEAD_DIM
    zrow = jnp.zeros((384 - hp, D_MODEL), w_out.dtype)
    w_out_pad = jnp.concatenate([w_out[0:hp], zrow, w_out[hp:2 * hp], zrow, w_out[2 * hp:]], axis=0).astype(BF16)
    x2 = _outproj(o_a, o_b, o_c, w_out_pad, x2)

    act = _ffn_up(x2, norm2_g, w_up_bf, layer, conv_w, conv_b, seq)
    return _ffn_down(act, w_down_bf, layer, x2, final_g, final_norm)


def kernel(x, norm1_g, w_in, cmp_pos, cmp_w1, cmp_w2, w_out, norm2_g, w_up, conv_w, conv_b, w_down, final_g):
    batch, seq, d = x.shape
    depth = w_in.shape[0]
    tabs = _rope_lane_tables(seq, HEAD_DIM, ROPE_DIM)
    tabs_i = _rope_lane_tables(seq, IDX_DIM, IDX_ROPE_DIM)
    ovt = _overlap_t(seq)
    x2 = x.reshape(batch * seq, d)
    w_up_bf, w_down_bf = w_up.astype(BF16), w_down.astype(BF16)
    for li in range(depth):
        x2 = _layer(x2, batch, seq, li, norm1_g[li], w_in, cmp_pos[li], cmp_w1[li], cmp_w2[li], w_out[li],
                    norm2_g[li], w_up_bf, conv_w[li], conv_b[li], w_down_bf, final_g, li == depth - 1,
                    tabs, tabs_i, ovt)
    return x2.reshape(batch, seq, d)
```

```python
import functools

import numpy as np
import jax
import jax.numpy as jnp
from jax import lax
from jax.experimental import pallas as pl
from jax.experimental.pallas import tpu as pltpu

F32 = jnp.float32
BF16 = jnp.bfloat16
I32 = jnp.int32

D_MODEL = 1024
HEAD_DIM = 64
ROPE_DIM = HEAD_DIM // 4
ROPE_THETA = 500000.0
NORM_EPS = 1e-6
SCALE = HEAD_DIM ** -0.5
LOG2E = 1.4426950408889634
NEG = -1e30
FORCE = 1e9
NSA_HEADS = 5
CMP_BLOCK = 32
CMP_STRIDE = 16
CMP_HIDDEN = 128
SLC_BLOCK = 64
SLC_TOPN = 16
WIN = 512
DSA_HEADS = 5
IDX_HEADS = 8
IDX_DIM = 32
IDX_ROPE_DIM = IDX_DIM // 4
DSA_TOPK_MAX = 256
DIL_PAIRS = ((128, 1), (512, 4), (2048, 16))
DIL_HEADS_PER_GROUP = 2
DIL_HEADS = len(DIL_PAIRS) * DIL_HEADS_PER_GROUP
D_FF = 2816
CONV_WIDTH = 3

LANES = 128
QB = 128
DSA_QW = 256
NSA_QW = 256
KC = 512
ATT_LONG = 2
ROW_TILE = 512
HALO = 16
VMEM_LIMIT = 48 * 1024 * 1024
INT_MIN = -2 ** 31

_A0 = 0
_B0 = 719
_C0 = 1463
_SEGS = (
    ("c_q", _C0, 384, 384), ("c_k", _C0 + 384, 384, 384), ("c_v", _C0 + 768, 384, 384),
    ("a_q", _A0, 320, 384), ("b_q", _B0, 320, 384),
    ("a_kcvc", _A0 + 320, 128, 128),
    ("b_iq", _B0 + 448, 256, 256),
    ("a_ksvs", _A0 + 448, 128, 128), ("a_kwvw", _A0 + 576, 128, 128),
    ("a_g", _A0 + 704, 15, 128),
    ("b_kv", _B0 + 320, 128, 128),
    ("b_ikiw", _B0 + 704, 40, 128),
)
P_COLS = sum(s[3] for s in _SEGS)


def _seg_offsets():
    offs, o = {}, 0
    for name, _, _, pw in _SEGS:
        offs[name] = o
        o += pw
    return offs


_OFF = _seg_offsets()


def _regroup_kernel(w_ref, o_ref):
    for name, src, wdt, pw in _SEGS:
        dst = _OFF[name]
        o_ref[:, dst:dst + wdt] = w_ref[:, src:src + wdt].astype(BF16)
        if pw > wdt:
            o_ref[:, dst + wdt:dst + pw] = jnp.zeros((o_ref.shape[0], pw - wdt), BF16)


def _regroup_w_in(w, layer):
    _, d, n_in = w.shape
    rt = 256
    return pl.pallas_call(
        _regroup_kernel,
        grid=(d // rt,),
        in_specs=[pl.BlockSpec((None, rt, n_in), lambda i: (layer, i, 0))],
        out_specs=pl.BlockSpec((rt, P_COLS), lambda i: (i, 0)),
        out_shape=jax.ShapeDtypeStruct((d, P_COLS), BF16),
        compiler_params=_params(1),
        name="regroup_w_in",
    )(w)


def _rope_lane_tables(L, head_dim, rot_dim):
    half = rot_dim // 2
    inv = 1.0 / (ROPE_THETA ** (np.arange(0, rot_dim, 2, dtype=np.float32) / np.float32(rot_dim)))
    ang = np.arange(L, dtype=np.float32)[:, None] * inv[None, :]
    cos, sin = np.cos(ang).astype(np.float32), np.sin(ang).astype(np.float32)
    d = np.arange(LANES) % head_dim
    lo, hi = d < half, (d >= half) & (d < rot_dim)
    c = np.ones((L, LANES), np.float32)
    s1 = np.zeros((L, LANES), np.float32)
    s2 = np.zeros((L, LANES), np.float32)
    c[:, lo] = cos[:, d[lo]]
    c[:, hi] = cos[:, d[hi] - half]
    s1[:, lo] = -sin[:, d[lo]]
    s2[:, hi] = sin[:, d[hi] - half]
    return jnp.asarray(np.stack([c, s1, s2]))


def _rope(x, c, s1, s2, half):
    xp = pltpu.roll(x, LANES - half, 1)
    xm = pltpu.roll(x, half, 1)
    return x * c + xp * s1 + xm * s2


def _rope_wide(x, c, s1, s2, half):
    n = x.shape[1] // LANES
    return jnp.concatenate([_rope(x[:, LANES * s:LANES * (s + 1)], c, s1, s2, half) for s in range(n)], axis=1)


def _softmax2_cols(s):
    m = jnp.max(s, axis=0, keepdims=True)
    e = jnp.exp2(s - m)
    return e, m, jnp.sum(e, axis=0, keepdims=True)


def _bias_table(ok):
    return jnp.asarray(np.where(ok, 0.0, NEG).astype(np.float32))


def _cmp_bias_table(seq, qw):
    shift = (qw // CMP_STRIDE) * (seq // qw - 1)
    u = np.arange(seq // CMP_STRIDE + shift)[:, None]
    lane = np.arange(qw)[None, :]
    return _bias_table(CMP_STRIDE * (u - shift) + CMP_BLOCK - 1 <= lane)


def _win_bias_table(seq, qw):
    span = min(WIN + qw, seq)
    cmax = span - qw
    u = np.arange(span + cmax)[:, None]
    diff = cmax + np.arange(qw)[None, :] - u
    return _bias_table((diff >= 0) & (diff < WIN))


def _causal_bias_table(qw):
    kmax = ATT_LONG * KC
    u = np.arange(2 * kmax)[:, None]
    return _bias_table(u <= kmax + np.arange(qw)[None, :])


def _params(n_grid, vmem=VMEM_LIMIT):
    return pltpu.CompilerParams(dimension_semantics=("arbitrary",) * n_grid, vmem_limit_bytes=vmem)


def _rmsnorm_rows(x, g):
    return x * lax.rsqrt(jnp.mean(x * x, axis=-1, keepdims=True) + NORM_EPS) * g


def _inproj_kernel(x_ref, g_ref, w_ref, o_ref):
    hn = _rmsnorm_rows(x_ref[...], g_ref[...]).astype(BF16)
    n = w_ref.shape[1]
    for c0 in range(0, n, 512):
        c1 = min(c0 + 512, n)
        o_ref[:, c0:c1] = jnp.dot(hn, w_ref[:, c0:c1], preferred_element_type=F32)


def _inproj(x2, g, w_bf):
    n, d = x2.shape
    pc = w_bf.shape[1]
    return pl.pallas_call(
        _inproj_kernel,
        grid=(n // ROW_TILE,),
        in_specs=[pl.BlockSpec((ROW_TILE, d), lambda i: (i, 0)),
                  pl.BlockSpec((1, d), lambda i: (0, 0)),
                  pl.BlockSpec((d, pc), lambda i: (0, 0))],
        out_specs=pl.BlockSpec((ROW_TILE, pc), lambda i: (i, 0)),
        out_shape=jax.ShapeDtypeStruct((n, pc), F32),
        compiler_params=_params(1),
        name="inproj",
    )(x2, g.reshape(1, d), w_bf)


def _outproj_kernel(oa_ref, ob_ref, oc_ref, w_ref, x_ref, o_ref):
    mix = jnp.concatenate([oa_ref[...], ob_ref[...], oc_ref[...]], axis=1)
    o_ref[...] = x_ref[...] + jnp.dot(mix, w_ref[...], preferred_element_type=F32)


def _outproj(oa, ob, oc, w_bf, x2):
    n, d = x2.shape
    k = w_bf.shape[0]
    mw = oa.shape[1]
    return pl.pallas_call(
        _outproj_kernel,
        grid=(n // ROW_TILE,),
        in_specs=[pl.BlockSpec((ROW_TILE, mw), lambda i: (i, 0)),
                  pl.BlockSpec((ROW_TILE, mw), lambda i: (i, 0)),
                  pl.BlockSpec((ROW_TILE, mw), lambda i: (i, 0)),
                  pl.BlockSpec((k, d), lambda i: (0, 0)),
                  pl.BlockSpec((ROW_TILE, d), lambda i: (i, 0))],
        out_specs=pl.BlockSpec((ROW_TILE, d), lambda i: (i, 0)),
        out_shape=jax.ShapeDtypeStruct((n, d), F32),
        compiler_params=_params(1),
        name="outproj",
    )(oa, ob, oc, w_bf, x2)


def _ffn_up_kernel(x_ref, xh_ref, g_ref, wa_ref, wu_ref, cw_ref, cb_ref, o_ref, hn_scr, hh_scr, a_scr, *, tiles_per_seq):
    i, j = pl.program_id(0), pl.program_id(1)
    tm = x_ref.shape[0]

    @pl.when(j == 0)
    def _norm():
        hn_scr[...] = _rmsnorm_rows(x_ref[...], g_ref[...]).astype(BF16)
        keep = jnp.where(i % tiles_per_seq == 0, 0.0, 1.0)
        hh_scr[...] = (_rmsnorm_rows(xh_ref[...], g_ref[...]) * keep).astype(BF16)

    hn = hn_scr[...]
    a = jnp.dot(hn, wa_ref[...], preferred_element_type=F32)
    u = jnp.dot(hn, wu_ref[...], preferred_element_type=F32)
    a_scr[0:HALO, :] = jnp.dot(hh_scr[...], wa_ref[...], preferred_element_type=F32)
    a_scr[HALO:HALO + tm, :] = a
    a1 = a_scr[pl.ds(HALO - 1, tm), :]
    a2 = a_scr[pl.ds(HALO - 2, tm), :]
    cw = cw_ref[...]
    conv = cw[0:1, :] * a2 + cw[1:2, :] * a1 + cw[2:3, :] * a + cb_ref[...]
    o_ref[...] = (conv * jax.nn.sigmoid(conv) * u).astype(BF16)


def _ffn_up(x2, g, w_up_bf, layer, conv_w, conv_b, seq_len):
    n, d = x2.shape
    tn = D_FF // 2
    nj = D_FF // tn
    hb = ROW_TILE // HALO
    kern = functools.partial(_ffn_up_kernel, tiles_per_seq=seq_len // ROW_TILE)
    return pl.pallas_call(
        kern,
        grid=(n // ROW_TILE, nj),
        in_specs=[pl.BlockSpec((ROW_TILE, d), lambda i, j: (i, 0)),
                  pl.BlockSpec((HALO, d), lambda i, j: (jnp.maximum(i * hb - 1, 0), 0)),
                  pl.BlockSpec((1, d), lambda i, j: (0, 0)),
                  pl.BlockSpec((None, d, tn), lambda i, j: (layer, 0, j)),
                  pl.BlockSpec((None, d, tn), lambda i, j: (layer, 0, j + nj)),
                  pl.BlockSpec((CONV_WIDTH, tn), lambda i, j: (0, j)),
                  pl.BlockSpec((1, tn), lambda i, j: (0, j))],
        out_specs=pl.BlockSpec((ROW_TILE, tn), lambda i, j: (i, j)),
        out_shape=jax.ShapeDtypeStruct((n, D_FF), BF16),
        scratch_shapes=[pltpu.VMEM((ROW_TILE, d), BF16), pltpu.VMEM((HALO, d), BF16),
                        pltpu.VMEM((ROW_TILE + HALO, tn), F32)],
        compiler_params=_params(2),
        name="ffn_up",
    )(x2, x2, g.reshape(1, d), w_up_bf, w_up_bf, conv_w, conv_b.reshape(1, D_FF))


def _ffn_down_kernel(a_ref, w_ref, x_ref, g_ref, o_ref, *, final_norm):
    y = x_ref[...] + jnp.dot(a_ref[...], w_ref[...], preferred_element_type=F32)
    if final_norm:
        y = _rmsnorm_rows(y, g_ref[...])
    o_ref[...] = y


def _ffn_down(act, w_bf, layer, x2, final_g, final_norm):
    n, d = x2.shape
    k = act.shape[1]
    return pl.pallas_call(
        functools.partial(_ffn_down_kernel, final_norm=final_norm),
        grid=(n // ROW_TILE,),
        in_specs=[pl.BlockSpec((ROW_TILE, k), lambda i: (i, 0)),
                  pl.BlockSpec((None, k, d), lambda i: (layer, 0, 0)),
                  pl.BlockSpec((ROW_TILE, d), lambda i: (i, 0)),
                  pl.BlockSpec((1, d), lambda i: (0, 0))],
        out_specs=pl.BlockSpec((ROW_TILE, d), lambda i: (i, 0)),
        out_shape=jax.ShapeDtypeStruct((n, d), F32),
        compiler_params=_params(1),
        name="ffn_down",
    )(act, w_bf, x2, final_g.reshape(1, d))


def _cmp_kernel(kv_ref, pos_ref, w1_ref, w2_ref, o_ref):
    ngrp = kv_ref.shape[0] // CMP_STRIDE
    lo = hi = None
    for p in range(CMP_STRIDE):
        x = kv_ref[pl.ds(p, ngrp, stride=CMP_STRIDE), :]
        a = jnp.dot((x + pos_ref[p:p + 1, :]).astype(BF16), w1_ref[p], preferred_element_type=F32)
        b = jnp.dot((x + pos_ref[CMP_STRIDE + p:CMP_STRIDE + p + 1, :]).astype(BF16), w1_ref[CMP_STRIDE + p],
                    preferred_element_type=F32)
        lo = a if lo is None else lo + a
        hi = b if hi is None else hi + b
    hid = jax.nn.gelu(lo + pltpu.roll(hi, ngrp - 1, 0))
    o_ref[0] = jnp.dot(hid.astype(BF16), w2_ref[...], preferred_element_type=F32)


def _nsa_compress(proj, pos_cat, w1_cat, w2_cat, batch, seq):
    ngrp = seq // CMP_STRIDE
    return pl.pallas_call(
        _cmp_kernel,
        grid=(batch,),
        in_specs=[pl.BlockSpec((seq, LANES), lambda i: (i, _OFF["a_kcvc"] // LANES)),
                  pl.BlockSpec(pos_cat.shape, lambda i: (0, 0)),
                  pl.BlockSpec(w1_cat.shape, lambda i: (0, 0, 0)),
                  pl.BlockSpec(w2_cat.shape, lambda i: (0, 0))],
        out_specs=pl.BlockSpec((1, ngrp, LANES), lambda i: (i, 0, 0)),
        out_shape=jax.ShapeDtypeStruct((batch, ngrp, LANES), F32),
        compiler_params=_params(1),
        name="nsa_compress",
    )(proj, pos_cat, w1_cat, w2_cat)


def _stack_heads_t(x_t, n_heads, hd):
    z = jnp.zeros((LANES - hd, x_t.shape[1]), F32)
    return jnp.concatenate(
        [jnp.concatenate([x_t[hd * h:hd * (h + 1), :], z], axis=0) for h in range(n_heads)], axis=1)


def _nsa_kernel(q_ref, g_ref, ksvs_ref, kwvw_ref, kvc_ref, c_ref, s1_ref, s2_ref, ovt_ref, cmpb_ref, winb_ref,
                caub_ref, o_ref, ks_scr, vst_scr, kw_scr, vwt_scr, kc_scr, vct_scr, blk_scr, *, n_top):
    i = pl.program_id(1)
    H = NSA_HEADS
    QW = q_ref.shape[0]
    HW = H * QW
    seq = ks_scr.shape[0]
    lane_q = lax.broadcasted_iota(I32, (QW, LANES), 1)

    @pl.when(i == 0)
    def _init():
        ks_scr[...] = jnp.zeros_like(ks_scr)
        kw_scr[...] = jnp.zeros_like(kw_scr)
        vst_scr[...] = jnp.zeros_like(vst_scr)
        vwt_scr[...] = jnp.zeros_like(vwt_scr)
        kvc = kvc_ref[0]
        lane_c = lax.broadcasted_iota(I32, kvc.shape, 1)
        kc_scr[...] = jnp.where(lane_c < HEAD_DIM, kvc, 0.0).astype(BF16)
        vct_scr[...] = kvc.T[HEAD_DIM:2 * HEAD_DIM, :].astype(BF16)

    c, s1, s2 = c_ref[...], s1_ref[...], s2_ref[...]
    r0 = pl.multiple_of(i * QW, QW)
    for src, kdst, vdst in ((ksvs_ref, ks_scr, vst_scr), (kwvw_ref, kw_scr, vwt_scr)):
        kv = src[...]
        kr = _rope(kv, c, s1, s2, ROPE_DIM // 2)
        kdst[pl.ds(r0, QW), :] = jnp.where(lane_q < HEAD_DIM, kr, 0.0).astype(BF16)
        vdst[:, pl.ds(r0, QW)] = kv.T[HEAD_DIM:2 * HEAD_DIM, :].astype(BF16)

    q = q_ref[...] * (SCALE * LOG2E)
    qr = _rope_wide(q, c, s1, s2, ROPE_DIM // 2)
    q_st = _stack_heads_t(q.T, H, HEAD_DIM).astype(BF16)
    qr_st = _stack_heads_t(qr.T, H, HEAD_DIM).astype(BF16)
    tile_h = lambda b: jnp.concatenate([b] * H, axis=1)

    ncp = kc_scr.shape[0]
    cshift = cmpb_ref.shape[0] - ncp
    cb = cmpb_ref[pl.ds(pl.multiple_of(cshift - (QW // CMP_STRIDE) * i, 8), ncp), :]
    s_c = jnp.dot(kc_scr[...], q_st, preferred_element_type=F32) + tile_h(cb)
    e_c, _, den_c = _softmax2_cols(s_c)
    t_row = i * QW + (lax.broadcasted_iota(I32, (1, HW), 1) & (QW - 1))
    r_c = jnp.where(t_row >= CMP_BLOCK - 1, 1.0 / den_c, 0.0)
    o_cmp = jnp.dot(vct_scr[...], e_c.astype(BF16), preferred_element_type=F32) * r_c
    psum = e_c[:, 0:QW] * r_c[:, 0:QW]
    for h in range(1, H):
        psum = psum + e_c[:, QW * h:QW * (h + 1)] * r_c[:, QW * h:QW * (h + 1)]

    hi = psum.astype(BF16)
    r1 = psum - hi.astype(F32)
    mid = r1.astype(BF16)
    lo = (r1 - mid.astype(F32)).astype(BF16)
    ovt = ovt_ref[...]
    imp = (jnp.dot(ovt, hi, preferred_element_type=F32) + jnp.dot(ovt, mid, preferred_element_type=F32)
           + jnp.dot(ovt, lo, preferred_element_type=F32))
    n_slc = imp.shape[0]
    jb = lax.broadcasted_iota(I32, (n_slc, QW), 0)
    tq = i * QW + lax.broadcasted_iota(I32, (n_slc, QW), 1)
    cur = tq >> (SLC_BLOCK.bit_length() - 1)
    forced = (jb == 0) | (jb == cur) | (jb == cur - 1)
    val = jnp.where(forced, FORCE, jnp.where(jb <= cur, imp, NEG))
    rank = jnp.zeros((n_slc, QW), I32)
    for jp in range(n_slc):
        row = val[jp:jp + 1, :]
        tie = jnp.where(jb > jp, 1, 0)
        rank = rank + jnp.where(row > val, 1, jnp.where(row == val, tie, 0))
    blk_scr[...] = jnp.where(rank < n_top, jnp.where(val > 0.5 * NEG, 0.0, NEG), NEG)

    kmax = caub_ref.shape[0] // 2

    def slc_chunk(size, base):
        def body(ci, carry):
            m, l, acc = carry
            k0 = pl.multiple_of(base + ci * size, KC)
            ahead = jnp.minimum(i * QW - k0, kmax)
            b0 = k0 // SLC_BLOCK
            rows = [jnp.broadcast_to(blk_scr[pl.ds(b0 + r, 1), :], (SLC_BLOCK, QW)) for r in range(size // SLC_BLOCK)]
            bias = jnp.concatenate(rows, axis=0) + caub_ref[pl.ds(pl.multiple_of(kmax - ahead, QB), size), :]
            s = jnp.dot(ks_scr[pl.ds(k0, size), :], qr_st, preferred_element_type=F32) + tile_h(bias)
            m_new = jnp.maximum(m, jnp.max(s, axis=0, keepdims=True))
            alpha = jnp.exp2(m - m_new)
            e = jnp.exp2(s - m_new)
            l = alpha * l + jnp.sum(e, axis=0, keepdims=True)
            acc = alpha * acc + jnp.dot(vst_scr[:, pl.ds(k0, size)], e.astype(BF16), preferred_element_type=F32)
            return m_new, l, acc
        return body

    n_chunks = (i * QW + QW + KC - 1) // KC
    n_long = n_chunks // ATT_LONG
    init = (jnp.full((1, HW), NEG, F32), jnp.zeros((1, HW), F32), jnp.zeros((HEAD_DIM, HW), F32))
    state = lax.fori_loop(0, n_long, slc_chunk(ATT_LONG * KC, 0), init)
    m_s, l_s, acc_s = lax.fori_loop(0, n_chunks - n_long * ATT_LONG, slc_chunk(KC, n_long * (ATT_LONG * KC)), state)
    o_slc = jnp.where(m_s > 0.5 * NEG, acc_s / l_s, 0.0)

    span = min(WIN + QW, seq)
    w0 = pl.multiple_of(jnp.maximum(i * QW + QW - span, 0), QB)
    wb = winb_ref[pl.ds(pl.multiple_of(span - QW - (i * QW - w0), QB), span), :]
    s_w = jnp.dot(kw_scr[pl.ds(w0, span), :], qr_st, preferred_element_type=F32) + tile_h(wb)
    e_w, _, den_w = _softmax2_cols(s_w)
    o_win = jnp.dot(vwt_scr[:, pl.ds(w0, span)], e_w.astype(BF16), preferred_element_type=F32) * (1.0 / den_w)

    gt = jax.nn.sigmoid(g_ref[...].T)
    outs = []
    for h in range(H):
        sl = slice(QW * h, QW * (h + 1))
        outs.append(gt[3 * h:3 * h + 1, :] * o_cmp[:, sl] + gt[3 * h + 1:3 * h + 2, :] * o_slc[:, sl]
                    + gt[3 * h + 2:3 * h + 3, :] * o_win[:, sl])
    outs.append(jnp.zeros((o_ref.shape[1] - H * HEAD_DIM, QW), F32))
    o_ref[...] = jnp.concatenate(outs, axis=0).T.astype(BF16)


def _nsa(proj, kvcmp, tabs, ovt, batch, seq):
    qw = NSA_QW
    nb = seq // qw
    n_slc = seq // SLC_BLOCK
    ncp = kvcmp.shape[1]
    col = lambda name, w: _OFF[name] // w
    qmap = lambda cidx: (lambda b, i: (b * nb + i, cidx))
    tmap = lambda k: (lambda b, i: (k, i, 0))
    kern = functools.partial(_nsa_kernel, n_top=min(SLC_TOPN, n_slc))
    cmpb, winb, caub = _cmp_bias_table(seq, qw), _win_bias_table(seq, qw), _causal_bias_table(qw)
    whole = lambda a: pl.BlockSpec(a.shape, lambda b, i: (0, 0))
    return pl.pallas_call(
        kern,
        grid=(batch, nb),
        in_specs=[pl.BlockSpec((qw, 384), qmap(col("a_q", 384))),
                  pl.BlockSpec((qw, LANES), qmap(col("a_g", LANES))),
                  pl.BlockSpec((qw, LANES), qmap(col("a_ksvs", LANES))),
                  pl.BlockSpec((qw, LANES), qmap(col("a_kwvw", LANES))),
                  pl.BlockSpec((1, ncp, LANES), lambda b, i: (b, 0, 0)),
                  pl.BlockSpec((None, qw, LANES), tmap(0)),
                  pl.BlockSpec((None, qw, LANES), tmap(1)),
                  pl.BlockSpec((None, qw, LANES), tmap(2)),
                  whole(ovt), whole(cmpb), whole(winb), whole(caub)],
        out_specs=pl.BlockSpec((qw, 384), lambda b, i: (b * nb + i, 0)),
        out_shape=jax.ShapeDtypeStruct((batch * seq, 384), BF16),
        scratch_shapes=[pltpu.VMEM((seq, LANES), BF16), pltpu.VMEM((HEAD_DIM, seq), BF16),
                        pltpu.VMEM((seq, LANES), BF16), pltpu.VMEM((HEAD_DIM, seq), BF16),
                        pltpu.VMEM((ncp, LANES), BF16), pltpu.VMEM((HEAD_DIM, ncp), BF16),
                        pltpu.VMEM((n_slc, qw), F32)],
        compiler_params=_params(2),
        name="nsa",
    )(proj, proj, proj, proj, kvcmp, tabs, tabs, tabs, ovt, cmpb, winb, caub)


SEARCH_BISECT_FROM = 24
SEARCH_MAX_PASSES = 64
SEARCH_KEPT_LIMIT = 6
SEARCH_BLIND_PASSES = 12
SEARCH_SNAP_KEYS = 3
INT_MAX = 2 ** 31 - 1
FLT_MIN_NORMAL = 1.17549435e-38
ZERO_BAND_END = 1 << 23


def _tree_rows(x, op):
    parts = [x[r:r + 8, :] for r in range(0, x.shape[0], 8)]
    while len(parts) > 1:
        nxt = [op(parts[a], parts[a + 1]) for a in range(0, len(parts) - 1, 2)]
        if len(parts) % 2:
            nxt.append(parts[-1])
        parts = nxt
    return parts[0]


def _f2key(v):
    bits = lax.bitcast_convert_type(v, I32)
    return bits ^ ((bits >> 31) & 0x7FFFFFFF)


def _key2f(k):
    return lax.bitcast_convert_type(k ^ ((k >> 31) & 0x7FFFFFFF), F32)


def _dsa_kernel(q_ref, iq_ref, ikw_ref, kv_ref, c_ref, s1_ref, s2_ref, ci_ref, si1_ref, si2_ref, o_ref,
                k_scr, vt_scr, ik_scr, key_scr, *, top, idx_bits):
    i = pl.program_id(1)
    H = DSA_HEADS
    QW = q_ref.shape[0]
    HW = H * QW
    seq = key_scr.shape[0]
    lane_q = lax.broadcasted_iota(I32, (QW, LANES), 1)

    @pl.when(i == 0)
    def _init():
        k_scr[...] = jnp.zeros_like(k_scr)
        vt_scr[...] = jnp.zeros_like(vt_scr)
        ik_scr[...] = jnp.zeros_like(ik_scr)

    c, s1, s2 = c_ref[...], s1_ref[...], s2_ref[...]
    ci, si1, si2 = ci_ref[...], si1_ref[...], si2_ref[...]
    r0 = pl.multiple_of(i * QW, QW)
    kv = kv_ref[...]
    k_scr[pl.ds(r0, QW), :] = jnp.where(lane_q < HEAD_DIM, _rope(kv, c, s1, s2, ROPE_DIM // 2), 0.0).astype(BF16)
    vt_scr[:, pl.ds(r0, QW)] = kv.T[HEAD_DIM:2 * HEAD_DIM, :].astype(BF16)
    ikw = ikw_ref[...]
    ik_scr[pl.ds(r0, QW), :] = jnp.where(lane_q < IDX_DIM, _rope(ikw, ci, si1, si2, IDX_ROPE_DIM // 2), 0.0).astype(BF16)

    qr = _rope_wide(q_ref[...] * (SCALE * LOG2E), c, s1, s2, ROPE_DIM // 2)
    qr_st = _stack_heads_t(qr.T, H, HEAD_DIM).astype(BF16)
    iqr = _rope_wide(iq_ref[...], ci, si1, si2, IDX_ROPE_DIM // 2)
    iq_st = _stack_heads_t(iqr.T, IDX_HEADS, IDX_DIM).astype(BF16)
    wt = ikw.T * ((IDX_DIM ** -0.5) * (IDX_HEADS ** -0.5))
    w_st = jnp.concatenate([wt[IDX_DIM + h:IDX_DIM + h + 1, :] for h in range(IDX_HEADS)], axis=1)

    n_chunks = (i * QW + QW + KC - 1) // KC
    k_io = lax.broadcasted_iota(I32, (KC, QW), 0)
    tq = i * QW + lax.broadcasted_iota(I32, (KC, QW), 1)

    sub_io = lax.broadcasted_iota(I32, (QB, QW), 0)
    sub_tq = i * QW + lax.broadcasted_iota(I32, (QB, QW), 1)
    zkey0 = seq - sub_io

    def score_body(cidx, carry):
        mx, mn = carry
        k0 = pl.multiple_of(cidx * KC, KC)
        for sb in range(KC // QB):
            kb = pl.multiple_of(k0 + sb * QB, QB)
            d = jnp.dot(ik_scr[pl.ds(kb, QB), :], iq_st, preferred_element_type=F32)
            r = jnp.maximum(d, 0.0) * w_st
            sc = r[:, 0:QW]
            for h in range(1, IDX_HEADS):
                sc = sc + r[:, QW * h:QW * (h + 1)]
            key = jnp.where(jnp.abs(sc) < FLT_MIN_NORMAL, zkey0 - kb, _f2key(sc))
            key_scr[pl.ds(kb, QB), :] = jnp.where(sub_io + kb <= sub_tq, key, INT_MIN)
            mx = jnp.maximum(mx, _tree_rows(key, jnp.maximum))
            mn = jnp.minimum(mn, _tree_rows(key, jnp.minimum))
        return mx, mn

    mx8, mn8 = lax.fori_loop(0, n_chunks, score_body,
                             (jnp.full((8, QW), INT_MIN, I32), jnp.full((8, QW), INT_MAX, I32)))

    def count(pred):
        def body(cidx, acc):
            k0 = pl.multiple_of(cidx * KC, KC)
            return acc + _tree_rows(pred(key_scr[pl.ds(k0, KC), :], k_io + k0), jnp.add)
        acc8 = lax.fori_loop(0, n_chunks, body, jnp.zeros((8, QW), F32))
        return jnp.sum(acc8, axis=0, keepdims=True)

    topf = float(top)
    lo0 = functools.reduce(jnp.minimum, [mn8[r:r + 1, :] for r in range(8)])
    hi0 = functools.reduce(jnp.maximum, [mx8[r:r + 1, :] for r in range(8)]) + 1
    n_causal = (i * QW + 1 + lax.broadcasted_iota(I32, (1, QW), 1)).astype(F32)
    log_top = float(np.log(top))

    def in_zero_band(lo, hi):
        return jnp.where(lo >= 1, jnp.where(hi <= ZERO_BAND_END, 1.0, 0.0), 0.0) > 0.5

    def open_keys(lo, hi, c_lo, c_hi):
        return jnp.max(jnp.where(c_lo > topf, jnp.where(hi > lo + 1, c_lo - c_hi, 0.0), 0.0)).astype(I32)

    def search_cond(c):
        return jnp.logical_and(c[0] < SEARCH_MAX_PASSES, c[1] > SEARCH_SNAP_KEYS)

    def search_pass(it, st):
        lo, hi, c_lo, c_hi, kept, w_lo, w_hi = st
        f_lo = (jnp.log(c_lo) - log_top) * w_lo
        f_hi = (log_top - jnp.log(jnp.maximum(c_hi, 0.5))) * w_hi
        frac = f_lo / (f_lo + f_hi)
        v_lo, v_hi = _key2f(lo), _key2f(hi)
        cand = _f2key(v_lo + frac * (v_hi - v_lo))
        lo_f, hi_f = lo.astype(F32), hi.astype(F32)
        cand = jnp.where(in_zero_band(lo, hi), (lo_f + frac * (hi_f - lo_f)).astype(I32), cand)
        key_mid = (lo >> 1) + (hi >> 1) + (lo & hi & 1)
        cand = jnp.where(jnp.abs(kept) >= SEARCH_KEPT_LIMIT, key_mid, cand)
        cand = jnp.where(it >= SEARCH_BISECT_FROM, key_mid, cand)
        cand = jnp.where(it == 0, 1, jnp.where(it == 1, ZERO_BAND_END, cand))
        cand = jnp.minimum(jnp.maximum(cand, lo + 1), hi - 1)
        cnt = count(lambda key, kpos: jnp.where(key >= cand, 1.0, 0.0))
        active = jnp.where(c_lo > topf, jnp.where(hi > lo + 1, 1.0, 0.0), 0.0) > 0.5
        up = jnp.where(active, jnp.where(cnt >= topf, 1.0, 0.0), 0.0) > 0.5
        dn = jnp.where(active, jnp.where(cnt >= topf, 0.0, 1.0), 0.0) > 0.5
        w_hi = jnp.where(up, jnp.where(kept < 0, 0.5 * w_hi, 1.0), jnp.where(dn, 1.0, w_hi))
        w_lo = jnp.where(dn, jnp.where(kept > 0, 0.5 * w_lo, 1.0), jnp.where(up, 1.0, w_lo))
        kept = jnp.where(up, jnp.where(kept < 0, kept - 1, -1), jnp.where(dn, jnp.where(kept > 0, kept + 1, 1), kept))
        lo, c_lo = jnp.where(up, cand, lo), jnp.where(up, cnt, c_lo)
        hi, c_hi = jnp.where(dn, cand, hi), jnp.where(dn, cnt, c_hi)
        hi = jnp.where(in_zero_band(lo, hi), jnp.minimum(hi, seq + 1), hi)
        return lo, hi, c_lo, c_hi, kept, w_lo, w_hi

    def search_body(c):
        st = search_pass(c[0], c[2])
        return c[0] + 1, open_keys(*st[:4]), st

    zero_f, one_f = jnp.zeros((1, QW), F32), jnp.ones((1, QW), F32)
    st0 = (lo0, hi0, n_causal, zero_f, jnp.zeros((1, QW), I32), one_f, one_f)
    n_blind = jnp.where(i * QW + QW > top, SEARCH_BLIND_PASSES, 0)
    st1 = lax.fori_loop(0, n_blind, search_pass, st0)
    _, _, st2 = lax.while_loop(search_cond, search_body, (n_blind, open_keys(*st1[:4]), st1))

    def snap_body(c):
        lo, hi, c_lo, c_hi = c[1]

        def below_hi(cidx, acc):
            k0 = pl.multiple_of(cidx * KC, KC)
            key = key_scr[pl.ds(k0, KC), :]
            return jnp.maximum(acc, _tree_rows(jnp.where(key < hi, key, INT_MIN), jnp.maximum))
        top8 = lax.fori_loop(0, n_chunks, below_hi, jnp.full((8, QW), INT_MIN, I32))
        cand = functools.reduce(jnp.maximum, [top8[r:r + 1, :] for r in range(8)])
        cnt = count(lambda key, kpos: jnp.where(key >= cand, 1.0, 0.0))
        active = jnp.where(c_lo > topf, jnp.where(hi > lo + 1, 1.0, 0.0), 0.0) > 0.5
        up = jnp.where(active, jnp.where(cnt >= topf, 1.0, 0.0), 0.0) > 0.5
        dn = jnp.where(active, jnp.where(cnt >= topf, 0.0, 1.0), 0.0) > 0.5
        lo, c_lo = jnp.where(up, cand, lo), jnp.where(up, cnt, c_lo)
        hi, c_hi = jnp.where(up, cand + 1, jnp.where(dn, cand, hi)), jnp.where(dn, cnt, c_hi)
        return open_keys(lo, hi, c_lo, c_hi), (lo, hi, c_lo, c_hi)

    _, (thr, _, c_lo, c_hi) = lax.while_loop(lambda c: c[0] > 0, snap_body, (open_keys(*st2[:4]), st2[:4]))

    tied = c_lo > topf
    need1 = (topf - 1.0) - c_hi
    all_pos = jnp.full((1, QW), 2 ** idx_bits - 1, I32)

    def tie_search():
        def tie_body(bi, cur):
            cand = cur | lax.shift_left(jnp.int32(1), idx_bits - 1 - bi)
            cnt = count(lambda key, kpos: jnp.where(key == thr, jnp.where(kpos < cand, 1.0, 0.0), 0.0))
            return jnp.where(cnt <= need1, cand, cur)
        return lax.fori_loop(0, idx_bits, tie_body, jnp.zeros((1, QW), I32))

    any_tied = jnp.max(jnp.where(tied, 1.0, 0.0)) > 0.5
    last = jnp.where(tied, lax.cond(any_tied, tie_search, lambda: all_pos), all_pos)

    half_neg = int(np.float32(0.5 * NEG).view(np.int32))
    key_floor = half_neg ^ 0x7FFFFFFF

    def att_chunk(size, base):
        pos_io = lax.broadcasted_iota(I32, (size, QW), 0)

        def body(cidx, state):
            m, l, acc = state
            k0 = pl.multiple_of(base + cidx * size, KC)
            key = key_scr[pl.ds(k0, size), :]
            kpos = pos_io + k0
            bias = jnp.where(key > thr, 0.0, jnp.where(key == thr, jnp.where(kpos <= last, 0.0, NEG), NEG))
            bias = jnp.where(key > key_floor, bias, NEG)
            s = (jnp.dot(k_scr[pl.ds(k0, size), :], qr_st, preferred_element_type=F32)
                 + jnp.concatenate([bias] * H, axis=1))
            m_new = jnp.maximum(m, jnp.max(s, axis=0, keepdims=True))
            alpha = jnp.exp2(m - m_new)
            e = jnp.exp2(s - m_new)
            l = alpha * l + jnp.sum(e, axis=0, keepdims=True)
            acc = alpha * acc + jnp.dot(vt_scr[:, pl.ds(k0, size)], e.astype(BF16), preferred_element_type=F32)
            return m_new, l, acc
        return body

    init = (jnp.full((1, HW), NEG, F32), jnp.zeros((1, HW), F32), jnp.zeros((HEAD_DIM, HW), F32))
    n_long = n_chunks // ATT_LONG
    state = lax.fori_loop(0, n_long, att_chunk(ATT_LONG * KC, 0), init)
    m_a, l_a, acc_a = lax.fori_loop(0, n_chunks - n_long * ATT_LONG, att_chunk(KC, n_long * (ATT_LONG * KC)), state)
    o = jnp.where(m_a > 0.5 * NEG, acc_a / l_a, 0.0)
    outs = [o[:, QW * h:QW * (h + 1)] for h in range(H)]
    outs.append(jnp.zeros((o_ref.shape[1] - H * HEAD_DIM, QW), F32))
    o_ref[...] = jnp.concatenate(outs, axis=0).T.astype(BF16)


def _dsa(proj, tabs, tabs_i, batch, seq):
    qw = DSA_QW
    nb = seq // qw
    col = lambda name, w: _OFF[name] // w
    qmap = lambda cidx: (lambda b, i: (b * nb + i, cidx))
    tmap = lambda k: (lambda b, i: (k, i, 0))
    top = min(DSA_TOPK_MAX, seq // 4)
    kern = functools.partial(_dsa_kernel, top=top, idx_bits=int(seq).bit_length())
    return pl.pallas_call(
        kern,
        grid=(batch, nb),
        in_specs=[pl.BlockSpec((qw, 384), qmap(col("b_q", 384))),
                  pl.BlockSpec((qw, 256), qmap(col("b_iq", 256))),
                  pl.BlockSpec((qw, LANES), qmap(col("b_ikiw", LANES))),
                  pl.BlockSpec((qw, LANES), qmap(col("b_kv", LANES))),
                  pl.BlockSpec((None, qw, LANES), tmap(0)),
                  pl.BlockSpec((None, qw, LANES), tmap(1)),
                  pl.BlockSpec((None, qw, LANES), tmap(2)),
                  pl.BlockSpec((None, qw, LANES), tmap(0)),
                  pl.BlockSpec((None, qw, LANES), tmap(1)),
                  pl.BlockSpec((None, qw, LANES), tmap(2))],
        out_specs=pl.BlockSpec((qw, 384), lambda b, i: (b * nb + i, 0)),
        out_shape=jax.ShapeDtypeStruct((batch * seq, 384), BF16),
        scratch_shapes=[pltpu.VMEM((seq, LANES), BF16), pltpu.VMEM((HEAD_DIM, seq), BF16),
                        pltpu.VMEM((seq, LANES), BF16), pltpu.VMEM((seq, qw), I32)],
        compiler_params=_params(2),
        name="dsa",
    )(proj, proj, proj, proj, tabs, tabs, tabs, tabs_i, tabs_i, tabs_i)


DIL_MB = max(d for _, d in DIL_PAIRS) * QB
DIL_VMEM_LIMIT = 56 * 1024 * 1024
DIL_UNROLL = 4


def _dil_layout(g):
    dil = DIL_PAIRS[g][1]
    per = DIL_MB // dil
    return dil, per, per + QB


def _dil_bias_table():
    u = np.arange(2 * QB)[:, None]
    diff = QB + np.arange(LANES)[None, :] - u
    ok = (diff >= 0) & (diff <= QB)
    return _bias_table(np.concatenate([ok, ok & (u >= QB)], axis=0))


def _dil_kernel(*refs):
    G, HG = len(DIL_PAIRS), DIL_HEADS_PER_GROUP
    q_refs, k_refs, v_refs = refs[0:G], refs[G:2 * G], refs[2 * G:3 * G]
    c_ref, s1_ref, s2_ref, bias_ref, o_ref = refs[3 * G:3 * G + 5]
    kds, vds, ogs, lss = (refs[3 * G + 5 + n * G:3 * G + 5 + (n + 1) * G] for n in range(4))
    step = pl.program_id(1)
    half = ROPE_DIM // 2

    @pl.when(step == 0)
    def _zero():
        for g in range(G):
            kds[g][...] = jnp.zeros_like(kds[g])
            vds[g][...] = jnp.zeros_like(vds[g])

    @pl.when(step > 0)
    def _carry_halo():
        for g in range(G):
            dil, per, stride = _dil_layout(g)
            for r in range(dil):
                b0 = r * stride
                kds[g][b0:b0 + QB, :] = kds[g][b0 + per:b0 + per + QB, :]
                vds[g][:, b0:b0 + QB] = vds[g][:, b0 + per:b0 + per + QB]

    for g in range(G):
        dil, per, stride = _dil_layout(g)
        piece = min(per, KC)
        for r in range(dil):
            for p0 in range(0, per, piece):
                rows = pl.ds(r + dil * p0, piece, stride=dil)
                kr = _rope(k_refs[g][rows, :], c_ref[rows, :], s1_ref[rows, :], s2_ref[rows, :], half)
                d0 = r * stride + QB + p0
                kds[g][d0:d0 + piece, :] = kr.astype(BF16)
                vds[g][:, d0:d0 + piece] = v_refs[g][rows, :].T.astype(BF16)

    zero = jnp.zeros((HEAD_DIM, LANES), F32)
    for g in range(G):
        dil, per, stride = _dil_layout(g)
        nblk = per // QB

        def block(jb, _, r=0, g=g, dil=dil, stride=stride):
            p0 = pl.multiple_of(dil * QB * jb, dil * QB)
            win = pl.ds(p0, dil * QB)
            rows = pl.ds(r, QB, stride=dil)
            q = _rope(q_refs[g].at[win, :][rows, :] * (SCALE * LOG2E), c_ref.at[win, :][rows, :],
                      s1_ref.at[win, :][rows, :], s2_ref.at[win, :][rows, :], half)
            qt = q.T
            q2 = jnp.concatenate([jnp.concatenate([qt[:HEAD_DIM], zero], axis=0),
                                  jnp.concatenate([zero, qt[HEAD_DIM:]], axis=0)], axis=1).astype(BF16)
            kb = pl.multiple_of(r * stride + QB * jb, QB)
            first = jnp.logical_and(step == 0, jb == 0)
            bias = bias_ref[pl.ds(pl.multiple_of(jnp.where(first, 2 * QB, 0), QB), 2 * QB), :]
            s = (jnp.dot(kds[g][pl.ds(kb, 2 * QB), :], q2, preferred_element_type=F32)
                 + jnp.concatenate([bias] * HG, axis=1))
            e, m, den = _softmax2_cols(s)
            lse = m + jnp.log2(den)
            rden = 1.0 / den
            parts, lparts = [], []
            for hg in range(HG):
                sl = slice(LANES * hg, LANES * (hg + 1))
                vt = vds[g][HEAD_DIM * hg:HEAD_DIM * (hg + 1), pl.ds(kb, 2 * QB)]
                parts.append(jnp.dot(vt, e[:, sl].astype(BF16), preferred_element_type=F32) * rden[:, sl])
                lparts.append(jnp.broadcast_to(lse[:, sl], (HEAD_DIM, LANES)))
            tile = jnp.concatenate(parts + lparts, axis=0).T
            ogs[g].at[win, :][rows, :] = tile[:, :LANES]
            lss[g].at[win, :][rows, :] = tile[:, LANES:]
            return 0

        for r in range(dil):
            lax.fori_loop(0, nblk, functools.partial(block, r=r), 0, unroll=min(nblk, DIL_UNROLL))

    def mix(ti, _):
        rows = pl.ds(pl.multiple_of(ti * KC, KC), KC)
        ls = [lss[g][rows, :] for g in range(G)]
        mx = functools.reduce(jnp.maximum, ls)
        ex = [jnp.exp2(x - mx) for x in ls]
        rtot = 1.0 / functools.reduce(lambda a, b: a + b, ex)
        o_ref[rows, :] = jnp.concatenate([ex[g] * rtot * ogs[g][rows, :] for g in range(G)], axis=1).astype(BF16)
        return 0

    lax.fori_loop(0, DIL_MB // KC, mix, 0)


def _dil(proj, tabs, batch, seq):
    assert seq % DIL_MB == 0 and all(w // d == QB for w, d in DIL_PAIRS)
    nm = seq // DIL_MB
    G = len(DIL_PAIRS)
    w = DIL_HEADS * HEAD_DIM
    gmap = lambda name, g: (lambda b, j: (b * nm + j, _OFF[name] // LANES + g))
    tmap = lambda k: (lambda b, j: (k, j, 0))
    bias = _dil_bias_table()
    rows = [_dil_layout(g)[0] * _dil_layout(g)[2] for g in range(G)]
    return pl.pallas_call(
        _dil_kernel,
        grid=(batch, nm),
        in_specs=([pl.BlockSpec((DIL_MB, LANES), gmap(name, g)) for name in ("c_q", "c_k", "c_v") for g in range(G)]
                  + [pl.BlockSpec((None, DIL_MB, LANES), tmap(k)) for k in range(3)]
                  + [pl.BlockSpec(bias.shape, lambda b, j: (0, 0))]),
        out_specs=pl.BlockSpec((DIL_MB, w), lambda b, j: (b * nm + j, 0)),
        out_shape=jax.ShapeDtypeStruct((batch * seq, w), BF16),
        scratch_shapes=([pltpu.VMEM((n, LANES), BF16) for n in rows] + [pltpu.VMEM((LANES, n), BF16) for n in rows]
                        + [pltpu.VMEM((DIL_MB, LANES), F32)] * (2 * G)),
        compiler_params=_params(2, DIL_VMEM_LIMIT),
        name="dilated",
    )(*([proj] * (3 * G)), tabs, tabs, tabs, bias)


def _overlap_t(seq):
    n_cmp_pad = seq // CMP_STRIDE
    n_slc = seq // SLC_BLOCK
    c_start = np.arange(n_cmp_pad) * CMP_STRIDE
    s_start = np.arange(n_slc) * SLC_BLOCK
    ov = ((c_start[None, :] < s_start[:, None] + SLC_BLOCK) & (c_start[None, :] + CMP_BLOCK > s_start[:, None]))
    return jnp.asarray(ov.astype(np.float32), dtype=BF16)


def _layer(x2, batch, seq, layer, norm1_g, w_in, cmp_pos, cmp_w1, cmp_w2, w_out, norm2_g, w_up_bf, conv_w, conv_b,
           w_down_bf, final_g, final_norm, tabs, tabs_i, ovt):
    n = batch * seq
    proj = _inproj(x2, norm1_g, _regroup_w_in(w_in, layer))

    w1 = cmp_w1.reshape(2, CMP_BLOCK, HEAD_DIM, CMP_HIDDEN)
    z1 = jnp.zeros_like(w1[0])
    w1_cat = jnp.concatenate([jnp.concatenate([w1[0], z1], axis=2),
                              jnp.concatenate([z1, w1[1]], axis=2)], axis=1).astype(BF16)
    z2 = jnp.zeros((CMP_HIDDEN, HEAD_DIM), cmp_w2.dtype)
    w2_cat = jnp.concatenate([jnp.concatenate([cmp_w2[0], z2], axis=1),
                              jnp.concatenate([z2, cmp_w2[1]], axis=1)], axis=0).astype(BF16)
    pos_cat = jnp.concatenate([cmp_pos[0], cmp_pos[1]], axis=1)
    kvcmp = _nsa_compress(proj, pos_cat, w1_cat, w2_cat, batch, seq)

    o_a = _nsa(proj, kvcmp, tabs, ovt, batch, seq)
    o_b = _dsa(proj, tabs, tabs_i, batch, seq)
    o_c = _dil(proj, tabs, batch, seq)

    hp = NSA_HEADS * HEAD_DIM
    zrow = jnp.zeros((384 - hp, D_MODEL), w_out.dtype)
    w_out_pad = jnp.concatenate([w_out[0:hp], zrow, w_out[hp:2 * hp], zrow, w_out[2 * hp:]], axis=0).astype(BF16)
    x2 = _outproj(o_a, o_b, o_c, w_out_pad, x2)

    act = _ffn_up(x2, norm2_g, w_up_bf, layer, conv_w, conv_b, seq)
    return _ffn_down(act, w_down_bf, layer, x2, final_g, final_norm)


def kernel(x, norm1_g, w_in, cmp_pos, cmp_w1, cmp_w2, w_out, norm2_g, w_up, conv_w, conv_b, w_down, final_g):
    batch, seq, d = x.shape
    depth = w_in.shape[0]
    tabs = _rope_lane_tables(seq, HEAD_DIM, ROPE_DIM)
    tabs_i = _rope_lane_tables(seq, IDX_DIM, IDX_ROPE_DIM)
    ovt = _overlap_t(seq)
    x2 = x.reshape(batch * seq, d)
    w_up_bf, w_down_bf = w_up.astype(BF16), w_down.astype(BF16)
    for li in range(depth):
        x2 = _layer(x2, batch, seq, li, norm1_g[li], w_in, cmp_pos[li], cmp_w1[li], cmp_w2[li], w_out[li],
                    norm2_g[li], w_up_bf, conv_w[li], conv_b[li], w_down_bf, final_g, li == depth - 1,
                    tabs, tabs_i, ovt)
    return x2.reshape(batch, seq, d)
```

```python
import functools

import numpy as np
import jax
import jax.numpy as jnp
from jax import lax
from jax.experimental import pallas as pl
from jax.experimental.pallas import tpu as pltpu

F32 = jnp.float32
BF16 = jnp.bfloat16
I32 = jnp.int32

D_MODEL = 1024
HEAD_DIM = 64
ROPE_DIM = HEAD_DIM // 4
ROPE_THETA = 500000.0
NORM_EPS = 1e-6
SCALE = HEAD_DIM ** -0.5
LOG2E = 1.4426950408889634
NEG = -1e30
FORCE = 1e9
NSA_HEADS = 5
CMP_BLOCK = 32
CMP_STRIDE = 16
CMP_HIDDEN = 128
SLC_BLOCK = 64
SLC_TOPN = 16
WIN = 512
DSA_HEADS = 5
IDX_HEADS = 8
IDX_DIM = 32
IDX_ROPE_DIM = IDX_DIM // 4
DSA_TOPK_MAX = 256
DIL_PAIRS = ((128, 1), (512, 4), (2048, 16))
DIL_HEADS_PER_GROUP = 2
DIL_HEADS = len(DIL_PAIRS) * DIL_HEADS_PER_GROUP
D_FF = 2816
CONV_WIDTH = 3

LANES = 128
QB = 128
DSA_QW = 256
NSA_QW = 256
KC = 512
ATT_LONG = 2
ROW_TILE = 512
FFN_UP_ROW_TILE = 1024
HALO = 16
VMEM_LIMIT = 48 * 1024 * 1024
INT_MIN = -2 ** 31

_A0 = 0
_B0 = 719
_C0 = 1463
_SEGS = (
    ("c_q", _C0, 384, 384), ("c_k", _C0 + 384, 384, 384), ("c_v", _C0 + 768, 384, 384),
    ("a_q", _A0, 320, 384), ("b_q", _B0, 320, 384),
    ("a_kcvc", _A0 + 320, 128, 128),
    ("b_iq", _B0 + 448, 256, 256),
    ("a_ksvs", _A0 + 448, 128, 128), ("a_kwvw", _A0 + 576, 128, 128),
    ("a_g", _A0 + 704, 15, 128),
    ("b_kv", _B0 + 320, 128, 128),
    ("b_ikiw", _B0 + 704, 40, 128),
)
P_COLS = sum(s[3] for s in _SEGS)


def _seg_offsets():
    offs, o = {}, 0
    for name, _, _, pw in _SEGS:
        offs[name] = o
        o += pw
    return offs


_OFF = _seg_offsets()


def _regroup_kernel(w_ref, o_ref):
    for name, src, wdt, pw in _SEGS:
        dst = _OFF[name]
        o_ref[:, dst:dst + wdt] = w_ref[:, src:src + wdt].astype(BF16)
        if pw > wdt:
            o_ref[:, dst + wdt:dst + pw] = jnp.zeros((o_ref.shape[0], pw - wdt), BF16)


def _regroup_w_in(w, layer):
    _, d, n_in = w.shape
    rt = 256
    return pl.pallas_call(
        _regroup_kernel,
        grid=(d // rt,),
        in_specs=[pl.BlockSpec((None, rt, n_in), lambda i: (layer, i, 0))],
        out_specs=pl.BlockSpec((rt, P_COLS), lambda i: (i, 0)),
        out_shape=jax.ShapeDtypeStruct((d, P_COLS), BF16),
        compiler_params=_params(1),
        name="regroup_w_in",
    )(w)


def _rope_lane_tables(L, head_dim, rot_dim):
    half = rot_dim // 2
    inv = 1.0 / (ROPE_THETA ** (np.arange(0, rot_dim, 2, dtype=np.float32) / np.float32(rot_dim)))
    ang = np.arange(L, dtype=np.float32)[:, None] * inv[None, :]
    cos, sin = np.cos(ang).astype(np.float32), np.sin(ang).astype(np.float32)
    d = np.arange(LANES) % head_dim
    lo, hi = d < half, (d >= half) & (d < rot_dim)
    c = np.ones((L, LANES), np.float32)
    s1 = np.zeros((L, LANES), np.float32)
    s2 = np.zeros((L, LANES), np.float32)
    c[:, lo] = cos[:, d[lo]]
    c[:, hi] = cos[:, d[hi] - half]
    s1[:, lo] = -sin[:, d[lo]]
    s2[:, hi] = sin[:, d[hi] - half]
    return jnp.asarray(np.stack([c, s1, s2]))


def _rope(x, c, s1, s2, half):
    xp = pltpu.roll(x, LANES - half, 1)
    xm = pltpu.roll(x, half, 1)
    return x * c + xp * s1 + xm * s2


def _rope_wide(x, c, s1, s2, half):
    n = x.shape[1] // LANES
    return jnp.concatenate([_rope(x[:, LANES * s:LANES * (s + 1)], c, s1, s2, half) for s in range(n)], axis=1)


def _softmax2_cols(s):
    m = jnp.max(s, axis=0, keepdims=True)
    e = jnp.exp2(s - m)
    return e, m, jnp.sum(e, axis=0, keepdims=True)


def _bias_table(ok):
    return jnp.asarray(np.where(ok, 0.0, NEG).astype(np.float32))


def _cmp_bias_table(seq, qw):
    shift = (qw // CMP_STRIDE) * (seq // qw - 1)
    u = np.arange(seq // CMP_STRIDE + shift)[:, None]
    lane = np.arange(qw)[None, :]
    return _bias_table(CMP_STRIDE * (u - shift) + CMP_BLOCK - 1 <= lane)


def _win_bias_table(seq, qw):
    span = min(WIN + qw, seq)
    cmax = span - qw
    u = np.arange(span + cmax)[:, None]
    diff = cmax + np.arange(qw)[None, :] - u
    return _bias_table((diff >= 0) & (diff < WIN))


def _causal_bias_table(qw):
    kmax = ATT_LONG * KC
    u = np.arange(2 * kmax)[:, None]
    return _bias_table(u <= kmax + np.arange(qw)[None, :])


def _params(n_grid, vmem=VMEM_LIMIT):
    return pltpu.CompilerParams(dimension_semantics=("arbitrary",) * n_grid, vmem_limit_bytes=vmem)


def _rmsnorm_rows(x, g):
    return x * lax.rsqrt(jnp.mean(x * x, axis=-1, keepdims=True) + NORM_EPS) * g


def _inproj_kernel(x_ref, g_ref, w_ref, o_ref):
    hn = _rmsnorm_rows(x_ref[...], g_ref[...]).astype(BF16)
    n = w_ref.shape[1]
    for c0 in range(0, n, 512):
        c1 = min(c0 + 512, n)
        o_ref[:, c0:c1] = jnp.dot(hn, w_ref[:, c0:c1], preferred_element_type=F32)


def _inproj(x2, g, w_bf):
    n, d = x2.shape
    pc = w_bf.shape[1]
    return pl.pallas_call(
        _inproj_kernel,
        grid=(n // ROW_TILE,),
        in_specs=[pl.BlockSpec((ROW_TILE, d), lambda i: (i, 0)),
                  pl.BlockSpec((1, d), lambda i: (0, 0)),
                  pl.BlockSpec((d, pc), lambda i: (0, 0))],
        out_specs=pl.BlockSpec((ROW_TILE, pc), lambda i: (i, 0)),
        out_shape=jax.ShapeDtypeStruct((n, pc), F32),
        compiler_params=_params(1),
        name="inproj",
    )(x2, g.reshape(1, d), w_bf)


def _outproj_kernel(oa_ref, ob_ref, oc_ref, w_ref, x_ref, o_ref):
    mix = jnp.concatenate([oa_ref[...], ob_ref[...], oc_ref[...]], axis=1)
    o_ref[...] = x_ref[...] + jnp.dot(mix, w_ref[...], preferred_element_type=F32)


def _outproj(oa, ob, oc, w_bf, x2):
    n, d = x2.shape
    k = w_bf.shape[0]
    mw = oa.shape[1]
    return pl.pallas_call(
        _outproj_kernel,
        grid=(n // ROW_TILE,),
        in_specs=[pl.BlockSpec((ROW_TILE, mw), lambda i: (i, 0)),
                  pl.BlockSpec((ROW_TILE, mw), lambda i: (i, 0)),
                  pl.BlockSpec((ROW_TILE, mw), lambda i: (i, 0)),
                  pl.BlockSpec((k, d), lambda i: (0, 0)),
                  pl.BlockSpec((ROW_TILE, d), lambda i: (i, 0))],
        out_specs=pl.BlockSpec((ROW_TILE, d), lambda i: (i, 0)),
        out_shape=jax.ShapeDtypeStruct((n, d), F32),
        compiler_params=_params(1),
        name="outproj",
    )(oa, ob, oc, w_bf, x2)


def _ffn_up_kernel(x_ref, xh_ref, g_ref, wa_ref, wu_ref, cw_ref, cb_ref, o_ref, hn_scr, hh_scr, a_scr, *, tiles_per_seq):
    i, j = pl.program_id(0), pl.program_id(1)
    tm = x_ref.shape[0]

    @pl.when(j == 0)
    def _norm():
        hn_scr[...] = _rmsnorm_rows(x_ref[...], g_ref[...]).astype(BF16)
        keep = jnp.where(i % tiles_per_seq == 0, 0.0, 1.0)
        hh_scr[...] = (_rmsnorm_rows(xh_ref[...], g_ref[...]) * keep).astype(BF16)

    hn = hn_scr[...]
    a = jnp.dot(hn, wa_ref[...], preferred_element_type=F32)
    u = jnp.dot(hn, wu_ref[...], preferred_element_type=F32)
    a_scr[0:HALO, :] = jnp.dot(hh_scr[...], wa_ref[...], preferred_element_type=F32)
    a_scr[HALO:HALO + tm, :] = a
    a1 = a_scr[pl.ds(HALO - 1, tm), :]
    a2 = a_scr[pl.ds(HALO - 2, tm), :]
    cw = cw_ref[...]
    conv = cw[0:1, :] * a2 + cw[1:2, :] * a1 + cw[2:3, :] * a + cb_ref[...]
    o_ref[...] = (conv * jax.nn.sigmoid(conv) * u).astype(BF16)


def _ffn_up(x2, g, w_up_bf, layer, conv_w, conv_b, seq_len):
    n, d = x2.shape
    tn = D_FF // 2
    nj = D_FF // tn
    tm = FFN_UP_ROW_TILE
    hb = tm // HALO
    kern = functools.partial(_ffn_up_kernel, tiles_per_seq=seq_len // tm)
    return pl.pallas_call(
        kern,
        grid=(n // tm, nj),
        in_specs=[pl.BlockSpec((tm, d), lambda i, j: (i, 0)),
                  pl.BlockSpec((HALO, d), lambda i, j: (jnp.maximum(i * hb - 1, 0), 0)),
                  pl.BlockSpec((1, d), lambda i, j: (0, 0)),
                  pl.BlockSpec((None, d, tn), lambda i, j: (layer, 0, j)),
                  pl.BlockSpec((None, d, tn), lambda i, j: (layer, 0, j + nj)),
                  pl.BlockSpec((CONV_WIDTH, tn), lambda i, j: (0, j)),
                  pl.BlockSpec((1, tn), lambda i, j: (0, j))],
        out_specs=pl.BlockSpec((tm, tn), lambda i, j: (i, j)),
        out_shape=jax.ShapeDtypeStruct((n, D_FF), BF16),
        scratch_shapes=[pltpu.VMEM((tm, d), BF16), pltpu.VMEM((HALO, d), BF16),
                        pltpu.VMEM((tm + HALO, tn), F32)],
        compiler_params=_params(2),
        name="ffn_up",
    )(x2, x2, g.reshape(1, d), w_up_bf, w_up_bf, conv_w, conv_b.reshape(1, D_FF))


def _ffn_down_kernel(a_ref, w_ref, x_ref, g_ref, o_ref, *, final_norm):
    y = x_ref[...] + jnp.dot(a_ref[...], w_ref[...], preferred_element_type=F32)
    if final_norm:
        y = _rmsnorm_rows(y, g_ref[...])
    o_ref[...] = y


def _ffn_down(act, w_bf, layer, x2, final_g, final_norm):
    n, d = x2.shape
    k = act.shape[1]
    return pl.pallas_call(
        functools.partial(_ffn_down_kernel, final_norm=final_norm),
        grid=(n // ROW_TILE,),
        in_specs=[pl.BlockSpec((ROW_TILE, k), lambda i: (i, 0)),
                  pl.BlockSpec((None, k, d), lambda i: (layer, 0, 0)),
                  pl.BlockSpec((ROW_TILE, d), lambda i: (i, 0)),
                  pl.BlockSpec((1, d), lambda i: (0, 0))],
        out_specs=pl.BlockSpec((ROW_TILE, d), lambda i: (i, 0)),
        out_shape=jax.ShapeDtypeStruct((n, d), F32),
        compiler_params=_params(1),
        name="ffn_down",
    )(act, w_bf, x2, final_g.reshape(1, d))


def _cmp_kernel(kv_ref, pos_ref, w1_ref, w2_ref, o_ref):
    ngrp = kv_ref.shape[0] // CMP_STRIDE
    lo = hi = None
    for p in range(CMP_STRIDE):
        x = kv_ref[pl.ds(p, ngrp, stride=CMP_STRIDE), :]
        a = jnp.dot((x + pos_ref[p:p + 1, :]).astype(BF16), w1_ref[p], preferred_element_type=F32)
        b = jnp.dot((x + pos_ref[CMP_STRIDE + p:CMP_STRIDE + p + 1, :]).astype(BF16), w1_ref[CMP_STRIDE + p],
                    preferred_element_type=F32)
        lo = a if lo is None else lo + a
        hi = b if hi is None else hi + b
    hid = jax.nn.gelu(lo + pltpu.roll(hi, ngrp - 1, 0))
    o_ref[0] = jnp.dot(hid.astype(BF16), w2_ref[...], preferred_element_type=F32)


def _nsa_compress(proj, pos_cat, w1_cat, w2_cat, batch, seq):
    ngrp = seq // CMP_STRIDE
    return pl.pallas_call(
        _cmp_kernel,
        grid=(batch,),
        in_specs=[pl.BlockSpec((seq, LANES), lambda i: (i, _OFF["a_kcvc"] // LANES)),
                  pl.BlockSpec(pos_cat.shape, lambda i: (0, 0)),
                  pl.BlockSpec(w1_cat.shape, lambda i: (0, 0, 0)),
                  pl.BlockSpec(w2_cat.shape, lambda i: (0, 0))],
        out_specs=pl.BlockSpec((1, ngrp, LANES), lambda i: (i, 0, 0)),
        out_shape=jax.ShapeDtypeStruct((batch, ngrp, LANES), F32),
        compiler_params=_params(1),
        name="nsa_compress",
    )(proj, pos_cat, w1_cat, w2_cat)


def _stack_heads_t(x_t, n_heads, hd):
    z = jnp.zeros((LANES - hd, x_t.shape[1]), F32)
    return jnp.concatenate(
        [jnp.concatenate([x_t[hd * h:hd * (h + 1), :], z], axis=0) for h in range(n_heads)], axis=1)


def _nsa_kernel(q_ref, g_ref, ksvs_ref, kwvw_ref, kvc_ref, c_ref, s1_ref, s2_ref, ovt_ref, cmpb_ref, winb_ref,
                caub_ref, o_ref, ks_scr, vst_scr, kw_scr, vwt_scr, kc_scr, vct_scr, blk_scr, *, n_top):
    i = pl.program_id(1)
    H = NSA_HEADS
    QW = q_ref.shape[0]
    HW = H * QW
    seq = ks_scr.shape[0]
    lane_q = lax.broadcasted_iota(I32, (QW, LANES), 1)

    @pl.when(i == 0)
    def _init():
        ks_scr[...] = jnp.zeros_like(ks_scr)
        kw_scr[...] = jnp.zeros_like(kw_scr)
        vst_scr[...] = jnp.zeros_like(vst_scr)
        vwt_scr[...] = jnp.zeros_like(vwt_scr)
        kvc = kvc_ref[0]
        lane_c = lax.broadcasted_iota(I32, kvc.shape, 1)
        kc_scr[...] = jnp.where(lane_c < HEAD_DIM, kvc, 0.0).astype(BF16)
        vct_scr[...] = kvc.T[HEAD_DIM:2 * HEAD_DIM, :].astype(BF16)

    c, s1, s2 = c_ref[...], s1_ref[...], s2_ref[...]
    r0 = pl.multiple_of(i * QW, QW)
    for src, kdst, vdst in ((ksvs_ref, ks_scr, vst_scr), (kwvw_ref, kw_scr, vwt_scr)):
        kv = src[...]
        kr = _rope(kv, c, s1, s2, ROPE_DIM // 2)
        kdst[pl.ds(r0, QW), :] = jnp.where(lane_q < HEAD_DIM, kr, 0.0).astype(BF16)
        vdst[:, pl.ds(r0, QW)] = kv.T[HEAD_DIM:2 * HEAD_DIM, :].astype(BF16)

    q = q_ref[...] * (SCALE * LOG2E)
    qr = _rope_wide(q, c, s1, s2, ROPE_DIM // 2)
    q_st = _stack_heads_t(q.T, H, HEAD_DIM).astype(BF16)
    qr_st = _stack_heads_t(qr.T, H, HEAD_DIM).astype(BF16)
    tile_h = lambda b: jnp.concatenate([b] * H, axis=1)

    ncp = kc_scr.shape[0]
    cshift = cmpb_ref.shape[0] - ncp
    cb = cmpb_ref[pl.ds(pl.multiple_of(cshift - (QW // CMP_STRIDE) * i, 8), ncp), :]
    s_c = jnp.dot(kc_scr[...], q_st, preferred_element_type=F32) + tile_h(cb)
    e_c, _, den_c = _softmax2_cols(s_c)
    t_row = i * QW + (lax.broadcasted_iota(I32, (1, HW), 1) & (QW - 1))
    r_c = jnp.where(t_row >= CMP_BLOCK - 1, 1.0 / den_c, 0.0)
    o_cmp = jnp.dot(vct_scr[...], e_c.astype(BF16), preferred_element_type=F32) * r_c
    psum = e_c[:, 0:QW] * r_c[:, 0:QW]
    for h in range(1, H):
        psum = psum + e_c[:, QW * h:QW * (h + 1)] * r_c[:, QW * h:QW * (h + 1)]

    hi = psum.astype(BF16)
    r1 = psum - hi.astype(F32)
    mid = r1.astype(BF16)
    lo = (r1 - mid.astype(F32)).astype(BF16)
    ovt = ovt_ref[...]
    imp = (jnp.dot(ovt, hi, preferred_element_type=F32) + jnp.dot(ovt, mid, preferred_element_type=F32)
           + jnp.dot(ovt, lo, preferred_element_type=F32))
    n_slc = imp.shape[0]
    jb = lax.broadcasted_iota(I32, (n_slc, QW), 0)
    tq = i * QW + lax.broadcasted_iota(I32, (n_slc, QW), 1)
    cur = tq >> (SLC_BLOCK.bit_length() - 1)
    forced = (jb == 0) | (jb == cur) | (jb == cur - 1)
    val = jnp.where(forced, FORCE, jnp.where(jb <= cur, imp, NEG))
    blk_scr[...] = val

    rows_per_step = QW // SLC_BLOCK

    def rank_body(g, rank):
        j0 = g * rows_per_step
        for r in range(rows_per_step):
            row = blk_scr[pl.ds(j0 + r, 1), :]
            rank = rank + jnp.where(row > val, 1, jnp.where(row == val, jnp.where(jb > j0 + r, 1, 0), 0))
        return rank

    rank = lax.fori_loop(0, i + 1, rank_body, jnp.zeros((n_slc, QW), I32))
    blk_scr[...] = jnp.where(rank < n_top, jnp.where(val > 0.5 * NEG, 0.0, NEG), NEG)

    kmax = caub_ref.shape[0] // 2

    def slc_chunk(size, base):
        def body(ci, carry):
            m, l, acc = carry
            k0 = pl.multiple_of(base + ci * size, KC)
            ahead = jnp.minimum(i * QW - k0, kmax)
            b0 = k0 // SLC_BLOCK
            rows = [jnp.broadcast_to(blk_scr[pl.ds(b0 + r, 1), :], (SLC_BLOCK, QW)) for r in range(size // SLC_BLOCK)]
            bias = jnp.concatenate(rows, axis=0) + caub_ref[pl.ds(pl.multiple_of(kmax - ahead, QB), size), :]
            s = jnp.dot(ks_scr[pl.ds(k0, size), :], qr_st, preferred_element_type=F32) + tile_h(bias)
            m_new = jnp.maximum(m, jnp.max(s, axis=0, keepdims=True))
            alpha = jnp.exp2(m - m_new)
            e = jnp.exp2(s - m_new)
            l = alpha * l + jnp.sum(e, axis=0, keepdims=True)
            acc = alpha * acc + jnp.dot(vst_scr[:, pl.ds(k0, size)], e.astype(BF16), preferred_element_type=F32)
            return m_new, l, acc
        return body

    n_chunks = (i * QW + QW + KC - 1) // KC
    n_long = n_chunks // ATT_LONG
    init = (jnp.full((1, HW), NEG, F32), jnp.zeros((1, HW), F32), jnp.zeros((HEAD_DIM, HW), F32))
    state = lax.fori_loop(0, n_long, slc_chunk(ATT_LONG * KC, 0), init)
    m_s, l_s, acc_s = lax.fori_loop(0, n_chunks - n_long * ATT_LONG, slc_chunk(KC, n_long * (ATT_LONG * KC)), state)
    o_slc = jnp.where(m_s > 0.5 * NEG, acc_s / l_s, 0.0)

    span = min(WIN + QW, seq)
    w0 = pl.multiple_of(jnp.maximum(i * QW + QW - span, 0), QB)
    wb = winb_ref[pl.ds(pl.multiple_of(span - QW - (i * QW - w0), QB), span), :]
    s_w = jnp.dot(kw_scr[pl.ds(w0, span), :], qr_st, preferred_element_type=F32) + tile_h(wb)
    e_w, _, den_w = _softmax2_cols(s_w)
    o_win = jnp.dot(vwt_scr[:, pl.ds(w0, span)], e_w.astype(BF16), preferred_element_type=F32) * (1.0 / den_w)

    gt = jax.nn.sigmoid(g_ref[...].T)
    outs = []
    for h in range(H):
        sl = slice(QW * h, QW * (h + 1))
        outs.append(gt[3 * h:3 * h + 1, :] * o_cmp[:, sl] + gt[3 * h + 1:3 * h + 2, :] * o_slc[:, sl]
                    + gt[3 * h + 2:3 * h + 3, :] * o_win[:, sl])
    outs.append(jnp.zeros((o_ref.shape[1] - H * HEAD_DIM, QW), F32))
    o_ref[...] = jnp.concatenate(outs, axis=0).T.astype(BF16)


def _nsa(proj, kvcmp, tabs, ovt, batch, seq):
    qw = NSA_QW
    nb = seq // qw
    n_slc = seq // SLC_BLOCK
    ncp = kvcmp.shape[1]
    col = lambda name, w: _OFF[name] // w
    qmap = lambda cidx: (lambda b, i: (b * nb + i, cidx))
    tmap = lambda k: (lambda b, i: (k, i, 0))
    kern = functools.partial(_nsa_kernel, n_top=min(SLC_TOPN, n_slc))
    cmpb, winb, caub = _cmp_bias_table(seq, qw), _win_bias_table(seq, qw), _causal_bias_table(qw)
    whole = lambda a: pl.BlockSpec(a.shape, lambda b, i: (0, 0))
    return pl.pallas_call(
        kern,
        grid=(batch, nb),
        in_specs=[pl.BlockSpec((qw, 384), qmap(col("a_q", 384))),
                  pl.BlockSpec((qw, LANES), qmap(col("a_g", LANES))),
                  pl.BlockSpec((qw, LANES), qmap(col("a_ksvs", LANES))),
                  pl.BlockSpec((qw, LANES), qmap(col("a_kwvw", LANES))),
                  pl.BlockSpec((1, ncp, LANES), lambda b, i: (b, 0, 0)),
                  pl.BlockSpec((None, qw, LANES), tmap(0)),
                  pl.BlockSpec((None, qw, LANES), tmap(1)),
                  pl.BlockSpec((None, qw, LANES), tmap(2)),
                  whole(ovt), whole(cmpb), whole(winb), whole(caub)],
        out_specs=pl.BlockSpec((qw, 384), lambda b, i: (b * nb + i, 0)),
        out_shape=jax.ShapeDtypeStruct((batch * seq, 384), BF16),
        scratch_shapes=[pltpu.VMEM((seq, LANES), BF16), pltpu.VMEM((HEAD_DIM, seq), BF16),
                        pltpu.VMEM((seq, LANES), BF16), pltpu.VMEM((HEAD_DIM, seq), BF16),
                        pltpu.VMEM((ncp, LANES), BF16), pltpu.VMEM((HEAD_DIM, ncp), BF16),
                        pltpu.VMEM((n_slc, qw), F32)],
        compiler_params=_params(2),
        name="nsa",
    )(proj, proj, proj, proj, kvcmp, tabs, tabs, tabs, ovt, cmpb, winb, caub)


SEARCH_BISECT_FROM = 24
SEARCH_MAX_PASSES = 64
SEARCH_KEPT_LIMIT = 6
SEARCH_BLIND_PASSES = 12
SEARCH_SNAP_KEYS = 3
INT_MAX = 2 ** 31 - 1
FLT_MIN_NORMAL = 1.17549435e-38
ZERO_BAND_END = 1 << 23


def _tree_rows(x, op):
    parts = [x[r:r + 8, :] for r in range(0, x.shape[0], 8)]
    while len(parts) > 1:
        nxt = [op(parts[a], parts[a + 1]) for a in range(0, len(parts) - 1, 2)]
        if len(parts) % 2:
            nxt.append(parts[-1])
        parts = nxt
    return parts[0]


def _f2key(v):
    bits = lax.bitcast_convert_type(v, I32)
    return bits ^ ((bits >> 31) & 0x7FFFFFFF)


def _key2f(k):
    return lax.bitcast_convert_type(k ^ ((k >> 31) & 0x7FFFFFFF), F32)


def _dsa_kernel(q_ref, iq_ref, ikw_ref, kv_ref, c_ref, s1_ref, s2_ref, ci_ref, si1_ref, si2_ref, o_ref,
                k_scr, vt_scr, ik_scr, key_scr, *, top, idx_bits):
    i = pl.program_id(1)
    H = DSA_HEADS
    QW = q_ref.shape[0]
    HW = H * QW
    seq = key_scr.shape[0]
    lane_q = lax.broadcasted_iota(I32, (QW, LANES), 1)

    @pl.when(i == 0)
    def _init():
        k_scr[...] = jnp.zeros_like(k_scr)
        vt_scr[...] = jnp.zeros_like(vt_scr)
        ik_scr[...] = jnp.zeros_like(ik_scr)

    c, s1, s2 = c_ref[...], s1_ref[...], s2_ref[...]
    ci, si1, si2 = ci_ref[...], si1_ref[...], si2_ref[...]
    r0 = pl.multiple_of(i * QW, QW)
    kv = kv_ref[...]
    k_scr[pl.ds(r0, QW), :] = jnp.where(lane_q < HEAD_DIM, _rope(kv, c, s1, s2, ROPE_DIM // 2), 0.0).astype(BF16)
    vt_scr[:, pl.ds(r0, QW)] = kv.T[HEAD_DIM:2 * HEAD_DIM, :].astype(BF16)
    ikw = ikw_ref[...]
    ik_scr[pl.ds(r0, QW), :] = jnp.where(lane_q < IDX_DIM, _rope(ikw, ci, si1, si2, IDX_ROPE_DIM // 2), 0.0).astype(BF16)

    qr = _rope_wide(q_ref[...] * (SCALE * LOG2E), c, s1, s2, ROPE_DIM // 2)
    qr_st = _stack_heads_t(qr.T, H, HEAD_DIM).astype(BF16)
    iqr = _rope_wide(iq_ref[...], ci, si1, si2, IDX_ROPE_DIM // 2)
    iq_st = _stack_heads_t(iqr.T, IDX_HEADS, IDX_DIM).astype(BF16)
    wt = ikw.T * ((IDX_DIM ** -0.5) * (IDX_HEADS ** -0.5))
    w_st = jnp.concatenate([wt[IDX_DIM + h:IDX_DIM + h + 1, :] for h in range(IDX_HEADS)], axis=1)

    n_chunks = (i * QW + QW + KC - 1) // KC
    k_io = lax.broadcasted_iota(I32, (KC, QW), 0)
    tq = i * QW + lax.broadcasted_iota(I32, (KC, QW), 1)

    sub_io = lax.broadcasted_iota(I32, (QB, QW), 0)
    sub_tq = i * QW + lax.broadcasted_iota(I32, (QB, QW), 1)
    zkey0 = seq - sub_io

    def score_body(cidx, carry):
        mx, mn = carry
        k0 = pl.multiple_of(cidx * KC, KC)
        for sb in range(KC // QB):
            kb = pl.multiple_of(k0 + sb * QB, QB)
            d = jnp.dot(ik_scr[pl.ds(kb, QB), :], iq_st, preferred_element_type=F32)
            r = jnp.maximum(d, 0.0) * w_st
            sc = r[:, 0:QW]
            for h in range(1, IDX_HEADS):
                sc = sc + r[:, QW * h:QW * (h + 1)]
            key = jnp.where(jnp.abs(sc) < FLT_MIN_NORMAL, zkey0 - kb, _f2key(sc))
            key_scr[pl.ds(kb, QB), :] = jnp.where(sub_io + kb <= sub_tq, key, INT_MIN)
            mx = jnp.maximum(mx, _tree_rows(key, jnp.maximum))
            mn = jnp.minimum(mn, _tree_rows(key, jnp.minimum))
        return mx, mn

    mx8, mn8 = lax.fori_loop(0, n_chunks, score_body,
                             (jnp.full((8, QW), INT_MIN, I32), jnp.full((8, QW), INT_MAX, I32)))

    def count(pred):
        def body(cidx, acc):
            k0 = pl.multiple_of(cidx * KC, KC)
            return acc + _tree_rows(pred(key_scr[pl.ds(k0, KC), :], k_io + k0), jnp.add)
        acc8 = lax.fori_loop(0, n_chunks, body, jnp.zeros((8, QW), F32))
        return jnp.sum(acc8, axis=0, keepdims=True)

    topf = float(top)
    lo0 = functools.reduce(jnp.minimum, [mn8[r:r + 1, :] for r in range(8)])
    hi0 = functools.reduce(jnp.maximum, [mx8[r:r + 1, :] for r in range(8)]) + 1
    n_causal = (i * QW + 1 + lax.broadcasted_iota(I32, (1, QW), 1)).astype(F32)
    log_top = float(np.log(top))

    def in_zero_band(lo, hi):
        return jnp.where(lo >= 1, jnp.where(hi <= ZERO_BAND_END, 1.0, 0.0), 0.0) > 0.5

    def open_keys(lo, hi, c_lo, c_hi):
        return jnp.max(jnp.where(c_lo > topf, jnp.where(hi > lo + 1, c_lo - c_hi, 0.0), 0.0)).astype(I32)

    def search_cond(c):
        return jnp.logical_and(c[0] < SEARCH_MAX_PASSES, c[1] > SEARCH_SNAP_KEYS)

    def search_pass(it, st):
        lo, hi, c_lo, c_hi, kept, w_lo, w_hi = st
        f_lo = (jnp.log(c_lo) - log_top) * w_lo
        f_hi = (log_top - jnp.log(jnp.maximum(c_hi, 0.5))) * w_hi
        frac = f_lo / (f_lo + f_hi)
        v_lo, v_hi = _key2f(lo), _key2f(hi)
        cand = _f2key(v_lo + frac * (v_hi - v_lo))
        lo_f, hi_f = lo.astype(F32), hi.astype(F32)
        cand = jnp.where(in_zero_band(lo, hi), (lo_f + frac * (hi_f - lo_f)).astype(I32), cand)
        key_mid = (lo >> 1) + (hi >> 1) + (lo & hi & 1)
        cand = jnp.where(jnp.abs(kept) >= SEARCH_KEPT_LIMIT, key_mid, cand)
        cand = jnp.where(it >= SEARCH_BISECT_FROM, key_mid, cand)
        cand = jnp.where(it == 0, 1, jnp.where(it == 1, ZERO_BAND_END, cand))
        cand = jnp.minimum(jnp.maximum(cand, lo + 1), hi - 1)
        cnt = count(lambda key, kpos: jnp.where(key >= cand, 1.0, 0.0))
        active = jnp.where(c_lo > topf, jnp.where(hi > lo + 1, 1.0, 0.0), 0.0) > 0.5
        up = jnp.where(active, jnp.where(cnt >= topf, 1.0, 0.0), 0.0) > 0.5
        dn = jnp.where(active, jnp.where(cnt >= topf, 0.0, 1.0), 0.0) > 0.5
        w_hi = jnp.where(up, jnp.where(kept < 0, 0.5 * w_hi, 1.0), jnp.where(dn, 1.0, w_hi))
        w_lo = jnp.where(dn, jnp.where(kept > 0, 0.5 * w_lo, 1.0), jnp.where(up, 1.0, w_lo))
        kept = jnp.where(up, jnp.where(kept < 0, kept - 1, -1), jnp.where(dn, jnp.where(kept > 0, kept + 1, 1), kept))
        lo, c_lo = jnp.where(up, cand, lo), jnp.where(up, cnt, c_lo)
        hi, c_hi = jnp.where(dn, cand, hi), jnp.where(dn, cnt, c_hi)
        hi = jnp.where(in_zero_band(lo, hi), jnp.minimum(hi, seq + 1), hi)
        return lo, hi, c_lo, c_hi, kept, w_lo, w_hi

    def search_body(c):
        st = search_pass(c[0], c[2])
        return c[0] + 1, open_keys(*st[:4]), st

    zero_f, one_f = jnp.zeros((1, QW), F32), jnp.ones((1, QW), F32)
    st0 = (lo0, hi0, n_causal, zero_f, jnp.zeros((1, QW), I32), one_f, one_f)
    n_blind = jnp.where(i * QW + QW > top, SEARCH_BLIND_PASSES, 0)
    st1 = lax.fori_loop(0, n_blind, search_pass, st0)
    _, _, st2 = lax.while_loop(search_cond, search_body, (n_blind, open_keys(*st1[:4]), st1))

    def snap_body(c):
        lo, hi, c_lo, c_hi = c[1]

        def below_hi(cidx, acc):
            k0 = pl.multiple_of(cidx * KC, KC)
            key = key_scr[pl.ds(k0, KC), :]
            return jnp.maximum(acc, _tree_rows(jnp.where(key < hi, key, INT_MIN), jnp.maximum))
        top8 = lax.fori_loop(0, n_chunks, below_hi, jnp.full((8, QW), INT_MIN, I32))
        cand = functools.reduce(jnp.maximum, [top8[r:r + 1, :] for r in range(8)])
        cnt = count(lambda key, kpos: jnp.where(key >= cand, 1.0, 0.0))
        active = jnp.where(c_lo > topf, jnp.where(hi > lo + 1, 1.0, 0.0), 0.0) > 0.5
        up = jnp.where(active, jnp.where(cnt >= topf, 1.0, 0.0), 0.0) > 0.5
        dn = jnp.where(active, jnp.where(cnt >= topf, 0.0, 1.0), 0.0) > 0.5
        lo, c_lo = jnp.where(up, cand, lo), jnp.where(up, cnt, c_lo)
        hi, c_hi = jnp.where(up, cand + 1, jnp.where(dn, cand, hi)), jnp.where(dn, cnt, c_hi)
        return open_keys(lo, hi, c_lo, c_hi), (lo, hi, c_lo, c_hi)

    _, (thr, _, c_lo, c_hi) = lax.while_loop(lambda c: c[0] > 0, snap_body, (open_keys(*st2[:4]), st2[:4]))

    tied = c_lo > topf
    need1 = (topf - 1.0) - c_hi
    all_pos = jnp.full((1, QW), 2 ** idx_bits - 1, I32)

    def tie_search():
        def tie_body(bi, cur):
            cand = cur | lax.shift_left(jnp.int32(1), idx_bits - 1 - bi)
            cnt = count(lambda key, kpos: jnp.where(key == thr, jnp.where(kpos < cand, 1.0, 0.0), 0.0))
            return jnp.where(cnt <= need1, cand, cur)
        return lax.fori_loop(0, idx_bits, tie_body, jnp.zeros((1, QW), I32))

    any_tied = jnp.max(jnp.where(tied, 1.0, 0.0)) > 0.5
    last = jnp.where(tied, lax.cond(any_tied, tie_search, lambda: all_pos), all_pos)

    half_neg = int(np.float32(0.5 * NEG).view(np.int32))
    key_floor = half_neg ^ 0x7FFFFFFF

    def att_chunk(size, base):
        pos_io = lax.broadcasted_iota(I32, (size, QW), 0)

        def body(cidx, state):
            m, l, acc = state
            k0 = pl.multiple_of(base + cidx * size, KC)
            key = key_scr[pl.ds(k0, size), :]
            kpos = pos_io + k0
            bias = jnp.where(key > thr, 0.0, jnp.where(key == thr, jnp.where(kpos <= last, 0.0, NEG), NEG))
            bias = jnp.where(key > key_floor, bias, NEG)
            s = (jnp.dot(k_scr[pl.ds(k0, size), :], qr_st, preferred_element_type=F32)
                 + jnp.concatenate([bias] * H, axis=1))
            m_new = jnp.maximum(m, jnp.max(s, axis=0, keepdims=True))
            alpha = jnp.exp2(m - m_new)
            e = jnp.exp2(s - m_new)
            l = alpha * l + jnp.sum(e, axis=0, keepdims=True)
            acc = alpha * acc + jnp.dot(vt_scr[:, pl.ds(k0, size)], e.astype(BF16), preferred_element_type=F32)
            return m_new, l, acc
        return body

    init = (jnp.full((1, HW), NEG, F32), jnp.zeros((1, HW), F32), jnp.zeros((HEAD_DIM, HW), F32))
    n_long = n_chunks // ATT_LONG
    state = lax.fori_loop(0, n_long, att_chunk(ATT_LONG * KC, 0), init)
    m_a, l_a, acc_a = lax.fori_loop(0, n_chunks - n_long * ATT_LONG, att_chunk(KC, n_long * (ATT_LONG * KC)), state)
    o = jnp.where(m_a > 0.5 * NEG, acc_a / l_a, 0.0)
    outs = [o[:, QW * h:QW * (h + 1)] for h in range(H)]
    outs.append(jnp.zeros((o_ref.shape[1] - H * HEAD_DIM, QW), F32))
    o_ref[...] = jnp.concatenate(outs, axis=0).T.astype(BF16)


def _dsa(proj, tabs, tabs_i, batch, seq):
    qw = DSA_QW
    nb = seq // qw
    col = lambda name, w: _OFF[name] // w
    qmap = lambda cidx: (lambda b, i: (b * nb + i, cidx))
    tmap = lambda k: (lambda b, i: (k, i, 0))
    top = min(DSA_TOPK_MAX, seq // 4)
    kern = functools.partial(_dsa_kernel, top=top, idx_bits=int(seq).bit_length())
    return pl.pallas_call(
        kern,
        grid=(batch, nb),
        in_specs=[pl.BlockSpec((qw, 384), qmap(col("b_q", 384))),
                  pl.BlockSpec((qw, 256), qmap(col("b_iq", 256))),
                  pl.BlockSpec((qw, LANES), qmap(col("b_ikiw", LANES))),
                  pl.BlockSpec((qw, LANES), qmap(col("b_kv", LANES))),
                  pl.BlockSpec((None, qw, LANES), tmap(0)),
                  pl.BlockSpec((None, qw, LANES), tmap(1)),
                  pl.BlockSpec((None, qw, LANES), tmap(2)),
                  pl.BlockSpec((None, qw, LANES), tmap(0)),
                  pl.BlockSpec((None, qw, LANES), tmap(1)),
                  pl.BlockSpec((None, qw, LANES), tmap(2))],
        out_specs=pl.BlockSpec((qw, 384), lambda b, i: (b * nb + i, 0)),
        out_shape=jax.ShapeDtypeStruct((batch * seq, 384), BF16),
        scratch_shapes=[pltpu.VMEM((seq, LANES), BF16), pltpu.VMEM((HEAD_DIM, seq), BF16),
                        pltpu.VMEM((seq, LANES), BF16), pltpu.VMEM((seq, qw), I32)],
        compiler_params=_params(2),
        name="dsa",
    )(proj, proj, proj, proj, tabs, tabs, tabs, tabs_i, tabs_i, tabs_i)


DIL_MB = max(d for _, d in DIL_PAIRS) * QB
DIL_VMEM_LIMIT = 56 * 1024 * 1024
DIL_UNROLL = 4


def _dil_layout(g):
    dil = DIL_PAIRS[g][1]
    per = DIL_MB // dil
    return dil, per, per + QB


def _dil_bias_table():
    u = np.arange(2 * QB)[:, None]
    diff = QB + np.arange(LANES)[None, :] - u
    ok = (diff >= 0) & (diff <= QB)
    return _bias_table(np.concatenate([ok, ok & (u >= QB)], axis=0))


def _dil_kernel(*refs):
    G, HG = len(DIL_PAIRS), DIL_HEADS_PER_GROUP
    q_refs, k_refs, v_refs = refs[0:G], refs[G:2 * G], refs[2 * G:3 * G]
    c_ref, s1_ref, s2_ref, bias_ref, o_ref = refs[3 * G:3 * G + 5]
    kds, vds, ogs, lss = (refs[3 * G + 5 + n * G:3 * G + 5 + (n + 1) * G] for n in range(4))
    step = pl.program_id(1)
    half = ROPE_DIM // 2

    @pl.when(step == 0)
    def _zero():
        for g in range(G):
            kds[g][...] = jnp.zeros_like(kds[g])
            vds[g][...] = jnp.zeros_like(vds[g])

    @pl.when(step > 0)
    def _carry_halo():
        for g in range(G):
            dil, per, stride = _dil_layout(g)
            for r in range(dil):
                b0 = r * stride
                kds[g][b0:b0 + QB, :] = kds[g][b0 + per:b0 + per + QB, :]
                vds[g][:, b0:b0 + QB] = vds[g][:, b0 + per:b0 + per + QB]

    for g in range(G):
        dil, per, stride = _dil_layout(g)
        piece = min(per, KC)
        for r in range(dil):
            for p0 in range(0, per, piece):
                rows = pl.ds(r + dil * p0, piece, stride=dil)
                kr = _rope(k_refs[g][rows, :], c_ref[rows, :], s1_ref[rows, :], s2_ref[rows, :], half)
                d0 = r * stride + QB + p0
                kds[g][d0:d0 + piece, :] = kr.astype(BF16)
                vds[g][:, d0:d0 + piece] = v_refs[g][rows, :].T.astype(BF16)

    zero = jnp.zeros((HEAD_DIM, LANES), F32)
    for g in range(G):
        dil, per, stride = _dil_layout(g)
        nblk = per // QB

        def block(jb, _, r=0, g=g, dil=dil, stride=stride):
            p0 = pl.multiple_of(dil * QB * jb, dil * QB)
            win = pl.ds(p0, dil * QB)
            rows = pl.ds(r, QB, stride=dil)
            q = _rope(q_refs[g].at[win, :][rows, :] * (SCALE * LOG2E), c_ref.at[win, :][rows, :],
                      s1_ref.at[win, :][rows, :], s2_ref.at[win, :][rows, :], half)
            qt = q.T
            q2 = jnp.concatenate([jnp.concatenate([qt[:HEAD_DIM], zero], axis=0),
                                  jnp.concatenate([zero, qt[HEAD_DIM:]], axis=0)], axis=1).astype(BF16)
            kb = pl.multiple_of(r * stride + QB * jb, QB)
            first = jnp.logical_and(step == 0, jb == 0)
            bias = bias_ref[pl.ds(pl.multiple_of(jnp.where(first, 2 * QB, 0), QB), 2 * QB), :]
            s = (jnp.dot(kds[g][pl.ds(kb, 2 * QB), :], q2, preferred_element_type=F32)
                 + jnp.concatenate([bias] * HG, axis=1))
            e, m, den = _softmax2_cols(s)
            lse = m + jnp.log2(den)
            rden = 1.0 / den
            parts, lparts = [], []
            for hg in range(HG):
                sl = slice(LANES * hg, LANES * (hg + 1))
                vt = vds[g][HEAD_DIM * hg:HEAD_DIM * (hg + 1), pl.ds(kb, 2 * QB)]
                parts.append(jnp.dot(vt, e[:, sl].astype(BF16), preferred_element_type=F32) * rden[:, sl])
                lparts.append(jnp.broadcast_to(lse[:, sl], (HEAD_DIM, LANES)))
            tile = jnp.concatenate(parts + lparts, axis=0).T
            ogs[g].at[win, :][rows, :] = tile[:, :LANES]
            lss[g].at[win, :][rows, :] = tile[:, LANES:]
            return 0

        for r in range(dil):
            lax.fori_loop(0, nblk, functools.partial(block, r=r), 0, unroll=min(nblk, DIL_UNROLL))

    def mix(ti, _):
        rows = pl.ds(pl.multiple_of(ti * KC, KC), KC)
        ls = [lss[g][rows, :] for g in range(G)]
        mx = functools.reduce(jnp.maximum, ls)
        ex = [jnp.exp2(x - mx) for x in ls]
        rtot = 1.0 / functools.reduce(lambda a, b: a + b, ex)
        o_ref[rows, :] = jnp.concatenate([ex[g] * rtot * ogs[g][rows, :] for g in range(G)], axis=1).astype(BF16)
        return 0

    lax.fori_loop(0, DIL_MB // KC, mix, 0)


def _dil(proj, tabs, batch, seq):
    assert seq % DIL_MB == 0 and all(w // d == QB for w, d in DIL_PAIRS)
    nm = seq // DIL_MB
    G = len(DIL_PAIRS)
    w = DIL_HEADS * HEAD_DIM
    gmap = lambda name, g: (lambda b, j: (b * nm + j, _OFF[name] // LANES + g))
    tmap = lambda k: (lambda b, j: (k, j, 0))
    bias = _dil_bias_table()
    rows = [_dil_layout(g)[0] * _dil_layout(g)[2] for g in range(G)]
    return pl.pallas_call(
        _dil_kernel,
        grid=(batch, nm),
        in_specs=([pl.BlockSpec((DIL_MB, LANES), gmap(name, g)) for name in ("c_q", "c_k", "c_v") for g in range(G)]
                  + [pl.BlockSpec((None, DIL_MB, LANES), tmap(k)) for k in range(3)]
                  + [pl.BlockSpec(bias.shape, lambda b, j: (0, 0))]),
        out_specs=pl.BlockSpec((DIL_MB, w), lambda b, j: (b * nm + j, 0)),
        out_shape=jax.ShapeDtypeStruct((batch * seq, w), BF16),
        scratch_shapes=([pltpu.VMEM((n, LANES), BF16) for n in rows] + [pltpu.VMEM((LANES, n), BF16) for n in rows]
                        + [pltpu.VMEM((DIL_MB, LANES), F32)] * (2 * G)),
        compiler_params=_params(2, DIL_VMEM_LIMIT),
        name="dilated",
    )(*([proj] * (3 * G)), tabs, tabs, tabs, bias)


def _overlap_t(seq):
    n_cmp_pad = seq // CMP_STRIDE
    n_slc = seq // SLC_BLOCK
    c_start = np.arange(n_cmp_pad) * CMP_STRIDE
    s_start = np.arange(n_slc) * SLC_BLOCK
    ov = ((c_start[None, :] < s_start[:, None] + SLC_BLOCK) & (c_start[None, :] + CMP_BLOCK > s_start[:, None]))
    return jnp.asarray(ov.astype(np.float32), dtype=BF16)


def _layer(x2, batch, seq, layer, norm1_g, w_in, cmp_pos, cmp_w1, cmp_w2, w_out, norm2_g, w_up_bf, conv_w, conv_b,
           w_down_bf, final_g, final_norm, tabs, tabs_i, ovt):
    n = batch * seq
    proj = _inproj(x2, norm1_g, _regroup_w_in(w_in, layer))

    w1 = cmp_w1.reshape(2, CMP_BLOCK, HEAD_DIM, CMP_HIDDEN)
    z1 = jnp.zeros_like(w1[0])
    w1_cat = jnp.concatenate([jnp.concatenate([w1[0], z1], axis=2),
                              jnp.concatenate([z1, w1[1]], axis=2)], axis=1).astype(BF16)
    z2 = jnp.zeros((CMP_HIDDEN, HEAD_DIM), cmp_w2.dtype)
    w2_cat = jnp.concatenate([jnp.concatenate([cmp_w2[0], z2], axis=1),
                              jnp.concatenate([z2, cmp_w2[1]], axis=1)], axis=0).astype(BF16)
    pos_cat = jnp.concatenate([cmp_pos[0], cmp_pos[1]], axis=1)
    kvcmp = _nsa_compress(proj, pos_cat, w1_cat, w2_cat, batch, seq)

    o_a = _nsa(proj, kvcmp, tabs, ovt, batch, seq)
    o_b = _dsa(proj, tabs, tabs_i, batch, seq)
    o_c = _dil(proj, tabs, batch, seq)

    hp = NSA_HEADS * HEAD_DIM
    zrow = jnp.zeros((384 - hp, D_MODEL), w_out.dtype)
    w_out_pad = jnp.concatenate([w_out[0:hp], zrow, w_out[hp:2 * hp], zrow, w_out[2 * hp:]], axis=0).astype(BF16)
    x2 = _outproj(o_a, o_b, o_c, w_out_pad, x2)

    act = _ffn_up(x2, norm2_g, w_up_bf, layer, conv_w, conv_b, seq)
    return _ffn_down(act, w_down_bf, layer, x2, final_g, final_norm)


def kernel(x, norm1_g, w_in, cmp_pos, cmp_w1, cmp_w2, w_out, norm2_g, w_up, conv_w, conv_b, w_down, final_g):
    batch, seq, d = x.shape
    depth = w_in.shape[0]
    tabs = _rope_lane_tables(seq, HEAD_DIM, ROPE_DIM)
    tabs_i = _rope_lane_tables(seq, IDX_DIM, IDX_ROPE_DIM)
    ovt = _overlap_t(seq)
    x2 = x.reshape(batch * seq, d)
    w_up_bf, w_down_bf = w_up.astype(BF16), w_down.astype(BF16)
    for li in range(depth):
        x2 = _layer(x2, batch, seq, li, norm1_g[li], w_in, cmp_pos[li], cmp_w1[li], cmp_w2[li], w_out[li],
                    norm2_g[li], w_up_bf, conv_w[li], conv_b[li], w_down_bf, final_g, li == depth - 1,
                    tabs, tabs_i, ovt)
    return x2.reshape(batch, seq, d)
```

```python
import functools

import numpy as np
import jax
import jax.numpy as jnp
from jax import lax
from jax.experimental import pallas as pl
from jax.experimental.pallas import tpu as pltpu

F32 = jnp.float32
BF16 = jnp.bfloat16
I32 = jnp.int32

D_MODEL = 1024
HEAD_DIM = 64
ROPE_DIM = HEAD_DIM // 4
ROPE_THETA = 500000.0
NORM_EPS = 1e-6
SCALE = HEAD_DIM ** -0.5
LOG2E = 1.4426950408889634
NEG = -1e30
FORCE = 1e9
NSA_HEADS = 5
CMP_BLOCK = 32
CMP_STRIDE = 16
CMP_HIDDEN = 128
SLC_BLOCK = 64
SLC_TOPN = 16
WIN = 512
DSA_HEADS = 5
IDX_HEADS = 8
IDX_DIM = 32
IDX_ROPE_DIM = IDX_DIM // 4
DSA_TOPK_MAX = 256
DIL_PAIRS = ((128, 1), (512, 4), (2048, 16))
DIL_HEADS_PER_GROUP = 2
DIL_HEADS = len(DIL_PAIRS) * DIL_HEADS_PER_GROUP
D_FF = 2816
CONV_WIDTH = 3

LANES = 128
QB = 128
DSA_QW = 256
NSA_QW = 256
KC = 512
ONES_ROWS = 16
ATT_LONG = 2
ROW_TILE = 512
FFN_UP_ROW_TILE = 1024
HALO = 16
VMEM_LIMIT = 48 * 1024 * 1024
INT_MIN = -2 ** 31

_A0 = 0
_B0 = 719
_C0 = 1463
_SEGS = (
    ("c_q", _C0, 384, 384), ("c_k", _C0 + 384, 384, 384), ("c_v", _C0 + 768, 384, 384),
    ("a_q", _A0, 320, 384), ("b_q", _B0, 320, 384),
    ("a_kcvc", _A0 + 320, 128, 128),
    ("b_iq", _B0 + 448, 256, 256),
    ("a_ksvs", _A0 + 448, 128, 128), ("a_kwvw", _A0 + 576, 128, 128),
    ("a_g", _A0 + 704, 15, 128),
    ("b_kv", _B0 + 320, 128, 128),
    ("b_ikiw", _B0 + 704, 40, 128),
)
P_COLS = sum(s[3] for s in _SEGS)


def _seg_offsets():
    offs, o = {}, 0
    for name, _, _, pw in _SEGS:
        offs[name] = o
        o += pw
    return offs


_OFF = _seg_offsets()


def _regroup_kernel(w_ref, o_ref):
    for name, src, wdt, pw in _SEGS:
        dst = _OFF[name]
        o_ref[:, dst:dst + wdt] = w_ref[:, src:src + wdt].astype(BF16)
        if pw > wdt:
            o_ref[:, dst + wdt:dst + pw] = jnp.zeros((o_ref.shape[0], pw - wdt), BF16)


def _regroup_w_in(w, layer):
    _, d, n_in = w.shape
    rt = 256
    return pl.pallas_call(
        _regroup_kernel,
        grid=(d // rt,),
        in_specs=[pl.BlockSpec((None, rt, n_in), lambda i: (layer, i, 0))],
        out_specs=pl.BlockSpec((rt, P_COLS), lambda i: (i, 0)),
        out_shape=jax.ShapeDtypeStruct((d, P_COLS), BF16),
        compiler_params=_params(1),
        name="regroup_w_in",
    )(w)


def _rope_lane_tables(L, head_dim, rot_dim):
    half = rot_dim // 2
    inv = 1.0 / (ROPE_THETA ** (np.arange(0, rot_dim, 2, dtype=np.float32) / np.float32(rot_dim)))
    ang = np.arange(L, dtype=np.float32)[:, None] * inv[None, :]
    cos, sin = np.cos(ang).astype(np.float32), np.sin(ang).astype(np.float32)
    d = np.arange(LANES) % head_dim
    lo, hi = d < half, (d >= half) & (d < rot_dim)
    c = np.ones((L, LANES), np.float32)
    s1 = np.zeros((L, LANES), np.float32)
    s2 = np.zeros((L, LANES), np.float32)
    c[:, lo] = cos[:, d[lo]]
    c[:, hi] = cos[:, d[hi] - half]
    s1[:, lo] = -sin[:, d[lo]]
    s2[:, hi] = sin[:, d[hi] - half]
    return jnp.asarray(np.stack([c, s1, s2]))


def _rope(x, c, s1, s2, half):
    xp = pltpu.roll(x, LANES - half, 1)
    xm = pltpu.roll(x, half, 1)
    return x * c + xp * s1 + xm * s2


def _rope_wide(x, c, s1, s2, half):
    n = x.shape[1] // LANES
    return jnp.concatenate([_rope(x[:, LANES * s:LANES * (s + 1)], c, s1, s2, half) for s in range(n)], axis=1)


def _softmax2_cols(s):
    m = jnp.max(s, axis=0, keepdims=True)
    e = jnp.exp2(s - m)
    return e, m, jnp.sum(e, axis=0, keepdims=True)


def _bias_table(ok):
    return jnp.asarray(np.where(ok, 0.0, NEG).astype(np.float32))


def _cmp_bias_table(seq, qw):
    shift = (qw // CMP_STRIDE) * (seq // qw - 1)
    u = np.arange(seq // CMP_STRIDE + shift)[:, None]
    lane = np.arange(qw)[None, :]
    return _bias_table(CMP_STRIDE * (u - shift) + CMP_BLOCK - 1 <= lane)


def _win_bias_table(seq, qw):
    span = min(WIN + qw, seq)
    cmax = span - qw
    u = np.arange(span + cmax)[:, None]
    diff = cmax + np.arange(qw)[None, :] - u
    return _bias_table((diff >= 0) & (diff < WIN))


def _causal_bias_table(qw):
    kmax = ATT_LONG * KC
    u = np.arange(2 * kmax)[:, None]
    return _bias_table(u <= kmax + np.arange(qw)[None, :])


def _params(n_grid, vmem=VMEM_LIMIT):
    return pltpu.CompilerParams(dimension_semantics=("arbitrary",) * n_grid, vmem_limit_bytes=vmem)


def _rmsnorm_rows(x, g):
    return x * lax.rsqrt(jnp.mean(x * x, axis=-1, keepdims=True) + NORM_EPS) * g


def _inproj_kernel(x_ref, g_ref, w_ref, o_ref):
    hn = _rmsnorm_rows(x_ref[...], g_ref[...]).astype(BF16)
    n = w_ref.shape[1]
    for c0 in range(0, n, 512):
        c1 = min(c0 + 512, n)
        o_ref[:, c0:c1] = jnp.dot(hn, w_ref[:, c0:c1], preferred_element_type=F32)


def _inproj(x2, g, w_bf):
    n, d = x2.shape
    pc = w_bf.shape[1]
    return pl.pallas_call(
        _inproj_kernel,
        grid=(n // ROW_TILE,),
        in_specs=[pl.BlockSpec((ROW_TILE, d), lambda i: (i, 0)),
                  pl.BlockSpec((1, d), lambda i: (0, 0)),
                  pl.BlockSpec((d, pc), lambda i: (0, 0))],
        out_specs=pl.BlockSpec((ROW_TILE, pc), lambda i: (i, 0)),
        out_shape=jax.ShapeDtypeStruct((n, pc), F32),
        compiler_params=_params(1),
        name="inproj",
    )(x2, g.reshape(1, d), w_bf)


def _outproj_kernel(oa_ref, ob_ref, oc_ref, w_ref, x_ref, o_ref):
    mix = jnp.concatenate([oa_ref[...], ob_ref[...], oc_ref[...]], axis=1)
    o_ref[...] = x_ref[...] + jnp.dot(mix, w_ref[...], preferred_element_type=F32)


def _outproj(oa, ob, oc, w_bf, x2):
    n, d = x2.shape
    k = w_bf.shape[0]
    mw = oa.shape[1]
    return pl.pallas_call(
        _outproj_kernel,
        grid=(n // ROW_TILE,),
        in_specs=[pl.BlockSpec((ROW_TILE, mw), lambda i: (i, 0)),
                  pl.BlockSpec((ROW_TILE, mw), lambda i: (i, 0)),
                  pl.BlockSpec((ROW_TILE, mw), lambda i: (i, 0)),
                  pl.BlockSpec((k, d), lambda i: (0, 0)),
                  pl.BlockSpec((ROW_TILE, d), lambda i: (i, 0))],
        out_specs=pl.BlockSpec((ROW_TILE, d), lambda i: (i, 0)),
        out_shape=jax.ShapeDtypeStruct((n, d), F32),
        compiler_params=_params(1),
        name="outproj",
    )(oa, ob, oc, w_bf, x2)


def _ffn_up_kernel(x_ref, xh_ref, g_ref, wa_ref, wu_ref, cw_ref, cb_ref, o_ref, hn_scr, hh_scr, a_scr, *, tiles_per_seq):
    i, j = pl.program_id(0), pl.program_id(1)
    tm = x_ref.shape[0]

    @pl.when(j == 0)
    def _norm():
        hn_scr[...] = _rmsnorm_rows(x_ref[...], g_ref[...]).astype(BF16)
        keep = jnp.where(i % tiles_per_seq == 0, 0.0, 1.0)
        hh_scr[...] = (_rmsnorm_rows(xh_ref[...], g_ref[...]) * keep).astype(BF16)

    hn = hn_scr[...]
    a = jnp.dot(hn, wa_ref[...], preferred_element_type=F32)
    u = jnp.dot(hn, wu_ref[...], preferred_element_type=F32)
    a_scr[0:HALO, :] = jnp.dot(hh_scr[...], wa_ref[...], preferred_element_type=F32)
    a_scr[HALO:HALO + tm, :] = a
    a1 = a_scr[pl.ds(HALO - 1, tm), :]
    a2 = a_scr[pl.ds(HALO - 2, tm), :]
    cw = cw_ref[...]
    conv = cw[0:1, :] * a2 + cw[1:2, :] * a1 + cw[2:3, :] * a + cb_ref[...]
    o_ref[...] = (conv * jax.nn.sigmoid(conv) * u).astype(BF16)


def _ffn_up(x2, g, w_up_bf, layer, conv_w, conv_b, seq_len):
    n, d = x2.shape
    tn = D_FF // 2
    nj = D_FF // tn
    tm = FFN_UP_ROW_TILE
    hb = tm // HALO
    kern = functools.partial(_ffn_up_kernel, tiles_per_seq=seq_len // tm)
    return pl.pallas_call(
        kern,
        grid=(n // tm, nj),
        in_specs=[pl.BlockSpec((tm, d), lambda i, j: (i, 0)),
                  pl.BlockSpec((HALO, d), lambda i, j: (jnp.maximum(i * hb - 1, 0), 0)),
                  pl.BlockSpec((1, d), lambda i, j: (0, 0)),
                  pl.BlockSpec((None, d, tn), lambda i, j: (layer, 0, j)),
                  pl.BlockSpec((None, d, tn), lambda i, j: (layer, 0, j + nj)),
                  pl.BlockSpec((CONV_WIDTH, tn), lambda i, j: (0, j)),
                  pl.BlockSpec((1, tn), lambda i, j: (0, j))],
        out_specs=pl.BlockSpec((tm, tn), lambda i, j: (i, j)),
        out_shape=jax.ShapeDtypeStruct((n, D_FF), BF16),
        scratch_shapes=[pltpu.VMEM((tm, d), BF16), pltpu.VMEM((HALO, d), BF16),
                        pltpu.VMEM((tm + HALO, tn), F32)],
        compiler_params=_params(2),
        name="ffn_up",
    )(x2, x2, g.reshape(1, d), w_up_bf, w_up_bf, conv_w, conv_b.reshape(1, D_FF))


def _ffn_down_kernel(a_ref, w_ref, x_ref, g_ref, o_ref, *, final_norm):
    y = x_ref[...] + jnp.dot(a_ref[...], w_ref[...], preferred_element_type=F32)
    if final_norm:
        y = _rmsnorm_rows(y, g_ref[...])
    o_ref[...] = y


def _ffn_down(act, w_bf, layer, x2, final_g, final_norm):
    n, d = x2.shape
    k = act.shape[1]
    return pl.pallas_call(
        functools.partial(_ffn_down_kernel, final_norm=final_norm),
        grid=(n // ROW_TILE,),
        in_specs=[pl.BlockSpec((ROW_TILE, k), lambda i: (i, 0)),
                  pl.BlockSpec((None, k, d), lambda i: (layer, 0, 0)),
                  pl.BlockSpec((ROW_TILE, d), lambda i: (i, 0)),
                  pl.BlockSpec((1, d), lambda i: (0, 0))],
        out_specs=pl.BlockSpec((ROW_TILE, d), lambda i: (i, 0)),
        out_shape=jax.ShapeDtypeStruct((n, d), F32),
        compiler_params=_params(1),
        name="ffn_down",
    )(act, w_bf, x2, final_g.reshape(1, d))


def _cmp_kernel(kv_ref, pos_ref, w1_ref, w2_ref, o_ref):
    ngrp = kv_ref.shape[0] // CMP_STRIDE
    lo = hi = None
    for p in range(CMP_STRIDE):
        x = kv_ref[pl.ds(p, ngrp, stride=CMP_STRIDE), :]
        a = jnp.dot((x + pos_ref[p:p + 1, :]).astype(BF16), w1_ref[p], preferred_element_type=F32)
        b = jnp.dot((x + pos_ref[CMP_STRIDE + p:CMP_STRIDE + p + 1, :]).astype(BF16), w1_ref[CMP_STRIDE + p],
                    preferred_element_type=F32)
        lo = a if lo is None else lo + a
        hi = b if hi is None else hi + b
    hid = jax.nn.gelu(lo + pltpu.roll(hi, ngrp - 1, 0))
    o_ref[0] = jnp.dot(hid.astype(BF16), w2_ref[...], preferred_element_type=F32)


def _nsa_compress(proj, pos_cat, w1_cat, w2_cat, batch, seq):
    ngrp = seq // CMP_STRIDE
    return pl.pallas_call(
        _cmp_kernel,
        grid=(batch,),
        in_specs=[pl.BlockSpec((seq, LANES), lambda i: (i, _OFF["a_kcvc"] // LANES)),
                  pl.BlockSpec(pos_cat.shape, lambda i: (0, 0)),
                  pl.BlockSpec(w1_cat.shape, lambda i: (0, 0, 0)),
                  pl.BlockSpec(w2_cat.shape, lambda i: (0, 0))],
        out_specs=pl.BlockSpec((1, ngrp, LANES), lambda i: (i, 0, 0)),
        out_shape=jax.ShapeDtypeStruct((batch, ngrp, LANES), F32),
        compiler_params=_params(1),
        name="nsa_compress",
    )(proj, pos_cat, w1_cat, w2_cat)


def _stack_heads_t(x_t, n_heads, hd):
    z = jnp.zeros((LANES - hd, x_t.shape[1]), F32)
    return jnp.concatenate(
        [jnp.concatenate([x_t[hd * h:hd * (h + 1), :], z], axis=0) for h in range(n_heads)], axis=1)


def _nsa_kernel(q_ref, g_ref, ksvs_ref, kwvw_ref, kvc_ref, c_ref, s1_ref, s2_ref, ovt_ref, cmpb_ref, winb_ref,
                caub_ref, o_ref, ks_scr, vst_scr, kw_scr, vwt_scr, kc_scr, vct_scr, blk_scr, *, n_top):
    i = pl.program_id(1)
    H = NSA_HEADS
    QW = q_ref.shape[0]
    HW = H * QW
    seq = ks_scr.shape[0]
    lane_q = lax.broadcasted_iota(I32, (QW, LANES), 1)

    @pl.when(i == 0)
    def _init():
        ks_scr[...] = jnp.zeros_like(ks_scr)
        kw_scr[...] = jnp.zeros_like(kw_scr)
        for vt in (vst_scr, vwt_scr):
            vt[0:HEAD_DIM, :] = jnp.zeros((HEAD_DIM, seq), BF16)
            vt[HEAD_DIM:, :] = jnp.ones((ONES_ROWS, seq), BF16)
        kvc = kvc_ref[0]
        lane_c = lax.broadcasted_iota(I32, kvc.shape, 1)
        kc_scr[...] = jnp.where(lane_c < HEAD_DIM, kvc, 0.0).astype(BF16)
        vct_scr[...] = kvc.T[HEAD_DIM:2 * HEAD_DIM, :].astype(BF16)

    c, s1, s2 = c_ref[...], s1_ref[...], s2_ref[...]
    r0 = pl.multiple_of(i * QW, QW)
    for src, kdst, vdst in ((ksvs_ref, ks_scr, vst_scr), (kwvw_ref, kw_scr, vwt_scr)):
        kv = src[...]
        kr = _rope(kv, c, s1, s2, ROPE_DIM // 2)
        kdst[pl.ds(r0, QW), :] = jnp.where(lane_q < HEAD_DIM, kr, 0.0).astype(BF16)
        vdst[0:HEAD_DIM, pl.ds(r0, QW)] = kv.T[HEAD_DIM:2 * HEAD_DIM, :].astype(BF16)

    q = q_ref[...] * (SCALE * LOG2E)
    qr = _rope_wide(q, c, s1, s2, ROPE_DIM // 2)
    q_st = _stack_heads_t(q.T, H, HEAD_DIM).astype(BF16)
    qr_st = _stack_heads_t(qr.T, H, HEAD_DIM).astype(BF16)
    tile_h = lambda b: jnp.concatenate([b] * H, axis=1)

    ncp = kc_scr.shape[0]
    cshift = cmpb_ref.shape[0] - ncp
    cb = cmpb_ref[pl.ds(pl.multiple_of(cshift - (QW // CMP_STRIDE) * i, 8), ncp), :]
    s_c = jnp.dot(kc_scr[...], q_st, preferred_element_type=F32) + tile_h(cb)
    e_c, _, den_c = _softmax2_cols(s_c)
    t_row = i * QW + (lax.broadcasted_iota(I32, (1, HW), 1) & (QW - 1))
    r_c = jnp.where(t_row >= CMP_BLOCK - 1, 1.0 / den_c, 0.0)
    o_cmp = jnp.dot(vct_scr[...], e_c.astype(BF16), preferred_element_type=F32) * r_c
    psum = e_c[:, 0:QW] * r_c[:, 0:QW]
    for h in range(1, H):
        psum = psum + e_c[:, QW * h:QW * (h + 1)] * r_c[:, QW * h:QW * (h + 1)]

    hi = psum.astype(BF16)
    r1 = psum - hi.astype(F32)
    mid = r1.astype(BF16)
    lo = (r1 - mid.astype(F32)).astype(BF16)
    ovt = ovt_ref[...]
    imp = (jnp.dot(ovt, hi, preferred_element_type=F32) + jnp.dot(ovt, mid, preferred_element_type=F32)
           + jnp.dot(ovt, lo, preferred_element_type=F32))
    n_slc = imp.shape[0]
    jb = lax.broadcasted_iota(I32, (n_slc, QW), 0)
    tq = i * QW + lax.broadcasted_iota(I32, (n_slc, QW), 1)
    cur = tq >> (SLC_BLOCK.bit_length() - 1)
    forced = (jb == 0) | (jb == cur) | (jb == cur - 1)
    val = jnp.where(forced, FORCE, jnp.where(jb <= cur, imp, NEG))
    blk_scr[...] = val

    rows_per_step = QW // SLC_BLOCK

    def rank_body(g, rank):
        j0 = g * rows_per_step
        for r in range(rows_per_step):
            row = blk_scr[pl.ds(j0 + r, 1), :]
            rank = rank + jnp.where(row > val, 1, jnp.where(row == val, jnp.where(jb > j0 + r, 1, 0), 0))
        return rank

    rank = lax.fori_loop(0, i + 1, rank_body, jnp.zeros((n_slc, QW), I32))
    blk_scr[...] = jnp.where(rank < n_top, jnp.where(val > 0.5 * NEG, 0.0, NEG), NEG)

    kmax = caub_ref.shape[0] // 2

    def slc_chunk(size, base):
        def body(ci, carry):
            m, acc = carry
            k0 = pl.multiple_of(base + ci * size, KC)
            ahead = jnp.minimum(i * QW - k0, kmax)
            b0 = k0 // SLC_BLOCK
            rows = [jnp.broadcast_to(blk_scr[pl.ds(b0 + r, 1), :], (SLC_BLOCK, QW)) for r in range(size // SLC_BLOCK)]
            bias = (jnp.concatenate(rows, axis=0)
                    + caub_ref[pl.ds(pl.multiple_of(kmax - ahead, QB), size), :]).astype(BF16)
            s = jnp.dot(ks_scr[pl.ds(k0, size), :], qr_st, preferred_element_type=F32).astype(BF16) + tile_h(bias)
            m_new = jnp.maximum(m, jnp.max(s, axis=0, keepdims=True))
            alpha = jnp.exp2((m - m_new).astype(F32))
            e = jnp.exp2(s - m_new)
            acc = alpha * acc + jnp.dot(vst_scr[:, pl.ds(k0, size)], e, preferred_element_type=F32)
            return m_new, acc
        return body

    n_chunks = (i * QW + QW + KC - 1) // KC
    n_long = n_chunks // ATT_LONG
    init = (jnp.full((1, HW), NEG, BF16), jnp.zeros((HEAD_DIM + ONES_ROWS, HW), F32))
    state = lax.fori_loop(0, n_long, slc_chunk(ATT_LONG * KC, 0), init)
    m_s, acc_s = lax.fori_loop(0, n_chunks - n_long * ATT_LONG, slc_chunk(KC, n_long * (ATT_LONG * KC)), state)
    o_slc = jnp.where(m_s.astype(F32) > 0.5 * NEG, acc_s[:HEAD_DIM] / acc_s[HEAD_DIM:HEAD_DIM + 1], 0.0)

    span = min(WIN + QW, seq)
    w0 = pl.multiple_of(jnp.maximum(i * QW + QW - span, 0), QB)
    wb = winb_ref[pl.ds(pl.multiple_of(span - QW - (i * QW - w0), QB), span), :].astype(BF16)
    s_w = jnp.dot(kw_scr[pl.ds(w0, span), :], qr_st, preferred_element_type=F32).astype(BF16) + tile_h(wb)
    e_w = jnp.exp2(s_w - jnp.max(s_w, axis=0, keepdims=True))
    acc_w = jnp.dot(vwt_scr[:, pl.ds(w0, span)], e_w, preferred_element_type=F32)
    o_win = acc_w[:HEAD_DIM] / acc_w[HEAD_DIM:HEAD_DIM + 1]

    gt = jax.nn.sigmoid(g_ref[...].T)
    outs = []
    for h in range(H):
        sl = slice(QW * h, QW * (h + 1))
        outs.append(gt[3 * h:3 * h + 1, :] * o_cmp[:, sl] + gt[3 * h + 1:3 * h + 2, :] * o_slc[:, sl]
                    + gt[3 * h + 2:3 * h + 3, :] * o_win[:, sl])
    outs.append(jnp.zeros((o_ref.shape[1] - H * HEAD_DIM, QW), F32))
    o_ref[...] = jnp.concatenate(outs, axis=0).T.astype(BF16)


def _nsa(proj, kvcmp, tabs, ovt, batch, seq):
    qw = NSA_QW
    nb = seq // qw
    n_slc = seq // SLC_BLOCK
    ncp = kvcmp.shape[1]
    col = lambda name, w: _OFF[name] // w
    qmap = lambda cidx: (lambda b, i: (b * nb + i, cidx))
    tmap = lambda k: (lambda b, i: (k, i, 0))
    kern = functools.partial(_nsa_kernel, n_top=min(SLC_TOPN, n_slc))
    cmpb, winb, caub = _cmp_bias_table(seq, qw), _win_bias_table(seq, qw), _causal_bias_table(qw)
    whole = lambda a: pl.BlockSpec(a.shape, lambda b, i: (0, 0))
    return pl.pallas_call(
        kern,
        grid=(batch, nb),
        in_specs=[pl.BlockSpec((qw, 384), qmap(col("a_q", 384))),
                  pl.BlockSpec((qw, LANES), qmap(col("a_g", LANES))),
                  pl.BlockSpec((qw, LANES), qmap(col("a_ksvs", LANES))),
                  pl.BlockSpec((qw, LANES), qmap(col("a_kwvw", LANES))),
                  pl.BlockSpec((1, ncp, LANES), lambda b, i: (b, 0, 0)),
                  pl.BlockSpec((None, qw, LANES), tmap(0)),
                  pl.BlockSpec((None, qw, LANES), tmap(1)),
                  pl.BlockSpec((None, qw, LANES), tmap(2)),
                  whole(ovt), whole(cmpb), whole(winb), whole(caub)],
        out_specs=pl.BlockSpec((qw, 384), lambda b, i: (b * nb + i, 0)),
        out_shape=jax.ShapeDtypeStruct((batch * seq, 384), BF16),
        scratch_shapes=[pltpu.VMEM((seq, LANES), BF16), pltpu.VMEM((HEAD_DIM + ONES_ROWS, seq), BF16),
                        pltpu.VMEM((seq, LANES), BF16), pltpu.VMEM((HEAD_DIM + ONES_ROWS, seq), BF16),
                        pltpu.VMEM((ncp, LANES), BF16), pltpu.VMEM((HEAD_DIM, ncp), BF16),
                        pltpu.VMEM((n_slc, qw), F32)],
        compiler_params=_params(2),
        name="nsa",
    )(proj, proj, proj, proj, kvcmp, tabs, tabs, tabs, ovt, cmpb, winb, caub)


SEARCH_BISECT_FROM = 24
SEARCH_MAX_PASSES = 64
SEARCH_KEPT_LIMIT = 6
SEARCH_BLIND_PASSES = 12
SEARCH_SNAP_KEYS = 3
INT_MAX = 2 ** 31 - 1
FLT_MIN_NORMAL = 1.17549435e-38
ZERO_BAND_END = 1 << 23


def _tree_rows(x, op):
    parts = [x[r:r + 8, :] for r in range(0, x.shape[0], 8)]
    while len(parts) > 1:
        nxt = [op(parts[a], parts[a + 1]) for a in range(0, len(parts) - 1, 2)]
        if len(parts) % 2:
            nxt.append(parts[-1])
        parts = nxt
    return parts[0]


def _f2key(v):
    bits = lax.bitcast_convert_type(v, I32)
    return bits ^ ((bits >> 31) & 0x7FFFFFFF)


def _key2f(k):
    return lax.bitcast_convert_type(k ^ ((k >> 31) & 0x7FFFFFFF), F32)


def _dsa_kernel(q_ref, iq_ref, ikw_ref, kv_ref, c_ref, s1_ref, s2_ref, ci_ref, si1_ref, si2_ref, o_ref,
                k_scr, vt_scr, ik_scr, key_scr, *, top, idx_bits):
    i = pl.program_id(1)
    H = DSA_HEADS
    QW = q_ref.shape[0]
    HW = H * QW
    seq = key_scr.shape[0]
    lane_q = lax.broadcasted_iota(I32, (QW, LANES), 1)

    @pl.when(i == 0)
    def _init():
        k_scr[...] = jnp.zeros_like(k_scr)
        vt_scr[0:HEAD_DIM, :] = jnp.zeros((HEAD_DIM, seq), BF16)
        vt_scr[HEAD_DIM:, :] = jnp.ones((ONES_ROWS, seq), BF16)
        ik_scr[...] = jnp.zeros_like(ik_scr)

    c, s1, s2 = c_ref[...], s1_ref[...], s2_ref[...]
    ci, si1, si2 = ci_ref[...], si1_ref[...], si2_ref[...]
    r0 = pl.multiple_of(i * QW, QW)
    kv = kv_ref[...]
    k_scr[pl.ds(r0, QW), :] = jnp.where(lane_q < HEAD_DIM, _rope(kv, c, s1, s2, ROPE_DIM // 2), 0.0).astype(BF16)
    vt_scr[0:HEAD_DIM, pl.ds(r0, QW)] = kv.T[HEAD_DIM:2 * HEAD_DIM, :].astype(BF16)
    ikw = ikw_ref[...]
    ik_scr[pl.ds(r0, QW), :] = jnp.where(lane_q < IDX_DIM, _rope(ikw, ci, si1, si2, IDX_ROPE_DIM // 2), 0.0).astype(BF16)

    qr = _rope_wide(q_ref[...] * (SCALE * LOG2E), c, s1, s2, ROPE_DIM // 2)
    qr_st = _stack_heads_t(qr.T, H, HEAD_DIM).astype(BF16)
    iqr = _rope_wide(iq_ref[...], ci, si1, si2, IDX_ROPE_DIM // 2)
    iq_st = _stack_heads_t(iqr.T, IDX_HEADS, IDX_DIM).astype(BF16)
    wt = ikw.T * ((IDX_DIM ** -0.5) * (IDX_HEADS ** -0.5))
    w_st = jnp.concatenate([wt[IDX_DIM + h:IDX_DIM + h + 1, :] for h in range(IDX_HEADS)], axis=1)

    n_chunks = (i * QW + QW + KC - 1) // KC
    k_io = lax.broadcasted_iota(I32, (KC, QW), 0)
    tq = i * QW + lax.broadcasted_iota(I32, (KC, QW), 1)

    sub_io = lax.broadcasted_iota(I32, (QB, QW), 0)
    sub_tq = i * QW + lax.broadcasted_iota(I32, (QB, QW), 1)
    zkey0 = seq - sub_io

    def score_body(cidx, carry):
        mx, mn = carry
        k0 = pl.multiple_of(cidx * KC, KC)
        for sb in range(KC // QB):
            kb = pl.multiple_of(k0 + sb * QB, QB)
            d = jnp.dot(ik_scr[pl.ds(kb, QB), :], iq_st, preferred_element_type=F32)
            r = jnp.maximum(d, 0.0) * w_st
            sc = r[:, 0:QW]
            for h in range(1, IDX_HEADS):
                sc = sc + r[:, QW * h:QW * (h + 1)]
            key = jnp.where(jnp.abs(sc) < FLT_MIN_NORMAL, zkey0 - kb, _f2key(sc))
            key_scr[pl.ds(kb, QB), :] = jnp.where(sub_io + kb <= sub_tq, key, INT_MIN)
            mx = jnp.maximum(mx, _tree_rows(key, jnp.maximum))
            mn = jnp.minimum(mn, _tree_rows(key, jnp.minimum))
        return mx, mn

    mx8, mn8 = lax.fori_loop(0, n_chunks, score_body,
                             (jnp.full((8, QW), INT_MIN, I32), jnp.full((8, QW), INT_MAX, I32)))

    def count(pred):
        def body(cidx, acc):
            k0 = pl.multiple_of(cidx * KC, KC)
            return acc + _tree_rows(pred(key_scr[pl.ds(k0, KC), :], k_io + k0), jnp.add)
        acc8 = lax.fori_loop(0, n_chunks, body, jnp.zeros((8, QW), F32))
        return jnp.sum(acc8, axis=0, keepdims=True)

    topf = float(top)
    lo0 = functools.reduce(jnp.minimum, [mn8[r:r + 1, :] for r in range(8)])
    hi0 = functools.reduce(jnp.maximum, [mx8[r:r + 1, :] for r in range(8)]) + 1
    n_causal = (i * QW + 1 + lax.broadcasted_iota(I32, (1, QW), 1)).astype(F32)
    log_top = float(np.log(top))

    def in_zero_band(lo, hi):
        return jnp.where(lo >= 1, jnp.where(hi <= ZERO_BAND_END, 1.0, 0.0), 0.0) > 0.5

    def open_keys(lo, hi, c_lo, c_hi):
        return jnp.max(jnp.where(c_lo > topf, jnp.where(hi > lo + 1, c_lo - c_hi, 0.0), 0.0)).astype(I32)

    def search_cond(c):
        return jnp.logical_and(c[0] < SEARCH_MAX_PASSES, c[1] > SEARCH_SNAP_KEYS)

    def search_pass(it, st):
        lo, hi, c_lo, c_hi, kept, w_lo, w_hi = st
        f_lo = (jnp.log(c_lo) - log_top) * w_lo
        f_hi = (log_top - jnp.log(jnp.maximum(c_hi, 0.5))) * w_hi
        frac = f_lo / (f_lo + f_hi)
        v_lo, v_hi = _key2f(lo), _key2f(hi)
        cand = _f2key(v_lo + frac * (v_hi - v_lo))
        lo_f, hi_f = lo.astype(F32), hi.astype(F32)
        cand = jnp.where(in_zero_band(lo, hi), (lo_f + frac * (hi_f - lo_f)).astype(I32), cand)
        key_mid = (lo >> 1) + (hi >> 1) + (lo & hi & 1)
        cand = jnp.where(jnp.abs(kept) >= SEARCH_KEPT_LIMIT, key_mid, cand)
        cand = jnp.where(it >= SEARCH_BISECT_FROM, key_mid, cand)
        cand = jnp.where(it == 0, 1, jnp.where(it == 1, ZERO_BAND_END, cand))
        cand = jnp.minimum(jnp.maximum(cand, lo + 1), hi - 1)
        cnt = count(lambda key, kpos: jnp.where(key >= cand, 1.0, 0.0))
        active = jnp.where(c_lo > topf, jnp.where(hi > lo + 1, 1.0, 0.0), 0.0) > 0.5
        up = jnp.where(active, jnp.where(cnt >= topf, 1.0, 0.0), 0.0) > 0.5
        dn = jnp.where(active, jnp.where(cnt >= topf, 0.0, 1.0), 0.0) > 0.5
        w_hi = jnp.where(up, jnp.where(kept < 0, 0.5 * w_hi, 1.0), jnp.where(dn, 1.0, w_hi))
        w_lo = jnp.where(dn, jnp.where(kept > 0, 0.5 * w_lo, 1.0), jnp.where(up, 1.0, w_lo))
        kept = jnp.where(up, jnp.where(kept < 0, kept - 1, -1), jnp.where(dn, jnp.where(kept > 0, kept + 1, 1), kept))
        lo, c_lo = jnp.where(up, cand, lo), jnp.where(up, cnt, c_lo)
        hi, c_hi = jnp.where(dn, cand, hi), jnp.where(dn, cnt, c_hi)
        hi = jnp.where(in_zero_band(lo, hi), jnp.minimum(hi, seq + 1), hi)
        return lo, hi, c_lo, c_hi, kept, w_lo, w_hi

    def search_body(c):
        st = search_pass(c[0], c[2])
        return c[0] + 1, open_keys(*st[:4]), st

    zero_f, one_f = jnp.zeros((1, QW), F32), jnp.ones((1, QW), F32)
    st0 = (lo0, hi0, n_causal, zero_f, jnp.zeros((1, QW), I32), one_f, one_f)
    n_blind = jnp.where(i * QW + QW > top, SEARCH_BLIND_PASSES, 0)
    st1 = lax.fori_loop(0, n_blind, search_pass, st0)
    _, _, st2 = lax.while_loop(search_cond, search_body, (n_blind, open_keys(*st1[:4]), st1))

    def snap_body(c):
        lo, hi, c_lo, c_hi = c[1]

        def below_hi(cidx, acc):
            k0 = pl.multiple_of(cidx * KC, KC)
            key = key_scr[pl.ds(k0, KC), :]
            return jnp.maximum(acc, _tree_rows(jnp.where(key < hi, key, INT_MIN), jnp.maximum))
        top8 = lax.fori_loop(0, n_chunks, below_hi, jnp.full((8, QW), INT_MIN, I32))
        cand = functools.reduce(jnp.maximum, [top8[r:r + 1, :] for r in range(8)])
        cnt = count(lambda key, kpos: jnp.where(key >= cand, 1.0, 0.0))
        active = jnp.where(c_lo > topf, jnp.where(hi > lo + 1, 1.0, 0.0), 0.0) > 0.5
        up = jnp.where(active, jnp.where(cnt >= topf, 1.0, 0.0), 0.0) > 0.5
        dn = jnp.where(active, jnp.where(cnt >= topf, 0.0, 1.0), 0.0) > 0.5
        lo, c_lo = jnp.where(up, cand, lo), jnp.where(up, cnt, c_lo)
        hi, c_hi = jnp.where(up, cand + 1, jnp.where(dn, cand, hi)), jnp.where(dn, cnt, c_hi)
        return open_keys(lo, hi, c_lo, c_hi), (lo, hi, c_lo, c_hi)

    _, (thr, _, c_lo, c_hi) = lax.while_loop(lambda c: c[0] > 0, snap_body, (open_keys(*st2[:4]), st2[:4]))

    tied = c_lo > topf
    need1 = (topf - 1.0) - c_hi
    all_pos = jnp.full((1, QW), 2 ** idx_bits - 1, I32)

    def tie_search():
        def tie_body(bi, cur):
            cand = cur | lax.shift_left(jnp.int32(1), idx_bits - 1 - bi)
            cnt = count(lambda key, kpos: jnp.where(key == thr, jnp.where(kpos < cand, 1.0, 0.0), 0.0))
            return jnp.where(cnt <= need1, cand, cur)
        return lax.fori_loop(0, idx_bits, tie_body, jnp.zeros((1, QW), I32))

    any_tied = jnp.max(jnp.where(tied, 1.0, 0.0)) > 0.5
    last = jnp.where(tied, lax.cond(any_tied, tie_search, lambda: all_pos), all_pos)

    half_neg = int(np.float32(0.5 * NEG).view(np.int32))
    key_floor = half_neg ^ 0x7FFFFFFF

    def att_chunk(size, base):
        pos_io = lax.broadcasted_iota(I32, (size, QW), 0)

        def body(cidx, state):
            m, acc = state
            k0 = pl.multiple_of(base + cidx * size, KC)
            key = key_scr[pl.ds(k0, size), :]
            kpos = pos_io + k0
            bias = jnp.where(key > thr, 0.0, jnp.where(key == thr, jnp.where(kpos <= last, 0.0, NEG), NEG))
            bias = jnp.where(key > key_floor, bias, NEG).astype(BF16)
            s = (jnp.dot(k_scr[pl.ds(k0, size), :], qr_st, preferred_element_type=F32).astype(BF16)
                 + jnp.concatenate([bias] * H, axis=1))
            m_new = jnp.maximum(m, jnp.max(s, axis=0, keepdims=True))
            alpha = jnp.exp2((m - m_new).astype(F32))
            e = jnp.exp2(s - m_new)
            acc = alpha * acc + jnp.dot(vt_scr[:, pl.ds(k0, size)], e, preferred_element_type=F32)
            return m_new, acc
        return body

    init = (jnp.full((1, HW), NEG, BF16), jnp.zeros((HEAD_DIM + ONES_ROWS, HW), F32))
    n_long = n_chunks // ATT_LONG
    state = lax.fori_loop(0, n_long, att_chunk(ATT_LONG * KC, 0), init)
    m_a, acc_a = lax.fori_loop(0, n_chunks - n_long * ATT_LONG, att_chunk(KC, n_long * (ATT_LONG * KC)), state)
    l_a = acc_a[HEAD_DIM:HEAD_DIM + 1, :]
    o = jnp.where(m_a.astype(F32) > 0.5 * NEG, acc_a[:HEAD_DIM, :] / l_a, 0.0)
    outs = [o[:, QW * h:QW * (h + 1)] for h in range(H)]
    outs.append(jnp.zeros((o_ref.shape[1] - H * HEAD_DIM, QW), F32))
    o_ref[...] = jnp.concatenate(outs, axis=0).T.astype(BF16)


def _dsa(proj, tabs, tabs_i, batch, seq):
    qw = DSA_QW
    nb = seq // qw
    col = lambda name, w: _OFF[name] // w
    qmap = lambda cidx: (lambda b, i: (b * nb + i, cidx))
    tmap = lambda k: (lambda b, i: (k, i, 0))
    top = min(DSA_TOPK_MAX, seq // 4)
    kern = functools.partial(_dsa_kernel, top=top, idx_bits=int(seq).bit_length())
    return pl.pallas_call(
        kern,
        grid=(batch, nb),
        in_specs=[pl.BlockSpec((qw, 384), qmap(col("b_q", 384))),
                  pl.BlockSpec((qw, 256), qmap(col("b_iq", 256))),
                  pl.BlockSpec((qw, LANES), qmap(col("b_ikiw", LANES))),
                  pl.BlockSpec((qw, LANES), qmap(col("b_kv", LANES))),
                  pl.BlockSpec((None, qw, LANES), tmap(0)),
                  pl.BlockSpec((None, qw, LANES), tmap(1)),
                  pl.BlockSpec((None, qw, LANES), tmap(2)),
                  pl.BlockSpec((None, qw, LANES), tmap(0)),
                  pl.BlockSpec((None, qw, LANES), tmap(1)),
                  pl.BlockSpec((None, qw, LANES), tmap(2))],
        out_specs=pl.BlockSpec((qw, 384), lambda b, i: (b * nb + i, 0)),
        out_shape=jax.ShapeDtypeStruct((batch * seq, 384), BF16),
        scratch_shapes=[pltpu.VMEM((seq, LANES), BF16), pltpu.VMEM((HEAD_DIM + ONES_ROWS, seq), BF16),
                        pltpu.VMEM((seq, LANES), BF16), pltpu.VMEM((seq, qw), I32)],
        compiler_params=_params(2),
        name="dsa",
    )(proj, proj, proj, proj, tabs, tabs, tabs, tabs_i, tabs_i, tabs_i)


DIL_MB = max(d for _, d in DIL_PAIRS) * QB
DIL_VMEM_LIMIT = 56 * 1024 * 1024
DIL_UNROLL = 4


def _dil_layout(g):
    dil = DIL_PAIRS[g][1]
    per = DIL_MB // dil
    return dil, per, per + QB


def _dil_bias_table():
    u = np.arange(2 * QB)[:, None]
    diff = QB + np.arange(LANES)[None, :] - u
    ok = (diff >= 0) & (diff <= QB)
    return _bias_table(np.concatenate([ok, ok & (u >= QB)], axis=0))


def _dil_kernel(*refs):
    G, HG = len(DIL_PAIRS), DIL_HEADS_PER_GROUP
    q_refs, k_refs, v_refs = refs[0:G], refs[G:2 * G], refs[2 * G:3 * G]
    c_ref, s1_ref, s2_ref, bias_ref, o_ref = refs[3 * G:3 * G + 5]
    kds, vds, ogs, lss = (refs[3 * G + 5 + n * G:3 * G + 5 + (n + 1) * G] for n in range(4))
    step = pl.program_id(1)
    half = ROPE_DIM // 2

    @pl.when(step == 0)
    def _zero():
        for g in range(G):
            kds[g][...] = jnp.zeros_like(kds[g])
            vds[g][...] = jnp.zeros_like(vds[g])

    @pl.when(step > 0)
    def _carry_halo():
        for g in range(G):
            dil, per, stride = _dil_layout(g)
            for r in range(dil):
                b0 = r * stride
                kds[g][b0:b0 + QB, :] = kds[g][b0 + per:b0 + per + QB, :]
                vds[g][:, b0:b0 + QB] = vds[g][:, b0 + per:b0 + per + QB]

    for g in range(G):
        dil, per, stride = _dil_layout(g)
        piece = min(per, KC)
        for r in range(dil):
            for p0 in range(0, per, piece):
                rows = pl.ds(r + dil * p0, piece, stride=dil)
                kr = _rope(k_refs[g][rows, :], c_ref[rows, :], s1_ref[rows, :], s2_ref[rows, :], half)
                d0 = r * stride + QB + p0
                kds[g][d0:d0 + piece, :] = kr.astype(BF16)
                vds[g][:, d0:d0 + piece] = v_refs[g][rows, :].T.astype(BF16)

    zero = jnp.zeros((HEAD_DIM, LANES), F32)
    for g in range(G):
        dil, per, stride = _dil_layout(g)
        nblk = per // QB

        def block(jb, _, r=0, g=g, dil=dil, stride=stride):
            p0 = pl.multiple_of(dil * QB * jb, dil * QB)
            win = pl.ds(p0, dil * QB)
            rows = pl.ds(r, QB, stride=dil)
            q = _rope(q_refs[g].at[win, :][rows, :] * (SCALE * LOG2E), c_ref.at[win, :][rows, :],
                      s1_ref.at[win, :][rows, :], s2_ref.at[win, :][rows, :], half)
            qt = q.T
            q2 = jnp.concatenate([jnp.concatenate([qt[:HEAD_DIM], zero], axis=0),
                                  jnp.concatenate([zero, qt[HEAD_DIM:]], axis=0)], axis=1).astype(BF16)
            kb = pl.multiple_of(r * stride + QB * jb, QB)
            first = jnp.logical_and(step == 0, jb == 0)
            bias = bias_ref[pl.ds(pl.multiple_of(jnp.where(first, 2 * QB, 0), QB), 2 * QB), :]
            s = (jnp.dot(kds[g][pl.ds(kb, 2 * QB), :], q2, preferred_element_type=F32)
                 + jnp.concatenate([bias] * HG, axis=1))
            e, m, den = _softmax2_cols(s)
            lse = m + jnp.log2(den)
            rden = 1.0 / den
            parts, lparts = [], []
            for hg in range(HG):
                sl = slice(LANES * hg, LANES * (hg + 1))
                vt = vds[g][HEAD_DIM * hg:HEAD_DIM * (hg + 1), pl.ds(kb, 2 * QB)]
                parts.append(jnp.dot(vt, e[:, sl].astype(BF16), preferred_element_type=F32) * rden[:, sl])
                lparts.append(jnp.broadcast_to(lse[:, sl], (HEAD_DIM, LANES)))
            tile = jnp.concatenate(parts + lparts, axis=0).T
            ogs[g].at[win, :][rows, :] = tile[:, :LANES]
            lss[g].at[win, :][rows, :] = tile[:, LANES:]
            return 0

        for r in range(dil):
            lax.fori_loop(0, nblk, functools.partial(block, r=r), 0, unroll=min(nblk, DIL_UNROLL))

    def mix(ti, _):
        rows = pl.ds(pl.multiple_of(ti * KC, KC), KC)
        ls = [lss[g][rows, :] for g in range(G)]
        mx = functools.reduce(jnp.maximum, ls)
        ex = [jnp.exp2(x - mx) for x in ls]
        rtot = 1.0 / functools.reduce(lambda a, b: a + b, ex)
        o_ref[rows, :] = jnp.concatenate([ex[g] * rtot * ogs[g][rows, :] for g in range(G)], axis=1).astype(BF16)
        return 0

    lax.fori_loop(0, DIL_MB // KC, mix, 0)


def _dil(proj, tabs, batch, seq):
    assert seq % DIL_MB == 0 and all(w // d == QB for w, d in DIL_PAIRS)
    nm = seq // DIL_MB
    G = len(DIL_PAIRS)
    w = DIL_HEADS * HEAD_DIM
    gmap = lambda name, g: (lambda b, j: (b * nm + j, _OFF[name] // LANES + g))
    tmap = lambda k: (lambda b, j: (k, j, 0))
    bias = _dil_bias_table()
    rows = [_dil_layout(g)[0] * _dil_layout(g)[2] for g in range(G)]
    return pl.pallas_call(
        _dil_kernel,
        grid=(batch, nm),
        in_specs=([pl.BlockSpec((DIL_MB, LANES), gmap(name, g)) for name in ("c_q", "c_k", "c_v") for g in range(G)]
                  + [pl.BlockSpec((None, DIL_MB, LANES), tmap(k)) for k in range(3)]
                  + [pl.BlockSpec(bias.shape, lambda b, j: (0, 0))]),
        out_specs=pl.BlockSpec((DIL_MB, w), lambda b, j: (b * nm + j, 0)),
        out_shape=jax.ShapeDtypeStruct((batch * seq, w), BF16),
        scratch_shapes=([pltpu.VMEM((n, LANES), BF16) for n in rows] + [pltpu.VMEM((LANES, n), BF16) for n in rows]
                        + [pltpu.VMEM((DIL_MB, LANES), F32)] * (2 * G)),
        compiler_params=_params(2, DIL_VMEM_LIMIT),
        name="dilated",
    )(*([proj] * (3 * G)), tabs, tabs, tabs, bias)


def _overlap_t(seq):
    n_cmp_pad = seq // CMP_STRIDE
    n_slc = seq // SLC_BLOCK
    c_start = np.arange(n_cmp_pad) * CMP_STRIDE
    s_start = np.arange(n_slc) * SLC_BLOCK
    ov = ((c_start[None, :] < s_start[:, None] + SLC_BLOCK) & (c_start[None, :] + CMP_BLOCK > s_start[:, None]))
    return jnp.asarray(ov.astype(np.float32), dtype=BF16)


def _layer(x2, batch, seq, layer, norm1_g, w_in, cmp_pos, cmp_w1, cmp_w2, w_out, norm2_g, w_up_bf, conv_w, conv_b,
           w_down_bf, final_g, final_norm, tabs, tabs_i, ovt):
    n = batch * seq
    proj = _inproj(x2, norm1_g, _regroup_w_in(w_in, layer))

    w1 = cmp_w1.reshape(2, CMP_BLOCK, HEAD_DIM, CMP_HIDDEN)
    z1 = jnp.zeros_like(w1[0])
    w1_cat = jnp.concatenate([jnp.concatenate([w1[0], z1], axis=2),
                              jnp.concatenate([z1, w1[1]], axis=2)], axis=1).astype(BF16)
    z2 = jnp.zeros((CMP_HIDDEN, HEAD_DIM), cmp_w2.dtype)
    w2_cat = jnp.concatenate([jnp.concatenate([cmp_w2[0], z2], axis=1),
                              jnp.concatenate([z2, cmp_w2[1]], axis=1)], axis=0).astype(BF16)
    pos_cat = jnp.concatenate([cmp_pos[0], cmp_pos[1]], axis=1)
    kvcmp = _nsa_compress(proj, pos_cat, w1_cat, w2_cat, batch, seq)

    o_a = _nsa(proj, kvcmp, tabs, ovt, batch, seq)
    o_b = _dsa(proj, tabs, tabs_i, batch, seq)
    o_c = _dil(proj, tabs, batch, seq)

    hp = NSA_HEADS * HEAD_DIM
    zrow = jnp.zeros((384 - hp, D_MODEL), w_out.dtype)
    w_out_pad = jnp.concatenate([w_out[0:hp], zrow, w_out[hp:2 * hp], zrow, w_out[2 * hp:]], axis=0).astype(BF16)
    x2 = _outproj(o_a, o_b, o_c, w_out_pad, x2)

    act = _ffn_up(x2, norm2_g, w_up_bf, layer, conv_w, conv_b, seq)
    return _ffn_down(act, w_down_bf, layer, x2, final_g, final_norm)


def kernel(x, norm1_g, w_in, cmp_pos, cmp_w1, cmp_w2, w_out, norm2_g, w_up, conv_w, conv_b, w_down, final_g):
    batch, seq, d = x.shape
    depth = w_in.shape[0]
    tabs = _rope_lane_tables(seq, HEAD_DIM, ROPE_DIM)
    tabs_i = _rope_lane_tables(seq, IDX_DIM, IDX_ROPE_DIM)
    ovt = _overlap_t(seq)
    x2 = x.reshape(batch * seq, d)
    w_up_bf, w_down_bf = w_up.astype(BF16), w_down.astype(BF16)
    for li in range(depth):
        x2 = _layer(x2, batch, seq, li, norm1_g[li], w_in, cmp_pos[li], cmp_w1[li], cmp_w2[li], w_out[li],
                    norm2_g[li], w_up_bf, conv_w[li], conv_b[li], w_down_bf, final_g, li == depth - 1,
                    tabs, tabs_i, ovt)
    return x2.reshape(batch, seq, d)
```

```python
import functools

import numpy as np
import jax
import jax.numpy as jnp
from jax import lax
from jax.experimental import pallas as pl
from jax.experimental.pallas import tpu as pltpu

F32 = jnp.float32
BF16 = jnp.bfloat16
I32 = jnp.int32

D_MODEL = 1024
HEAD_DIM = 64
ROPE_DIM = HEAD_DIM // 4
ROPE_THETA = 500000.0
NORM_EPS = 1e-6
SCALE = HEAD_DIM ** -0.5
LOG2E = 1.4426950408889634
NEG = -1e30
FORCE = 1e9
NSA_HEADS = 5
CMP_BLOCK = 32
CMP_STRIDE = 16
CMP_HIDDEN = 128
SLC_BLOCK = 64
SLC_TOPN = 16
WIN = 512
DSA_HEADS = 5
IDX_HEADS = 8
IDX_DIM = 32
IDX_ROPE_DIM = IDX_DIM // 4
DSA_TOPK_MAX = 256
DIL_PAIRS = ((128, 1), (512, 4), (2048, 16))
DIL_HEADS_PER_GROUP = 2
DIL_HEADS = len(DIL_PAIRS) * DIL_HEADS_PER_GROUP
D_FF = 2816
CONV_WIDTH = 3

LANES = 128
QB = 128
DSA_QW = 256
NSA_QW = 256
KC = 512
ONES_ROWS = 16
ATT_LONG = 2
ROW_TILE = 512
FFN_UP_ROW_TILE = 1024
WIDE_ROW_TILE = 1024
HALO = 16
VMEM_LIMIT = 48 * 1024 * 1024
INT_MIN = -2 ** 31

_A0 = 0
_B0 = 719
_C0 = 1463
_SEGS = (
    ("c_q", _C0, 384, 384), ("c_k", _C0 + 384, 384, 384), ("c_v", _C0 + 768, 384, 384),
    ("a_q", _A0, 320, 384), ("b_q", _B0, 320, 384),
    ("a_kcvc", _A0 + 320, 128, 128),
    ("b_iq", _B0 + 448, 256, 256),
    ("a_ksvs", _A0 + 448, 128, 128), ("a_kwvw", _A0 + 576, 128, 128),
    ("a_g", _A0 + 704, 15, 128),
    ("b_kv", _B0 + 320, 128, 128),
    ("b_ikiw", _B0 + 704, 40, 128),
)
P_COLS = sum(s[3] for s in _SEGS)


def _seg_offsets():
    offs, o = {}, 0
    for name, _, _, pw in _SEGS:
        offs[name] = o
        o += pw
    return offs


_OFF = _seg_offsets()


def _regroup_kernel(w_ref, o_ref):
    for name, src, wdt, pw in _SEGS:
        dst = _OFF[name]
        o_ref[:, dst:dst + wdt] = w_ref[:, src:src + wdt].astype(BF16)
        if pw > wdt:
            o_ref[:, dst + wdt:dst + pw] = jnp.zeros((o_ref.shape[0], pw - wdt), BF16)


def _regroup_w_in(w, layer):
    _, d, n_in = w.shape
    rt = 256
    return pl.pallas_call(
        _regroup_kernel,
        grid=(d // rt,),
        in_specs=[pl.BlockSpec((None, rt, n_in), lambda i: (layer, i, 0))],
        out_specs=pl.BlockSpec((rt, P_COLS), lambda i: (i, 0)),
        out_shape=jax.ShapeDtypeStruct((d, P_COLS), BF16),
        compiler_params=_params(1),
        name="regroup_w_in",
    )(w)


def _rope_lane_tables(L, head_dim, rot_dim):
    half = rot_dim // 2
    inv = 1.0 / (ROPE_THETA ** (np.arange(0, rot_dim, 2, dtype=np.float32) / np.float32(rot_dim)))
    ang = np.arange(L, dtype=np.float32)[:, None] * inv[None, :]
    cos, sin = np.cos(ang).astype(np.float32), np.sin(ang).astype(np.float32)
    d = np.arange(LANES) % head_dim
    lo, hi = d < half, (d >= half) & (d < rot_dim)
    c = np.ones((L, LANES), np.float32)
    s1 = np.zeros((L, LANES), np.float32)
    s2 = np.zeros((L, LANES), np.float32)
    c[:, lo] = cos[:, d[lo]]
    c[:, hi] = cos[:, d[hi] - half]
    s1[:, lo] = -sin[:, d[lo]]
    s2[:, hi] = sin[:, d[hi] - half]
    return jnp.asarray(np.stack([c, s1, s2]))


def _rope(x, c, s1, s2, half):
    xp = pltpu.roll(x, LANES - half, 1)
    xm = pltpu.roll(x, half, 1)
    return x * c + xp * s1 + xm * s2


def _rope_wide(x, c, s1, s2, half):
    n = x.shape[1] // LANES
    return jnp.concatenate([_rope(x[:, LANES * s:LANES * (s + 1)], c, s1, s2, half) for s in range(n)], axis=1)


def _softmax2_cols(s):
    m = jnp.max(s, axis=0, keepdims=True)
    e = jnp.exp2(s - m)
    return e, m, jnp.sum(e, axis=0, keepdims=True)


def _bias_table(ok):
    return jnp.asarray(np.where(ok, 0.0, NEG).astype(np.float32))


def _cmp_bias_table(seq, qw):
    shift = (qw // CMP_STRIDE) * (seq // qw - 1)
    u = np.arange(seq // CMP_STRIDE + shift)[:, None]
    lane = np.arange(qw)[None, :]
    return _bias_table(CMP_STRIDE * (u - shift) + CMP_BLOCK - 1 <= lane)


def _win_bias_table(seq, qw):
    span = min(WIN + qw, seq)
    cmax = span - qw
    u = np.arange(span + cmax)[:, None]
    diff = cmax + np.arange(qw)[None, :] - u
    return _bias_table((diff >= 0) & (diff < WIN))


def _causal_bias_table(qw):
    kmax = ATT_LONG * KC
    u = np.arange(2 * kmax)[:, None]
    return _bias_table(u <= kmax + np.arange(qw)[None, :])


def _params(n_grid, vmem=VMEM_LIMIT):
    return pltpu.CompilerParams(dimension_semantics=("arbitrary",) * n_grid, vmem_limit_bytes=vmem)


def _rmsnorm_rows(x, g):
    return x * lax.rsqrt(jnp.mean(x * x, axis=-1, keepdims=True) + NORM_EPS) * g


def _inproj_kernel(x_ref, g_ref, w_ref, o_ref):
    hn = _rmsnorm_rows(x_ref[...], g_ref[...]).astype(BF16)
    n = w_ref.shape[1]
    for c0 in range(0, n, 512):
        c1 = min(c0 + 512, n)
        o_ref[:, c0:c1] = jnp.dot(hn, w_ref[:, c0:c1], preferred_element_type=F32)


def _inproj(x2, g, w_bf):
    n, d = x2.shape
    pc = w_bf.shape[1]
    return pl.pallas_call(
        _inproj_kernel,
        grid=(n // ROW_TILE,),
        in_specs=[pl.BlockSpec((ROW_TILE, d), lambda i: (i, 0)),
                  pl.BlockSpec((1, d), lambda i: (0, 0)),
                  pl.BlockSpec((d, pc), lambda i: (0, 0))],
        out_specs=pl.BlockSpec((ROW_TILE, pc), lambda i: (i, 0)),
        out_shape=jax.ShapeDtypeStruct((n, pc), F32),
        compiler_params=_params(1),
        name="inproj",
    )(x2, g.reshape(1, d), w_bf)


def _outproj_kernel(oa_ref, ob_ref, oc_ref, w_ref, x_ref, o_ref):
    mix = jnp.concatenate([oa_ref[...], ob_ref[...], oc_ref[...]], axis=1)
    o_ref[...] = x_ref[...] + jnp.dot(mix, w_ref[...], preferred_element_type=F32)


def _outproj(oa, ob, oc, w_bf, x2):
    n, d = x2.shape
    k = w_bf.shape[0]
    mw = oa.shape[1]
    tm = WIDE_ROW_TILE
    return pl.pallas_call(
        _outproj_kernel,
        grid=(n // tm,),
        in_specs=[pl.BlockSpec((tm, mw), lambda i: (i, 0)),
                  pl.BlockSpec((tm, mw), lambda i: (i, 0)),
                  pl.BlockSpec((tm, mw), lambda i: (i, 0)),
                  pl.BlockSpec((k, d), lambda i: (0, 0)),
                  pl.BlockSpec((tm, d), lambda i: (i, 0))],
        out_specs=pl.BlockSpec((tm, d), lambda i: (i, 0)),
        out_shape=jax.ShapeDtypeStruct((n, d), F32),
        compiler_params=_params(1),
        name="outproj",
    )(oa, ob, oc, w_bf, x2)


def _ffn_up_kernel(x_ref, xh_ref, g_ref, wa_ref, wu_ref, cw_ref, cb_ref, o_ref, hn_scr, hh_scr, a_scr, *, tiles_per_seq):
    i, j = pl.program_id(0), pl.program_id(1)
    tm = x_ref.shape[0]

    @pl.when(j == 0)
    def _norm():
        hn_scr[...] = _rmsnorm_rows(x_ref[...], g_ref[...]).astype(BF16)
        keep = jnp.where(i % tiles_per_seq == 0, 0.0, 1.0)
        hh_scr[...] = (_rmsnorm_rows(xh_ref[...], g_ref[...]) * keep).astype(BF16)

    hn = hn_scr[...]
    a = jnp.dot(hn, wa_ref[...], preferred_element_type=F32)
    u = jnp.dot(hn, wu_ref[...], preferred_element_type=F32)
    a_scr[0:HALO, :] = jnp.dot(hh_scr[...], wa_ref[...], preferred_element_type=F32)
    a_scr[HALO:HALO + tm, :] = a
    a1 = a_scr[pl.ds(HALO - 1, tm), :]
    a2 = a_scr[pl.ds(HALO - 2, tm), :]
    cw = cw_ref[...]
    conv = cw[0:1, :] * a2 + cw[1:2, :] * a1 + cw[2:3, :] * a + cb_ref[...]
    o_ref[...] = (conv * jax.nn.sigmoid(conv) * u).astype(BF16)


def _ffn_up(x2, g, w_up_bf, layer, conv_w, conv_b, seq_len):
    n, d = x2.shape
    tn = D_FF // 2
    nj = D_FF // tn
    tm = FFN_UP_ROW_TILE
    hb = tm // HALO
    kern = functools.partial(_ffn_up_kernel, tiles_per_seq=seq_len // tm)
    return pl.pallas_call(
        kern,
        grid=(n // tm, nj),
        in_specs=[pl.BlockSpec((tm, d), lambda i, j: (i, 0)),
                  pl.BlockSpec((HALO, d), lambda i, j: (jnp.maximum(i * hb - 1, 0), 0)),
                  pl.BlockSpec((1, d), lambda i, j: (0, 0)),
                  pl.BlockSpec((None, d, tn), lambda i, j: (layer, 0, j)),
                  pl.BlockSpec((None, d, tn), lambda i, j: (layer, 0, j + nj)),
                  pl.BlockSpec((CONV_WIDTH, tn), lambda i, j: (0, j)),
                  pl.BlockSpec((1, tn), lambda i, j: (0, j))],
        out_specs=pl.BlockSpec((tm, tn), lambda i, j: (i, j)),
        out_shape=jax.ShapeDtypeStruct((n, D_FF), BF16),
        scratch_shapes=[pltpu.VMEM((tm, d), BF16), pltpu.VMEM((HALO, d), BF16),
                        pltpu.VMEM((tm + HALO, tn), F32)],
        compiler_params=_params(2),
        name="ffn_up",
    )(x2, x2, g.reshape(1, d), w_up_bf, w_up_bf, conv_w, conv_b.reshape(1, D_FF))


def _ffn_down_kernel(a_ref, w_ref, x_ref, g_ref, o_ref, *, final_norm):
    y = x_ref[...] + jnp.dot(a_ref[...], w_ref[...], preferred_element_type=F32)
    if final_norm:
        y = _rmsnorm_rows(y, g_ref[...])
    o_ref[...] = y


def _ffn_down(act, w_bf, layer, x2, final_g, final_norm):
    n, d = x2.shape
    k = act.shape[1]
    tm = WIDE_ROW_TILE
    return pl.pallas_call(
        functools.partial(_ffn_down_kernel, final_norm=final_norm),
        grid=(n // tm,),
        in_specs=[pl.BlockSpec((tm, k), lambda i: (i, 0)),
                  pl.BlockSpec((None, k, d), lambda i: (layer, 0, 0)),
                  pl.BlockSpec((tm, d), lambda i: (i, 0)),
                  pl.BlockSpec((1, d), lambda i: (0, 0))],
        out_specs=pl.BlockSpec((tm, d), lambda i: (i, 0)),
        out_shape=jax.ShapeDtypeStruct((n, d), F32),
        compiler_params=_params(1),
        name="ffn_down",
    )(act, w_bf, x2, final_g.reshape(1, d))


def _cmp_kernel(kv_ref, pos_ref, w1_ref, w2_ref, o_ref):
    ngrp = kv_ref.shape[0] // CMP_STRIDE
    lo = hi = None
    for p in range(CMP_STRIDE):
        x = kv_ref[pl.ds(p, ngrp, stride=CMP_STRIDE), :]
        a = jnp.dot((x + pos_ref[p:p + 1, :]).astype(BF16), w1_ref[p], preferred_element_type=F32)
        b = jnp.dot((x + pos_ref[CMP_STRIDE + p:CMP_STRIDE + p + 1, :]).astype(BF16), w1_ref[CMP_STRIDE + p],
                    preferred_element_type=F32)
        lo = a if lo is None else lo + a
        hi = b if hi is None else hi + b
    hid = jax.nn.gelu(lo + pltpu.roll(hi, ngrp - 1, 0))
    o_ref[0] = jnp.dot(hid.astype(BF16), w2_ref[...], preferred_element_type=F32)


def _nsa_compress(proj, pos_cat, w1_cat, w2_cat, batch, seq):
    ngrp = seq // CMP_STRIDE
    return pl.pallas_call(
        _cmp_kernel,
        grid=(batch,),
        in_specs=[pl.BlockSpec((seq, LANES), lambda i: (i, _OFF["a_kcvc"] // LANES)),
                  pl.BlockSpec(pos_cat.shape, lambda i: (0, 0)),
                  pl.BlockSpec(w1_cat.shape, lambda i: (0, 0, 0)),
                  pl.BlockSpec(w2_cat.shape, lambda i: (0, 0))],
        out_specs=pl.BlockSpec((1, ngrp, LANES), lambda i: (i, 0, 0)),
        out_shape=jax.ShapeDtypeStruct((batch, ngrp, LANES), F32),
        compiler_params=_params(1),
        name="nsa_compress",
    )(proj, pos_cat, w1_cat, w2_cat)


def _stack_heads_t(x_t, n_heads, hd):
    z = jnp.zeros((LANES - hd, x_t.shape[1]), F32)
    return jnp.concatenate(
        [jnp.concatenate([x_t[hd * h:hd * (h + 1), :], z], axis=0) for h in range(n_heads)], axis=1)


def _nsa_kernel(q_ref, g_ref, ksvs_ref, kwvw_ref, kvc_ref, c_ref, s1_ref, s2_ref, ovt_ref, cmpb_ref, winb_ref,
                caub_ref, o_ref, ks_scr, vst_scr, kw_scr, vwt_scr, kc_scr, vct_scr, blk_scr, *, n_top):
    i = pl.program_id(1)
    H = NSA_HEADS
    QW = q_ref.shape[0]
    HW = H * QW
    seq = ks_scr.shape[0]
    lane_q = lax.broadcasted_iota(I32, (QW, LANES), 1)

    @pl.when(i == 0)
    def _init():
        ks_scr[...] = jnp.zeros_like(ks_scr)
        kw_scr[...] = jnp.zeros_like(kw_scr)
        for vt in (vst_scr, vwt_scr):
            vt[0:HEAD_DIM, :] = jnp.zeros((HEAD_DIM, seq), BF16)
            vt[HEAD_DIM:, :] = jnp.ones((ONES_ROWS, seq), BF16)
        kvc = kvc_ref[0]
        lane_c = lax.broadcasted_iota(I32, kvc.shape, 1)
        kc_scr[...] = jnp.where(lane_c < HEAD_DIM, kvc, 0.0).astype(BF16)
        vct_scr[...] = kvc.T[HEAD_DIM:2 * HEAD_DIM, :].astype(BF16)

    c, s1, s2 = c_ref[...], s1_ref[...], s2_ref[...]
    r0 = pl.multiple_of(i * QW, QW)
    for src, kdst, vdst in ((ksvs_ref, ks_scr, vst_scr), (kwvw_ref, kw_scr, vwt_scr)):
        kv = src[...]
        kr = _rope(kv, c, s1, s2, ROPE_DIM // 2)
        kdst[pl.ds(r0, QW), :] = jnp.where(lane_q < HEAD_DIM, kr, 0.0).astype(BF16)
        vdst[0:HEAD_DIM, pl.ds(r0, QW)] = kv.T[HEAD_DIM:2 * HEAD_DIM, :].astype(BF16)

    q = q_ref[...] * (SCALE * LOG2E)
    qr = _rope_wide(q, c, s1, s2, ROPE_DIM // 2)
    q_st = _stack_heads_t(q.T, H, HEAD_DIM).astype(BF16)
    qr_st = _stack_heads_t(qr.T, H, HEAD_DIM).astype(BF16)
    tile_h = lambda b: jnp.concatenate([b] * H, axis=1)

    ncp = kc_scr.shape[0]
    cshift = cmpb_ref.shape[0] - ncp
    cb = cmpb_ref[pl.ds(pl.multiple_of(cshift - (QW // CMP_STRIDE) * i, 8), ncp), :]
    s_c = jnp.dot(kc_scr[...], q_st, preferred_element_type=F32) + tile_h(cb)
    e_c, _, den_c = _softmax2_cols(s_c)
    t_row = i * QW + (lax.broadcasted_iota(I32, (1, HW), 1) & (QW - 1))
    r_c = jnp.where(t_row >= CMP_BLOCK - 1, 1.0 / den_c, 0.0)
    o_cmp = jnp.dot(vct_scr[...], e_c.astype(BF16), preferred_element_type=F32) * r_c
    psum = e_c[:, 0:QW] * r_c[:, 0:QW]
    for h in range(1, H):
        psum = psum + e_c[:, QW * h:QW * (h + 1)] * r_c[:, QW * h:QW * (h + 1)]

    hi = psum.astype(BF16)
    r1 = psum - hi.astype(F32)
    mid = r1.astype(BF16)
    lo = (r1 - mid.astype(F32)).astype(BF16)
    ovt = ovt_ref[...]
    imp = (jnp.dot(ovt, hi, preferred_element_type=F32) + jnp.dot(ovt, mid, preferred_element_type=F32)
           + jnp.dot(ovt, lo, preferred_element_type=F32))
    n_slc = imp.shape[0]
    jb = lax.broadcasted_iota(I32, (n_slc, QW), 0)
    tq = i * QW + lax.broadcasted_iota(I32, (n_slc, QW), 1)
    cur = tq >> (SLC_BLOCK.bit_length() - 1)
    forced = (jb == 0) | (jb == cur) | (jb == cur - 1)
    val = jnp.where(forced, FORCE, jnp.where(jb <= cur, imp, NEG))
    blk_scr[...] = val

    rows_per_step = QW // SLC_BLOCK

    def rank_body(g, rank):
        j0 = g * rows_per_step
        for r in range(rows_per_step):
            row = blk_scr[pl.ds(j0 + r, 1), :]
            rank = rank + jnp.where(row > val, 1, jnp.where(row == val, jnp.where(jb > j0 + r, 1, 0), 0))
        return rank

    rank = lax.fori_loop(0, i + 1, rank_body, jnp.zeros((n_slc, QW), I32))
    blk_scr[...] = jnp.where(rank < n_top, jnp.where(val > 0.5 * NEG, 0.0, NEG), NEG)

    kmax = caub_ref.shape[0] // 2

    def slc_chunk(size, base):
        def body(ci, carry):
            m, acc = carry
            k0 = pl.multiple_of(base + ci * size, KC)
            ahead = jnp.minimum(i * QW - k0, kmax)
            b0 = k0 // SLC_BLOCK
            rows = [jnp.broadcast_to(blk_scr[pl.ds(b0 + r, 1), :], (SLC_BLOCK, QW)) for r in range(size // SLC_BLOCK)]
            bias = (jnp.concatenate(rows, axis=0)
                    + caub_ref[pl.ds(pl.multiple_of(kmax - ahead, QB), size), :]).astype(BF16)
            s = jnp.dot(ks_scr[pl.ds(k0, size), :], qr_st, preferred_element_type=F32).astype(BF16) + tile_h(bias)
            m_new = jnp.maximum(m, jnp.max(s, axis=0, keepdims=True))
            alpha = jnp.exp2((m - m_new).astype(F32))
            e = jnp.exp2(s - m_new)
            acc = alpha * acc + jnp.dot(vst_scr[:, pl.ds(k0, size)], e, preferred_element_type=F32)
            return m_new, acc
        return body

    n_chunks = (i * QW + QW + KC - 1) // KC
    n_long = n_chunks // ATT_LONG
    init = (jnp.full((1, HW), NEG, BF16), jnp.zeros((HEAD_DIM + ONES_ROWS, HW), F32))
    state = lax.fori_loop(0, n_long, slc_chunk(ATT_LONG * KC, 0), init)
    m_s, acc_s = lax.fori_loop(0, n_chunks - n_long * ATT_LONG, slc_chunk(KC, n_long * (ATT_LONG * KC)), state)
    o_slc = jnp.where(m_s.astype(F32) > 0.5 * NEG, acc_s[:HEAD_DIM] / acc_s[HEAD_DIM:HEAD_DIM + 1], 0.0)

    span = min(WIN + QW, seq)
    w0 = pl.multiple_of(jnp.maximum(i * QW + QW - span, 0), QB)
    wb = winb_ref[pl.ds(pl.multiple_of(span - QW - (i * QW - w0), QB), span), :].astype(BF16)
    s_w = jnp.dot(kw_scr[pl.ds(w0, span), :], qr_st, preferred_element_type=F32).astype(BF16) + tile_h(wb)
    e_w = jnp.exp2(s_w - jnp.max(s_w, axis=0, keepdims=True))
    acc_w = jnp.dot(vwt_scr[:, pl.ds(w0, span)], e_w, preferred_element_type=F32)
    o_win = acc_w[:HEAD_DIM] / acc_w[HEAD_DIM:HEAD_DIM + 1]

    gt = jax.nn.sigmoid(g_ref[...].T)
    outs = []
    for h in range(H):
        sl = slice(QW * h, QW * (h + 1))
        outs.append(gt[3 * h:3 * h + 1, :] * o_cmp[:, sl] + gt[3 * h + 1:3 * h + 2, :] * o_slc[:, sl]
                    + gt[3 * h + 2:3 * h + 3, :] * o_win[:, sl])
    outs.append(jnp.zeros((o_ref.shape[1] - H * HEAD_DIM, QW), F32))
    o_ref[...] = jnp.concatenate(outs, axis=0).T.astype(BF16)


def _nsa(proj, kvcmp, tabs, ovt, batch, seq):
    qw = NSA_QW
    nb = seq // qw
    n_slc = seq // SLC_BLOCK
    ncp = kvcmp.shape[1]
    col = lambda name, w: _OFF[name] // w
    qmap = lambda cidx: (lambda b, i: (b * nb + i, cidx))
    tmap = lambda k: (lambda b, i: (k, i, 0))
    kern = functools.partial(_nsa_kernel, n_top=min(SLC_TOPN, n_slc))
    cmpb, winb, caub = _cmp_bias_table(seq, qw), _win_bias_table(seq, qw), _causal_bias_table(qw)
    whole = lambda a: pl.BlockSpec(a.shape, lambda b, i: (0, 0))
    return pl.pallas_call(
        kern,
        grid=(batch, nb),
        in_specs=[pl.BlockSpec((qw, 384), qmap(col("a_q", 384))),
                  pl.BlockSpec((qw, LANES), qmap(col("a_g", LANES))),
                  pl.BlockSpec((qw, LANES), qmap(col("a_ksvs", LANES))),
                  pl.BlockSpec((qw, LANES), qmap(col("a_kwvw", LANES))),
                  pl.BlockSpec((1, ncp, LANES), lambda b, i: (b, 0, 0)),
                  pl.BlockSpec((None, qw, LANES), tmap(0)),
                  pl.BlockSpec((None, qw, LANES), tmap(1)),
                  pl.BlockSpec((None, qw, LANES), tmap(2)),
                  whole(ovt), whole(cmpb), whole(winb), whole(caub)],
        out_specs=pl.BlockSpec((qw, 384), lambda b, i: (b * nb + i, 0)),
        out_shape=jax.ShapeDtypeStruct((batch * seq, 384), BF16),
        scratch_shapes=[pltpu.VMEM((seq, LANES), BF16), pltpu.VMEM((HEAD_DIM + ONES_ROWS, seq), BF16),
                        pltpu.VMEM((seq, LANES), BF16), pltpu.VMEM((HEAD_DIM + ONES_ROWS, seq), BF16),
                        pltpu.VMEM((ncp, LANES), BF16), pltpu.VMEM((HEAD_DIM, ncp), BF16),
                        pltpu.VMEM((n_slc, qw), F32)],
        compiler_params=_params(2),
        name="nsa",
    )(proj, proj, proj, proj, kvcmp, tabs, tabs, tabs, ovt, cmpb, winb, caub)


SEARCH_BISECT_FROM = 24
SEARCH_MAX_PASSES = 64
SEARCH_KEPT_LIMIT = 6
SEARCH_BLIND_PASSES = 12
SEARCH_SNAP_KEYS = 3
INT_MAX = 2 ** 31 - 1
FLT_MIN_NORMAL = 1.17549435e-38
ZERO_BAND_END = 1 << 23


def _tree_rows(x, op):
    parts = [x[r:r + 8, :] for r in range(0, x.shape[0], 8)]
    while len(parts) > 1:
        nxt = [op(parts[a], parts[a + 1]) for a in range(0, len(parts) - 1, 2)]
        if len(parts) % 2:
            nxt.append(parts[-1])
        parts = nxt
    return parts[0]


def _f2key(v):
    bits = lax.bitcast_convert_type(v, I32)
    return bits ^ ((bits >> 31) & 0x7FFFFFFF)


def _key2f(k):
    return lax.bitcast_convert_type(k ^ ((k >> 31) & 0x7FFFFFFF), F32)


def _dsa_kernel(q_ref, iq_ref, ikw_ref, kv_ref, c_ref, s1_ref, s2_ref, ci_ref, si1_ref, si2_ref, o_ref,
                k_scr, vt_scr, ik_scr, key_scr, *, top, idx_bits):
    i = pl.program_id(1)
    H = DSA_HEADS
    QW = q_ref.shape[0]
    HW = H * QW
    seq = key_scr.shape[0]
    lane_q = lax.broadcasted_iota(I32, (QW, LANES), 1)

    @pl.when(i == 0)
    def _init():
        k_scr[...] = jnp.zeros_like(k_scr)
        vt_scr[0:HEAD_DIM, :] = jnp.zeros((HEAD_DIM, seq), BF16)
        vt_scr[HEAD_DIM:, :] = jnp.ones((ONES_ROWS, seq), BF16)
        ik_scr[...] = jnp.zeros_like(ik_scr)

    c, s1, s2 = c_ref[...], s1_ref[...], s2_ref[...]
    ci, si1, si2 = ci_ref[...], si1_ref[...], si2_ref[...]
    r0 = pl.multiple_of(i * QW, QW)
    kv = kv_ref[...]
    k_scr[pl.ds(r0, QW), :] = jnp.where(lane_q < HEAD_DIM, _rope(kv, c, s1, s2, ROPE_DIM // 2), 0.0).astype(BF16)
    vt_scr[0:HEAD_DIM, pl.ds(r0, QW)] = kv.T[HEAD_DIM:2 * HEAD_DIM, :].astype(BF16)
    ikw = ikw_ref[...]
    ik_scr[pl.ds(r0, QW), :] = jnp.where(lane_q < IDX_DIM, _rope(ikw, ci, si1, si2, IDX_ROPE_DIM // 2), 0.0).astype(BF16)

    qr = _rope_wide(q_ref[...] * (SCALE * LOG2E), c, s1, s2, ROPE_DIM // 2)
    qr_st = _stack_heads_t(qr.T, H, HEAD_DIM).astype(BF16)
    iqr = _rope_wide(iq_ref[...], ci, si1, si2, IDX_ROPE_DIM // 2)
    iq_st = _stack_heads_t(iqr.T, IDX_HEADS, IDX_DIM).astype(BF16)
    wt = ikw.T * ((IDX_DIM ** -0.5) * (IDX_HEADS ** -0.5))
    w_st = jnp.concatenate([wt[IDX_DIM + h:IDX_DIM + h + 1, :] for h in range(IDX_HEADS)], axis=1)

    n_chunks = (i * QW + QW + KC - 1) // KC
    k_io = lax.broadcasted_iota(I32, (KC, QW), 0)
    tq = i * QW + lax.broadcasted_iota(I32, (KC, QW), 1)

    sub_io = lax.broadcasted_iota(I32, (QB, QW), 0)
    sub_tq = i * QW + lax.broadcasted_iota(I32, (QB, QW), 1)
    zkey0 = seq - sub_io

    def score_body(cidx, carry):
        mx, mn = carry
        k0 = pl.multiple_of(cidx * KC, KC)
        for sb in range(KC // QB):
            kb = pl.multiple_of(k0 + sb * QB, QB)
            d = jnp.dot(ik_scr[pl.ds(kb, QB), :], iq_st, preferred_element_type=F32)
            r = jnp.maximum(d, 0.0) * w_st
            sc = r[:, 0:QW]
            for h in range(1, IDX_HEADS):
                sc = sc + r[:, QW * h:QW * (h + 1)]
            key = jnp.where(jnp.abs(sc) < FLT_MIN_NORMAL, zkey0 - kb, _f2key(sc))
            key_scr[pl.ds(kb, QB), :] = jnp.where(sub_io + kb <= sub_tq, key, INT_MIN)
            mx = jnp.maximum(mx, _tree_rows(key, jnp.maximum))
            mn = jnp.minimum(mn, _tree_rows(key, jnp.minimum))
        return mx, mn

    mx8, mn8 = lax.fori_loop(0, n_chunks, score_body,
                             (jnp.full((8, QW), INT_MIN, I32), jnp.full((8, QW), INT_MAX, I32)))

    def count(pred):
        def body(cidx, acc):
            k0 = pl.multiple_of(cidx * KC, KC)
            return acc + _tree_rows(pred(key_scr[pl.ds(k0, KC), :], k_io + k0), jnp.add)
        acc8 = lax.fori_loop(0, n_chunks, body, jnp.zeros((8, QW), F32))
        return jnp.sum(acc8, axis=0, keepdims=True)

    topf = float(top)
    lo0 = functools.reduce(jnp.minimum, [mn8[r:r + 1, :] for r in range(8)])
    hi0 = functools.reduce(jnp.maximum, [mx8[r:r + 1, :] for r in range(8)]) + 1
    n_causal = (i * QW + 1 + lax.broadcasted_iota(I32, (1, QW), 1)).astype(F32)
    log_top = float(np.log(top))

    def in_zero_band(lo, hi):
        return jnp.where(lo >= 1, jnp.where(hi <= ZERO_BAND_END, 1.0, 0.0), 0.0) > 0.5

    def open_keys(lo, hi, c_lo, c_hi):
        return jnp.max(jnp.where(c_lo > topf, jnp.where(hi > lo + 1, c_lo - c_hi, 0.0), 0.0)).astype(I32)

    def search_cond(c):
        return jnp.logical_and(c[0] < SEARCH_MAX_PASSES, c[1] > SEARCH_SNAP_KEYS)

    def search_pass(it, st):
        lo, hi, c_lo, c_hi, kept, w_lo, w_hi = st
        f_lo = (jnp.log(c_lo) - log_top) * w_lo
        f_hi = (log_top - jnp.log(jnp.maximum(c_hi, 0.5))) * w_hi
        frac = f_lo / (f_lo + f_hi)
        v_lo, v_hi = _key2f(lo), _key2f(hi)
        cand = _f2key(v_lo + frac * (v_hi - v_lo))
        lo_f, hi_f = lo.astype(F32), hi.astype(F32)
        cand = jnp.where(in_zero_band(lo, hi), (lo_f + frac * (hi_f - lo_f)).astype(I32), cand)
        key_mid = (lo >> 1) + (hi >> 1) + (lo & hi & 1)
        cand = jnp.where(jnp.abs(kept) >= SEARCH_KEPT_LIMIT, key_mid, cand)
        cand = jnp.where(it >= SEARCH_BISECT_FROM, key_mid, cand)
        cand = jnp.where(it == 0, 1, jnp.where(it == 1, ZERO_BAND_END, cand))
        cand = jnp.minimum(jnp.maximum(cand, lo + 1), hi - 1)
        cnt = count(lambda key, kpos: jnp.where(key >= cand, 1.0, 0.0))
        active = jnp.where(c_lo > topf, jnp.where(hi > lo + 1, 1.0, 0.0), 0.0) > 0.5
        up = jnp.where(active, jnp.where(cnt >= topf, 1.0, 0.0), 0.0) > 0.5
        dn = jnp.where(active, jnp.where(cnt >= topf, 0.0, 1.0), 0.0) > 0.5
        w_hi = jnp.where(up, jnp.where(kept < 0, 0.5 * w_hi, 1.0), jnp.where(dn, 1.0, w_hi))
        w_lo = jnp.where(dn, jnp.where(kept > 0, 0.5 * w_lo, 1.0), jnp.where(up, 1.0, w_lo))
        kept = jnp.where(up, jnp.where(kept < 0, kept - 1, -1), jnp.where(dn, jnp.where(kept > 0, kept + 1, 1), kept))
        lo, c_lo = jnp.where(up, cand, lo), jnp.where(up, cnt, c_lo)
        hi, c_hi = jnp.where(dn, cand, hi), jnp.where(dn, cnt, c_hi)
        hi = jnp.where(in_zero_band(lo, hi), jnp.minimum(hi, seq + 1), hi)
        return lo, hi, c_lo, c_hi, kept, w_lo, w_hi

    def search_body(c):
        st = search_pass(c[0], c[2])
        return c[0] + 1, open_keys(*st[:4]), st

    zero_f, one_f = jnp.zeros((1, QW), F32), jnp.ones((1, QW), F32)
    st0 = (lo0, hi0, n_causal, zero_f, jnp.zeros((1, QW), I32), one_f, one_f)
    n_blind = jnp.where(i * QW + QW > top, SEARCH_BLIND_PASSES, 0)
    st1 = lax.fori_loop(0, n_blind, search_pass, st0)
    _, _, st2 = lax.while_loop(search_cond, search_body, (n_blind, open_keys(*st1[:4]), st1))

    def snap_body(c):
        lo, hi, c_lo, c_hi = c[1]

        def below_hi(cidx, acc):
            k0 = pl.multiple_of(cidx * KC, KC)
            key = key_scr[pl.ds(k0, KC), :]
            return jnp.maximum(acc, _tree_rows(jnp.where(key < hi, key, INT_MIN), jnp.maximum))
        top8 = lax.fori_loop(0, n_chunks, below_hi, jnp.full((8, QW), INT_MIN, I32))
        cand = functools.reduce(jnp.maximum, [top8[r:r + 1, :] for r in range(8)])
        cnt = count(lambda key, kpos: jnp.where(key >= cand, 1.0, 0.0))
        active = jnp.where(c_lo > topf, jnp.where(hi > lo + 1, 1.0, 0.0), 0.0) > 0.5
        up = jnp.where(active, jnp.where(cnt >= topf, 1.0, 0.0), 0.0) > 0.5
        dn = jnp.where(active, jnp.where(cnt >= topf, 0.0, 1.0), 0.0) > 0.5
        lo, c_lo = jnp.where(up, cand, lo), jnp.where(up, cnt, c_lo)
        hi, c_hi = jnp.where(up, cand + 1, jnp.where(dn, cand, hi)), jnp.where(dn, cnt, c_hi)
        return open_keys(lo, hi, c_lo, c_hi), (lo, hi, c_lo, c_hi)

    _, (thr, _, c_lo, c_hi) = lax.while_loop(lambda c: c[0] > 0, snap_body, (open_keys(*st2[:4]), st2[:4]))

    tied = c_lo > topf
    need1 = (topf - 1.0) - c_hi
    all_pos = jnp.full((1, QW), 2 ** idx_bits - 1, I32)

    def tie_search():
        def tie_body(bi, cur):
            cand = cur | lax.shift_left(jnp.int32(1), idx_bits - 1 - bi)
            cnt = count(lambda key, kpos: jnp.where(key == thr, jnp.where(kpos < cand, 1.0, 0.0), 0.0))
            return jnp.where(cnt <= need1, cand, cur)
        return lax.fori_loop(0, idx_bits, tie_body, jnp.zeros((1, QW), I32))

    any_tied = jnp.max(jnp.where(tied, 1.0, 0.0)) > 0.5
    last = jnp.where(tied, lax.cond(any_tied, tie_search, lambda: all_pos), all_pos)

    half_neg = int(np.float32(0.5 * NEG).view(np.int32))
    key_floor = half_neg ^ 0x7FFFFFFF

    def att_chunk(size, base):
        pos_io = lax.broadcasted_iota(I32, (size, QW), 0)

        def body(cidx, state):
            m, acc = state
            k0 = pl.multiple_of(base + cidx * size, KC)
            key = key_scr[pl.ds(k0, size), :]
            kpos = pos_io + k0
            bias = jnp.where(key > thr, 0.0, jnp.where(key == thr, jnp.where(kpos <= last, 0.0, NEG), NEG))
            bias = jnp.where(key > key_floor, bias, NEG).astype(BF16)
            s = (jnp.dot(k_scr[pl.ds(k0, size), :], qr_st, preferred_element_type=F32).astype(BF16)
                 + jnp.concatenate([bias] * H, axis=1))
            m_new = jnp.maximum(m, jnp.max(s, axis=0, keepdims=True))
            alpha = jnp.exp2((m - m_new).astype(F32))
            e = jnp.exp2(s - m_new)
            acc = alpha * acc + jnp.dot(vt_scr[:, pl.ds(k0, size)], e, preferred_element_type=F32)
            return m_new, acc
        return body

    init = (jnp.full((1, HW), NEG, BF16), jnp.zeros((HEAD_DIM + ONES_ROWS, HW), F32))
    n_long = n_chunks // ATT_LONG
    state = lax.fori_loop(0, n_long, att_chunk(ATT_LONG * KC, 0), init)
    m_a, acc_a = lax.fori_loop(0, n_chunks - n_long * ATT_LONG, att_chunk(KC, n_long * (ATT_LONG * KC)), state)
    l_a = acc_a[HEAD_DIM:HEAD_DIM + 1, :]
    o = jnp.where(m_a.astype(F32) > 0.5 * NEG, acc_a[:HEAD_DIM, :] / l_a, 0.0)
    outs = [o[:, QW * h:QW * (h + 1)] for h in range(H)]
    outs.append(jnp.zeros((o_ref.shape[1] - H * HEAD_DIM, QW), F32))
    o_ref[...] = jnp.concatenate(outs, axis=0).T.astype(BF16)


def _dsa(proj, tabs, tabs_i, batch, seq):
    qw = DSA_QW
    nb = seq // qw
    col = lambda name, w: _OFF[name] // w
    qmap = lambda cidx: (lambda b, i: (b * nb + i, cidx))
    tmap = lambda k: (lambda b, i: (k, i, 0))
    top = min(DSA_TOPK_MAX, seq // 4)
    kern = functools.partial(_dsa_kernel, top=top, idx_bits=int(seq).bit_length())
    return pl.pallas_call(
        kern,
        grid=(batch, nb),
        in_specs=[pl.BlockSpec((qw, 384), qmap(col("b_q", 384))),
                  pl.BlockSpec((qw, 256), qmap(col("b_iq", 256))),
                  pl.BlockSpec((qw, LANES), qmap(col("b_ikiw", LANES))),
                  pl.BlockSpec((qw, LANES), qmap(col("b_kv", LANES))),
                  pl.BlockSpec((None, qw, LANES), tmap(0)),
                  pl.BlockSpec((None, qw, LANES), tmap(1)),
                  pl.BlockSpec((None, qw, LANES), tmap(2)),
                  pl.BlockSpec((None, qw, LANES), tmap(0)),
                  pl.BlockSpec((None, qw, LANES), tmap(1)),
                  pl.BlockSpec((None, qw, LANES), tmap(2))],
        out_specs=pl.BlockSpec((qw, 384), lambda b, i: (b * nb + i, 0)),
        out_shape=jax.ShapeDtypeStruct((batch * seq, 384), BF16),
        scratch_shapes=[pltpu.VMEM((seq, LANES), BF16), pltpu.VMEM((HEAD_DIM + ONES_ROWS, seq), BF16),
                        pltpu.VMEM((seq, LANES), BF16), pltpu.VMEM((seq, qw), I32)],
        compiler_params=_params(2),
        name="dsa",
    )(proj, proj, proj, proj, tabs, tabs, tabs, tabs_i, tabs_i, tabs_i)


DIL_MB = max(d for _, d in DIL_PAIRS) * QB
DIL_VMEM_LIMIT = 56 * 1024 * 1024
DIL_UNROLL = 8


def _dil_layout(g):
    dil = DIL_PAIRS[g][1]
    per = DIL_MB // dil
    return dil, per, per + QB


def _dil_bias_table():
    u = np.arange(2 * QB)[:, None]
    diff = QB + np.arange(LANES)[None, :] - u
    ok = (diff >= 0) & (diff <= QB)
    return _bias_table(np.concatenate([ok, ok & (u >= QB)], axis=0))


def _dil_kernel(*refs):
    G, HG = len(DIL_PAIRS), DIL_HEADS_PER_GROUP
    q_refs, k_refs, v_refs = refs[0:G], refs[G:2 * G], refs[2 * G:3 * G]
    c_ref, s1_ref, s2_ref, bias_ref, o_ref = refs[3 * G:3 * G + 5]
    kds, vds, ogs, lss = (refs[3 * G + 5 + n * G:3 * G + 5 + (n + 1) * G] for n in range(4))
    step = pl.program_id(1)
    half = ROPE_DIM // 2

    @pl.when(step == 0)
    def _zero():
        for g in range(G):
            kds[g][...] = jnp.zeros_like(kds[g])
            vds[g][...] = jnp.zeros_like(vds[g])

    @pl.when(step > 0)
    def _carry_halo():
        for g in range(G):
            dil, per, stride = _dil_layout(g)
            for r in range(dil):
                b0 = r * stride
                kds[g][b0:b0 + QB, :] = kds[g][b0 + per:b0 + per + QB, :]
                vds[g][:, b0:b0 + QB] = vds[g][:, b0 + per:b0 + per + QB]

    for g in range(G):
        dil, per, stride = _dil_layout(g)
        piece = min(per, KC)
        for r in range(dil):
            for p0 in range(0, per, piece):
                rows = pl.ds(r + dil * p0, piece, stride=dil)
                kr = _rope(k_refs[g][rows, :], c_ref[rows, :], s1_ref[rows, :], s2_ref[rows, :], half)
                d0 = r * stride + QB + p0
                kds[g][d0:d0 + piece, :] = kr.astype(BF16)
                vds[g][:, d0:d0 + piece] = v_refs[g][rows, :].T.astype(BF16)

    zero = jnp.zeros((HEAD_DIM, LANES), F32)
    for g in range(G):
        dil, per, stride = _dil_layout(g)
        nblk = per // QB

        def block(jb, _, r=0, g=g, dil=dil, stride=stride):
            p0 = pl.multiple_of(dil * QB * jb, dil * QB)
            win = pl.ds(p0, dil * QB)
            rows = pl.ds(r, QB, stride=dil)
            q = _rope(q_refs[g].at[win, :][rows, :] * (SCALE * LOG2E), c_ref.at[win, :][rows, :],
                      s1_ref.at[win, :][rows, :], s2_ref.at[win, :][rows, :], half)
            qt = q.T
            q2 = jnp.concatenate([jnp.concatenate([qt[:HEAD_DIM], zero], axis=0),
                                  jnp.concatenate([zero, qt[HEAD_DIM:]], axis=0)], axis=1).astype(BF16)
            kb = pl.multiple_of(r * stride + QB * jb, QB)
            first = jnp.logical_and(step == 0, jb == 0)
            bias = bias_ref[pl.ds(pl.multiple_of(jnp.where(first, 2 * QB, 0), QB), 2 * QB), :]
            s = (jnp.dot(kds[g][pl.ds(kb, 2 * QB), :], q2, preferred_element_type=F32)
                 + jnp.concatenate([bias] * HG, axis=1))
            e, m, den = _softmax2_cols(s)
            lse = m + jnp.log2(den)
            rden = 1.0 / den
            parts, lparts = [], []
            for hg in range(HG):
                sl = slice(LANES * hg, LANES * (hg + 1))
                vt = vds[g][HEAD_DIM * hg:HEAD_DIM * (hg + 1), pl.ds(kb, 2 * QB)]
                parts.append(jnp.dot(vt, e[:, sl].astype(BF16), preferred_element_type=F32) * rden[:, sl])
                lparts.append(jnp.broadcast_to(lse[:, sl], (HEAD_DIM, LANES)))
            tile = jnp.concatenate(parts + lparts, axis=0).T
            ogs[g].at[win, :][rows, :] = tile[:, :LANES]
            lss[g].at[win, :][rows, :] = tile[:, LANES:]
            return 0

        for r in range(dil):
            lax.fori_loop(0, nblk, functools.partial(block, r=r), 0, unroll=min(nblk, DIL_UNROLL))

    def mix(ti, _):
        rows = pl.ds(pl.multiple_of(ti * KC, KC), KC)
        ls = [lss[g][rows, :] for g in range(G)]
        mx = functools.reduce(jnp.maximum, ls)
        ex = [jnp.exp2(x - mx) for x in ls]
        rtot = 1.0 / functools.reduce(lambda a, b: a + b, ex)
        o_ref[rows, :] = jnp.concatenate([ex[g] * rtot * ogs[g][rows, :] for g in range(G)], axis=1).astype(BF16)
        return 0

    lax.fori_loop(0, DIL_MB // KC, mix, 0)


def _dil(proj, tabs, batch, seq):
    assert seq % DIL_MB == 0 and all(w // d == QB for w, d in DIL_PAIRS)
    nm = seq // DIL_MB
    G = len(DIL_PAIRS)
    w = DIL_HEADS * HEAD_DIM
    gmap = lambda name, g: (lambda b, j: (b * nm + j, _OFF[name] // LANES + g))
    tmap = lambda k: (lambda b, j: (k, j, 0))
    bias = _dil_bias_table()
    rows = [_dil_layout(g)[0] * _dil_layout(g)[2] for g in range(G)]
    return pl.pallas_call(
        _dil_kernel,
        grid=(batch, nm),
        in_specs=([pl.BlockSpec((DIL_MB, LANES), gmap(name, g)) for name in ("c_q", "c_k", "c_v") for g in range(G)]
                  + [pl.BlockSpec((None, DIL_MB, LANES), tmap(k)) for k in range(3)]
                  + [pl.BlockSpec(bias.shape, lambda b, j: (0, 0))]),
        out_specs=pl.BlockSpec((DIL_MB, w), lambda b, j: (b * nm + j, 0)),
        out_shape=jax.ShapeDtypeStruct((batch * seq, w), BF16),
        scratch_shapes=([pltpu.VMEM((n, LANES), BF16) for n in rows] + [pltpu.VMEM((LANES, n), BF16) for n in rows]
                        + [pltpu.VMEM((DIL_MB, LANES), F32)] * (2 * G)),
        compiler_params=_params(2, DIL_VMEM_LIMIT),
        name="dilated",
    )(*([proj] * (3 * G)), tabs, tabs, tabs, bias)


def _overlap_t(seq):
    n_cmp_pad = seq // CMP_STRIDE
    n_slc = seq // SLC_BLOCK
    c_start = np.arange(n_cmp_pad) * CMP_STRIDE
    s_start = np.arange(n_slc) * SLC_BLOCK
    ov = ((c_start[None, :] < s_start[:, None] + SLC_BLOCK) & (c_start[None, :] + CMP_BLOCK > s_start[:, None]))
    return jnp.asarray(ov.astype(np.float32), dtype=BF16)


def _layer(x2, batch, seq, layer, norm1_g, w_in, cmp_pos, cmp_w1, cmp_w2, w_out, norm2_g, w_up_bf, conv_w, conv_b,
           w_down_bf, final_g, final_norm, tabs, tabs_i, ovt):
    n = batch * seq
    proj = _inproj(x2, norm1_g, _regroup_w_in(w_in, layer))

    w1 = cmp_w1.reshape(2, CMP_BLOCK, HEAD_DIM, CMP_HIDDEN)
    z1 = jnp.zeros_like(w1[0])
    w1_cat = jnp.concatenate([jnp.concatenate([w1[0], z1], axis=2),
                              jnp.concatenate([z1, w1[1]], axis=2)], axis=1).astype(BF16)
    z2 = jnp.zeros((CMP_HIDDEN, HEAD_DIM), cmp_w2.dtype)
    w2_cat = jnp.concatenate([jnp.concatenate([cmp_w2[0], z2], axis=1),
                              jnp.concatenate([z2, cmp_w2[1]], axis=1)], axis=0).astype(BF16)
    pos_cat = jnp.concatenate([cmp_pos[0], cmp_pos[1]], axis=1)
    kvcmp = _nsa_compress(proj, pos_cat, w1_cat, w2_cat, batch, seq)

    o_a = _nsa(proj, kvcmp, tabs, ovt, batch, seq)
    o_b = _dsa(proj, tabs, tabs_i, batch, seq)
    o_c = _dil(proj, tabs, batch, seq)

    hp = NSA_HEADS * HEAD_DIM
    zrow = jnp.zeros((384 - hp, D_MODEL), w_out.dtype)
    w_out_pad = jnp.concatenate([w_out[0:hp], zrow, w_out[hp:2 * hp], zrow, w_out[2 * hp:]], axis=0).astype(BF16)
    x2 = _outproj(o_a, o_b, o_c, w_out_pad, x2)

    act = _ffn_up(x2, norm2_g, w_up_bf, layer, conv_w, conv_b, seq)
    return _ffn_down(act, w_down_bf, layer, x2, final_g, final_norm)


def kernel(x, norm1_g, w_in, cmp_pos, cmp_w1, cmp_w2, w_out, norm2_g, w_up, conv_w, conv_b, w_down, final_g):
    batch, seq, d = x.shape
    depth = w_in.shape[0]
    tabs = _rope_lane_tables(seq, HEAD_DIM, ROPE_DIM)
    tabs_i = _rope_lane_tables(seq, IDX_DIM, IDX_ROPE_DIM)
    ovt = _overlap_t(seq)
    x2 = x.reshape(batch * seq, d)
    w_up_bf, w_down_bf = w_up.astype(BF16), w_down.astype(BF16)
    for li in range(depth):
        x2 = _layer(x2, batch, seq, li, norm1_g[li], w_in, cmp_pos[li], cmp_w1[li], cmp_w2[li], w_out[li],
                    norm2_g[li], w_up_bf, conv_w[li], conv_b[li], w_down_bf, final_g, li == depth - 1,
                    tabs, tabs_i, ovt)
    return x2.reshape(batch, seq, d)
```

```python
import functools

import numpy as np
import jax
import jax.numpy as jnp
from jax import lax
from jax.experimental import pallas as pl
from jax.experimental.pallas import tpu as pltpu

F32 = jnp.float32
BF16 = jnp.bfloat16
I32 = jnp.int32

D_MODEL = 1024
HEAD_DIM = 64
ROPE_DIM = HEAD_DIM // 4
ROPE_THETA = 500000.0
NORM_EPS = 1e-6
SCALE = HEAD_DIM ** -0.5
LOG2E = 1.4426950408889634
NEG = -1e30
FORCE = 1e9
NSA_HEADS = 5
CMP_BLOCK = 32
CMP_STRIDE = 16
CMP_HIDDEN = 128
SLC_BLOCK = 64
SLC_TOPN = 16
WIN = 512
DSA_HEADS = 5
IDX_HEADS = 8
IDX_DIM = 32
IDX_ROPE_DIM = IDX_DIM // 4
DSA_TOPK_MAX = 256
DIL_PAIRS = ((128, 1), (512, 4), (2048, 16))
DIL_HEADS_PER_GROUP = 2
DIL_HEADS = len(DIL_PAIRS) * DIL_HEADS_PER_GROUP
D_FF = 2816
CONV_WIDTH = 3

LANES = 128
QB = 128
DSA_QW = 256
NSA_QW = 256
KC = 512
ONES_ROWS = 16
ATT_LONG = 2
ROW_TILE = 512
FFN_UP_ROW_TILE = 1024
STREAM_BUFFERS = 3
WIDE_ROW_TILE = 1024
HALO = 16
VMEM_LIMIT = 48 * 1024 * 1024
INT_MIN = -2 ** 31

_A0 = 0
_B0 = 719
_C0 = 1463
_SEGS = (
    ("c_q", _C0, 384, 384), ("c_k", _C0 + 384, 384, 384), ("c_v", _C0 + 768, 384, 384),
    ("a_q", _A0, 320, 384), ("b_q", _B0, 320, 384),
    ("a_kcvc", _A0 + 320, 128, 128),
    ("b_iq", _B0 + 448, 256, 256),
    ("a_ksvs", _A0 + 448, 128, 128), ("a_kwvw", _A0 + 576, 128, 128),
    ("a_g", _A0 + 704, 15, 128),
    ("b_kv", _B0 + 320, 128, 128),
    ("b_ikiw", _B0 + 704, 40, 128),
)
P_COLS = sum(s[3] for s in _SEGS)


def _seg_offsets():
    offs, o = {}, 0
    for name, _, _, pw in _SEGS:
        offs[name] = o
        o += pw
    return offs


_OFF = _seg_offsets()


def _regroup_kernel(w_ref, o_ref):
    for name, src, wdt, pw in _SEGS:
        dst = _OFF[name]
        o_ref[:, dst:dst + wdt] = w_ref[:, src:src + wdt].astype(BF16)
        if pw > wdt:
            o_ref[:, dst + wdt:dst + pw] = jnp.zeros((o_ref.shape[0], pw - wdt), BF16)


def _regroup_w_in(w, layer):
    _, d, n_in = w.shape
    rt = 256
    return pl.pallas_call(
        _regroup_kernel,
        grid=(d // rt,),
        in_specs=[pl.BlockSpec((None, rt, n_in), lambda i: (layer, i, 0))],
        out_specs=pl.BlockSpec((rt, P_COLS), lambda i: (i, 0)),
        out_shape=jax.ShapeDtypeStruct((d, P_COLS), BF16),
        compiler_params=_params(1),
        name="regroup_w_in",
    )(w)


def _rope_lane_tables(L, head_dim, rot_dim):
    half = rot_dim // 2
    inv = 1.0 / (ROPE_THETA ** (np.arange(0, rot_dim, 2, dtype=np.float32) / np.float32(rot_dim)))
    ang = np.arange(L, dtype=np.float32)[:, None] * inv[None, :]
    cos, sin = np.cos(ang).astype(np.float32), np.sin(ang).astype(np.float32)
    d = np.arange(LANES) % head_dim
    lo, hi = d < half, (d >= half) & (d < rot_dim)
    c = np.ones((L, LANES), np.float32)
    s1 = np.zeros((L, LANES), np.float32)
    s2 = np.zeros((L, LANES), np.float32)
    c[:, lo] = cos[:, d[lo]]
    c[:, hi] = cos[:, d[hi] - half]
    s1[:, lo] = -sin[:, d[lo]]
    s2[:, hi] = sin[:, d[hi] - half]
    return jnp.asarray(np.stack([c, s1, s2]))


def _rope(x, c, s1, s2, half):
    xp = pltpu.roll(x, LANES - half, 1)
    xm = pltpu.roll(x, half, 1)
    return x * c + xp * s1 + xm * s2


def _rope_wide(x, c, s1, s2, half):
    n = x.shape[1] // LANES
    return jnp.concatenate([_rope(x[:, LANES * s:LANES * (s + 1)], c, s1, s2, half) for s in range(n)], axis=1)


def _softmax2_cols(s):
    m = jnp.max(s, axis=0, keepdims=True)
    e = jnp.exp2(s - m)
    return e, m, jnp.sum(e, axis=0, keepdims=True)


def _bias_table(ok):
    return jnp.asarray(np.where(ok, 0.0, NEG).astype(np.float32))


def _cmp_bias_table(seq, qw):
    shift = (qw // CMP_STRIDE) * (seq // qw - 1)
    u = np.arange(seq // CMP_STRIDE + shift)[:, None]
    lane = np.arange(qw)[None, :]
    return _bias_table(CMP_STRIDE * (u - shift) + CMP_BLOCK - 1 <= lane)


def _win_bias_table(seq, qw):
    span = min(WIN + qw, seq)
    cmax = span - qw
    u = np.arange(span + cmax)[:, None]
    diff = cmax + np.arange(qw)[None, :] - u
    return _bias_table((diff >= 0) & (diff < WIN))


def _causal_bias_table(qw):
    kmax = ATT_LONG * KC
    u = np.arange(2 * kmax)[:, None]
    return _bias_table(u <= kmax + np.arange(qw)[None, :])


def _params(n_grid, vmem=VMEM_LIMIT):
    return pltpu.CompilerParams(dimension_semantics=("arbitrary",) * n_grid, vmem_limit_bytes=vmem)


def _rmsnorm_rows(x, g):
    return x * lax.rsqrt(jnp.mean(x * x, axis=-1, keepdims=True) + NORM_EPS) * g


def _inproj_kernel(x_ref, g_ref, w_ref, o_ref):
    hn = _rmsnorm_rows(x_ref[...], g_ref[...]).astype(BF16)
    n = w_ref.shape[1]
    for c0 in range(0, n, 512):
        c1 = min(c0 + 512, n)
        o_ref[:, c0:c1] = jnp.dot(hn, w_ref[:, c0:c1], preferred_element_type=F32)


def _inproj(x2, g, w_bf):
    n, d = x2.shape
    pc = w_bf.shape[1]
    return pl.pallas_call(
        _inproj_kernel,
        grid=(n // ROW_TILE,),
        in_specs=[pl.BlockSpec((ROW_TILE, d), lambda i: (i, 0)),
                  pl.BlockSpec((1, d), lambda i: (0, 0)),
                  pl.BlockSpec((d, pc), lambda i: (0, 0))],
        out_specs=pl.BlockSpec((ROW_TILE, pc), lambda i: (i, 0)),
        out_shape=jax.ShapeDtypeStruct((n, pc), F32),
        compiler_params=_params(1),
        name="inproj",
    )(x2, g.reshape(1, d), w_bf)


def _outproj(oa, ob, oc, w_bf, x2):
    n, d = x2.shape
    mw = oa.shape[1]
    tm = WIDE_ROW_TILE

    def outer(oa_hbm, ob_hbm, oc_hbm, w_ref, x_hbm, o_hbm):
        def step(oa_ref, ob_ref, oc_ref, x_ref, o_ref):
            mix = jnp.concatenate([oa_ref[...], ob_ref[...], oc_ref[...]], axis=1)
            o_ref[...] = x_ref[...] + jnp.dot(mix, w_ref[...], preferred_element_type=F32)

        deep = lambda w: pl.BlockSpec((tm, w), lambda i: (i, 0), pipeline_mode=pl.Buffered(STREAM_BUFFERS))
        pltpu.emit_pipeline(step, grid=(n // tm,), in_specs=[deep(mw), deep(mw), deep(mw), deep(d)],
                            out_specs=[pl.BlockSpec((tm, d), lambda i: (i, 0))])(oa_hbm, ob_hbm, oc_hbm, x_hbm, o_hbm)

    anywhere = pl.BlockSpec(memory_space=pl.ANY)
    return pl.pallas_call(
        outer,
        in_specs=[anywhere, anywhere, anywhere, pl.BlockSpec(memory_space=pltpu.VMEM), anywhere],
        out_specs=anywhere,
        out_shape=jax.ShapeDtypeStruct((n, d), F32),
        compiler_params=pltpu.CompilerParams(vmem_limit_bytes=VMEM_LIMIT),
        name="outproj",
    )(oa, ob, oc, w_bf, x2)


def _ffn_up_kernel(x_ref, xh_ref, g_ref, wa_ref, wu_ref, cw_ref, cb_ref, o_ref, hn_scr, hh_scr, a_scr, *, tiles_per_seq):
    i, j = pl.program_id(0), pl.program_id(1)
    tm = x_ref.shape[0]

    @pl.when(j == 0)
    def _norm():
        hn_scr[...] = _rmsnorm_rows(x_ref[...], g_ref[...]).astype(BF16)
        keep = jnp.where(i % tiles_per_seq == 0, 0.0, 1.0)
        hh_scr[...] = (_rmsnorm_rows(xh_ref[...], g_ref[...]) * keep).astype(BF16)

    hn = hn_scr[...]
    a = jnp.dot(hn, wa_ref[...], preferred_element_type=F32)
    u = jnp.dot(hn, wu_ref[...], preferred_element_type=F32)
    a_scr[0:HALO, :] = jnp.dot(hh_scr[...], wa_ref[...], preferred_element_type=F32)
    a_scr[HALO:HALO + tm, :] = a
    a1 = a_scr[pl.ds(HALO - 1, tm), :]
    a2 = a_scr[pl.ds(HALO - 2, tm), :]
    cw = cw_ref[...]
    conv = cw[0:1, :] * a2 + cw[1:2, :] * a1 + cw[2:3, :] * a + cb_ref[...]
    o_ref[...] = (conv * jax.nn.sigmoid(conv) * u).astype(BF16)


def _ffn_up(x2, g, w_up_bf, layer, conv_w, conv_b, seq_len):
    n, d = x2.shape
    tn = D_FF // 2
    nj = D_FF // tn
    tm = FFN_UP_ROW_TILE
    hb = tm // HALO
    kern = functools.partial(_ffn_up_kernel, tiles_per_seq=seq_len // tm)
    return pl.pallas_call(
        kern,
        grid=(n // tm, nj),
        in_specs=[pl.BlockSpec((tm, d), lambda i, j: (i, 0)),
                  pl.BlockSpec((HALO, d), lambda i, j: (jnp.maximum(i * hb - 1, 0), 0)),
                  pl.BlockSpec((1, d), lambda i, j: (0, 0)),
                  pl.BlockSpec((None, d, tn), lambda i, j: (layer, 0, j)),
                  pl.BlockSpec((None, d, tn), lambda i, j: (layer, 0, j + nj)),
                  pl.BlockSpec((CONV_WIDTH, tn), lambda i, j: (0, j)),
                  pl.BlockSpec((1, tn), lambda i, j: (0, j))],
        out_specs=pl.BlockSpec((tm, tn), lambda i, j: (i, j)),
        out_shape=jax.ShapeDtypeStruct((n, D_FF), BF16),
        scratch_shapes=[pltpu.VMEM((tm, d), BF16), pltpu.VMEM((HALO, d), BF16),
                        pltpu.VMEM((tm + HALO, tn), F32)],
        compiler_params=_params(2),
        name="ffn_up",
    )(x2, x2, g.reshape(1, d), w_up_bf, w_up_bf, conv_w, conv_b.reshape(1, D_FF))


def _ffn_down_kernel(a_ref, w_ref, x_ref, g_ref, o_ref, *, final_norm):
    y = x_ref[...] + jnp.dot(a_ref[...], w_ref[...], preferred_element_type=F32)
    if final_norm:
        y = _rmsnorm_rows(y, g_ref[...])
    o_ref[...] = y


def _ffn_down(act, w_bf, layer, x2, final_g, final_norm):
    n, d = x2.shape
    k = act.shape[1]
    tm = WIDE_ROW_TILE
    return pl.pallas_call(
        functools.partial(_ffn_down_kernel, final_norm=final_norm),
        grid=(n // tm,),
        in_specs=[pl.BlockSpec((tm, k), lambda i: (i, 0)),
                  pl.BlockSpec((None, k, d), lambda i: (layer, 0, 0)),
                  pl.BlockSpec((tm, d), lambda i: (i, 0)),
                  pl.BlockSpec((1, d), lambda i: (0, 0))],
        out_specs=pl.BlockSpec((tm, d), lambda i: (i, 0)),
        out_shape=jax.ShapeDtypeStruct((n, d), F32),
        compiler_params=_params(1),
        name="ffn_down",
    )(act, w_bf, x2, final_g.reshape(1, d))


def _cmp_kernel(kv_ref, pos_ref, w1_ref, w2_ref, o_ref):
    ngrp = kv_ref.shape[0] // CMP_STRIDE
    lo = hi = None
    for p in range(CMP_STRIDE):
        x = kv_ref[pl.ds(p, ngrp, stride=CMP_STRIDE), :]
        a = jnp.dot((x + pos_ref[p:p + 1, :]).astype(BF16), w1_ref[p], preferred_element_type=F32)
        b = jnp.dot((x + pos_ref[CMP_STRIDE + p:CMP_STRIDE + p + 1, :]).astype(BF16), w1_ref[CMP_STRIDE + p],
                    preferred_element_type=F32)
        lo = a if lo is None else lo + a
        hi = b if hi is None else hi + b
    hid = jax.nn.gelu(lo + pltpu.roll(hi, ngrp - 1, 0))
    o_ref[0] = jnp.dot(hid.astype(BF16), w2_ref[...], preferred_element_type=F32)


def _nsa_compress(proj, pos_cat, w1_cat, w2_cat, batch, seq):
    ngrp = seq // CMP_STRIDE
    return pl.pallas_call(
        _cmp_kernel,
        grid=(batch,),
        in_specs=[pl.BlockSpec((seq, LANES), lambda i: (i, _OFF["a_kcvc"] // LANES)),
                  pl.BlockSpec(pos_cat.shape, lambda i: (0, 0)),
                  pl.BlockSpec(w1_cat.shape, lambda i: (0, 0, 0)),
                  pl.BlockSpec(w2_cat.shape, lambda i: (0, 0))],
        out_specs=pl.BlockSpec((1, ngrp, LANES), lambda i: (i, 0, 0)),
        out_shape=jax.ShapeDtypeStruct((batch, ngrp, LANES), F32),
        compiler_params=_params(1),
        name="nsa_compress",
    )(proj, pos_cat, w1_cat, w2_cat)


def _stack_heads_t(x_t, n_heads, hd):
    z = jnp.zeros((LANES - hd, x_t.shape[1]), F32)
    return jnp.concatenate(
        [jnp.concatenate([x_t[hd * h:hd * (h + 1), :], z], axis=0) for h in range(n_heads)], axis=1)


def _nsa_kernel(q_ref, g_ref, ksvs_ref, kwvw_ref, kvc_ref, c_ref, s1_ref, s2_ref, ovt_ref, cmpb_ref, winb_ref,
                caub_ref, o_ref, ks_scr, vst_scr, kw_scr, vwt_scr, kc_scr, vct_scr, blk_scr, *, n_top):
    i = pl.program_id(1)
    H = NSA_HEADS
    QW = q_ref.shape[0]
    HW = H * QW
    seq = ks_scr.shape[0]
    lane_q = lax.broadcasted_iota(I32, (QW, LANES), 1)

    @pl.when(i == 0)
    def _init():
        ks_scr[...] = jnp.zeros_like(ks_scr)
        kw_scr[...] = jnp.zeros_like(kw_scr)
        for vt in (vst_scr, vwt_scr):
            vt[0:HEAD_DIM, :] = jnp.zeros((HEAD_DIM, seq), BF16)
            vt[HEAD_DIM:, :] = jnp.ones((ONES_ROWS, seq), BF16)
        kvc = kvc_ref[0]
        lane_c = lax.broadcasted_iota(I32, kvc.shape, 1)
        kc_scr[...] = jnp.where(lane_c < HEAD_DIM, kvc, 0.0).astype(BF16)
        vct_scr[...] = kvc.T[HEAD_DIM:2 * HEAD_DIM, :].astype(BF16)

    c, s1, s2 = c_ref[...], s1_ref[...], s2_ref[...]
    r0 = pl.multiple_of(i * QW, QW)
    for src, kdst, vdst in ((ksvs_ref, ks_scr, vst_scr), (kwvw_ref, kw_scr, vwt_scr)):
        kv = src[...]
        kr = _rope(kv, c, s1, s2, ROPE_DIM // 2)
        kdst[pl.ds(r0, QW), :] = jnp.where(lane_q < HEAD_DIM, kr, 0.0).astype(BF16)
        vdst[0:HEAD_DIM, pl.ds(r0, QW)] = kv.T[HEAD_DIM:2 * HEAD_DIM, :].astype(BF16)

    q = q_ref[...] * (SCALE * LOG2E)
    qr = _rope_wide(q, c, s1, s2, ROPE_DIM // 2)
    q_st = _stack_heads_t(q.T, H, HEAD_DIM).astype(BF16)
    qr_st = _stack_heads_t(qr.T, H, HEAD_DIM).astype(BF16)
    tile_h = lambda b: jnp.concatenate([b] * H, axis=1)

    ncp = kc_scr.shape[0]
    cshift = cmpb_ref.shape[0] - ncp
    cb = cmpb_ref[pl.ds(pl.multiple_of(cshift - (QW // CMP_STRIDE) * i, 8), ncp), :]
    s_c = jnp.dot(kc_scr[...], q_st, preferred_element_type=F32) + tile_h(cb)
    e_c, _, den_c = _softmax2_cols(s_c)
    t_row = i * QW + (lax.broadcasted_iota(I32, (1, HW), 1) & (QW - 1))
    r_c = jnp.where(t_row >= CMP_BLOCK - 1, 1.0 / den_c, 0.0)
    o_cmp = jnp.dot(vct_scr[...], e_c.astype(BF16), preferred_element_type=F32) * r_c
    psum = e_c[:, 0:QW] * r_c[:, 0:QW]
    for h in range(1, H):
        psum = psum + e_c[:, QW * h:QW * (h + 1)] * r_c[:, QW * h:QW * (h + 1)]

    hi = psum.astype(BF16)
    r1 = psum - hi.astype(F32)
    mid = r1.astype(BF16)
    lo = (r1 - mid.astype(F32)).astype(BF16)
    ovt = ovt_ref[...]
    imp = (jnp.dot(ovt, hi, preferred_element_type=F32) + jnp.dot(ovt, mid, preferred_element_type=F32)
           + jnp.dot(ovt, lo, preferred_element_type=F32))
    n_slc = imp.shape[0]
    jb = lax.broadcasted_iota(I32, (n_slc, QW), 0)
    tq = i * QW + lax.broadcasted_iota(I32, (n_slc, QW), 1)
    cur = tq >> (SLC_BLOCK.bit_length() - 1)
    forced = (jb == 0) | (jb == cur) | (jb == cur - 1)
    val = jnp.where(forced, FORCE, jnp.where(jb <= cur, imp, NEG))
    blk_scr[...] = val

    rows_per_step = QW // SLC_BLOCK

    def rank_body(g, rank):
        j0 = g * rows_per_step
        for r in range(rows_per_step):
            row = blk_scr[pl.ds(j0 + r, 1), :]
            rank = rank + jnp.where(row > val, 1, jnp.where(row == val, jnp.where(jb > j0 + r, 1, 0), 0))
        return rank

    rank = lax.fori_loop(0, i + 1, rank_body, jnp.zeros((n_slc, QW), I32))
    blk_scr[...] = jnp.where(rank < n_top, jnp.where(val > 0.5 * NEG, 0.0, NEG), NEG)

    kmax = caub_ref.shape[0] // 2

    def slc_chunk(size, base):
        def body(ci, carry):
            m, acc = carry
            k0 = pl.multiple_of(base + ci * size, KC)
            ahead = jnp.minimum(i * QW - k0, kmax)
            b0 = k0 // SLC_BLOCK
            rows = [jnp.broadcast_to(blk_scr[pl.ds(b0 + r, 1), :], (SLC_BLOCK, QW)) for r in range(size // SLC_BLOCK)]
            bias = (jnp.concatenate(rows, axis=0)
                    + caub_ref[pl.ds(pl.multiple_of(kmax - ahead, QB), size), :]).astype(BF16)
            s = jnp.dot(ks_scr[pl.ds(k0, size), :], qr_st, preferred_element_type=F32).astype(BF16) + tile_h(bias)
            m_new = jnp.maximum(m, jnp.max(s, axis=0, keepdims=True))
            alpha = jnp.exp2((m - m_new).astype(F32))
            e = jnp.exp2(s - m_new)
            acc = alpha * acc + jnp.dot(vst_scr[:, pl.ds(k0, size)], e, preferred_element_type=F32)
            return m_new, acc
        return body

    n_chunks = (i * QW + QW + KC - 1) // KC
    n_long = n_chunks // ATT_LONG
    init = (jnp.full((1, HW), NEG, BF16), jnp.zeros((HEAD_DIM + ONES_ROWS, HW), F32))
    state = lax.fori_loop(0, n_long, slc_chunk(ATT_LONG * KC, 0), init)
    m_s, acc_s = lax.fori_loop(0, n_chunks - n_long * ATT_LONG, slc_chunk(KC, n_long * (ATT_LONG * KC)), state)
    o_slc = jnp.where(m_s.astype(F32) > 0.5 * NEG, acc_s[:HEAD_DIM] / acc_s[HEAD_DIM:HEAD_DIM + 1], 0.0)

    span = min(WIN + QW, seq)
    w0 = pl.multiple_of(jnp.maximum(i * QW + QW - span, 0), QB)
    wb = winb_ref[pl.ds(pl.multiple_of(span - QW - (i * QW - w0), QB), span), :].astype(BF16)
    s_w = jnp.dot(kw_scr[pl.ds(w0, span), :], qr_st, preferred_element_type=F32).astype(BF16) + tile_h(wb)
    e_w = jnp.exp2(s_w - jnp.max(s_w, axis=0, keepdims=True))
    acc_w = jnp.dot(vwt_scr[:, pl.ds(w0, span)], e_w, preferred_element_type=F32)
    o_win = acc_w[:HEAD_DIM] / acc_w[HEAD_DIM:HEAD_DIM + 1]

    gt = jax.nn.sigmoid(g_ref[...].T)
    outs = []
    for h in range(H):
        sl = slice(QW * h, QW * (h + 1))
        outs.append(gt[3 * h:3 * h + 1, :] * o_cmp[:, sl] + gt[3 * h + 1:3 * h + 2, :] * o_slc[:, sl]
                    + gt[3 * h + 2:3 * h + 3, :] * o_win[:, sl])
    outs.append(jnp.zeros((o_ref.shape[1] - H * HEAD_DIM, QW), F32))
    o_ref[...] = jnp.concatenate(outs, axis=0).T.astype(BF16)


def _nsa(proj, kvcmp, tabs, ovt, batch, seq):
    qw = NSA_QW
    nb = seq // qw
    n_slc = seq // SLC_BLOCK
    ncp = kvcmp.shape[1]
    col = lambda name, w: _OFF[name] // w
    qmap = lambda cidx: (lambda b, i: (b * nb + i, cidx))
    tmap = lambda k: (lambda b, i: (k, i, 0))
    kern = functools.partial(_nsa_kernel, n_top=min(SLC_TOPN, n_slc))
    cmpb, winb, caub = _cmp_bias_table(seq, qw), _win_bias_table(seq, qw), _causal_bias_table(qw)
    whole = lambda a: pl.BlockSpec(a.shape, lambda b, i: (0, 0))
    return pl.pallas_call(
        kern,
        grid=(batch, nb),
        in_specs=[pl.BlockSpec((qw, 384), qmap(col("a_q", 384))),
                  pl.BlockSpec((qw, LANES), qmap(col("a_g", LANES))),
                  pl.BlockSpec((qw, LANES), qmap(col("a_ksvs", LANES))),
                  pl.BlockSpec((qw, LANES), qmap(col("a_kwvw", LANES))),
                  pl.BlockSpec((1, ncp, LANES), lambda b, i: (b, 0, 0)),
                  pl.BlockSpec((None, qw, LANES), tmap(0)),
                  pl.BlockSpec((None, qw, LANES), tmap(1)),
                  pl.BlockSpec((None, qw, LANES), tmap(2)),
                  whole(ovt), whole(cmpb), whole(winb), whole(caub)],
        out_specs=pl.BlockSpec((qw, 384), lambda b, i: (b * nb + i, 0)),
        out_shape=jax.ShapeDtypeStruct((batch * seq, 384), BF16),
        scratch_shapes=[pltpu.VMEM((seq, LANES), BF16), pltpu.VMEM((HEAD_DIM + ONES_ROWS, seq), BF16),
                        pltpu.VMEM((seq, LANES), BF16), pltpu.VMEM((HEAD_DIM + ONES_ROWS, seq), BF16),
                        pltpu.VMEM((ncp, LANES), BF16), pltpu.VMEM((HEAD_DIM, ncp), BF16),
                        pltpu.VMEM((n_slc, qw), F32)],
        compiler_params=_params(2),
        name="nsa",
    )(proj, proj, proj, proj, kvcmp, tabs, tabs, tabs, ovt, cmpb, winb, caub)


SEARCH_BISECT_FROM = 24
SEARCH_MAX_PASSES = 64
SEARCH_KEPT_LIMIT = 6
SEARCH_BLIND_PASSES = 12
SEARCH_SNAP_KEYS = 3
INT_MAX = 2 ** 31 - 1
FLT_MIN_NORMAL = 1.17549435e-38
ZERO_BAND_END = 1 << 23


def _tree_rows(x, op):
    parts = [x[r:r + 8, :] for r in range(0, x.shape[0], 8)]
    while len(parts) > 1:
        nxt = [op(parts[a], parts[a + 1]) for a in range(0, len(parts) - 1, 2)]
        if len(parts) % 2:
            nxt.append(parts[-1])
        parts = nxt
    return parts[0]


def _f2key(v):
    bits = lax.bitcast_convert_type(v, I32)
    return bits ^ ((bits >> 31) & 0x7FFFFFFF)


def _key2f(k):
    return lax.bitcast_convert_type(k ^ ((k >> 31) & 0x7FFFFFFF), F32)


def _dsa_kernel(q_ref, iq_ref, ikw_ref, kv_ref, c_ref, s1_ref, s2_ref, ci_ref, si1_ref, si2_ref, o_ref,
                k_scr, vt_scr, ik_scr, key_scr, *, top, idx_bits):
    i = pl.program_id(1)
    H = DSA_HEADS
    QW = q_ref.shape[0]
    HW = H * QW
    seq = key_scr.shape[0]
    lane_q = lax.broadcasted_iota(I32, (QW, LANES), 1)

    @pl.when(i == 0)
    def _init():
        k_scr[...] = jnp.zeros_like(k_scr)
        vt_scr[0:HEAD_DIM, :] = jnp.zeros((HEAD_DIM, seq), BF16)
        vt_scr[HEAD_DIM:, :] = jnp.ones((ONES_ROWS, seq), BF16)
        ik_scr[...] = jnp.zeros_like(ik_scr)

    c, s1, s2 = c_ref[...], s1_ref[...], s2_ref[...]
    ci, si1, si2 = ci_ref[...], si1_ref[...], si2_ref[...]
    r0 = pl.multiple_of(i * QW, QW)
    kv = kv_ref[...]
    k_scr[pl.ds(r0, QW), :] = jnp.where(lane_q < HEAD_DIM, _rope(kv, c, s1, s2, ROPE_DIM // 2), 0.0).astype(BF16)
    vt_scr[0:HEAD_DIM, pl.ds(r0, QW)] = kv.T[HEAD_DIM:2 * HEAD_DIM, :].astype(BF16)
    ikw = ikw_ref[...]
    ik_scr[pl.ds(r0, QW), :] = jnp.where(lane_q < IDX_DIM, _rope(ikw, ci, si1, si2, IDX_ROPE_DIM // 2), 0.0).astype(BF16)

    qr = _rope_wide(q_ref[...] * (SCALE * LOG2E), c, s1, s2, ROPE_DIM // 2)
    qr_st = _stack_heads_t(qr.T, H, HEAD_DIM).astype(BF16)
    iqr = _rope_wide(iq_ref[...], ci, si1, si2, IDX_ROPE_DIM // 2)
    iq_st = _stack_heads_t(iqr.T, IDX_HEADS, IDX_DIM).astype(BF16)
    wt = ikw.T * ((IDX_DIM ** -0.5) * (IDX_HEADS ** -0.5))
    w_st = jnp.concatenate([wt[IDX_DIM + h:IDX_DIM + h + 1, :] for h in range(IDX_HEADS)], axis=1)

    n_chunks = (i * QW + QW + KC - 1) // KC
    k_io = lax.broadcasted_iota(I32, (KC, QW), 0)
    tq = i * QW + lax.broadcasted_iota(I32, (KC, QW), 1)

    sub_io = lax.broadcasted_iota(I32, (QB, QW), 0)
    sub_tq = i * QW + lax.broadcasted_iota(I32, (QB, QW), 1)
    zkey0 = seq - sub_io

    def score_body(cidx, carry):
        mx, mn = carry
        k0 = pl.multiple_of(cidx * KC, KC)
        for sb in range(KC // QB):
            kb = pl.multiple_of(k0 + sb * QB, QB)
            d = jnp.dot(ik_scr[pl.ds(kb, QB), :], iq_st, preferred_element_type=F32)
            r = jnp.maximum(d, 0.0) * w_st
            sc = r[:, 0:QW]
            for h in range(1, IDX_HEADS):
                sc = sc + r[:, QW * h:QW * (h + 1)]
            key = jnp.where(jnp.abs(sc) < FLT_MIN_NORMAL, zkey0 - kb, _f2key(sc))
            key_scr[pl.ds(kb, QB), :] = jnp.where(sub_io + kb <= sub_tq, key, INT_MIN)
            mx = jnp.maximum(mx, _tree_rows(key, jnp.maximum))
            mn = jnp.minimum(mn, _tree_rows(key, jnp.minimum))
        return mx, mn

    mx8, mn8 = lax.fori_loop(0, n_chunks, score_body,
                             (jnp.full((8, QW), INT_MIN, I32), jnp.full((8, QW), INT_MAX, I32)))

    def count(pred):
        def body(cidx, acc):
            k0 = pl.multiple_of(cidx * KC, KC)
            return acc + _tree_rows(pred(key_scr[pl.ds(k0, KC), :], k_io + k0), jnp.add)
        acc8 = lax.fori_loop(0, n_chunks, body, jnp.zeros((8, QW), F32))
        return jnp.sum(acc8, axis=0, keepdims=True)

    topf = float(top)
    lo0 = functools.reduce(jnp.minimum, [mn8[r:r + 1, :] for r in range(8)])
    hi0 = functools.reduce(jnp.maximum, [mx8[r:r + 1, :] for r in range(8)]) + 1
    n_causal = (i * QW + 1 + lax.broadcasted_iota(I32, (1, QW), 1)).astype(F32)
    log_top = float(np.log(top))

    def in_zero_band(lo, hi):
        return jnp.where(lo >= 1, jnp.where(hi <= ZERO_BAND_END, 1.0, 0.0), 0.0) > 0.5

    def open_keys(lo, hi, c_lo, c_hi):
        return jnp.max(jnp.where(c_lo > topf, jnp.where(hi > lo + 1, c_lo - c_hi, 0.0), 0.0)).astype(I32)

    def search_cond(c):
        return jnp.logical_and(c[0] < SEARCH_MAX_PASSES, c[1] > SEARCH_SNAP_KEYS)

    def search_pass(it, st):
        lo, hi, c_lo, c_hi, kept, w_lo, w_hi = st
        f_lo = (jnp.log(c_lo) - log_top) * w_lo
        f_hi = (log_top - jnp.log(jnp.maximum(c_hi, 0.5))) * w_hi
        frac = f_lo / (f_lo + f_hi)
        v_lo, v_hi = _key2f(lo), _key2f(hi)
        cand = _f2key(v_lo + frac * (v_hi - v_lo))
        lo_f, hi_f = lo.astype(F32), hi.astype(F32)
        cand = jnp.where(in_zero_band(lo, hi), (lo_f + frac * (hi_f - lo_f)).astype(I32), cand)
        key_mid = (lo >> 1) + (hi >> 1) + (lo & hi & 1)
        cand = jnp.where(jnp.abs(kept) >= SEARCH_KEPT_LIMIT, key_mid, cand)
        cand = jnp.where(it >= SEARCH_BISECT_FROM, key_mid, cand)
        cand = jnp.where(it == 0, 1, jnp.where(it == 1, ZERO_BAND_END, cand))
        cand = jnp.minimum(jnp.maximum(cand, lo + 1), hi - 1)
        cnt = count(lambda key, kpos: jnp.where(key >= cand, 1.0, 0.0))
        active = jnp.where(c_lo > topf, jnp.where(hi > lo + 1, 1.0, 0.0), 0.0) > 0.5
        up = jnp.where(active, jnp.where(cnt >= topf, 1.0, 0.0), 0.0) > 0.5
        dn = jnp.where(active, jnp.where(cnt >= topf, 0.0, 1.0), 0.0) > 0.5
        w_hi = jnp.where(up, jnp.where(kept < 0, 0.5 * w_hi, 1.0), jnp.where(dn, 1.0, w_hi))
        w_lo = jnp.where(dn, jnp.where(kept > 0, 0.5 * w_lo, 1.0), jnp.where(up, 1.0, w_lo))
        kept = jnp.where(up, jnp.where(kept < 0, kept - 1, -1), jnp.where(dn, jnp.where(kept > 0, kept + 1, 1), kept))
        lo, c_lo = jnp.where(up, cand, lo), jnp.where(up, cnt, c_lo)
        hi, c_hi = jnp.where(dn, cand, hi), jnp.where(dn, cnt, c_hi)
        hi = jnp.where(in_zero_band(lo, hi), jnp.minimum(hi, seq + 1), hi)
        return lo, hi, c_lo, c_hi, kept, w_lo, w_hi

    def search_body(c):
        st = search_pass(c[0], c[2])
        return c[0] + 1, open_keys(*st[:4]), st

    zero_f, one_f = jnp.zeros((1, QW), F32), jnp.ones((1, QW), F32)
    st0 = (lo0, hi0, n_causal, zero_f, jnp.zeros((1, QW), I32), one_f, one_f)
    n_blind = jnp.where(i * QW + QW > top, SEARCH_BLIND_PASSES, 0)
    st1 = lax.fori_loop(0, n_blind, search_pass, st0)
    _, _, st2 = lax.while_loop(search_cond, search_body, (n_blind, open_keys(*st1[:4]), st1))

    def snap_body(c):
        lo, hi, c_lo, c_hi = c[1]

        def below_hi(cidx, acc):
            k0 = pl.multiple_of(cidx * KC, KC)
            key = key_scr[pl.ds(k0, KC), :]
            return jnp.maximum(acc, _tree_rows(jnp.where(key < hi, key, INT_MIN), jnp.maximum))
        top8 = lax.fori_loop(0, n_chunks, below_hi, jnp.full((8, QW), INT_MIN, I32))
        cand = functools.reduce(jnp.maximum, [top8[r:r + 1, :] for r in range(8)])
        cnt = count(lambda key, kpos: jnp.where(key >= cand, 1.0, 0.0))
        active = jnp.where(c_lo > topf, jnp.where(hi > lo + 1, 1.0, 0.0), 0.0) > 0.5
        up = jnp.where(active, jnp.where(cnt >= topf, 1.0, 0.0), 0.0) > 0.5
        dn = jnp.where(active, jnp.where(cnt >= topf, 0.0, 1.0), 0.0) > 0.5
        lo, c_lo = jnp.where(up, cand, lo), jnp.where(up, cnt, c_lo)
        hi, c_hi = jnp.where(up, cand + 1, jnp.where(dn, cand, hi)), jnp.where(dn, cnt, c_hi)
        return open_keys(lo, hi, c_lo, c_hi), (lo, hi, c_lo, c_hi)

    _, (thr, _, c_lo, c_hi) = lax.while_loop(lambda c: c[0] > 0, snap_body, (open_keys(*st2[:4]), st2[:4]))

    tied = c_lo > topf
    need1 = (topf - 1.0) - c_hi
    all_pos = jnp.full((1, QW), 2 ** idx_bits - 1, I32)

    def tie_search():
        def tie_body(bi, cur):
            cand = cur | lax.shift_left(jnp.int32(1), idx_bits - 1 - bi)
            cnt = count(lambda key, kpos: jnp.where(key == thr, jnp.where(kpos < cand, 1.0, 0.0), 0.0))
            return jnp.where(cnt <= need1, cand, cur)
        return lax.fori_loop(0, idx_bits, tie_body, jnp.zeros((1, QW), I32))

    any_tied = jnp.max(jnp.where(tied, 1.0, 0.0)) > 0.5
    last = jnp.where(tied, lax.cond(any_tied, tie_search, lambda: all_pos), all_pos)

    half_neg = int(np.float32(0.5 * NEG).view(np.int32))
    key_floor = half_neg ^ 0x7FFFFFFF

    def att_chunk(size, base):
        pos_io = lax.broadcasted_iota(I32, (size, QW), 0)

        def body(cidx, state):
            m, acc = state
            k0 = pl.multiple_of(base + cidx * size, KC)
            key = key_scr[pl.ds(k0, size), :]
            kpos = pos_io + k0
            bias = jnp.where(key > thr, 0.0, jnp.where(key == thr, jnp.where(kpos <= last, 0.0, NEG), NEG))
            bias = jnp.where(key > key_floor, bias, NEG).astype(BF16)
            s = (jnp.dot(k_scr[pl.ds(k0, size), :], qr_st, preferred_element_type=F32).astype(BF16)
                 + jnp.concatenate([bias] * H, axis=1))
            m_new = jnp.maximum(m, jnp.max(s, axis=0, keepdims=True))
            alpha = jnp.exp2((m - m_new).astype(F32))
            e = jnp.exp2(s - m_new)
            acc = alpha * acc + jnp.dot(vt_scr[:, pl.ds(k0, size)], e, preferred_element_type=F32)
            return m_new, acc
        return body

    init = (jnp.full((1, HW), NEG, BF16), jnp.zeros((HEAD_DIM + ONES_ROWS, HW), F32))
    n_long = n_chunks // ATT_LONG
    state = lax.fori_loop(0, n_long, att_chunk(ATT_LONG * KC, 0), init)
    m_a, acc_a = lax.fori_loop(0, n_chunks - n_long * ATT_LONG, att_chunk(KC, n_long * (ATT_LONG * KC)), state)
    l_a = acc_a[HEAD_DIM:HEAD_DIM + 1, :]
    o = jnp.where(m_a.astype(F32) > 0.5 * NEG, acc_a[:HEAD_DIM, :] / l_a, 0.0)
    outs = [o[:, QW * h:QW * (h + 1)] for h in range(H)]
    outs.append(jnp.zeros((o_ref.shape[1] - H * HEAD_DIM, QW), F32))
    o_ref[...] = jnp.concatenate(outs, axis=0).T.astype(BF16)


def _dsa(proj, tabs, tabs_i, batch, seq):
    qw = DSA_QW
    nb = seq // qw
    col = lambda name, w: _OFF[name] // w
    qmap = lambda cidx: (lambda b, i: (b * nb + i, cidx))
    tmap = lambda k: (lambda b, i: (k, i, 0))
    top = min(DSA_TOPK_MAX, seq // 4)
    kern = functools.partial(_dsa_kernel, top=top, idx_bits=int(seq).bit_length())
    return pl.pallas_call(
        kern,
        grid=(batch, nb),
        in_specs=[pl.BlockSpec((qw, 384), qmap(col("b_q", 384))),
                  pl.BlockSpec((qw, 256), qmap(col("b_iq", 256))),
                  pl.BlockSpec((qw, LANES), qmap(col("b_ikiw", LANES))),
                  pl.BlockSpec((qw, LANES), qmap(col("b_kv", LANES))),
                  pl.BlockSpec((None, qw, LANES), tmap(0)),
                  pl.BlockSpec((None, qw, LANES), tmap(1)),
                  pl.BlockSpec((None, qw, LANES), tmap(2)),
                  pl.BlockSpec((None, qw, LANES), tmap(0)),
                  pl.BlockSpec((None, qw, LANES), tmap(1)),
                  pl.BlockSpec((None, qw, LANES), tmap(2))],
        out_specs=pl.BlockSpec((qw, 384), lambda b, i: (b * nb + i, 0)),
        out_shape=jax.ShapeDtypeStruct((batch * seq, 384), BF16),
        scratch_shapes=[pltpu.VMEM((seq, LANES), BF16), pltpu.VMEM((HEAD_DIM + ONES_ROWS, seq), BF16),
                        pltpu.VMEM((seq, LANES), BF16), pltpu.VMEM((seq, qw), I32)],
        compiler_params=_params(2),
        name="dsa",
    )(proj, proj, proj, proj, tabs, tabs, tabs, tabs_i, tabs_i, tabs_i)


DIL_MB = max(d for _, d in DIL_PAIRS) * QB
DIL_VMEM_LIMIT = 56 * 1024 * 1024
DIL_UNROLL = 8


def _dil_layout(g):
    dil = DIL_PAIRS[g][1]
    per = DIL_MB // dil
    return dil, per, per + QB


def _dil_bias_table():
    u = np.arange(2 * QB)[:, None]
    diff = QB + np.arange(LANES)[None, :] - u
    ok = (diff >= 0) & (diff <= QB)
    return _bias_table(np.concatenate([ok, ok & (u >= QB)], axis=0))


def _dil_kernel(*refs):
    G, HG = len(DIL_PAIRS), DIL_HEADS_PER_GROUP
    q_refs, k_refs, v_refs = refs[0:G], refs[G:2 * G], refs[2 * G:3 * G]
    c_ref, s1_ref, s2_ref, bias_ref, o_ref = refs[3 * G:3 * G + 5]
    kds, vds, ogs, lss = (refs[3 * G + 5 + n * G:3 * G + 5 + (n + 1) * G] for n in range(4))
    step = pl.program_id(1)
    half = ROPE_DIM // 2

    @pl.when(step == 0)
    def _zero():
        for g in range(G):
            kds[g][...] = jnp.zeros_like(kds[g])
            vds[g][...] = jnp.zeros_like(vds[g])

    @pl.when(step > 0)
    def _carry_halo():
        for g in range(G):
            dil, per, stride = _dil_layout(g)
            for r in range(dil):
                b0 = r * stride
                kds[g][b0:b0 + QB, :] = kds[g][b0 + per:b0 + per + QB, :]
                vds[g][:, b0:b0 + QB] = vds[g][:, b0 + per:b0 + per + QB]

    for g in range(G):
        dil, per, stride = _dil_layout(g)
        piece = min(per, KC)
        for r in range(dil):
            for p0 in range(0, per, piece):
                rows = pl.ds(r + dil * p0, piece, stride=dil)
                kr = _rope(k_refs[g][rows, :], c_ref[rows, :], s1_ref[rows, :], s2_ref[rows, :], half)
                d0 = r * stride + QB + p0
                kds[g][d0:d0 + piece, :] = kr.astype(BF16)
                vds[g][:, d0:d0 + piece] = v_refs[g][rows, :].T.astype(BF16)

    zero = jnp.zeros((HEAD_DIM, LANES), F32)
    for g in range(G):
        dil, per, stride = _dil_layout(g)
        nblk = per // QB

        def block(jb, _, r=0, g=g, dil=dil, stride=stride):
            p0 = pl.multiple_of(dil * QB * jb, dil * QB)
            win = pl.ds(p0, dil * QB)
            rows = pl.ds(r, QB, stride=dil)
            q = _rope(q_refs[g].at[win, :][rows, :] * (SCALE * LOG2E), c_ref.at[win, :][rows, :],
                      s1_ref.at[win, :][rows, :], s2_ref.at[win, :][rows, :], half)
            qt = q.T
            q2 = jnp.concatenate([jnp.concatenate([qt[:HEAD_DIM], zero], axis=0),
                                  jnp.concatenate([zero, qt[HEAD_DIM:]], axis=0)], axis=1).astype(BF16)
            kb = pl.multiple_of(r * stride + QB * jb, QB)
            first = jnp.logical_and(step == 0, jb == 0)
            bias = bias_ref[pl.ds(pl.multiple_of(jnp.where(first, 2 * QB, 0), QB), 2 * QB), :]
            s = (jnp.dot(kds[g][pl.ds(kb, 2 * QB), :], q2, preferred_element_type=F32)
                 + jnp.concatenate([bias] * HG, axis=1))
            e, m, den = _softmax2_cols(s)
            lse = m + jnp.log2(den)
            rden = 1.0 / den
            parts, lparts = [], []
            for hg in range(HG):
                sl = slice(LANES * hg, LANES * (hg + 1))
                vt = vds[g][HEAD_DIM * hg:HEAD_DIM * (hg + 1), pl.ds(kb, 2 * QB)]
                parts.append(jnp.dot(vt, e[:, sl].astype(BF16), preferred_element_type=F32) * rden[:, sl])
                lparts.append(jnp.broadcast_to(lse[:, sl], (HEAD_DIM, LANES)))
            tile = jnp.concatenate(parts + lparts, axis=0).T
            ogs[g].at[win, :][rows, :] = tile[:, :LANES]
            lss[g].at[win, :][rows, :] = tile[:, LANES:]
            return 0

        for r in range(dil):
            lax.fori_loop(0, nblk, functools.partial(block, r=r), 0, unroll=min(nblk, DIL_UNROLL))

    def mix(ti, _):
        rows = pl.ds(pl.multiple_of(ti * KC, KC), KC)
        ls = [lss[g][rows, :] for g in range(G)]
        mx = functools.reduce(jnp.maximum, ls)
        ex = [jnp.exp2(x - mx) for x in ls]
        rtot = 1.0 / functools.reduce(lambda a, b: a + b, ex)
        o_ref[rows, :] = jnp.concatenate([ex[g] * rtot * ogs[g][rows, :] for g in range(G)], axis=1).astype(BF16)
        return 0

    lax.fori_loop(0, DIL_MB // KC, mix, 0)


def _dil(proj, tabs, batch, seq):
    assert seq % DIL_MB == 0 and all(w // d == QB for w, d in DIL_PAIRS)
    nm = seq // DIL_MB
    G = len(DIL_PAIRS)
    w = DIL_HEADS * HEAD_DIM
    gmap = lambda name, g: (lambda b, j: (b * nm + j, _OFF[name] // LANES + g))
    tmap = lambda k: (lambda b, j: (k, j, 0))
    bias = _dil_bias_table()
    rows = [_dil_layout(g)[0] * _dil_layout(g)[2] for g in range(G)]
    return pl.pallas_call(
        _dil_kernel,
        grid=(batch, nm),
        in_specs=([pl.BlockSpec((DIL_MB, LANES), gmap(name, g)) for name in ("c_q", "c_k", "c_v") for g in range(G)]
                  + [pl.BlockSpec((None, DIL_MB, LANES), tmap(k)) for k in range(3)]
                  + [pl.BlockSpec(bias.shape, lambda b, j: (0, 0))]),
        out_specs=pl.BlockSpec((DIL_MB, w), lambda b, j: (b * nm + j, 0)),
        out_shape=jax.ShapeDtypeStruct((batch * seq, w), BF16),
        scratch_shapes=([pltpu.VMEM((n, LANES), BF16) for n in rows] + [pltpu.VMEM((LANES, n), BF16) for n in rows]
                        + [pltpu.VMEM((DIL_MB, LANES), F32)] * (2 * G)),
        compiler_params=_params(2, DIL_VMEM_LIMIT),
        name="dilated",
    )(*([proj] * (3 * G)), tabs, tabs, tabs, bias)


def _overlap_t(seq):
    n_cmp_pad = seq // CMP_STRIDE
    n_slc = seq // SLC_BLOCK
    c_start = np.arange(n_cmp_pad) * CMP_STRIDE
    s_start = np.arange(n_slc) * SLC_BLOCK
    ov = ((c_start[None, :] < s_start[:, None] + SLC_BLOCK) & (c_start[None, :] + CMP_BLOCK > s_start[:, None]))
    return jnp.asarray(ov.astype(np.float32), dtype=BF16)


def _layer(x2, batch, seq, layer, norm1_g, w_in, cmp_pos, cmp_w1, cmp_w2, w_out, norm2_g, w_up_bf, conv_w, conv_b,
           w_down_bf, final_g, final_norm, tabs, tabs_i, ovt):
    n = batch * seq
    proj = _inproj(x2, norm1_g, _regroup_w_in(w_in, layer))

    w1 = cmp_w1.reshape(2, CMP_BLOCK, HEAD_DIM, CMP_HIDDEN)
    z1 = jnp.zeros_like(w1[0])
    w1_cat = jnp.concatenate([jnp.concatenate([w1[0], z1], axis=2),
                              jnp.concatenate([z1, w1[1]], axis=2)], axis=1).astype(BF16)
    z2 = jnp.zeros((CMP_HIDDEN, HEAD_DIM), cmp_w2.dtype)
    w2_cat = jnp.concatenate([jnp.concatenate([cmp_w2[0], z2], axis=1),
                              jnp.concatenate([z2, cmp_w2[1]], axis=1)], axis=0).astype(BF16)
    pos_cat = jnp.concatenate([cmp_pos[0], cmp_pos[1]], axis=1)
    kvcmp = _nsa_compress(proj, pos_cat, w1_cat, w2_cat, batch, seq)

    o_a = _nsa(proj, kvcmp, tabs, ovt, batch, seq)
    o_b = _dsa(proj, tabs, tabs_i, batch, seq)
    o_c = _dil(proj, tabs, batch, seq)

    hp = NSA_HEADS * HEAD_DIM
    zrow = jnp.zeros((384 - hp, D_MODEL), w_out.dtype)
    w_out_pad = jnp.concatenate([w_out[0:hp], zrow, w_out[hp:2 * hp], zrow, w_out[2 * hp:]], axis=0).astype(BF16)
    x2 = _outproj(o_a, o_b, o_c, w_out_pad, x2)

    act = _ffn_up(x2, norm2_g, w_up_bf, layer, conv_w, conv_b, seq)
    return _ffn_down(act, w_down_bf, layer, x2, final_g, final_norm)


def kernel(x, norm1_g, w_in, cmp_pos, cmp_w1, cmp_w2, w_out, norm2_g, w_up, conv_w, conv_b, w_down, final_g):
    batch, seq, d = x.shape
    depth = w_in.shape[0]
    tabs = _rope_lane_tables(seq, HEAD_DIM, ROPE_DIM)
    tabs_i = _rope_lane_tables(seq, IDX_DIM, IDX_ROPE_DIM)
    ovt = _overlap_t(seq)
    x2 = x.reshape(batch * seq, d)
    w_up_bf, w_down_bf = w_up.astype(BF16), w_down.astype(BF16)
    for li in range(depth):
        x2 = _layer(x2, batch, seq, li, norm1_g[li], w_in, cmp_pos[li], cmp_w1[li], cmp_w2[li], w_out[li],
                    norm2_g[li], w_up_bf, conv_w[li], conv_b[li], w_down_bf, final_g, li == depth - 1,
                    tabs, tabs_i, ovt)
    return x2.reshape(batch, seq, d)
```
